```python
import jax, jax.numpy as jnp
from jax import lax
import numpy as np

D_MODEL = 1024
BATCH = 8
SEQ = 4096
DEPTH = 4

HEAD_DIM = 64
PLE_DIM = 256
D_FF = 4 * D_MODEL
EPS = 1e-6

GDN_WIDTH = D_MODEL // 4
GDN_HEADS = GDN_WIDTH // HEAD_DIM
CONV_WIDTH = 4
GDN_CHUNK = 64

MLSTM_WIDTH = D_MODEL // 4
MLSTM_HEADS = MLSTM_WIDTH // HEAD_DIM
MLSTM_CHUNK = 64
GATE_SOFTCAP = 15.0

SWA_WIDTH = D_MODEL - GDN_WIDTH - MLSTM_WIDTH
SWA_Q_HEADS = SWA_WIDTH // HEAD_DIM
SWA_KV_HEADS = SWA_Q_HEADS // 4
SWA_GROUP = SWA_Q_HEADS // SWA_KV_HEADS
SWA_WINDOW = 128
SWA_BLOCK = 128
ROPE_THETA = 500000.0
ROPE_DIM = HEAD_DIM // 4

MIX_WIDTH = GDN_WIDTH + MLSTM_WIDTH + SWA_WIDTH
IN_SPLITS = (
    GDN_WIDTH, GDN_WIDTH, GDN_WIDTH, GDN_WIDTH, GDN_HEADS, GDN_HEADS,
    MLSTM_WIDTH, MLSTM_WIDTH, MLSTM_WIDTH, MLSTM_WIDTH, MLSTM_HEADS, MLSTM_HEADS,
    SWA_WIDTH, SWA_KV_HEADS * HEAD_DIM, SWA_KV_HEADS * HEAD_DIM,
)
IN_COLS = sum(IN_SPLITS)

kernel_name = 'hybrid_gdn_mlstm_swa_trunk'


def rms_norm(x, g):
    xf = x.astype(jnp.float32)
    y = xf * lax.rsqrt(jnp.mean(xf * xf, axis=-1, keepdims=True) + EPS)
    return (y * g.astype(jnp.float32)).astype(x.dtype)


def split_cols(t, sizes):
    offs = np.cumsum(np.array(sizes))[:-1].tolist()
    return jnp.split(t, offs, axis=-1)


def l2_normalize(x):
    return x * lax.rsqrt(jnp.sum(x * x, axis=-1, keepdims=True) + EPS)


def softcap(x, cap):
    return cap * jnp.tanh(x / cap)


def causal_depthwise_conv(x, w):
    K = w.shape[0]
    seq = x.shape[1]
    xp = jnp.pad(x, ((0, 0), (K - 1, 0), (0, 0)))
    out = xp[:, 0:seq] * w[0]
    for j in range(1, K):
        out = out + xp[:, j:j + seq] * w[j]
    return out


def to_chunked_heads(t, n_heads, chunk):
    bsz, seq, _ = t.shape
    return t.reshape(bsz, seq // chunk, chunk, n_heads, -1).transpose(0, 3, 1, 2, 4)


def to_chunked_gates(t, chunk):
    bsz, seq, n_heads = t.shape
    return t.reshape(bsz, seq // chunk, chunk, n_heads).transpose(0, 3, 1, 2)


def from_scan_heads(o):
    n, bsz, h, c, d = o.shape
    return o.transpose(1, 0, 3, 2, 4).reshape(bsz, n * c, h, d)


def rope_tables(positions):
    inv_freq = ROPE_THETA ** (-jnp.arange(0, ROPE_DIM, 2, dtype=jnp.float32) / ROPE_DIM)
    ang = positions.astype(jnp.float32)[..., None] * inv_freq
    return jnp.cos(ang), jnp.sin(ang)


def apply_partial_rope(x, cos, sin):
    half = ROPE_DIM // 2
    cos = cos.astype(x.dtype)
    sin = sin.astype(x.dtype)
    x1 = x[..., :half]
    x2 = x[..., half:ROPE_DIM]
    return jnp.concatenate([x1 * cos - x2 * sin, x2 * cos + x1 * sin, x[..., ROPE_DIM:]], axis=-1)


def gated_delta_net(q, k, v, z, beta_pre, a_pre, conv_w, a_log, dt_bias, norm_w):
    bsz, seq, _ = q.shape
    H, D, C = GDN_HEADS, HEAD_DIM, GDN_CHUNK
    f32 = jnp.float32
    qkv = jax.nn.silu(causal_depthwise_conv(jnp.concatenate([q, k, v], axis=-1), conv_w).astype(f32))
    q, k, v = (to_chunked_heads(t, H, C) for t in jnp.split(qkv, 3, axis=-1))
    q = l2_normalize(q) * (D ** -0.5)
    k = l2_normalize(k)
    beta = to_chunked_gates(jax.nn.sigmoid(beta_pre.astype(f32)), C)
    g = -jnp.exp(a_log.astype(f32)) * jax.nn.softplus(a_pre.astype(f32) + dt_bias.astype(f32))
    gc = jnp.cumsum(to_chunked_gates(g, C), axis=-1)
    causal = jnp.tril(jnp.ones((C, C), dtype=bool))
    strict = jnp.tril(jnp.ones((C, C), dtype=bool), -1)
    decay = jnp.exp(jnp.where(causal, gc[..., :, None] - gc[..., None, :], -jnp.inf))
    k_beta = k * beta[..., None]
    kk = jnp.einsum('bhncd,bhnsd->bhncs', k_beta, k) * decay
    lhs = jnp.eye(C, dtype=f32) + jnp.where(strict, kk, 0.0)
    rhs = jnp.concatenate([v * beta[..., None], k_beta * jnp.exp(gc)[..., None]], axis=-1)
    u, w = jnp.split(lax.linalg.triangular_solve(lhs, rhs, left_side=True, lower=True), 2, axis=-1)
    qk = jnp.einsum('bhncd,bhnsd->bhncs', q, k) * decay
    q_dec = q * jnp.exp(gc)[..., None]
    g_last = gc[..., -1]
    k_dec = k * jnp.exp(g_last[..., None] - gc)[..., None]

    def step(state, xs):
        u_c, w_c, qk_c, qd_c, kd_c, gl_c = xs
        v_new = u_c - jnp.einsum('bhcd,bhde->bhce', w_c, state)
        o = jnp.einsum('bhcd,bhde->bhce', qd_c, state) + jnp.einsum('bhcs,bhse->bhce', qk_c, v_new)
        state = state * jnp.exp(gl_c)[..., None, None] + jnp.einsum('bhcd,bhce->bhde', kd_c, v_new)
        return state, o

    xs = tuple(jnp.moveaxis(t, 2, 0) for t in (u, w, qk, q_dec, k_dec, g_last))
    _, o = lax.scan(step, jnp.zeros((bsz, H, D, D), f32), xs)
    o = from_scan_heads(o)
    o = rms_norm(o, norm_w) * jax.nn.silu(z.astype(f32).reshape(bsz, seq, H, D))
    return o.reshape(bsz, seq, H * D)


def mlstm(q, k, v, o_pre, i_pre, f_pre, i_bias, f_bias, norm_w):
    bsz, seq, _ = q.shape
    H, D, L = MLSTM_HEADS, HEAD_DIM, MLSTM_CHUNK
    f32 = jnp.float32
    q = to_chunked_heads(q.astype(f32), H, L)
    k = to_chunked_heads(k.astype(f32), H, L) * (D ** -0.5)
    v = to_chunked_heads(v.astype(f32), H, L)
    ig = to_chunked_gates(softcap(i_pre.astype(f32) + i_bias.astype(f32), GATE_SOFTCAP), L)
    lf = to_chunked_gates(jax.nn.log_sigmoid(softcap(f_pre.astype(f32) + f_bias.astype(f32), GATE_SOFTCAP)), L)
    b = jnp.cumsum(lf, axis=-1)
    causal = jnp.tril(jnp.ones((L, L), dtype=bool))
    dmat = jnp.where(causal, b[..., :, None] - b[..., None, :] + ig[..., None, :], -jnp.inf)
    m_intra = jnp.max(dmat, axis=-1)
    qk = jnp.einsum('bhnld,bhnsd->bhnls', q, k) * jnp.exp(dmat - m_intra[..., None])
    num_intra = jnp.einsum('bhnls,bhnse->bhnle', qk, v)
    den_intra = jnp.sum(qk, axis=-1)
    w_end = b[..., -1:] - b + ig
    m_chunk = jnp.max(w_end, axis=-1)
    e_end = jnp.exp(w_end - m_chunk[..., None])
    c_chunk = jnp.einsum('bhnl,bhnld,bhnle->bhnde', e_end, k, v)
    n_chunk = jnp.einsum('bhnl,bhnld->bhnd', e_end, k)
    b_last = b[..., -1]

    def step(carry, xs):
        c, n, m = carry
        q_c, b_c, mi_c, num_c, den_c, bl_c, mc_c, cc_c, nc_c = xs
        a = b_c + m[..., None]
        m_t = jnp.maximum(a, mi_c)
        s_inter = jnp.exp(a - m_t)
        s_intra = jnp.exp(mi_c - m_t)
        num = s_inter[..., None] * jnp.einsum('bhld,bhde->bhle', q_c, c) + s_intra[..., None] * num_c
        den = s_inter * jnp.einsum('bhld,bhd->bhl', q_c, n) + s_intra * den_c
        h = num / jnp.maximum(jnp.abs(den), jnp.exp(-m_t))[..., None]
        m_new = jnp.maximum(bl_c + m, mc_c)
        s_old = jnp.exp(bl_c + m - m_new)
        s_new = jnp.exp(mc_c - m_new)
        c = s_old[..., None, None] * c + s_new[..., None, None] * cc_c
        n = s_old[..., None] * n + s_new[..., None] * nc_c
        return (c, n, m_new), h

    xs = tuple(jnp.moveaxis(t, 2, 0) for t in (q, b, m_intra, num_intra, den_intra, b_last, m_chunk, c_chunk, n_chunk))
    init = (jnp.zeros((bsz, H, D, D), f32), jnp.zeros((bsz, H, D), f32), jnp.zeros((bsz, H), f32))
    _, h = lax.scan(step, init, xs)
    h = rms_norm(from_scan_heads(h), norm_w.reshape(H, D))
    h = h * jax.nn.sigmoid(o_pre.astype(f32).reshape(bsz, seq, H, D))
    return h.reshape(bsz, seq, H * D)


def sliding_window_attention(q, k, v, sinks, cos, sin):
    bsz, seq, _ = q.shape
    Hkv, G, D, T = SWA_KV_HEADS, SWA_GROUP, HEAD_DIM, SWA_BLOCK
    NB = seq // T
    q = apply_partial_rope(q.reshape(bsz, seq, Hkv, G, D), cos[:, :, None, None, :], sin[:, :, None, None, :])
    k = apply_partial_rope(k.reshape(bsz, seq, Hkv, D), cos[:, :, None, :], sin[:, :, None, :])
    v = v.reshape(bsz, seq, Hkv, D)
    qb = q.reshape(bsz, NB, T, Hkv, G, D)

    def band(t):
        tb = t.reshape(bsz, NB, T, Hkv, D)
        prev = jnp.pad(tb, ((0, 0), (1, 0), (0, 0), (0, 0), (0, 0)))[:, :-1]
        return jnp.concatenate([prev, tb], axis=2)

    kw, vw = band(k), band(v)
    s = jnp.einsum('bnqhgd,bnkhd->bnhgqk', qb, kw).astype(jnp.float32) * (D ** -0.5)
    qi = jnp.arange(T)[:, None] + T
    ki = jnp.arange(2 * T)[None, :]
    in_window = (ki <= qi) & (ki > qi - SWA_WINDOW)
    has_prev = (jnp.arange(NB) > 0)[:, None, None] | (ki >= T)[None]
    mask = in_window[None] & has_prev
    s = jnp.where(mask[None, :, None, None], s, -jnp.inf)
    sink = jnp.broadcast_to(sinks.astype(jnp.float32).reshape(1, 1, Hkv, G, 1, 1), s.shape[:-1] + (1,))
    probs = jax.nn.softmax(jnp.concatenate([s, sink], axis=-1), axis=-1)[..., :-1]
    out = jnp.einsum('bnhgqk,bnkhd->bnqhgd', probs.astype(v.dtype), vw)
    return out.reshape(bsz, seq, Hkv * G * D)


def setup_inputs(seed: int = 0) -> dict:
    key = jax.random.key(seed)
    ks = jax.random.split(key, 24)
    f32 = jnp.float32

    def nrm(k_, shape, scale):
        return jax.random.normal(k_, shape, f32) * scale

    def gain(k_, shape):
        return 1.0 + 0.02 * jax.random.normal(k_, shape, f32)

    x = jax.random.normal(ks[0], (BATCH, SEQ, D_MODEL), f32)
    p = jax.random.normal(ks[1], (DEPTH, BATCH, SEQ, PLE_DIM), f32)
    offsets = jax.random.randint(ks[2], (BATCH, 1), 0, 1024, dtype=jnp.int32)
    positions = offsets + jnp.arange(SEQ, dtype=jnp.int32)[None, :]
    w_in = nrm(ks[3], (DEPTH, D_MODEL, IN_COLS), D_MODEL ** -0.5)
    conv_w = nrm(ks[4], (DEPTH, CONV_WIDTH, 3 * GDN_WIDTH), CONV_WIDTH ** -0.5)
    gdn_a_log = jnp.log(jax.random.uniform(ks[5], (DEPTH, GDN_HEADS), f32, 1.0, 16.0))
    dt = jnp.exp(jax.random.uniform(ks[6], (DEPTH, GDN_HEADS), f32, np.log(1e-3), np.log(1e-1)))
    gdn_dt_bias = dt + jnp.log(-jnp.expm1(-dt))
    gdn_norm = gain(ks[7], (DEPTH, HEAD_DIM))
    mlstm_i_bias = nrm(ks[8], (DEPTH, MLSTM_HEADS), 0.1)
    mlstm_f_bias = jax.random.uniform(ks[9], (DEPTH, MLSTM_HEADS), f32, 3.0, 6.0)
    mlstm_norm = gain(ks[10], (DEPTH, MLSTM_WIDTH))
    attn_sinks = nrm(ks[11], (DEPTH, SWA_Q_HEADS), 0.5)
    w_out = nrm(ks[12], (DEPTH, MIX_WIDTH, D_MODEL), MIX_WIDTH ** -0.5)
    norm_mix = gain(ks[13], (DEPTH, D_MODEL))
    norm_mlp = gain(ks[14], (DEPTH, D_MODEL))
    w_up = nrm(ks[15], (DEPTH, D_MODEL, D_FF), D_MODEL ** -0.5)
    w_down = nrm(ks[16], (DEPTH, D_FF, D_MODEL), D_FF ** -0.5)
    norm_ple = gain(ks[17], (DEPTH, D_MODEL))
    w_ple_gate = nrm(ks[18], (DEPTH, D_MODEL, D_MODEL), D_MODEL ** -0.5)
    w_ple_proj = nrm(ks[19], (DEPTH, PLE_DIM, D_MODEL), PLE_DIM ** -0.5)
    norm_final = gain(ks[20], (D_MODEL,))
    return {'x': x, 'p': p, 'positions': positions, 'w_in': w_in, 'conv_w': conv_w,
            'gdn_a_log': gdn_a_log, 'gdn_dt_bias': gdn_dt_bias, 'gdn_norm': gdn_norm,
            'mlstm_i_bias': mlstm_i_bias, 'mlstm_f_bias': mlstm_f_bias, 'mlstm_norm': mlstm_norm,
            'attn_sinks': attn_sinks, 'w_out': w_out, 'norm_mix': norm_mix, 'norm_mlp': norm_mlp,
            'w_up': w_up, 'w_down': w_down, 'norm_ple': norm_ple, 'w_ple_gate': w_ple_gate,
            'w_ple_proj': w_ple_proj, 'norm_final': norm_final}


def reference(x, p, positions, w_in, conv_w, gdn_a_log, gdn_dt_bias, gdn_norm,
              mlstm_i_bias, mlstm_f_bias, mlstm_norm, attn_sinks, w_out, norm_mix, norm_mlp,
              w_up, w_down, norm_ple, w_ple_gate, w_ple_proj, norm_final):
    cos, sin = rope_tables(positions)
    for i in range(DEPTH):
        h = rms_norm(x, norm_mix[i])
        proj = jnp.einsum('bsd,dc->bsc', h, w_in[i])
        (gq, gk, gv, gz, gb, ga, mq, mk, mv, mo, mi, mf, sq, sk, sv) = split_cols(proj, IN_SPLITS)
        y_a = gated_delta_net(gq, gk, gv, gz, gb, ga, conv_w[i], gdn_a_log[i], gdn_dt_bias[i], gdn_norm[i])
        y_b = mlstm(mq, mk, mv, mo, mi, mf, mlstm_i_bias[i], mlstm_f_bias[i], mlstm_norm[i])
        y_c = sliding_window_attention(sq, sk, sv, attn_sinks[i], cos, sin)
        y = jnp.concatenate([y_a.astype(x.dtype), y_b.astype(x.dtype), y_c.astype(x.dtype)], axis=-1)
        x = x + jnp.einsum('bsc,cd->bsd', y, w_out[i])
        u = jnp.einsum('bsd,df->bsf', rms_norm(x, norm_mlp[i]), w_up[i])
        x = x + jnp.einsum('bsf,fd->bsd', jnp.square(jax.nn.relu(u)), w_down[i])
        gate = jax.nn.sigmoid(jnp.einsum('bsd,de->bse', rms_norm(x, norm_ple[i]), w_ple_gate[i]))
        x = x + gate * jnp.einsum('bsk,kd->bsd', p[i], w_ple_proj[i])
    return rms_norm(x, norm_final)
```

```python
import functools

import jax
import jax.numpy as jnp
from jax import lax
from jax.experimental import pallas as pl
from jax.experimental.pallas import tpu as pltpu

F32 = jnp.float32
BF16 = jnp.bfloat16

D_MODEL = 1024
DEPTH = 4
HEAD_DIM = 64
PLE_DIM = 256
D_FF = 4 * D_MODEL
EPS = 1e-6
HEADS = 4
WIDTH = HEADS * HEAD_DIM
CHUNK = 64
CONV_WIDTH = 4
GATE_SOFTCAP = 15.0
SWA_WIDTH = 512
SWA_KV_WIDTH = 128
SWA_BLOCK = 128
ROPE_DIM = 16
ROPE_THETA = 500000.0
GATE_LANES = 128
IN_COLS_PACKED = 2944
OFF_GQKV, OFF_GZ, OFF_M, OFF_SQ, OFF_SKV, OFF_GATES = 0, 768, 1024, 2048, 2560, 2816

VMEM_LIMIT = 56 * 1024 * 1024


def _dot(a, b):
    return jnp.dot(a, b, preferred_element_type=F32)


def _dot_nt(a, b):
    return lax.dot_general(a, b, (((1,), (1,)), ((), ())), preferred_element_type=F32)


def _dot_tn(a, b):
    return lax.dot_general(a, b, (((0,), (0,)), ((), ())), preferred_element_type=F32)


def _split2(x):
    hi = x.astype(BF16)
    lo = (x - hi.astype(F32)).astype(BF16)
    return hi, lo


def _split3(x):
    hi = x.astype(BF16)
    r = x - hi.astype(F32)
    mid = r.astype(BF16)
    lo = (r - mid.astype(F32)).astype(BF16)
    return hi, mid, lo


def _dot_sel(x, sel):
    hi, mid, lo = _split3(x)
    return _dot(hi, sel) + _dot(mid, sel) + _dot(lo, sel)


def _sel_dot(sel, x):
    hi, mid, lo = _split3(x)
    return _dot(sel, hi) + _dot(sel, mid) + _dot(sel, lo)


def _seg_sum(x, block_ones):
    hi, lo = _split2(x)
    return _dot(hi, block_ones) + _dot(lo, block_ones)


def _sigmoid(x):
    return 1.0 / (1.0 + jnp.exp(-x))


def _softplus(x):
    return jnp.maximum(x, 0.0) + jnp.log1p(jnp.exp(-jnp.abs(x)))


def _rms(x, g):
    return x * lax.rsqrt(jnp.mean(x * x, axis=-1, keepdims=True) + EPS) * g


def _iota2(shape, dim):
    return lax.broadcasted_iota(jnp.int32, shape, dim)


def _head_mask():
    return (_iota2((WIDTH, WIDTH), 0) >> 6) == (_iota2((WIDTH, WIDTH), 1) >> 6)


def _block_diag(x, head_mask):
    return jnp.where(head_mask, jnp.concatenate([x, x, x, x], axis=0), jnp.zeros((), x.dtype))


def _chunk_tril(t):
    r = _iota2((t, t), 0)
    c = _iota2((t, t), 1)
    return ((c <= r) & ((r >> 6) == (c >> 6))).astype(BF16)


def _expander(first_col):
    r = _iota2((GATE_LANES, WIDTH), 0)
    c = _iota2((GATE_LANES, WIDTH), 1)
    return (r == first_col + (c >> 6)).astype(BF16)


def _rope(x, cos_t, sin_a, sin_b, reps):
    n = x.shape[-1]
    if reps > 1:
        cos_t = jnp.concatenate([cos_t] * reps, axis=-1)
        sin_a = jnp.concatenate([sin_a] * reps, axis=-1)
        sin_b = jnp.concatenate([sin_b] * reps, axis=-1)
    half = ROPE_DIM // 2
    x_up = pltpu.roll(x, n - half, axis=1)
    x_dn = pltpu.roll(x, half, axis=1)
    return x * cos_t + x_up * sin_a + x_dn * sin_b


def _proj_kernel(x_ref, g_ref, w_ref, cos_ref, sa_ref, sb_ref,
                 gqkv_ref, gz_ref, m_ref, sq_ref, skv_ref, gates_ref):
    hb = _rms(x_ref[...], g_ref[...]).astype(BF16)
    gqkv_ref[...] = _dot(hb, w_ref[:, OFF_GQKV:OFF_GZ])
    gz_ref[...] = _dot(hb, w_ref[:, OFF_GZ:OFF_M])
    m_ref[...] = _dot(hb, w_ref[:, OFF_M:OFF_SQ])
    gates_ref[...] = _dot(hb, w_ref[:, OFF_GATES:IN_COLS_PACKED])
    cos_t, sin_a, sin_b = cos_ref[...], sa_ref[...], sb_ref[...]
    q = _dot(hb, w_ref[:, OFF_SQ:OFF_SKV])
    sq_ref[...] = _rope(q, cos_t, sin_a, sin_b, SWA_WIDTH // 128) * (HEAD_DIM ** -0.5)
    kv = _dot(hb, w_ref[:, OFF_SKV:OFF_GATES])
    skv_ref[:, 0:SWA_KV_WIDTH] = _rope(kv[:, 0:SWA_KV_WIDTH], cos_t, sin_a, sin_b, 1)
    skv_ref[:, SWA_KV_WIDTH:] = kv[:, SWA_KV_WIDTH:]


def _proj(x, norm_w, w_packed, cos_t, sin_a, sin_b, layer, tm):
    m = x.shape[0]
    row = lambda n: pl.BlockSpec((tm, n), lambda i: (i, 0))
    widths = (768, 256, 1024, SWA_WIDTH, 2 * SWA_KV_WIDTH, GATE_LANES)
    return pl.pallas_call(
        _proj_kernel,
        grid=(m // tm,),
        in_specs=[
            row(D_MODEL),
            pl.BlockSpec((None, 1, D_MODEL), lambda i: (layer, 0, 0)),
            pl.BlockSpec((None, D_MODEL, IN_COLS_PACKED), lambda i: (layer, 0, 0),
                         pipeline_mode=pl.Buffered(1)),
            row(128), row(128), row(128),
        ],
        out_specs=[row(n) for n in widths],
        out_shape=[jax.ShapeDtypeStruct((m, n), F32) for n in widths],
        compiler_params=pltpu.CompilerParams(
            dimension_semantics=("arbitrary",), vmem_limit_bytes=VMEM_LIMIT),
    )(x, norm_w, w_packed, cos_t, sin_a, sin_b)


def _gdn_kernel(qkv_ref, z_ref, gt_ref, cw_ref, par_ref, nw_ref, o_ref,
                state_ref, tail_ref, buf_ref, q_s, k_s, v_s, beta_s, gc_s, *, t):
    c = CHUNK

    @pl.when(pl.program_id(1) == 0)
    def _():
        state_ref[...] = jnp.zeros_like(state_ref)
        tail_ref[...] = jnp.zeros_like(tail_ref)

    head_mask = _head_mask()
    block_ones = head_mask.astype(BF16)

    buf_ref[0:8, :] = tail_ref[...]
    buf_ref[8:8 + t, :] = qkv_ref[...]
    tail_ref[...] = qkv_ref[t - 8:t, :]
    cw = cw_ref[...]
    conv = buf_ref[5:5 + t, :] * cw[0:1, :]
    for j in range(1, CONV_WIDTH):
        conv = conv + buf_ref[5 + j:5 + j + t, :] * cw[j:j + 1, :]
    qkv = conv * _sigmoid(conv)
    q = qkv[:, 0:WIDTH]
    k = qkv[:, WIDTH:2 * WIDTH]
    q_s[...] = q * lax.rsqrt(_seg_sum(q * q, block_ones) + EPS) * (HEAD_DIM ** -0.5)
    k_s[...] = k * lax.rsqrt(_seg_sum(k * k, block_ones) + EPS)
    v_s[...] = qkv[:, 2 * WIDTH:3 * WIDTH]

    gates = gt_ref[...]
    par = par_ref[...]
    beta = _sigmoid(gates)
    g = -jnp.exp(par[0:1, :]) * _softplus(gates + par[1:2, :])
    gc = _sel_dot(_chunk_tril(t), g)
    beta_s[...] = _dot_sel(beta, _expander(0))
    gc_s[...] = _dot_sel(gc, _expander(HEADS))

    row = _iota2((c, WIDTH), 0)
    col = _iota2((c, WIDTH), 1) & (c - 1)
    causal = col <= row
    strict = col < row
    eye = col == row
    nw = nw_ref[...]

    def bd(x):
        return _block_diag(x, head_mask)

    def mm_split(a, b):
        a_hi, a_lo = _split2(a)
        b_hi, b_lo = _split2(b)
        bd_hi = bd(b_hi)
        return _dot(a_hi, bd_hi) + _dot(a_lo, bd_hi) + _dot(a_hi, bd(b_lo))

    def chunk_step(i, carry):
        r = pl.ds(pl.multiple_of(i * c, c), c)
        q = q_s[r, :]
        k = k_s[r, :]
        v = v_s[r, :]
        beta = beta_s[r, :]
        gc = gc_s[r, :]
        g_last = gc_s[pl.ds(i * c + c - 1, 1), :]
        gc_row = jnp.sum(jnp.where(eye, gc, 0.0), axis=0, keepdims=True)
        decay = jnp.exp(jnp.where(causal, gc - gc_row, -jnp.inf))
        egc = jnp.exp(gc)
        kb = k * beta
        vb = v * beta
        kbe = kb * egc
        qd = q * egc
        kd = k * jnp.exp(g_last - gc)
        kkqk = _dot_nt(jnp.concatenate([kb, q], axis=0).astype(BF16), bd(k.astype(BF16)))
        nm = jnp.where(strict, kkqk[0:c] * decay, 0.0)
        qkd = kkqk[c:2 * c] * decay
        x = eye.astype(F32) - jnp.where((row >> 1) == (col >> 1), nm, 0.0)
        for lb in range(1, 6):
            off = ((row >> (lb + 1)) == (col >> (lb + 1))) & ((row >> lb) != (col >> lb))
            t1 = mm_split(jnp.where(off, nm, 0.0), x)
            x = x - mm_split(x, t1)
        u = mm_split(x, vb)
        w = mm_split(x, kbe)
        s = state_ref[...]
        s_b = s.astype(BF16)
        v_new = u - _dot(w.astype(BF16), s_b)
        o = _dot(qd.astype(BF16), s_b) + _dot(qkd.astype(BF16), bd(v_new.astype(BF16)))
        upd = _dot_tn(kd.astype(BF16), v_new.astype(BF16))
        state_ref[...] = s * jnp.exp(g_last) + jnp.where(head_mask, upd, 0.0)
        z = z_ref[r, :]
        ms = _seg_sum(o * o, block_ones) * (1.0 / HEAD_DIM)
        o_ref[r, :] = o * lax.rsqrt(ms + EPS) * nw * (z * _sigmoid(z))
        return carry

    lax.fori_loop(0, t // c, chunk_step, 0)


def _gdn(gqkv, gz, gates, conv_w, par, norm_w, layer, bsz, seq, t):
    m = bsz * seq
    nt = seq // t
    row = lambda n: pl.BlockSpec((t, n), lambda b, j: (b * nt + j, 0))
    lay = lambda a, b2: pl.BlockSpec((None, a, b2), lambda b, j: (layer, 0, 0))
    return pl.pallas_call(
        functools.partial(_gdn_kernel, t=t),
        grid=(bsz, nt),
        in_specs=[row(3 * WIDTH), row(WIDTH), row(GATE_LANES),
                  lay(CONV_WIDTH, 3 * WIDTH), lay(2, GATE_LANES), lay(1, WIDTH)],
        out_specs=row(WIDTH),
        out_shape=jax.ShapeDtypeStruct((m, WIDTH), F32),
        scratch_shapes=[
            pltpu.VMEM((WIDTH, WIDTH), F32),
            pltpu.VMEM((8, 3 * WIDTH), F32),
            pltpu.VMEM((t + 8, 3 * WIDTH), F32),
            pltpu.VMEM((t, WIDTH), F32), pltpu.VMEM((t, WIDTH), F32), pltpu.VMEM((t, WIDTH), F32),
            pltpu.VMEM((t, WIDTH), F32), pltpu.VMEM((t, WIDTH), F32),
        ],
        compiler_params=pltpu.CompilerParams(
            dimension_semantics=("arbitrary", "arbitrary"), vmem_limit_bytes=VMEM_LIMIT),
    )(gqkv, gz, gates, conv_w, par, norm_w)


def _mlstm_kernel(m_ref, gt_ref, par_ref, nw_ref, o_ref,
                  c_ref, n_ref, mx_ref, ig_s, b_s, cm_s, *, t):
    c = CHUNK

    @pl.when(pl.program_id(1) == 0)
    def _():
        c_ref[...] = jnp.zeros_like(c_ref)
        n_ref[...] = jnp.zeros_like(n_ref)
        mx_ref[...] = jnp.zeros_like(mx_ref)

    head_mask = _head_mask()
    block_ones = head_mask.astype(BF16)

    pre = gt_ref[...] + par_ref[...]
    capped = GATE_SOFTCAP * jnp.tanh(pre * (1.0 / GATE_SOFTCAP))
    log_f = -_softplus(-capped)
    b_all = _sel_dot(_chunk_tril(t), log_f)
    ig = _dot_sel(capped, _expander(2 * HEADS))
    b = _dot_sel(b_all, _expander(3 * HEADS))
    pos = _iota2((t, WIDTH), 0) & (c - 1)
    cm = ig - b
    for sh in (1, 2, 4, 8, 16, 32):
        cm = jnp.where(pos >= sh, jnp.maximum(cm, pltpu.roll(cm, sh, axis=0)), cm)
    ig_s[...] = ig
    b_s[...] = b
    cm_s[...] = cm

    row = _iota2((c, WIDTH), 0)
    col = _iota2((c, WIDTH), 1) & (c - 1)
    causal = col <= row
    eye = col == row
    nw = nw_ref[...]
    ones = jnp.ones((c, WIDTH), BF16)

    def bd(x):
        return _block_diag(x, head_mask)

    def chunk_step(i, carry):
        r = pl.ds(pl.multiple_of(i * c, c), c)
        last = pl.ds(i * c + c - 1, 1)
        q = m_ref[r, 0:WIDTH]
        k = m_ref[r, WIDTH:2 * WIDTH] * (HEAD_DIM ** -0.5)
        v = m_ref[r, 2 * WIDTH:3 * WIDTH]
        o_pre = m_ref[r, 3 * WIDTH:4 * WIDTH]
        ig = ig_s[r, :]
        b = b_s[r, :]
        m_intra = b + cm_s[r, :]
        b_last = b_s[last, :]
        m_chunk = b_last + cm_s[last, :]
        gate_row = jnp.sum(jnp.where(eye, ig - b, 0.0), axis=0, keepdims=True)
        dmat = jnp.where(causal, b + gate_row, -jnp.inf)
        qb = q.astype(BF16)
        vb = v.astype(BF16)
        qk = _dot_nt(qb, bd(k.astype(BF16))) * jnp.exp(dmat - m_intra)
        num_intra = _dot(qk.astype(BF16), bd(vb))
        den_intra = _seg_sum(qk, block_ones)
        m_prev = mx_ref[...]
        a = b + m_prev
        m_t = jnp.maximum(a, m_intra)
        s_inter = jnp.exp(a - m_t)
        s_intra = jnp.exp(m_intra - m_t)
        c_st = c_ref[...]
        n_st = n_ref[...]
        num = s_inter * _dot(qb, c_st.astype(BF16)) + s_intra * num_intra
        den = s_inter * _dot(qb, n_st.astype(BF16)) + s_intra * den_intra
        h = num / jnp.maximum(jnp.abs(den), jnp.exp(-m_t))
        e_end = jnp.exp(b_last - b + ig - m_chunk)
        ke = (k * e_end).astype(BF16)
        m_new = jnp.maximum(b_last + m_prev, m_chunk)
        s_old = jnp.exp(b_last + m_prev - m_new)
        s_new = jnp.exp(m_chunk - m_new)
        c_ref[...] = s_old * c_st + s_new * jnp.where(head_mask, _dot_tn(ke, vb), 0.0)
        n_ref[...] = s_old * n_st + s_new * jnp.where(head_mask, _dot_tn(ke, ones), 0.0)
        mx_ref[...] = m_new
        ms = _seg_sum(h * h, block_ones) * (1.0 / HEAD_DIM)
        o_ref[r, :] = h * lax.rsqrt(ms + EPS) * nw * _sigmoid(o_pre)
        return carry

    lax.fori_loop(0, t // c, chunk_step, 0)


def _mlstm(mqkvo, gates, par, norm_w, layer, bsz, seq, t):
    m = bsz * seq
    nt = seq // t
    row = lambda n: pl.BlockSpec((t, n), lambda b, j: (b * nt + j, 0))
    lay = lambda a, b2: pl.BlockSpec((None, a, b2), lambda b, j: (layer, 0, 0))
    return pl.pallas_call(
        functools.partial(_mlstm_kernel, t=t),
        grid=(bsz, nt),
        in_specs=[row(4 * WIDTH), row(GATE_LANES), lay(1, GATE_LANES), lay(1, WIDTH)],
        out_specs=row(WIDTH),
        out_shape=jax.ShapeDtypeStruct((m, WIDTH), F32),
        scratch_shapes=[
            pltpu.VMEM((WIDTH, WIDTH), F32),
            pltpu.VMEM((WIDTH, WIDTH), F32),
            pltpu.VMEM((1, WIDTH), F32),
            pltpu.VMEM((t, WIDTH), F32), pltpu.VMEM((t, WIDTH), F32), pltpu.VMEM((t, WIDTH), F32),
        ],
        compiler_params=pltpu.CompilerParams(
            dimension_semantics=("arbitrary", "arbitrary"), vmem_limit_bytes=VMEM_LIMIT),
    )(mqkvo, gates, par, norm_w)


def _swa_kernel(q_ref, kvc_ref, kvp_ref, sink_ref, o_ref):
    tb = SWA_BLOCK
    kvw = SWA_KV_WIDTH
    first = pl.program_id(1) == 0
    kw = jnp.concatenate([kvp_ref[:, 0:kvw], kvc_ref[:, 0:kvw]], axis=0)
    vw = jnp.concatenate([kvp_ref[:, kvw:2 * kvw], kvc_ref[:, kvw:2 * kvw]], axis=0)
    kr = pltpu.roll(kw, HEAD_DIM, axis=1)
    vr = pltpu.roll(vw, HEAD_DIM, axis=1)
    lo = _iota2((2 * tb, kvw), 1) < HEAD_DIM

    def place(x_lo, x_hi):
        return jnp.concatenate([jnp.where(lo, x_lo, 0.0), jnp.where(lo, 0.0, x_hi)], axis=0).astype(BF16)

    kk = (place(kw, kr), place(kr, kw))
    vv = (place(vw, vr), place(vr, vw))
    qi = _iota2((tb, 2 * tb), 0)
    ki = _iota2((tb, 2 * tb), 1)
    mask = (ki > qi) & (ki <= qi + tb) & (ki >= jnp.where(first, tb, 0))
    left = _iota2((tb, 128), 1) < HEAD_DIM
    for j in range(SWA_WIDTH // 128):
        g = j // 2
        s = _dot_nt(q_ref[:, 128 * j:128 * (j + 1)].astype(BF16), kk[g])
        ps, inv = [], []
        for e in range(2):
            sink = sink_ref[j:j + 1, e * 2 * tb:e * 2 * tb + 1]
            se = jnp.where(mask, s[:, e * 2 * tb:(e + 1) * 2 * tb], -jnp.inf)
            mx = jnp.maximum(jnp.max(se, axis=-1, keepdims=True), sink)
            p = jnp.exp(se - mx)
            ps.append(p.astype(BF16))
            inv.append(1.0 / (jnp.sum(p, axis=-1, keepdims=True) + jnp.exp(sink - mx)))
        out = _dot(jnp.concatenate(ps, axis=-1), vv[g])
        o_ref[:, 128 * j:128 * (j + 1)] = out * jnp.where(left, inv[0], inv[1])


def _swa(sq, skv, sinks_e, layer, bsz, seq):
    m = bsz * seq
    nb = seq // SWA_BLOCK
    return pl.pallas_call(
        _swa_kernel,
        grid=(bsz, nb),
        in_specs=[
            pl.BlockSpec((SWA_BLOCK, SWA_WIDTH), lambda b, n: (b * nb + n, 0)),
            pl.BlockSpec((SWA_BLOCK, 2 * SWA_KV_WIDTH), lambda b, n: (b * nb + n, 0)),
            pl.BlockSpec((SWA_BLOCK, 2 * SWA_KV_WIDTH),
                         lambda b, n: (b * nb + jnp.maximum(n - 1, 0), 0)),
            pl.BlockSpec((None, SWA_WIDTH // 128, 4 * SWA_BLOCK), lambda b, n: (layer, 0, 0)),
        ],
        out_specs=pl.BlockSpec((SWA_BLOCK, SWA_WIDTH), lambda b, n: (b * nb + n, 0)),
        out_shape=jax.ShapeDtypeStruct((m, SWA_WIDTH), F32),
        compiler_params=pltpu.CompilerParams(
            dimension_semantics=("arbitrary", "arbitrary"), vmem_limit_bytes=VMEM_LIMIT),
    )(sq, skv, skv, sinks_e)


def _post_kernel(x_ref, ya_ref, yb_ref, yc_ref, p_ref, wo_ref, nmlp_ref, wup_ref, wdn_ref,
                 nple_ref, wg_ref, wp_ref, nfin_ref, o_ref, *, final, tf):
    x = x_ref[...]
    x = x + _dot(ya_ref[...].astype(BF16), wo_ref[0:WIDTH, :])
    x = x + _dot(yb_ref[...].astype(BF16), wo_ref[WIDTH:2 * WIDTH, :])
    x = x + _dot(yc_ref[...].astype(BF16), wo_ref[2 * WIDTH:, :])
    h = _rms(x, nmlp_ref[...]).astype(BF16)
    acc = jnp.zeros_like(x)
    for f in range(0, D_FF, tf):
        u = jnp.maximum(_dot(h, wup_ref[:, f:f + tf]), 0.0)
        acc = acc + _dot((u * u).astype(BF16), wdn_ref[f:f + tf, :])
    x = x + acc
    gate = _sigmoid(_dot(_rms(x, nple_ref[...]).astype(BF16), wg_ref[...]))
    x = x + gate * _dot(p_ref[...].astype(BF16), wp_ref[...])
    if final:
        x = _rms(x, nfin_ref[...])
    o_ref[...] = x


def _post(x, ya, yb, yc, p, w_out, norm_mlp, w_up, w_down, norm_ple, w_gate, w_proj, norm_final,
          layer, tm, final):
    m = x.shape[0]
    row = lambda n: pl.BlockSpec((tm, n), lambda i: (i, 0))
    lay = lambda a, b: pl.BlockSpec((None, a, b), lambda i: (layer, 0, 0),
                                    pipeline_mode=pl.Buffered(1))
    return pl.pallas_call(
        functools.partial(_post_kernel, final=final, tf=1024),
        grid=(m // tm,),
        in_specs=[
            row(D_MODEL), row(WIDTH), row(WIDTH), row(SWA_WIDTH),
            pl.BlockSpec((None, tm, PLE_DIM), lambda i: (layer, i, 0)),
            lay(D_MODEL, D_MODEL), lay(1, D_MODEL), lay(D_MODEL, D_FF), lay(D_FF, D_MODEL),
            lay(1, D_MODEL), lay(D_MODEL, D_MODEL), lay(PLE_DIM, D_MODEL),
            pl.BlockSpec((1, D_MODEL), lambda i: (0, 0)),
        ],
        out_specs=row(D_MODEL),
        out_shape=jax.ShapeDtypeStruct((m, D_MODEL), F32),
        compiler_params=pltpu.CompilerParams(
            dimension_semantics=("arbitrary",), vmem_limit_bytes=VMEM_LIMIT),
    )(x, ya, yb, yc, p, w_out, norm_mlp, w_up, w_down, norm_ple, w_gate, w_proj, norm_final)


def _pack_w_in(w_in):
    depth = w_in.shape[0]
    pad = jnp.zeros((depth, D_MODEL, GATE_LANES - 4 * HEADS), w_in.dtype)
    return jnp.concatenate(
        [w_in[..., 0:1024], w_in[..., 1032:2056], w_in[..., 2064:2832],
         w_in[..., 1024:1032], w_in[..., 2056:2064], pad], axis=-1).astype(BF16)


def _gate_row(depth, pieces):
    out = jnp.zeros((depth, 1, GATE_LANES), F32)
    for first_col, vals in pieces:
        out = out.at[:, 0, first_col:first_col + HEADS].set(vals.astype(F32))
    return out


def _rope_tables(positions):
    half = ROPE_DIM // 2
    inv_freq = ROPE_THETA ** (-jnp.arange(0, ROPE_DIM, 2, dtype=F32) / ROPE_DIM)
    ang = positions.astype(F32).reshape(-1)[:, None] * inv_freq
    cos, sin = jnp.cos(ang), jnp.sin(ang)
    n = ang.shape[0]
    rest = HEAD_DIM - ROPE_DIM
    cos_t = jnp.concatenate([cos, cos, jnp.ones((n, rest), F32)], axis=-1)
    sin_a = jnp.concatenate([-sin, jnp.zeros((n, half + rest), F32)], axis=-1)
    sin_b = jnp.concatenate([jnp.zeros((n, half), F32), sin, jnp.zeros((n, rest), F32)], axis=-1)
    return tuple(jnp.concatenate([a, a], axis=-1) for a in (cos_t, sin_a, sin_b))


def kernel(x, p, positions, w_in, conv_w, gdn_a_log, gdn_dt_bias, gdn_norm, mlstm_i_bias,
           mlstm_f_bias, mlstm_norm, attn_sinks, w_out, norm_mix, norm_mlp, w_up, w_down,
           norm_ple, w_ple_gate, w_ple_proj, norm_final):
    bsz, seq, d = x.shape
    depth = w_in.shape[0]
    m = bsz * seq
    tm = min(512, m)
    t_mix = min(512, seq)

    cos_t, sin_a, sin_b = _rope_tables(positions)
    w_packed = _pack_w_in(w_in)
    gdn_par = jnp.concatenate([_gate_row(depth, [(HEADS, gdn_a_log)]),
                               _gate_row(depth, [(HEADS, gdn_dt_bias)])], axis=1)
    mlstm_par = _gate_row(depth, [(2 * HEADS, mlstm_i_bias), (3 * HEADS, mlstm_f_bias)])
    gdn_nw = jnp.tile(gdn_norm.astype(F32), (1, HEADS))[:, None, :]
    mlstm_nw = mlstm_norm.astype(F32)[:, None, :]
    sinks_e = jnp.repeat(attn_sinks.astype(F32), 2 * SWA_BLOCK, axis=-1).reshape(
        depth, SWA_WIDTH // 128, 4 * SWA_BLOCK)
    row3 = lambda a: a.astype(F32)[:, None, :]
    wo_b, wup_b, wdn_b = w_out.astype(BF16), w_up.astype(BF16), w_down.astype(BF16)
    wg_b, wp_b = w_ple_gate.astype(BF16), w_ple_proj.astype(BF16)
    nmix, nmlp, nple = row3(norm_mix), row3(norm_mlp), row3(norm_ple)
    nfin = norm_final.astype(F32)[None, :]
    p2 = p.reshape(depth, m, PLE_DIM)

    xf = x.reshape(m, d)
    for i in range(depth):
        gqkv, gz, mqkvo, sq, skv, gates = _proj(xf, nmix, w_packed, cos_t, sin_a, sin_b, i, tm)
        ya = _gdn(gqkv, gz, gates, conv_w, gdn_par, gdn_nw, i, bsz, seq, t_mix)
        yb = _mlstm(mqkvo, gates, mlstm_par, mlstm_nw, i, bsz, seq, t_mix)
        yc = _swa(sq, skv, sinks_e, i, bsz, seq)
        xf = _post(xf, ya, yb, yc, p2, wo_b, nmlp, wup_b, wdn_b, nple, wg_b, wp_b, nfin,
                   i, tm, final=(i == depth - 1))
    return xf.reshape(bsz, seq, d)
```

```python
import functools

import jax
import jax.numpy as jnp
from jax import lax
from jax.experimental import pallas as pl
from jax.experimental.pallas import tpu as pltpu

F32 = jnp.float32
BF16 = jnp.bfloat16

D_MODEL = 1024
DEPTH = 4
HEAD_DIM = 64
PLE_DIM = 256
D_FF = 4 * D_MODEL
EPS = 1e-6
HEADS = 4
WIDTH = HEADS * HEAD_DIM
CHUNK = 64
CONV_WIDTH = 4
GATE_SOFTCAP = 15.0
SWA_WIDTH = 512
SWA_KV_WIDTH = 128
SWA_BLOCK = 128
ROPE_DIM = 16
ROPE_THETA = 500000.0
GATE_LANES = 128
IN_COLS_PACKED = 2944
OFF_GQKV, OFF_GZ, OFF_M, OFF_SQ, OFF_SKV, OFF_GATES = 0, 768, 1024, 2048, 2560, 2816

VMEM_LIMIT = 56 * 1024 * 1024


def _dot(a, b):
    return jnp.dot(a, b, preferred_element_type=F32)


def _dot_nt(a, b):
    return lax.dot_general(a, b, (((1,), (1,)), ((), ())), preferred_element_type=F32)


def _dot_tn(a, b):
    return lax.dot_general(a, b, (((0,), (0,)), ((), ())), preferred_element_type=F32)


def _split2(x):
    hi = x.astype(BF16)
    lo = (x - hi.astype(F32)).astype(BF16)
    return hi, lo


def _split3(x):
    hi = x.astype(BF16)
    r = x - hi.astype(F32)
    mid = r.astype(BF16)
    lo = (r - mid.astype(F32)).astype(BF16)
    return hi, mid, lo


def _dot_sel(x, sel):
    hi, mid, lo = _split3(x)
    return _dot(hi, sel) + _dot(mid, sel) + _dot(lo, sel)


def _sel_dot(sel, x):
    hi, mid, lo = _split3(x)
    return _dot(sel, hi) + _dot(sel, mid) + _dot(sel, lo)


def _seg_sum(x, block_ones):
    hi, lo = _split2(x)
    return _dot(hi, block_ones) + _dot(lo, block_ones)


def _sigmoid(x):
    return 1.0 / (1.0 + jnp.exp(-x))


def _softplus(x):
    return jnp.maximum(x, 0.0) + jnp.log1p(jnp.exp(-jnp.abs(x)))


def _rms(x, g):
    return x * lax.rsqrt(jnp.mean(x * x, axis=-1, keepdims=True) + EPS) * g


def _iota2(shape, dim):
    return lax.broadcasted_iota(jnp.int32, shape, dim)


def _head_mask():
    return (_iota2((WIDTH, WIDTH), 0) >> 6) == (_iota2((WIDTH, WIDTH), 1) >> 6)


def _block_diag(x, head_mask=None):
    rows = x.shape[0]
    low = _iota2((rows, 128), 1) < HEAD_DIM
    zero = jnp.zeros((rows, 128), x.dtype)
    x0, x1 = x[:, 0:128], x[:, 128:256]
    return jnp.concatenate([
        jnp.concatenate([jnp.where(low, x0, zero), zero], axis=1),
        jnp.concatenate([jnp.where(low, zero, x0), zero], axis=1),
        jnp.concatenate([zero, jnp.where(low, x1, zero)], axis=1),
        jnp.concatenate([zero, jnp.where(low, zero, x1)], axis=1)], axis=0)


def _chunk_tril(t):
    r = _iota2((t, t), 0)
    c = _iota2((t, t), 1)
    return ((c <= r) & ((r >> 6) == (c >> 6))).astype(BF16)


def _expander(first_col):
    r = _iota2((GATE_LANES, WIDTH), 0)
    c = _iota2((GATE_LANES, WIDTH), 1)
    return (r == first_col + (c >> 6)).astype(BF16)


def _rope(x, cos_t, sin_a, sin_b, reps):
    n = x.shape[-1]
    if reps > 1:
        cos_t = jnp.concatenate([cos_t] * reps, axis=-1)
        sin_a = jnp.concatenate([sin_a] * reps, axis=-1)
        sin_b = jnp.concatenate([sin_b] * reps, axis=-1)
    half = ROPE_DIM // 2
    x_up = pltpu.roll(x, n - half, axis=1)
    x_dn = pltpu.roll(x, half, axis=1)
    return x * cos_t + x_up * sin_a + x_dn * sin_b


def _proj_kernel(x_ref, g_ref, w_ref, cos_ref, sa_ref, sb_ref,
                 gqkv_ref, gz_ref, m_ref, sq_ref, skv_ref, gates_ref):
    hb = _rms(x_ref[...], g_ref[...]).astype(BF16)
    gqkv_ref[...] = _dot(hb, w_ref[:, OFF_GQKV:OFF_GZ])
    gz_ref[...] = _dot(hb, w_ref[:, OFF_GZ:OFF_M])
    m_ref[...] = _dot(hb, w_ref[:, OFF_M:OFF_SQ])
    gates_ref[...] = _dot(hb, w_ref[:, OFF_GATES:IN_COLS_PACKED])
    cos_t, sin_a, sin_b = cos_ref[...], sa_ref[...], sb_ref[...]
    q = _dot(hb, w_ref[:, OFF_SQ:OFF_SKV])
    sq_ref[...] = _rope(q, cos_t, sin_a, sin_b, SWA_WIDTH // 128) * (HEAD_DIM ** -0.5)
    kv = _dot(hb, w_ref[:, OFF_SKV:OFF_GATES])
    skv_ref[:, 0:SWA_KV_WIDTH] = _rope(kv[:, 0:SWA_KV_WIDTH], cos_t, sin_a, sin_b, 1)
    skv_ref[:, SWA_KV_WIDTH:] = kv[:, SWA_KV_WIDTH:]


def _proj(x, norm_w, w_packed, cos_t, sin_a, sin_b, layer, tm):
    m = x.shape[0]
    row = lambda n: pl.BlockSpec((tm, n), lambda i: (i, 0))
    widths = (768, 256, 1024, SWA_WIDTH, 2 * SWA_KV_WIDTH, GATE_LANES)
    return pl.pallas_call(
        _proj_kernel,
        grid=(m // tm,),
        in_specs=[
            row(D_MODEL),
            pl.BlockSpec((None, 1, D_MODEL), lambda i: (layer, 0, 0)),
            pl.BlockSpec((None, D_MODEL, IN_COLS_PACKED), lambda i: (layer, 0, 0),
                         pipeline_mode=pl.Buffered(1)),
            row(128), row(128), row(128),
        ],
        out_specs=[row(n) for n in widths],
        out_shape=[jax.ShapeDtypeStruct((m, n), F32) for n in widths],
        compiler_params=pltpu.CompilerParams(
            dimension_semantics=("arbitrary",), vmem_limit_bytes=VMEM_LIMIT),
    )(x, norm_w, w_packed, cos_t, sin_a, sin_b)


def _gdn_kernel(qkv_ref, z_ref, gt_ref, cw_ref, par_ref, nw_ref, tril_ref, exp_ref, ones_ref, o_ref,
                state_ref, tail_ref, buf_ref, q_s, k_s, v_s, beta_s, gc_s,
                u_s, w_s, qd_s, kd_s, qkd_s, *, t, group, nb):
    c = CHUNK

    @pl.when(pl.program_id(1) == 0)
    def _():
        state_ref[...] = jnp.zeros_like(state_ref)
        tail_ref[...] = jnp.zeros_like(tail_ref)

    head_mask = _head_mask()
    block_ones = ones_ref[...]

    def front(a, carry):
        buf_ref[0:8, :] = tail_ref[a]
        buf_ref[8:8 + t, :] = qkv_ref[a]
        tail_ref[a] = qkv_ref[a, t - 8:t, :]
        cw = cw_ref[...]
        conv = buf_ref[5:5 + t, :] * cw[0:1, :]
        for j in range(1, CONV_WIDTH):
            conv = conv + buf_ref[5 + j:5 + j + t, :] * cw[j:j + 1, :]
        qkv = conv * _sigmoid(conv)
        q = qkv[:, 0:WIDTH]
        k = qkv[:, WIDTH:2 * WIDTH]
        q_s[a] = q * lax.rsqrt(_dot((q * q).astype(BF16), block_ones) + EPS) * (HEAD_DIM ** -0.5)
        k_s[a] = k * lax.rsqrt(_dot((k * k).astype(BF16), block_ones) + EPS)
        v_s[a] = qkv[:, 2 * WIDTH:3 * WIDTH]
        gates = gt_ref[a]
        par = par_ref[...]
        beta = _sigmoid(gates)
        g = -jnp.exp(par[0:1, :]) * _softplus(gates + par[1:2, :])
        gc = _sel_dot(tril_ref[...], g)
        beta_s[a] = _dot_sel(beta, exp_ref[0])
        gc_s[a] = _dot_sel(gc, exp_ref[1])
        return carry

    lax.fori_loop(0, nb, front, 0)

    row = _iota2((c, WIDTH), 0)
    col = _iota2((c, WIDTH), 1) & (c - 1)
    causal = col <= row
    strict = col < row
    eye = col == row
    nw = nw_ref[...]

    def bd(x):
        return _block_diag(x, head_mask)

    def mm(a, b):
        return _dot(a.astype(BF16), bd(b.astype(BF16)))

    groups_per_seq = t // c // group

    def prepare_group(idx, carry):
        a = idx // groups_per_seq
        gi = idx - a * groups_per_seq
        q_a, k_a, v_a, beta_a, gc_a = q_s.at[a], k_s.at[a], v_s.at[a], beta_s.at[a], gc_s.at[a]
        u_a, w_a, qd_a, kd_a, qkd_a = u_s.at[a], w_s.at[a], qd_s.at[a], kd_s.at[a], qkd_s.at[a]
        n = range(group)
        rs = [pl.ds(pl.multiple_of((gi * group + j) * c, c), c) for j in n]
        q = [q_a[r, :] for r in rs]
        k = [k_a[r, :] for r in rs]
        gc = [gc_a[r, :] for r in rs]
        g_last = [gc_a[pl.ds((gi * group + j) * c + c - 1, 1), :] for j in n]
        kb = [k[j] * beta_a[rs[j], :] for j in n]
        kkqk = [_dot_nt(jnp.concatenate([kb[j], q[j]], axis=0).astype(BF16), bd(k[j].astype(BF16)))
                for j in n]
        gc_row = [jnp.sum(jnp.where(eye, gc[j], 0.0), axis=0, keepdims=True) for j in n]
        decay = [jnp.exp(jnp.where(causal, gc[j] - gc_row[j], -jnp.inf)) for j in n]
        nm = [jnp.where(strict, kkqk[j][0:c] * decay[j], 0.0) for j in n]
        for j in n:
            qkd_a[rs[j], :] = (kkqk[j][c:2 * c] * decay[j]).astype(BF16)
        x = [eye.astype(F32) - jnp.where((row >> 1) == (col >> 1), nm[j], 0.0) for j in n]
        for lb in range(1, 6):
            off = ((row >> (lb + 1)) == (col >> (lb + 1))) & ((row >> lb) != (col >> lb))
            t1 = [mm(jnp.where(off, nm[j], 0.0), x[j]) for j in n]
            t2 = [mm(x[j], t1[j]) for j in n]
            x = [x[j] - t2[j] for j in n]
        xb = [x[j].astype(BF16) for j in n]
        egc = [jnp.exp(gc[j]) for j in n]
        for j in n:
            u_a[rs[j], :] = _dot(xb[j], bd((v_a[rs[j], :] * beta_a[rs[j], :]).astype(BF16)))
        for j in n:
            w_a[rs[j], :] = _dot(xb[j], bd((kb[j] * egc[j]).astype(BF16))).astype(BF16)
        for j in n:
            qd_a[rs[j], :] = (q[j] * egc[j]).astype(BF16)
            kd_a[rs[j], :] = (k[j] * jnp.exp(g_last[j] - gc[j])).astype(BF16)
        return carry

    lax.fori_loop(0, nb * groups_per_seq, prepare_group, 0)

    def chunk_step(i, carry):
        n = range(nb)
        r = pl.ds(pl.multiple_of(i * c, c), c)
        g_last = [gc_s[a, pl.ds(i * c + c - 1, 1), :] for a in n]
        s = [state_ref[a] for a in n]
        ws = [_dot(jnp.concatenate([w_s[a, r, :], qd_s[a, r, :]], axis=0), s[a].astype(BF16))
              for a in n]
        v_new = [(u_s[a, r, :] - ws[a][0:c]).astype(BF16) for a in n]
        upd = [_dot_tn(kd_s[a, r, :], v_new[a]) for a in n]
        for a in n:
            state_ref[a] = s[a] * jnp.exp(g_last[a]) + jnp.where(head_mask, upd[a], 0.0)
        o = [ws[a][c:2 * c] + _dot(qkd_s[a, r, :], bd(v_new[a])) for a in n]
        ms = [_dot((o[a] * o[a]).astype(BF16), block_ones) * (1.0 / HEAD_DIM) for a in n]
        for a in n:
            z = z_ref[a, r, :]
            o_ref[a, r, :] = o[a] * lax.rsqrt(ms[a] + EPS) * nw * (z * _sigmoid(z))
        return carry

    lax.fori_loop(0, t // c, chunk_step, 0)


def _gdn(gqkv, gz, gates, conv_w, par, norm_w, layer, bsz, seq, t, nb):
    row = lambda n: pl.BlockSpec((nb, t, n), lambda b, j: (b, j, 0))
    lay = lambda a, b2: pl.BlockSpec((None, a, b2), lambda b, j: (layer, 0, 0))
    const = lambda shape: pl.BlockSpec(shape, lambda b, j: (0,) * len(shape))
    seq3 = lambda a: a.reshape(bsz, seq, a.shape[-1])
    f32_buf = pltpu.VMEM((nb, t, WIDTH), F32)
    bf16_buf = pltpu.VMEM((nb, t, WIDTH), BF16)
    out = pl.pallas_call(
        functools.partial(_gdn_kernel, t=t, group=min(8, t // CHUNK), nb=nb),
        grid=(bsz // nb, seq // t),
        in_specs=[row(3 * WIDTH), row(WIDTH), row(GATE_LANES),
                  lay(CONV_WIDTH, 3 * WIDTH), lay(2, GATE_LANES), lay(1, WIDTH),
                  const((t, t)), const((2, GATE_LANES, WIDTH)), const((WIDTH, WIDTH))],
        out_specs=row(WIDTH),
        out_shape=jax.ShapeDtypeStruct((bsz, seq, WIDTH), F32),
        scratch_shapes=[
            pltpu.VMEM((nb, WIDTH, WIDTH), F32),
            pltpu.VMEM((nb, 8, 3 * WIDTH), F32),
            pltpu.VMEM((t + 8, 3 * WIDTH), F32),
            f32_buf, f32_buf, f32_buf, f32_buf, f32_buf,
            f32_buf,
            bf16_buf, bf16_buf, bf16_buf, bf16_buf,
        ],
        compiler_params=pltpu.CompilerParams(
            dimension_semantics=("arbitrary", "arbitrary"), vmem_limit_bytes=VMEM_LIMIT),
    )(seq3(gqkv), seq3(gz), seq3(gates), conv_w, par, norm_w,
      _chunk_tril(t), jnp.stack([_expander(0), _expander(HEADS)]), _head_mask().astype(BF16))
    return out.reshape(bsz * seq, WIDTH)


def _mlstm_kernel(m_ref, gt_ref, par_ref, nw_ref, o_ref,
                  c_ref, n_ref, mx_ref, ig_s, b_s, cm_s, *, t):
    c = CHUNK

    @pl.when(pl.program_id(1) == 0)
    def _():
        c_ref[...] = jnp.zeros_like(c_ref)
        n_ref[...] = jnp.zeros_like(n_ref)
        mx_ref[...] = jnp.zeros_like(mx_ref)

    head_mask = _head_mask()
    block_ones = head_mask.astype(BF16)

    pre = gt_ref[...] + par_ref[...]
    capped = GATE_SOFTCAP * jnp.tanh(pre * (1.0 / GATE_SOFTCAP))
    log_f = -_softplus(-capped)
    b_all = _sel_dot(_chunk_tril(t), log_f)
    ig = _dot_sel(capped, _expander(2 * HEADS))
    b = _dot_sel(b_all, _expander(3 * HEADS))
    pos = _iota2((t, WIDTH), 0) & (c - 1)
    cm = ig - b
    for sh in (1, 2, 4, 8, 16, 32):
        cm = jnp.where(pos >= sh, jnp.maximum(cm, pltpu.roll(cm, sh, axis=0)), cm)
    ig_s[...] = ig
    b_s[...] = b
    cm_s[...] = cm

    row = _iota2((c, WIDTH), 0)
    col = _iota2((c, WIDTH), 1) & (c - 1)
    causal = col <= row
    eye = col == row
    nw = nw_ref[...]
    ones = jnp.ones((c, WIDTH), BF16)

    def bd(x):
        return _block_diag(x, head_mask)

    def chunk_step(i, carry):
        r = pl.ds(pl.multiple_of(i * c, c), c)
        last = pl.ds(i * c + c - 1, 1)
        q = m_ref[r, 0:WIDTH]
        k = m_ref[r, WIDTH:2 * WIDTH] * (HEAD_DIM ** -0.5)
        v = m_ref[r, 2 * WIDTH:3 * WIDTH]
        o_pre = m_ref[r, 3 * WIDTH:4 * WIDTH]
        ig = ig_s[r, :]
        b = b_s[r, :]
        m_intra = b + cm_s[r, :]
        b_last = b_s[last, :]
        m_chunk = b_last + cm_s[last, :]
        gate_row = jnp.sum(jnp.where(eye, ig - b, 0.0), axis=0, keepdims=True)
        dmat = jnp.where(causal, b + gate_row, -jnp.inf)
        qb = q.astype(BF16)
        vb = v.astype(BF16)
        qk = _dot_nt(qb, bd(k.astype(BF16))) * jnp.exp(dmat - m_intra)
        num_intra = _dot(qk.astype(BF16), bd(vb))
        den_intra = _seg_sum(qk, block_ones)
        m_prev = mx_ref[...]
        a = b + m_prev
        m_t = jnp.maximum(a, m_intra)
        s_inter = jnp.exp(a - m_t)
        s_intra = jnp.exp(m_intra - m_t)
        c_st = c_ref[...]
        n_st = n_ref[...]
        num = s_inter * _dot(qb, c_st.astype(BF16)) + s_intra * num_intra
        den = s_inter * _dot(qb, n_st.astype(BF16)) + s_intra * den_intra
        h = num / jnp.maximum(jnp.abs(den), jnp.exp(-m_t))
        e_end = jnp.exp(b_last - b + ig - m_chunk)
        ke = (k * e_end).astype(BF16)
        m_new = jnp.maximum(b_last + m_prev, m_chunk)
        s_old = jnp.exp(b_last + m_prev - m_new)
        s_new = jnp.exp(m_chunk - m_new)
        c_ref[...] = s_old * c_st + s_new * jnp.where(head_mask, _dot_tn(ke, vb), 0.0)
        n_ref[...] = s_old * n_st + s_new * jnp.where(head_mask, _dot_tn(ke, ones), 0.0)
        mx_ref[...] = m_new
        ms = _seg_sum(h * h, block_ones) * (1.0 / HEAD_DIM)
        o_ref[r, :] = h * lax.rsqrt(ms + EPS) * nw * _sigmoid(o_pre)
        return carry

    lax.fori_loop(0, t // c, chunk_step, 0)


def _mlstm(mqkvo, gates, par, norm_w, layer, bsz, seq, t):
    m = bsz * seq
    nt = seq // t
    row = lambda n: pl.BlockSpec((t, n), lambda b, j: (b * nt + j, 0))
    lay = lambda a, b2: pl.BlockSpec((None, a, b2), lambda b, j: (layer, 0, 0))
    return pl.pallas_call(
        functools.partial(_mlstm_kernel, t=t),
        grid=(bsz, nt),
        in_specs=[row(4 * WIDTH), row(GATE_LANES), lay(1, GATE_LANES), lay(1, WIDTH)],
        out_specs=row(WIDTH),
        out_shape=jax.ShapeDtypeStruct((m, WIDTH), F32),
        scratch_shapes=[
            pltpu.VMEM((WIDTH, WIDTH), F32),
            pltpu.VMEM((WIDTH, WIDTH), F32),
            pltpu.VMEM((1, WIDTH), F32),
            pltpu.VMEM((t, WIDTH), F32), pltpu.VMEM((t, WIDTH), F32), pltpu.VMEM((t, WIDTH), F32),
        ],
        compiler_params=pltpu.CompilerParams(
            dimension_semantics=("arbitrary", "arbitrary"), vmem_limit_bytes=VMEM_LIMIT),
    )(mqkvo, gates, par, norm_w)


def _swa_kernel(q_ref, kvc_ref, kvp_ref, sink_ref, o_ref):
    tb = SWA_BLOCK
    kvw = SWA_KV_WIDTH
    first = pl.program_id(1) == 0
    kw = jnp.concatenate([kvp_ref[:, 0:kvw], kvc_ref[:, 0:kvw]], axis=0)
    vw = jnp.concatenate([kvp_ref[:, kvw:2 * kvw], kvc_ref[:, kvw:2 * kvw]], axis=0)
    kr = pltpu.roll(kw, HEAD_DIM, axis=1)
    vr = pltpu.roll(vw, HEAD_DIM, axis=1)
    lo = _iota2((2 * tb, kvw), 1) < HEAD_DIM

    def place(x_lo, x_hi):
        return jnp.concatenate([jnp.where(lo, x_lo, 0.0), jnp.where(lo, 0.0, x_hi)], axis=0).astype(BF16)

    kk = (place(kw, kr), place(kr, kw))
    vv = (place(vw, vr), place(vr, vw))
    qi = _iota2((tb, 2 * tb), 0)
    ki = _iota2((tb, 2 * tb), 1)
    mask = (ki > qi) & (ki <= qi + tb) & (ki >= jnp.where(first, tb, 0))
    left = _iota2((tb, 128), 1) < HEAD_DIM
    for j in range(SWA_WIDTH // 128):
        g = j // 2
        s = _dot_nt(q_ref[:, 128 * j:128 * (j + 1)].astype(BF16), kk[g])
        ps, inv = [], []
        for e in range(2):
            sink = sink_ref[j:j + 1, e * 2 * tb:e * 2 * tb + 1]
            se = jnp.where(mask, s[:, e * 2 * tb:(e + 1) * 2 * tb], -jnp.inf)
            mx = jnp.maximum(jnp.max(se, axis=-1, keepdims=True), sink)
            p = jnp.exp(se - mx)
            ps.append(p.astype(BF16))
            inv.append(1.0 / (jnp.sum(p, axis=-1, keepdims=True) + jnp.exp(sink - mx)))
        out = _dot(jnp.concatenate(ps, axis=-1), vv[g])
        o_ref[:, 128 * j:128 * (j + 1)] = out * jnp.where(left, inv[0], inv[1])


def _swa(sq, skv, sinks_e, layer, bsz, seq):
    m = bsz * seq
    nb = seq // SWA_BLOCK
    return pl.pallas_call(
        _swa_kernel,
        grid=(bsz, nb),
        in_specs=[
            pl.BlockSpec((SWA_BLOCK, SWA_WIDTH), lambda b, n: (b * nb + n, 0)),
            pl.BlockSpec((SWA_BLOCK, 2 * SWA_KV_WIDTH), lambda b, n: (b * nb + n, 0)),
            pl.BlockSpec((SWA_BLOCK, 2 * SWA_KV_WIDTH),
                         lambda b, n: (b * nb + jnp.maximum(n - 1, 0), 0)),
            pl.BlockSpec((None, SWA_WIDTH // 128, 4 * SWA_BLOCK), lambda b, n: (layer, 0, 0)),
        ],
        out_specs=pl.BlockSpec((SWA_BLOCK, SWA_WIDTH), lambda b, n: (b * nb + n, 0)),
        out_shape=jax.ShapeDtypeStruct((m, SWA_WIDTH), F32),
        compiler_params=pltpu.CompilerParams(
            dimension_semantics=("arbitrary", "arbitrary"), vmem_limit_bytes=VMEM_LIMIT),
    )(sq, skv, skv, sinks_e)


def _post_kernel(x_ref, ya_ref, yb_ref, yc_ref, p_ref, wo_ref, nmlp_ref, wup_ref, wdn_ref,
                 nple_ref, wg_ref, wp_ref, nfin_ref, o_ref, *, final, tf):
    x = x_ref[...]
    x = x + _dot(ya_ref[...].astype(BF16), wo_ref[0:WIDTH, :])
    x = x + _dot(yb_ref[...].astype(BF16), wo_ref[WIDTH:2 * WIDTH, :])
    x = x + _dot(yc_ref[...].astype(BF16), wo_ref[2 * WIDTH:, :])
    h = _rms(x, nmlp_ref[...]).astype(BF16)
    acc = jnp.zeros_like(x)
    for f in range(0, D_FF, tf):
        u = jnp.maximum(_dot(h, wup_ref[:, f:f + tf]), 0.0)
        acc = acc + _dot((u * u).astype(BF16), wdn_ref[f:f + tf, :])
    x = x + acc
    gate = _sigmoid(_dot(_rms(x, nple_ref[...]).astype(BF16), wg_ref[...]))
    x = x + gate * _dot(p_ref[...].astype(BF16), wp_ref[...])
    if final:
        x = _rms(x, nfin_ref[...])
    o_ref[...] = x


def _post(x, ya, yb, yc, p, w_out, norm_mlp, w_up, w_down, norm_ple, w_gate, w_proj, norm_final,
          layer, tm, final):
    m = x.shape[0]
    row = lambda n: pl.BlockSpec((tm, n), lambda i: (i, 0))
    lay = lambda a, b: pl.BlockSpec((None, a, b), lambda i: (layer, 0, 0),
                                    pipeline_mode=pl.Buffered(1))
    return pl.pallas_call(
        functools.partial(_post_kernel, final=final, tf=1024),
        grid=(m // tm,),
        in_specs=[
            row(D_MODEL), row(WIDTH), row(WIDTH), row(SWA_WIDTH),
            pl.BlockSpec((None, tm, PLE_DIM), lambda i: (layer, i, 0)),
            lay(D_MODEL, D_MODEL), lay(1, D_MODEL), lay(D_MODEL, D_FF), lay(D_FF, D_MODEL),
            lay(1, D_MODEL), lay(D_MODEL, D_MODEL), lay(PLE_DIM, D_MODEL),
            pl.BlockSpec((1, D_MODEL), lambda i: (0, 0)),
        ],
        out_specs=row(D_MODEL),
        out_shape=jax.ShapeDtypeStruct((m, D_MODEL), F32),
        compiler_params=pltpu.CompilerParams(
            dimension_semantics=("arbitrary",), vmem_limit_bytes=VMEM_LIMIT),
    )(x, ya, yb, yc, p, w_out, norm_mlp, w_up, w_down, norm_ple, w_gate, w_proj, norm_final)


def _pack_w_in(w_in):
    depth = w_in.shape[0]
    pad = jnp.zeros((depth, D_MODEL, GATE_LANES - 4 * HEADS), w_in.dtype)
    return jnp.concatenate(
        [w_in[..., 0:1024], w_in[..., 1032:2056], w_in[..., 2064:2832],
         w_in[..., 1024:1032], w_in[..., 2056:2064], pad], axis=-1).astype(BF16)


def _gate_row(depth, pieces):
    out = jnp.zeros((depth, 1, GATE_LANES), F32)
    for first_col, vals in pieces:
        out = out.at[:, 0, first_col:first_col + HEADS].set(vals.astype(F32))
    return out


def _rope_tables(positions):
    half = ROPE_DIM // 2
    inv_freq = ROPE_THETA ** (-jnp.arange(0, ROPE_DIM, 2, dtype=F32) / ROPE_DIM)
    ang = positions.astype(F32).reshape(-1)[:, None] * inv_freq
    cos, sin = jnp.cos(ang), jnp.sin(ang)
    n = ang.shape[0]
    rest = HEAD_DIM - ROPE_DIM
    cos_t = jnp.concatenate([cos, cos, jnp.ones((n, rest), F32)], axis=-1)
    sin_a = jnp.concatenate([-sin, jnp.zeros((n, half + rest), F32)], axis=-1)
    sin_b = jnp.concatenate([jnp.zeros((n, half), F32), sin, jnp.zeros((n, rest), F32)], axis=-1)
    return tuple(jnp.concatenate([a, a], axis=-1) for a in (cos_t, sin_a, sin_b))


def kernel(x, p, positions, w_in, conv_w, gdn_a_log, gdn_dt_bias, gdn_norm, mlstm_i_bias,
           mlstm_f_bias, mlstm_norm, attn_sinks, w_out, norm_mix, norm_mlp, w_up, w_down,
           norm_ple, w_ple_gate, w_ple_proj, norm_final):
    bsz, seq, d = x.shape
    depth = w_in.shape[0]
    m = bsz * seq
    tm = min(512, m)
    t_mix = min(512, seq)
    nb_mix = 4 if bsz % 4 == 0 else 1

    cos_t, sin_a, sin_b = _rope_tables(positions)
    w_packed = _pack_w_in(w_in)
    gdn_par = jnp.concatenate([_gate_row(depth, [(HEADS, gdn_a_log)]),
                               _gate_row(depth, [(HEADS, gdn_dt_bias)])], axis=1)
    mlstm_par = _gate_row(depth, [(2 * HEADS, mlstm_i_bias), (3 * HEADS, mlstm_f_bias)])
    gdn_nw = jnp.tile(gdn_norm.astype(F32), (1, HEADS))[:, None, :]
    mlstm_nw = mlstm_norm.astype(F32)[:, None, :]
    sinks_e = jnp.repeat(attn_sinks.astype(F32), 2 * SWA_BLOCK, axis=-1).reshape(
        depth, SWA_WIDTH // 128, 4 * SWA_BLOCK)
    row3 = lambda a: a.astype(F32)[:, None, :]
    wo_b, wup_b, wdn_b = w_out.astype(BF16), w_up.astype(BF16), w_down.astype(BF16)
    wg_b, wp_b = w_ple_gate.astype(BF16), w_ple_proj.astype(BF16)
    nmix, nmlp, nple = row3(norm_mix), row3(norm_mlp), row3(norm_ple)
    nfin = norm_final.astype(F32)[None, :]
    p2 = p.reshape(depth, m, PLE_DIM)

    xf = x.reshape(m, d)
    for i in range(depth):
        gqkv, gz, mqkvo, sq, skv, gates = _proj(xf, nmix, w_packed, cos_t, sin_a, sin_b, i, tm)
        ya = _gdn(gqkv, gz, gates, conv_w, gdn_par, gdn_nw, i, bsz, seq, t_mix, nb_mix)
        yb = _mlstm(mqkvo, gates, mlstm_par, mlstm_nw, i, bsz, seq, t_mix)
        yc = _swa(sq, skv, sinks_e, i, bsz, seq)
        xf = _post(xf, ya, yb, yc, p2, wo_b, nmlp, wup_b, wdn_b, nple, wg_b, wp_b, nfin,
                   i, tm, final=(i == depth - 1))
    return xf.reshape(bsz, seq, d)
```

```python
import functools

import jax
import jax.numpy as jnp
from jax import lax
from jax.experimental import pallas as pl
from jax.experimental.pallas import tpu as pltpu

F32 = jnp.float32
BF16 = jnp.bfloat16

D_MODEL = 1024
DEPTH = 4
HEAD_DIM = 64
PLE_DIM = 256
D_FF = 4 * D_MODEL
EPS = 1e-6
HEADS = 4
WIDTH = HEADS * HEAD_DIM
CHUNK = 64
CONV_WIDTH = 4
GATE_SOFTCAP = 15.0
SWA_WIDTH = 512
SWA_KV_WIDTH = 128
SWA_BLOCK = 128
ROPE_DIM = 16
ROPE_THETA = 500000.0
GATE_LANES = 128
IN_COLS_PACKED = 2944
OFF_GQKV, OFF_GZ, OFF_M, OFF_SQ, OFF_SKV, OFF_GATES = 0, 768, 1024, 2048, 2560, 2816

VMEM_LIMIT = 56 * 1024 * 1024


def _dot(a, b):
    return jnp.dot(a, b, preferred_element_type=F32)


def _dot_nt(a, b):
    return lax.dot_general(a, b, (((1,), (1,)), ((), ())), preferred_element_type=F32)


def _dot_tn(a, b):
    return lax.dot_general(a, b, (((0,), (0,)), ((), ())), preferred_element_type=F32)


def _split2(x):
    hi = x.astype(BF16)
    lo = (x - hi.astype(F32)).astype(BF16)
    return hi, lo


def _split3(x):
    hi = x.astype(BF16)
    r = x - hi.astype(F32)
    mid = r.astype(BF16)
    lo = (r - mid.astype(F32)).astype(BF16)
    return hi, mid, lo


def _dot_sel(x, sel):
    hi, mid, lo = _split3(x)
    return _dot(hi, sel) + _dot(mid, sel) + _dot(lo, sel)


def _sel_dot(sel, x):
    hi, mid, lo = _split3(x)
    return _dot(sel, hi) + _dot(sel, mid) + _dot(sel, lo)


def _seg_sum(x, block_ones):
    hi, lo = _split2(x)
    return _dot(hi, block_ones) + _dot(lo, block_ones)


def _sigmoid(x):
    return 1.0 / (1.0 + jnp.exp(-x))


def _softplus(x):
    return jnp.maximum(x, 0.0) + jnp.log1p(jnp.exp(-jnp.abs(x)))


def _rms(x, g):
    return x * lax.rsqrt(jnp.mean(x * x, axis=-1, keepdims=True) + EPS) * g


def _iota2(shape, dim):
    return lax.broadcasted_iota(jnp.int32, shape, dim)


def _head_mask():
    return (_iota2((WIDTH, WIDTH), 0) >> 6) == (_iota2((WIDTH, WIDTH), 1) >> 6)


def _block_diag(x, head_mask=None):
    rows = x.shape[0]
    low = _iota2((rows, 128), 1) < HEAD_DIM
    zero = jnp.zeros((rows, 128), x.dtype)
    x0, x1 = x[:, 0:128], x[:, 128:256]
    return jnp.concatenate([
        jnp.concatenate([jnp.where(low, x0, zero), zero], axis=1),
        jnp.concatenate([jnp.where(low, zero, x0), zero], axis=1),
        jnp.concatenate([zero, jnp.where(low, x1, zero)], axis=1),
        jnp.concatenate([zero, jnp.where(low, zero, x1)], axis=1)], axis=0)


def _chunk_tril(t):
    r = _iota2((t, t), 0)
    c = _iota2((t, t), 1)
    return ((c <= r) & ((r >> 6) == (c >> 6))).astype(BF16)


def _expander(first_col):
    r = _iota2((GATE_LANES, WIDTH), 0)
    c = _iota2((GATE_LANES, WIDTH), 1)
    return (r == first_col + (c >> 6)).astype(BF16)


def _rope(x, cos_t, sin_a, sin_b, reps):
    n = x.shape[-1]
    if reps > 1:
        cos_t = jnp.concatenate([cos_t] * reps, axis=-1)
        sin_a = jnp.concatenate([sin_a] * reps, axis=-1)
        sin_b = jnp.concatenate([sin_b] * reps, axis=-1)
    half = ROPE_DIM // 2
    x_up = pltpu.roll(x, n - half, axis=1)
    x_dn = pltpu.roll(x, half, axis=1)
    return x * cos_t + x_up * sin_a + x_dn * sin_b


def _proj_kernel(x_ref, g_ref, w_ref, cos_ref, sa_ref, sb_ref,
                 gqkv_ref, gz_ref, m_ref, sq_ref, skv_ref, gates_ref):
    hb = _rms(x_ref[...], g_ref[...]).astype(BF16)
    gqkv_ref[...] = _dot(hb, w_ref[:, OFF_GQKV:OFF_GZ])
    gz_ref[...] = _dot(hb, w_ref[:, OFF_GZ:OFF_M])
    m_ref[...] = _dot(hb, w_ref[:, OFF_M:OFF_SQ])
    gates_ref[...] = _dot(hb, w_ref[:, OFF_GATES:IN_COLS_PACKED])
    cos_t, sin_a, sin_b = cos_ref[...], sa_ref[...], sb_ref[...]
    q = _dot(hb, w_ref[:, OFF_SQ:OFF_SKV])
    sq_ref[...] = _rope(q, cos_t, sin_a, sin_b, SWA_WIDTH // 128) * (HEAD_DIM ** -0.5)
    kv = _dot(hb, w_ref[:, OFF_SKV:OFF_GATES])
    skv_ref[:, 0:SWA_KV_WIDTH] = _rope(kv[:, 0:SWA_KV_WIDTH], cos_t, sin_a, sin_b, 1)
    skv_ref[:, SWA_KV_WIDTH:] = kv[:, SWA_KV_WIDTH:]


def _proj(x, norm_w, w_packed, cos_t, sin_a, sin_b, layer, tm):
    m = x.shape[0]
    row = lambda n: pl.BlockSpec((tm, n), lambda i: (i, 0))
    widths = (768, 256, 1024, SWA_WIDTH, 2 * SWA_KV_WIDTH, GATE_LANES)
    return pl.pallas_call(
        _proj_kernel,
        grid=(m // tm,),
        in_specs=[
            row(D_MODEL),
            pl.BlockSpec((None, 1, D_MODEL), lambda i: (layer, 0, 0)),
            pl.BlockSpec((None, D_MODEL, IN_COLS_PACKED), lambda i: (layer, 0, 0),
                         pipeline_mode=pl.Buffered(1)),
            row(128), row(128), row(128),
        ],
        out_specs=[row(n) for n in widths],
        out_shape=[jax.ShapeDtypeStruct((m, n), F32) for n in widths],
        compiler_params=pltpu.CompilerParams(
            dimension_semantics=("arbitrary",), vmem_limit_bytes=VMEM_LIMIT),
    )(x, norm_w, w_packed, cos_t, sin_a, sin_b)


def _gdn_kernel(qkv_ref, z_ref, gt_ref, cw_ref, par_ref, nw_ref, tril_ref, exp_ref, ones_ref, o_ref,
                state_ref, tail_ref, buf_ref, q_s, k_s, v_s, beta_s, gc_s,
                u_s, w_s, qd_s, kd_s, qkd_s, *, t, group, nb):
    c = CHUNK

    @pl.when(pl.program_id(1) == 0)
    def _():
        state_ref[...] = jnp.zeros_like(state_ref)
        tail_ref[...] = jnp.zeros_like(tail_ref)

    head_mask = _head_mask()
    block_ones = ones_ref[...]

    def front(a, carry):
        buf_ref[0:8, :] = tail_ref[a]
        buf_ref[8:8 + t, :] = qkv_ref[a]
        tail_ref[a] = qkv_ref[a, t - 8:t, :]
        cw = cw_ref[...]
        conv = buf_ref[5:5 + t, :] * cw[0:1, :]
        for j in range(1, CONV_WIDTH):
            conv = conv + buf_ref[5 + j:5 + j + t, :] * cw[j:j + 1, :]
        qkv = conv * _sigmoid(conv)
        q = qkv[:, 0:WIDTH]
        k = qkv[:, WIDTH:2 * WIDTH]
        q_s[a] = q * lax.rsqrt(_dot((q * q).astype(BF16), block_ones) + EPS) * (HEAD_DIM ** -0.5)
        k_s[a] = k * lax.rsqrt(_dot((k * k).astype(BF16), block_ones) + EPS)
        v_s[a] = qkv[:, 2 * WIDTH:3 * WIDTH]
        gates = gt_ref[a]
        par = par_ref[...]
        beta = _sigmoid(gates)
        g = -jnp.exp(par[0:1, :]) * _softplus(gates + par[1:2, :])
        gc = _sel_dot(tril_ref[...], g)
        beta_s[a] = _dot_sel(beta, exp_ref[0])
        gc_s[a] = _dot_sel(gc, exp_ref[1])
        return carry

    lax.fori_loop(0, nb, front, 0)

    row = _iota2((c, WIDTH), 0)
    col = _iota2((c, WIDTH), 1) & (c - 1)
    causal = col <= row
    strict = col < row
    eye = col == row
    nw = nw_ref[...]

    def bd(x):
        return _block_diag(x, head_mask)

    def mm(a, b):
        return _dot(a.astype(BF16), bd(b.astype(BF16)))

    groups_per_seq = t // c // group

    def prepare_group(idx, carry):
        a = idx // groups_per_seq
        gi = idx - a * groups_per_seq
        q_a, k_a, v_a, beta_a, gc_a = q_s.at[a], k_s.at[a], v_s.at[a], beta_s.at[a], gc_s.at[a]
        u_a, w_a, qd_a, kd_a, qkd_a = u_s.at[a], w_s.at[a], qd_s.at[a], kd_s.at[a], qkd_s.at[a]
        n = range(group)
        rs = [pl.ds(pl.multiple_of((gi * group + j) * c, c), c) for j in n]
        q = [q_a[r, :] for r in rs]
        k = [k_a[r, :] for r in rs]
        gc = [gc_a[r, :] for r in rs]
        g_last = [gc_a[pl.ds((gi * group + j) * c + c - 1, 1), :] for j in n]
        kb = [k[j] * beta_a[rs[j], :] for j in n]
        kkqk = [_dot_nt(jnp.concatenate([kb[j], q[j]], axis=0).astype(BF16), bd(k[j].astype(BF16)))
                for j in n]
        gc_row = [jnp.sum(jnp.where(eye, gc[j], 0.0), axis=0, keepdims=True) for j in n]
        decay = [jnp.exp(jnp.where(causal, gc[j] - gc_row[j], -jnp.inf)) for j in n]
        nm = [jnp.where(strict, kkqk[j][0:c] * decay[j], 0.0) for j in n]
        for j in n:
            qkd_a[rs[j], :] = (kkqk[j][c:2 * c] * decay[j]).astype(BF16)
        x = [eye.astype(F32) - jnp.where((row >> 1) == (col >> 1), nm[j], 0.0) for j in n]
        for lb in range(1, 6):
            off = ((row >> (lb + 1)) == (col >> (lb + 1))) & ((row >> lb) != (col >> lb))
            t1 = [mm(jnp.where(off, nm[j], 0.0), x[j]) for j in n]
            t2 = [mm(x[j], t1[j]) for j in n]
            x = [x[j] - t2[j] for j in n]
        xb = [x[j].astype(BF16) for j in n]
        egc = [jnp.exp(gc[j]) for j in n]
        for j in n:
            u_a[rs[j], :] = _dot(xb[j], bd((v_a[rs[j], :] * beta_a[rs[j], :]).astype(BF16)))
        for j in n:
            w_a[rs[j], :] = _dot(xb[j], bd((kb[j] * egc[j]).astype(BF16))).astype(BF16)
        for j in n:
            qd_a[rs[j], :] = (q[j] * egc[j]).astype(BF16)
            kd_a[rs[j], :] = (k[j] * jnp.exp(g_last[j] - gc[j])).astype(BF16)
        return carry

    lax.fori_loop(0, nb * groups_per_seq, prepare_group, 0)

    def chunk_step(i, carry):
        n = range(nb)
        r = pl.ds(pl.multiple_of(i * c, c), c)
        g_last = [gc_s[a, pl.ds(i * c + c - 1, 1), :] for a in n]
        s = [state_ref[a] for a in n]
        ws = [_dot(jnp.concatenate([w_s[a, r, :], qd_s[a, r, :]], axis=0), s[a].astype(BF16))
              for a in n]
        v_new = [(u_s[a, r, :] - ws[a][0:c]).astype(BF16) for a in n]
        upd = [_dot_tn(kd_s[a, r, :], v_new[a]) for a in n]
        for a in n:
            state_ref[a] = s[a] * jnp.exp(g_last[a]) + jnp.where(head_mask, upd[a], 0.0)
        o = [ws[a][c:2 * c] + _dot(qkd_s[a, r, :], bd(v_new[a])) for a in n]
        ms = [_dot((o[a] * o[a]).astype(BF16), block_ones) * (1.0 / HEAD_DIM) for a in n]
        for a in n:
            z = z_ref[a, r, :]
            o_ref[a, r, :] = o[a] * lax.rsqrt(ms[a] + EPS) * nw * (z * _sigmoid(z))
        return carry

    lax.fori_loop(0, t // c, chunk_step, 0)


def _gdn(gqkv, gz, gates, conv_w, par, norm_w, layer, bsz, seq, t, nb):
    row = lambda n: pl.BlockSpec((nb, t, n), lambda b, j: (b, j, 0))
    lay = lambda a, b2: pl.BlockSpec((None, a, b2), lambda b, j: (layer, 0, 0))
    const = lambda shape: pl.BlockSpec(shape, lambda b, j: (0,) * len(shape))
    seq3 = lambda a: a.reshape(bsz, seq, a.shape[-1])
    f32_buf = pltpu.VMEM((nb, t, WIDTH), F32)
    bf16_buf = pltpu.VMEM((nb, t, WIDTH), BF16)
    out = pl.pallas_call(
        functools.partial(_gdn_kernel, t=t, group=min(8, t // CHUNK), nb=nb),
        grid=(bsz // nb, seq // t),
        in_specs=[row(3 * WIDTH), row(WIDTH), row(GATE_LANES),
                  lay(CONV_WIDTH, 3 * WIDTH), lay(2, GATE_LANES), lay(1, WIDTH),
                  const((t, t)), const((2, GATE_LANES, WIDTH)), const((WIDTH, WIDTH))],
        out_specs=row(WIDTH),
        out_shape=jax.ShapeDtypeStruct((bsz, seq, WIDTH), F32),
        scratch_shapes=[
            pltpu.VMEM((nb, WIDTH, WIDTH), F32),
            pltpu.VMEM((nb, 8, 3 * WIDTH), F32),
            pltpu.VMEM((t + 8, 3 * WIDTH), F32),
            f32_buf, f32_buf, f32_buf, f32_buf, f32_buf,
            f32_buf,
            bf16_buf, bf16_buf, bf16_buf, bf16_buf,
        ],
        compiler_params=pltpu.CompilerParams(
            dimension_semantics=("arbitrary", "arbitrary"), vmem_limit_bytes=VMEM_LIMIT),
    )(seq3(gqkv), seq3(gz), seq3(gates), conv_w, par, norm_w,
      _chunk_tril(t), jnp.stack([_expander(0), _expander(HEADS)]), _head_mask().astype(BF16))
    return out.reshape(bsz * seq, WIDTH)


def _mlstm_kernel(m_ref, gt_ref, par_ref, nw_ref, tril_ref, exp_ref, ones_ref, o_ref,
                  c_ref, n_ref, mx_ref, ig_s, b_s, cm_s, *, t, nb):
    c = CHUNK

    @pl.when(pl.program_id(1) == 0)
    def _():
        c_ref[...] = jnp.zeros_like(c_ref)
        n_ref[...] = jnp.zeros_like(n_ref)
        mx_ref[...] = jnp.zeros_like(mx_ref)

    head_mask = _head_mask()
    block_ones = ones_ref[...]

    def front(a, carry):
        pre = gt_ref[a] + par_ref[...]
        capped = GATE_SOFTCAP * jnp.tanh(pre * (1.0 / GATE_SOFTCAP))
        log_f = -_softplus(-capped)
        b_all = _sel_dot(tril_ref[...], log_f)
        ig = _dot_sel(capped, exp_ref[0])
        b = _dot_sel(b_all, exp_ref[1])
        pos = _iota2((t, WIDTH), 0) & (c - 1)
        cm = ig - b
        for sh in (1, 2, 4, 8, 16, 32):
            cm = jnp.where(pos >= sh, jnp.maximum(cm, pltpu.roll(cm, sh, axis=0)), cm)
        ig_s[a] = ig
        b_s[a] = b
        cm_s[a] = cm
        return carry

    lax.fori_loop(0, nb, front, 0)

    row = _iota2((c, WIDTH), 0)
    col = _iota2((c, WIDTH), 1) & (c - 1)
    causal = col <= row
    eye = col == row
    nw = nw_ref[...]
    ones = jnp.ones((c, WIDTH), BF16)
    bd = _block_diag

    def chunk_step(i, carry):
        n = range(nb)
        r = pl.ds(pl.multiple_of(i * c, c), c)
        last = pl.ds(i * c + c - 1, 1)
        qb = [m_ref[a, r, 0:WIDTH].astype(BF16) for a in n]
        k = [m_ref[a, r, WIDTH:2 * WIDTH] * (HEAD_DIM ** -0.5) for a in n]
        vb = [m_ref[a, r, 2 * WIDTH:3 * WIDTH].astype(BF16) for a in n]
        ig = [ig_s[a, r, :] for a in n]
        b = [b_s[a, r, :] for a in n]
        m_intra = [b[a] + cm_s[a, r, :] for a in n]
        b_last = [b_s[a, last, :] for a in n]
        m_chunk = [b_last[a] + cm_s[a, last, :] for a in n]
        qk = [_dot_nt(qb[a], bd(k[a].astype(BF16))) for a in n]
        gate_row = [jnp.sum(jnp.where(eye, ig[a] - b[a], 0.0), axis=0, keepdims=True) for a in n]
        qk = [qk[a] * jnp.exp(jnp.where(causal, b[a] + gate_row[a], -jnp.inf) - m_intra[a]) for a in n]
        num_intra = [_dot(qk[a].astype(BF16), bd(vb[a])) for a in n]
        den_intra = [_seg_sum(qk[a], block_ones) for a in n]
        m_prev = [mx_ref[a] for a in n]
        c_st = [c_ref[a] for a in n]
        n_st = [n_ref[a] for a in n]
        inter = [_dot(qb[a], jnp.concatenate([c_st[a].astype(BF16), n_st[a].astype(BF16)], axis=1))
                 for a in n]
        ke = [(k[a] * jnp.exp(b_last[a] - b[a] + ig[a] - m_chunk[a])).astype(BF16) for a in n]
        own = [_dot_tn(ke[a], jnp.concatenate([vb[a], ones], axis=1)) for a in n]
        for a in n:
            m_new = jnp.maximum(b_last[a] + m_prev[a], m_chunk[a])
            s_old = jnp.exp(b_last[a] + m_prev[a] - m_new)
            s_new = jnp.exp(m_chunk[a] - m_new)
            c_ref[a] = s_old * c_st[a] + s_new * jnp.where(head_mask, own[a][:, 0:WIDTH], 0.0)
            n_ref[a] = s_old * n_st[a] + s_new * jnp.where(head_mask, own[a][:, WIDTH:2 * WIDTH], 0.0)
            mx_ref[a] = m_new
        h = []
        for a in n:
            pre_m = b[a] + m_prev[a]
            m_t = jnp.maximum(pre_m, m_intra[a])
            s_inter = jnp.exp(pre_m - m_t)
            s_intra = jnp.exp(m_intra[a] - m_t)
            num = s_inter * inter[a][:, 0:WIDTH] + s_intra * num_intra[a]
            den = s_inter * inter[a][:, WIDTH:2 * WIDTH] + s_intra * den_intra[a]
            h.append(num / jnp.maximum(jnp.abs(den), jnp.exp(-m_t)))
        ms = [_dot((h[a] * h[a]).astype(BF16), block_ones) * (1.0 / HEAD_DIM) for a in n]
        for a in n:
            o_ref[a, r, :] = (h[a] * lax.rsqrt(ms[a] + EPS) * nw
                              * _sigmoid(m_ref[a, r, 3 * WIDTH:4 * WIDTH]))
        return carry

    lax.fori_loop(0, t // c, chunk_step, 0)


def _mlstm(mqkvo, gates, par, norm_w, layer, bsz, seq, t, nb):
    row = lambda n: pl.BlockSpec((nb, t, n), lambda b, j: (b, j, 0))
    lay = lambda a, b2: pl.BlockSpec((None, a, b2), lambda b, j: (layer, 0, 0))
    const = lambda shape: pl.BlockSpec(shape, lambda b, j: (0,) * len(shape))
    seq3 = lambda a: a.reshape(bsz, seq, a.shape[-1])
    f32_buf = pltpu.VMEM((nb, t, WIDTH), F32)
    out = pl.pallas_call(
        functools.partial(_mlstm_kernel, t=t, nb=nb),
        grid=(bsz // nb, seq // t),
        in_specs=[row(4 * WIDTH), row(GATE_LANES), lay(1, GATE_LANES), lay(1, WIDTH),
                  const((t, t)), const((2, GATE_LANES, WIDTH)), const((WIDTH, WIDTH))],
        out_specs=row(WIDTH),
        out_shape=jax.ShapeDtypeStruct((bsz, seq, WIDTH), F32),
        scratch_shapes=[
            pltpu.VMEM((nb, WIDTH, WIDTH), F32),
            pltpu.VMEM((nb, WIDTH, WIDTH), F32),
            pltpu.VMEM((nb, 1, WIDTH), F32),
            f32_buf, f32_buf, f32_buf,
        ],
        compiler_params=pltpu.CompilerParams(
            dimension_semantics=("arbitrary", "arbitrary"), vmem_limit_bytes=VMEM_LIMIT),
    )(seq3(mqkvo), seq3(gates), par, norm_w,
      _chunk_tril(t), jnp.stack([_expander(2 * HEADS), _expander(3 * HEADS)]),
      _head_mask().astype(BF16))
    return out.reshape(bsz * seq, WIDTH)


def _swa_kernel(q_ref, kvc_ref, kvp_ref, sink_ref, o_ref, *, tq):
    tb = SWA_BLOCK
    kvw = SWA_KV_WIDTH
    nq = tq // tb
    first = pl.program_id(1) == 0
    kcat = jnp.concatenate([kvp_ref[:, 0:kvw], kvc_ref[:, 0:kvw]], axis=0)
    vcat = jnp.concatenate([kvp_ref[:, kvw:2 * kvw], kvc_ref[:, kvw:2 * kvw]], axis=0)
    v_t = vcat.T.astype(BF16)
    kr = pltpu.roll(kcat, HEAD_DIM, axis=1)
    lo = _iota2((tb + tq, kvw), 1) < HEAD_DIM

    def place(x_lo, x_hi):
        return jnp.where(lo, x_lo, 0.0).astype(BF16), jnp.where(lo, 0.0, x_hi).astype(BF16)

    k_placed = (place(kcat, kr), place(kr, kcat))
    ki = _iota2((2 * tb, tb), 0)
    qi = _iota2((2 * tb, tb), 1)
    in_window = (ki > qi) & (ki <= qi + tb)

    chains = [(i, g) for i in range(nq) for g in range(2)]
    s_t = []
    for i, g in chains:
        keys = jnp.concatenate([k_placed[g][0][i * tb:(i + 2) * tb],
                                k_placed[g][1][i * tb:(i + 2) * tb]], axis=0)
        qs = jnp.concatenate([q_ref[i * tb:(i + 1) * tb, 256 * g:256 * g + 128],
                              q_ref[i * tb:(i + 1) * tb, 256 * g + 128:256 * g + 256]], axis=0)
        s_t.append(_dot_nt(keys, qs.astype(BF16)))
    p_all, inv_all = [], []
    for (i, g), sc in zip(chains, s_t):
        mask = in_window & (ki >= jnp.where(first, tb, 0)) if i == 0 else in_window
        ps, invs = [], []
        for part in range(2):
            for e in range(2):
                sink = sink_ref[2 * g + part:2 * g + part + 1, e * 2 * tb:e * 2 * tb + 1]
                se = jnp.where(mask, sc[e * 2 * tb:(e + 1) * 2 * tb, part * tb:(part + 1) * tb],
                               -jnp.inf)
                mx = jnp.maximum(jnp.max(se, axis=0, keepdims=True), sink)
                p = jnp.exp(se - mx)
                ps.append(p.astype(BF16))
                invs.append(1.0 / (jnp.sum(p, axis=0, keepdims=True) + jnp.exp(sink - mx)))
        p_all.append(jnp.concatenate(ps, axis=-1))
        inv_all.append(jnp.concatenate(invs, axis=-1))
    out_t = [_dot(v_t[g * HEAD_DIM:(g + 1) * HEAD_DIM, i * tb:(i + 2) * tb], p) * inv
             for (i, g), p, inv in zip(chains, p_all, inv_all)]
    for (i, g), o in zip(chains, out_t):
        for part in range(2):
            j = 2 * g + part
            pair_t = jnp.concatenate([o[:, 2 * part * tb:(2 * part + 1) * tb],
                                      o[:, (2 * part + 1) * tb:(2 * part + 2) * tb]], axis=0)
            o_ref[i * tb:(i + 1) * tb, 128 * j:128 * (j + 1)] = pair_t.T


def _swa(sq, skv, sinks_e, layer, bsz, seq, tq):
    m = bsz * seq
    nt = seq // tq
    nq = tq // SWA_BLOCK
    nb = seq // SWA_BLOCK
    return pl.pallas_call(
        functools.partial(_swa_kernel, tq=tq),
        grid=(bsz, nt),
        in_specs=[
            pl.BlockSpec((tq, SWA_WIDTH), lambda b, n: (b * nt + n, 0)),
            pl.BlockSpec((tq, 2 * SWA_KV_WIDTH), lambda b, n: (b * nt + n, 0)),
            pl.BlockSpec((SWA_BLOCK, 2 * SWA_KV_WIDTH),
                         lambda b, n: (b * nb + jnp.maximum(n * nq - 1, 0), 0)),
            pl.BlockSpec((None, SWA_WIDTH // 128, 4 * SWA_BLOCK), lambda b, n: (layer, 0, 0)),
        ],
        out_specs=pl.BlockSpec((tq, SWA_WIDTH), lambda b, n: (b * nt + n, 0)),
        out_shape=jax.ShapeDtypeStruct((m, SWA_WIDTH), F32),
        compiler_params=pltpu.CompilerParams(
            dimension_semantics=("arbitrary", "arbitrary"), vmem_limit_bytes=VMEM_LIMIT),
    )(sq, skv, skv, sinks_e)


def _post_kernel(x_ref, ya_ref, yb_ref, yc_ref, p_ref, wo_ref, nmlp_ref, wup_ref, wdn_ref,
                 nple_ref, wg_ref, wp_ref, nfin_ref, o_ref, *, final, tf):
    x = x_ref[...]
    x = x + _dot(ya_ref[...].astype(BF16), wo_ref[0:WIDTH, :])
    x = x + _dot(yb_ref[...].astype(BF16), wo_ref[WIDTH:2 * WIDTH, :])
    x = x + _dot(yc_ref[...].astype(BF16), wo_ref[2 * WIDTH:, :])
    h = _rms(x, nmlp_ref[...]).astype(BF16)
    acc = jnp.zeros_like(x)
    for f in range(0, D_FF, tf):
        u = jnp.maximum(_dot(h, wup_ref[:, f:f + tf]), 0.0)
        acc = acc + _dot((u * u).astype(BF16), wdn_ref[f:f + tf, :])
    x = x + acc
    gate = _sigmoid(_dot(_rms(x, nple_ref[...]).astype(BF16), wg_ref[...]))
    x = x + gate * _dot(p_ref[...].astype(BF16), wp_ref[...])
    if final:
        x = _rms(x, nfin_ref[...])
    o_ref[...] = x


def _post(x, ya, yb, yc, p, w_out, norm_mlp, w_up, w_down, norm_ple, w_gate, w_proj, norm_final,
          layer, tm, final):
    m = x.shape[0]
    row = lambda n: pl.BlockSpec((tm, n), lambda i: (i, 0))
    lay = lambda a, b: pl.BlockSpec((None, a, b), lambda i: (layer, 0, 0),
                                    pipeline_mode=pl.Buffered(1))
    return pl.pallas_call(
        functools.partial(_post_kernel, final=final, tf=1024),
        grid=(m // tm,),
        in_specs=[
            row(D_MODEL), row(WIDTH), row(WIDTH), row(SWA_WIDTH),
            pl.BlockSpec((None, tm, PLE_DIM), lambda i: (layer, i, 0)),
            lay(D_MODEL, D_MODEL), lay(1, D_MODEL), lay(D_MODEL, D_FF), lay(D_FF, D_MODEL),
            lay(1, D_MODEL), lay(D_MODEL, D_MODEL), lay(PLE_DIM, D_MODEL),
            pl.BlockSpec((1, D_MODEL), lambda i: (0, 0)),
        ],
        out_specs=row(D_MODEL),
        out_shape=jax.ShapeDtypeStruct((m, D_MODEL), F32),
        compiler_params=pltpu.CompilerParams(
            dimension_semantics=("arbitrary",), vmem_limit_bytes=VMEM_LIMIT),
    )(x, ya, yb, yc, p, w_out, norm_mlp, w_up, w_down, norm_ple, w_gate, w_proj, norm_final)


def _pack_w_in(w_in):
    depth = w_in.shape[0]
    pad = jnp.zeros((depth, D_MODEL, GATE_LANES - 4 * HEADS), w_in.dtype)
    return jnp.concatenate(
        [w_in[..., 0:1024], w_in[..., 1032:2056], w_in[..., 2064:2832],
         w_in[..., 1024:1032], w_in[..., 2056:2064], pad], axis=-1).astype(BF16)


def _gate_row(depth, pieces):
    out = jnp.zeros((depth, 1, GATE_LANES), F32)
    for first_col, vals in pieces:
        out = out.at[:, 0, first_col:first_col + HEADS].set(vals.astype(F32))
    return out


def _rope_tables(positions):
    half = ROPE_DIM // 2
    inv_freq = ROPE_THETA ** (-jnp.arange(0, ROPE_DIM, 2, dtype=F32) / ROPE_DIM)
    ang = positions.astype(F32).reshape(-1)[:, None] * inv_freq
    cos, sin = jnp.cos(ang), jnp.sin(ang)
    n = ang.shape[0]
    rest = HEAD_DIM - ROPE_DIM
    cos_t = jnp.concatenate([cos, cos, jnp.ones((n, rest), F32)], axis=-1)
    sin_a = jnp.concatenate([-sin, jnp.zeros((n, half + rest), F32)], axis=-1)
    sin_b = jnp.concatenate([jnp.zeros((n, half), F32), sin, jnp.zeros((n, rest), F32)], axis=-1)
    return tuple(jnp.concatenate([a, a], axis=-1) for a in (cos_t, sin_a, sin_b))


def kernel(x, p, positions, w_in, conv_w, gdn_a_log, gdn_dt_bias, gdn_norm, mlstm_i_bias,
           mlstm_f_bias, mlstm_norm, attn_sinks, w_out, norm_mix, norm_mlp, w_up, w_down,
           norm_ple, w_ple_gate, w_ple_proj, norm_final):
    bsz, seq, d = x.shape
    depth = w_in.shape[0]
    m = bsz * seq
    tm = min(512, m)
    t_mix = min(512, seq)
    nb_mix = 4 if bsz % 4 == 0 else 1

    cos_t, sin_a, sin_b = _rope_tables(positions)
    w_packed = _pack_w_in(w_in)
    gdn_par = jnp.concatenate([_gate_row(depth, [(HEADS, gdn_a_log)]),
                               _gate_row(depth, [(HEADS, gdn_dt_bias)])], axis=1)
    mlstm_par = _gate_row(depth, [(2 * HEADS, mlstm_i_bias), (3 * HEADS, mlstm_f_bias)])
    gdn_nw = jnp.tile(gdn_norm.astype(F32), (1, HEADS))[:, None, :]
    mlstm_nw = mlstm_norm.astype(F32)[:, None, :]
    sinks_e = jnp.repeat(attn_sinks.astype(F32), 2 * SWA_BLOCK, axis=-1).reshape(
        depth, SWA_WIDTH // 128, 4 * SWA_BLOCK)
    row3 = lambda a: a.astype(F32)[:, None, :]
    wo_b, wup_b, wdn_b = w_out.astype(BF16), w_up.astype(BF16), w_down.astype(BF16)
    wg_b, wp_b = w_ple_gate.astype(BF16), w_ple_proj.astype(BF16)
    nmix, nmlp, nple = row3(norm_mix), row3(norm_mlp), row3(norm_ple)
    nfin = norm_final.astype(F32)[None, :]
    p2 = p.reshape(depth, m, PLE_DIM)

    xf = x.reshape(m, d)
    for i in range(depth):
        gqkv, gz, mqkvo, sq, skv, gates = _proj(xf, nmix, w_packed, cos_t, sin_a, sin_b, i, tm)
        ya = _gdn(gqkv, gz, gates, conv_w, gdn_par, gdn_nw, i, bsz, seq, t_mix, nb_mix)
        yb = _mlstm(mqkvo, gates, mlstm_par, mlstm_nw, i, bsz, seq, t_mix, nb_mix)
        yc = _swa(sq, skv, sinks_e, i, bsz, seq, min(256, seq))
        xf = _post(xf, ya, yb, yc, p2, wo_b, nmlp, wup_b, wdn_b, nple, wg_b, wp_b, nfin,
                   i, tm, final=(i == depth - 1))
    return xf.reshape(bsz, seq, d)
```

```python
import functools

import jax
import jax.numpy as jnp
from jax import lax
from jax.experimental import pallas as pl
from jax.experimental.pallas import tpu as pltpu

F32 = jnp.float32
BF16 = jnp.bfloat16

D_MODEL = 1024
DEPTH = 4
HEAD_DIM = 64
PLE_DIM = 256
D_FF = 4 * D_MODEL
EPS = 1e-6
HEADS = 4
WIDTH = HEADS * HEAD_DIM
CHUNK = 64
CONV_WIDTH = 4
GATE_SOFTCAP = 15.0
SWA_WIDTH = 512
SWA_KV_WIDTH = 128
SWA_BLOCK = 128
ROPE_DIM = 16
ROPE_THETA = 500000.0
GATE_LANES = 128
IN_COLS_PACKED = 2944
OFF_GQKV, OFF_GZ, OFF_M, OFF_SQ, OFF_SKV, OFF_GATES = 0, 768, 1024, 2048, 2560, 2816

VMEM_LIMIT = 56 * 1024 * 1024


def _dot(a, b):
    return jnp.dot(a, b, preferred_element_type=F32)


def _dot_nt(a, b):
    return lax.dot_general(a, b, (((1,), (1,)), ((), ())), preferred_element_type=F32)


def _dot_tn(a, b):
    return lax.dot_general(a, b, (((0,), (0,)), ((), ())), preferred_element_type=F32)


def _split2(x):
    hi = x.astype(BF16)
    lo = (x - hi.astype(F32)).astype(BF16)
    return hi, lo


def _dot_sel(x, sel):
    hi, lo = _split2(x)
    return _dot(hi, sel) + _dot(lo, sel)


def _chunk_cumsum(x):
    pos = _iota2(x.shape, 0) & (CHUNK - 1)
    sh = 1
    while sh < CHUNK:
        x = x + jnp.where(pos >= sh, pltpu.roll(x, sh, axis=0), 0.0)
        sh *= 2
    return x


def _seg_sum(x, block_ones):
    hi, lo = _split2(x)
    return _dot(hi, block_ones) + _dot(lo, block_ones)


def _sigmoid(x):
    return 1.0 / (1.0 + jnp.exp(-x))


def _softplus(x):
    return jnp.maximum(x, 0.0) + jnp.log1p(jnp.exp(-jnp.abs(x)))


def _rms(x, g):
    return x * lax.rsqrt(jnp.mean(x * x, axis=-1, keepdims=True) + EPS) * g


def _iota2(shape, dim):
    return lax.broadcasted_iota(jnp.int32, shape, dim)


def _head_mask():
    return (_iota2((WIDTH, WIDTH), 0) >> 6) == (_iota2((WIDTH, WIDTH), 1) >> 6)


def _block_diag(x, head_mask=None):
    rows = x.shape[0]
    low = _iota2((rows, 128), 1) < HEAD_DIM
    zero = jnp.zeros((rows, 128), x.dtype)
    x0, x1 = x[:, 0:128], x[:, 128:256]
    return jnp.concatenate([
        jnp.concatenate([jnp.where(low, x0, zero), zero], axis=1),
        jnp.concatenate([jnp.where(low, zero, x0), zero], axis=1),
        jnp.concatenate([zero, jnp.where(low, x1, zero)], axis=1),
        jnp.concatenate([zero, jnp.where(low, zero, x1)], axis=1)], axis=0)


def _expander(first_col):
    r = _iota2((GATE_LANES, WIDTH), 0)
    c = _iota2((GATE_LANES, WIDTH), 1)
    return (r == first_col + (c >> 6)).astype(BF16)


def _rope(x, cos_t, sin_a, sin_b, reps):
    n = x.shape[-1]
    if reps > 1:
        cos_t = jnp.concatenate([cos_t] * reps, axis=-1)
        sin_a = jnp.concatenate([sin_a] * reps, axis=-1)
        sin_b = jnp.concatenate([sin_b] * reps, axis=-1)
    half = ROPE_DIM // 2
    x_up = pltpu.roll(x, n - half, axis=1)
    x_dn = pltpu.roll(x, half, axis=1)
    return x * cos_t + x_up * sin_a + x_dn * sin_b


def _proj_kernel(x_ref, g_ref, w_ref, cos_ref, sa_ref, sb_ref, cw_ref, ones_ref,
                 gqkv_ref, gz_ref, m_ref, sq_ref, skv_ref, gates_ref, tail_ref, buf_ref,
                 *, tiles_per_seq):
    t = x_ref.shape[0]

    @pl.when(pl.program_id(0) % tiles_per_seq == 0)
    def _():
        tail_ref[...] = jnp.zeros_like(tail_ref)

    hb = _rms(x_ref[...], g_ref[...]).astype(BF16)
    buf_ref[0:8, :] = tail_ref[...]
    buf_ref[8:8 + t, :] = _dot(hb, w_ref[:, OFF_GQKV:OFF_GZ])
    tail_ref[...] = buf_ref[t:t + 8, :]
    cw = cw_ref[...]
    block_ones = ones_ref[...]

    def conv_silu(lo, hi):
        acc = buf_ref[5:5 + t, lo:hi] * cw[0:1, lo:hi]
        for j in range(1, CONV_WIDTH):
            acc = acc + buf_ref[5 + j:5 + j + t, lo:hi] * cw[j:j + 1, lo:hi]
        return acc * _sigmoid(acc)

    def l2n(v):
        return v * lax.rsqrt(_dot((v * v).astype(BF16), block_ones) + EPS)

    gz_ref[...] = _dot(hb, w_ref[:, OFF_GZ:OFF_M])
    q = conv_silu(0, WIDTH)
    m_ref[:, 0:2 * WIDTH] = _dot(hb, w_ref[:, OFF_M:OFF_M + 2 * WIDTH])
    gqkv_ref[:, 0:WIDTH] = l2n(q) * (HEAD_DIM ** -0.5)
    k = conv_silu(WIDTH, 2 * WIDTH)
    m_ref[:, 2 * WIDTH:4 * WIDTH] = _dot(hb, w_ref[:, OFF_M + 2 * WIDTH:OFF_SQ])
    gqkv_ref[:, WIDTH:2 * WIDTH] = l2n(k)
    gates_ref[...] = _dot(hb, w_ref[:, OFF_GATES:IN_COLS_PACKED])
    cos_t, sin_a, sin_b = cos_ref[...], sa_ref[...], sb_ref[...]
    sq = _dot(hb, w_ref[:, OFF_SQ:OFF_SKV])
    gqkv_ref[:, 2 * WIDTH:3 * WIDTH] = conv_silu(2 * WIDTH, 3 * WIDTH)
    sq_ref[...] = _rope(sq, cos_t, sin_a, sin_b, SWA_WIDTH // 128) * (HEAD_DIM ** -0.5)
    kv = _dot(hb, w_ref[:, OFF_SKV:OFF_GATES])
    skv_ref[:, 0:SWA_KV_WIDTH] = _rope(kv[:, 0:SWA_KV_WIDTH], cos_t, sin_a, sin_b, 1)
    skv_ref[:, SWA_KV_WIDTH:] = kv[:, SWA_KV_WIDTH:]


def _proj(x, norm_w, w_packed, cos_t, sin_a, sin_b, conv_w, layer, tm, seq):
    m = x.shape[0]
    row = lambda n: pl.BlockSpec((tm, n), lambda i: (i, 0))
    widths = (768, 256, 1024, SWA_WIDTH, 2 * SWA_KV_WIDTH, GATE_LANES)
    return pl.pallas_call(
        functools.partial(_proj_kernel, tiles_per_seq=seq // tm),
        grid=(m // tm,),
        in_specs=[
            row(D_MODEL),
            pl.BlockSpec((None, 1, D_MODEL), lambda i: (layer, 0, 0)),
            pl.BlockSpec((None, D_MODEL, IN_COLS_PACKED), lambda i: (layer, 0, 0),
                         pipeline_mode=pl.Buffered(1)),
            row(128), row(128), row(128),
            pl.BlockSpec((None, CONV_WIDTH, 3 * WIDTH), lambda i: (layer, 0, 0)),
            pl.BlockSpec((WIDTH, WIDTH), lambda i: (0, 0)),
        ],
        out_specs=[row(n) for n in widths],
        out_shape=[jax.ShapeDtypeStruct((m, n), F32) for n in widths],
        scratch_shapes=[
            pltpu.VMEM((8, 3 * WIDTH), F32),
            pltpu.VMEM((tm + 8, 3 * WIDTH), F32),
        ],
        compiler_params=pltpu.CompilerParams(
            dimension_semantics=("arbitrary",), vmem_limit_bytes=VMEM_LIMIT),
    )(x, norm_w, w_packed, cos_t, sin_a, sin_b, conv_w, _head_mask().astype(BF16))


def _gdn_kernel(qkv_ref, z_ref, gt_ref, par_ref, nw_ref, exp_ref, ones_ref, o_ref,
                state_ref, beta_s, gc_s, u_s, w_s, qd_s, kd_s, qkd_s, *, t, group, nb):
    c = CHUNK

    @pl.when(pl.program_id(1) == 0)
    def _():
        state_ref[...] = jnp.zeros_like(state_ref)

    head_mask = _head_mask()
    block_ones = ones_ref[...]

    def front(a, carry):
        gates = gt_ref[a]
        par = par_ref[...]
        beta = _sigmoid(gates)
        g = -jnp.exp(par[0:1, :]) * _softplus(gates + par[1:2, :])
        gc = _chunk_cumsum(g)
        beta_s[a] = _dot_sel(beta, exp_ref[0])
        gc_s[a] = _dot_sel(gc, exp_ref[1])
        return carry

    lax.fori_loop(0, nb, front, 0)

    row = _iota2((c, WIDTH), 0)
    col = _iota2((c, WIDTH), 1) & (c - 1)
    causal = col <= row
    strict = col < row
    eye = col == row
    nw = nw_ref[...]

    def bd(x):
        return _block_diag(x, head_mask)

    def mm(a, b):
        return _dot(a.astype(BF16), bd(b.astype(BF16)))

    groups_per_seq = t // c // group

    def prepare_group(idx, carry):
        a = idx // groups_per_seq
        gi = idx - a * groups_per_seq
        qkv_a, beta_a, gc_a = qkv_ref.at[a], beta_s.at[a], gc_s.at[a]
        u_a, w_a, qd_a, kd_a, qkd_a = u_s.at[a], w_s.at[a], qd_s.at[a], kd_s.at[a], qkd_s.at[a]
        n = range(group)
        rs = [pl.ds(pl.multiple_of((gi * group + j) * c, c), c) for j in n]
        q = [qkv_a[r, 0:WIDTH] for r in rs]
        k = [qkv_a[r, WIDTH:2 * WIDTH] for r in rs]
        gc = [gc_a[r, :] for r in rs]
        g_last = [gc_a[pl.ds((gi * group + j) * c + c - 1, 1), :] for j in n]
        kb = [k[j] * beta_a[rs[j], :] for j in n]
        kkqk = [_dot_nt(jnp.concatenate([kb[j], q[j]], axis=0).astype(BF16), bd(k[j].astype(BF16)))
                for j in n]
        gc_row = [jnp.sum(jnp.where(eye, gc[j], 0.0), axis=0, keepdims=True) for j in n]
        decay = [jnp.exp(jnp.where(causal, gc[j] - gc_row[j], -jnp.inf)) for j in n]
        nm = [jnp.where(strict, kkqk[j][0:c] * decay[j], 0.0) for j in n]
        for j in n:
            qkd_a[rs[j], :] = (kkqk[j][c:2 * c] * decay[j]).astype(BF16)
        x = [eye.astype(F32) - jnp.where((row >> 1) == (col >> 1), nm[j], 0.0) for j in n]
        for lb in range(1, 6):
            off = ((row >> (lb + 1)) == (col >> (lb + 1))) & ((row >> lb) != (col >> lb))
            t1 = [mm(jnp.where(off, nm[j], 0.0), x[j]) for j in n]
            t2 = [mm(x[j], t1[j]) for j in n]
            x = [x[j] - t2[j] for j in n]
        xb = [x[j].astype(BF16) for j in n]
        egc = [jnp.exp(gc[j]) for j in n]
        for j in n:
            u_a[rs[j], :] = _dot(
                xb[j], bd((qkv_a[rs[j], 2 * WIDTH:3 * WIDTH] * beta_a[rs[j], :]).astype(BF16)))
        for j in n:
            w_a[rs[j], :] = _dot(xb[j], bd((kb[j] * egc[j]).astype(BF16))).astype(BF16)
        for j in n:
            qd_a[rs[j], :] = (q[j] * egc[j]).astype(BF16)
            kd_a[rs[j], :] = (k[j] * jnp.exp(g_last[j] - gc[j])).astype(BF16)
        return carry

    lax.fori_loop(0, nb * groups_per_seq, prepare_group, 0)

    def chunk_step(i, carry):
        n = range(nb)
        r = pl.ds(pl.multiple_of(i * c, c), c)
        g_last = [gc_s[a, pl.ds(i * c + c - 1, 1), :] for a in n]
        s = [state_ref[a] for a in n]
        ws = [_dot(jnp.concatenate([w_s[a, r, :], qd_s[a, r, :]], axis=0), s[a].astype(BF16))
              for a in n]
        v_new = [(u_s[a, r, :] - ws[a][0:c]).astype(BF16) for a in n]
        upd = [_dot_tn(kd_s[a, r, :], v_new[a]) for a in n]
        for a in n:
            state_ref[a] = s[a] * jnp.exp(g_last[a]) + jnp.where(head_mask, upd[a], 0.0)
        o = [ws[a][c:2 * c] + _dot(qkd_s[a, r, :], bd(v_new[a])) for a in n]
        ms = [_dot((o[a] * o[a]).astype(BF16), block_ones) * (1.0 / HEAD_DIM) for a in n]
        for a in n:
            z = z_ref[a, r, :]
            o_ref[a, r, :] = o[a] * lax.rsqrt(ms[a] + EPS) * nw * (z * _sigmoid(z))
        return carry

    lax.fori_loop(0, t // c, chunk_step, 0)


def _gdn(gqkv, gz, gates, par, norm_w, layer, bsz, seq, t, nb):
    row = lambda n: pl.BlockSpec((nb, t, n), lambda b, j: (b, j, 0))
    lay = lambda a, b2: pl.BlockSpec((None, a, b2), lambda b, j: (layer, 0, 0))
    const = lambda shape: pl.BlockSpec(shape, lambda b, j: (0,) * len(shape))
    seq3 = lambda a: a.reshape(bsz, seq, a.shape[-1])
    f32_buf = pltpu.VMEM((nb, t, WIDTH), F32)
    bf16_buf = pltpu.VMEM((nb, t, WIDTH), BF16)
    out = pl.pallas_call(
        functools.partial(_gdn_kernel, t=t, group=min(8, t // CHUNK), nb=nb),
        grid=(bsz // nb, seq // t),
        in_specs=[row(3 * WIDTH), row(WIDTH), row(GATE_LANES),
                  lay(2, GATE_LANES), lay(1, WIDTH),
                  const((2, GATE_LANES, WIDTH)), const((WIDTH, WIDTH))],
        out_specs=row(WIDTH),
        out_shape=jax.ShapeDtypeStruct((bsz, seq, WIDTH), F32),
        scratch_shapes=[
            pltpu.VMEM((nb, WIDTH, WIDTH), F32),
            f32_buf, f32_buf,
            f32_buf,
            bf16_buf, bf16_buf, bf16_buf, bf16_buf,
        ],
        compiler_params=pltpu.CompilerParams(
            dimension_semantics=("arbitrary", "arbitrary"), vmem_limit_bytes=VMEM_LIMIT),
    )(seq3(gqkv), seq3(gz), seq3(gates), par, norm_w,
      jnp.stack([_expander(0), _expander(HEADS)]), _head_mask().astype(BF16))
    return out.reshape(bsz * seq, WIDTH)


def _mlstm_kernel(m_ref, gt_ref, par_ref, nw_ref, exp_ref, ones_ref, o_ref,
                  c_ref, n_ref, mx_ref, ig_s, b_s, cm_s, *, t, nb):
    c = CHUNK

    @pl.when(pl.program_id(1) == 0)
    def _():
        c_ref[...] = jnp.zeros_like(c_ref)
        n_ref[...] = jnp.zeros_like(n_ref)
        mx_ref[...] = jnp.zeros_like(mx_ref)

    head_mask = _head_mask()
    block_ones = ones_ref[...]

    def front(a, carry):
        pre = gt_ref[a] + par_ref[...]
        capped = GATE_SOFTCAP * jnp.tanh(pre * (1.0 / GATE_SOFTCAP))
        log_f = -_softplus(-capped)
        b_all = _chunk_cumsum(log_f)
        ig = _dot_sel(capped, exp_ref[0])
        b = _dot_sel(b_all, exp_ref[1])
        pos = _iota2((t, WIDTH), 0) & (c - 1)
        cm = ig - b
        for sh in (1, 2, 4, 8, 16, 32):
            cm = jnp.where(pos >= sh, jnp.maximum(cm, pltpu.roll(cm, sh, axis=0)), cm)
        ig_s[a] = ig
        b_s[a] = b
        cm_s[a] = cm
        return carry

    lax.fori_loop(0, nb, front, 0)

    row = _iota2((c, WIDTH), 0)
    col = _iota2((c, WIDTH), 1) & (c - 1)
    causal = col <= row
    eye = col == row
    nw = nw_ref[...]
    ones = jnp.ones((c, WIDTH), BF16)
    bd = _block_diag

    def chunk_step(i, carry):
        n = range(nb)
        r = pl.ds(pl.multiple_of(i * c, c), c)
        last = pl.ds(i * c + c - 1, 1)
        qb = [m_ref[a, r, 0:WIDTH].astype(BF16) for a in n]
        k = [m_ref[a, r, WIDTH:2 * WIDTH] * (HEAD_DIM ** -0.5) for a in n]
        vb = [m_ref[a, r, 2 * WIDTH:3 * WIDTH].astype(BF16) for a in n]
        ig = [ig_s[a, r, :] for a in n]
        b = [b_s[a, r, :] for a in n]
        m_intra = [b[a] + cm_s[a, r, :] for a in n]
        b_last = [b_s[a, last, :] for a in n]
        m_chunk = [b_last[a] + cm_s[a, last, :] for a in n]
        qk = [_dot_nt(qb[a], bd(k[a].astype(BF16))) for a in n]
        gate_row = [jnp.sum(jnp.where(eye, ig[a] - b[a], 0.0), axis=0, keepdims=True) for a in n]
        qk = [qk[a] * jnp.exp(jnp.where(causal, b[a] + gate_row[a], -jnp.inf) - m_intra[a]) for a in n]
        num_intra = [_dot(qk[a].astype(BF16), bd(vb[a])) for a in n]
        den_intra = [_seg_sum(qk[a], block_ones) for a in n]
        m_prev = [mx_ref[a] for a in n]
        c_st = [c_ref[a] for a in n]
        n_st = [n_ref[a] for a in n]
        inter = [_dot(qb[a], jnp.concatenate([c_st[a].astype(BF16), n_st[a].astype(BF16)], axis=1))
                 for a in n]
        ke = [(k[a] * jnp.exp(b_last[a] - b[a] + ig[a] - m_chunk[a])).astype(BF16) for a in n]
        own = [_dot_tn(ke[a], jnp.concatenate([vb[a], ones], axis=1)) for a in n]
        for a in n:
            m_new = jnp.maximum(b_last[a] + m_prev[a], m_chunk[a])
            s_old = jnp.exp(b_last[a] + m_prev[a] - m_new)
            s_new = jnp.exp(m_chunk[a] - m_new)
            c_ref[a] = s_old * c_st[a] + s_new * jnp.where(head_mask, own[a][:, 0:WIDTH], 0.0)
            n_ref[a] = s_old * n_st[a] + s_new * jnp.where(head_mask, own[a][:, WIDTH:2 * WIDTH], 0.0)
            mx_ref[a] = m_new
        h = []
        for a in n:
            pre_m = b[a] + m_prev[a]
            m_t = jnp.maximum(pre_m, m_intra[a])
            s_inter = jnp.exp(pre_m - m_t)
            s_intra = jnp.exp(m_intra[a] - m_t)
            num = s_inter * inter[a][:, 0:WIDTH] + s_intra * num_intra[a]
            den = s_inter * inter[a][:, WIDTH:2 * WIDTH] + s_intra * den_intra[a]
            h.append(num / jnp.maximum(jnp.abs(den), jnp.exp(-m_t)))
        ms = [_dot((h[a] * h[a]).astype(BF16), block_ones) * (1.0 / HEAD_DIM) for a in n]
        for a in n:
            o_ref[a, r, :] = (h[a] * lax.rsqrt(ms[a] + EPS) * nw
                              * _sigmoid(m_ref[a, r, 3 * WIDTH:4 * WIDTH]))
        return carry

    lax.fori_loop(0, t // c, chunk_step, 0)


def _mlstm(mqkvo, gates, par, norm_w, layer, bsz, seq, t, nb):
    row = lambda n: pl.BlockSpec((nb, t, n), lambda b, j: (b, j, 0))
    lay = lambda a, b2: pl.BlockSpec((None, a, b2), lambda b, j: (layer, 0, 0))
    const = lambda shape: pl.BlockSpec(shape, lambda b, j: (0,) * len(shape))
    seq3 = lambda a: a.reshape(bsz, seq, a.shape[-1])
    f32_buf = pltpu.VMEM((nb, t, WIDTH), F32)
    out = pl.pallas_call(
        functools.partial(_mlstm_kernel, t=t, nb=nb),
        grid=(bsz // nb, seq // t),
        in_specs=[row(4 * WIDTH), row(GATE_LANES), lay(1, GATE_LANES), lay(1, WIDTH),
                  const((2, GATE_LANES, WIDTH)), const((WIDTH, WIDTH))],
        out_specs=row(WIDTH),
        out_shape=jax.ShapeDtypeStruct((bsz, seq, WIDTH), F32),
        scratch_shapes=[
            pltpu.VMEM((nb, WIDTH, WIDTH), F32),
            pltpu.VMEM((nb, WIDTH, WIDTH), F32),
            pltpu.VMEM((nb, 1, WIDTH), F32),
            f32_buf, f32_buf, f32_buf,
        ],
        compiler_params=pltpu.CompilerParams(
            dimension_semantics=("arbitrary", "arbitrary"), vmem_limit_bytes=VMEM_LIMIT),
    )(seq3(mqkvo), seq3(gates), par, norm_w,
      jnp.stack([_expander(2 * HEADS), _expander(3 * HEADS)]),
      _head_mask().astype(BF16))
    return out.reshape(bsz * seq, WIDTH)


def _swa_kernel(q_ref, kvc_ref, kvp_ref, sink_ref, o_ref, *, tq):
    tb = SWA_BLOCK
    kvw = SWA_KV_WIDTH
    nq = tq // tb
    first = pl.program_id(1) == 0
    kcat = jnp.concatenate([kvp_ref[:, 0:kvw], kvc_ref[:, 0:kvw]], axis=0)
    vcat = jnp.concatenate([kvp_ref[:, kvw:2 * kvw], kvc_ref[:, kvw:2 * kvw]], axis=0)
    v_t = vcat.T.astype(BF16)
    kr = pltpu.roll(kcat, HEAD_DIM, axis=1)
    lo = _iota2((tb + tq, kvw), 1) < HEAD_DIM

    def place(x_lo, x_hi):
        return jnp.where(lo, x_lo, 0.0).astype(BF16), jnp.where(lo, 0.0, x_hi).astype(BF16)

    k_placed = (place(kcat, kr), place(kr, kcat))
    ki = _iota2((2 * tb, tb), 0)
    qi = _iota2((2 * tb, tb), 1)
    in_window = (ki > qi) & (ki <= qi + tb)

    chains = [(i, g) for i in range(nq) for g in range(2)]
    s_t = []
    for i, g in chains:
        keys = jnp.concatenate([k_placed[g][0][i * tb:(i + 2) * tb],
                                k_placed[g][1][i * tb:(i + 2) * tb]], axis=0)
        qs = jnp.concatenate([q_ref[i * tb:(i + 1) * tb, 256 * g:256 * g + 128],
                              q_ref[i * tb:(i + 1) * tb, 256 * g + 128:256 * g + 256]], axis=0)
        s_t.append(_dot_nt(keys, qs.astype(BF16)))
    p_all, inv_all = [], []
    for (i, g), sc in zip(chains, s_t):
        mask = in_window & (ki >= jnp.where(first, tb, 0)) if i == 0 else in_window
        ps, invs = [], []
        for part in range(2):
            for e in range(2):
                sink = sink_ref[2 * g + part:2 * g + part + 1, e * 2 * tb:e * 2 * tb + 1]
                se = jnp.where(mask, sc[e * 2 * tb:(e + 1) * 2 * tb, part * tb:(part + 1) * tb],
                               -jnp.inf)
                mx = jnp.maximum(jnp.max(se, axis=0, keepdims=True), sink)
                p = jnp.exp(se - mx)
                ps.append(p.astype(BF16))
                invs.append(1.0 / (jnp.sum(p, axis=0, keepdims=True) + jnp.exp(sink - mx)))
        p_all.append(jnp.concatenate(ps, axis=-1))
        inv_all.append(jnp.concatenate(invs, axis=-1))
    out_t = [_dot(v_t[g * HEAD_DIM:(g + 1) * HEAD_DIM, i * tb:(i + 2) * tb], p) * inv
             for (i, g), p, inv in zip(chains, p_all, inv_all)]
    for (i, g), o in zip(chains, out_t):
        for part in range(2):
            j = 2 * g + part
            pair_t = jnp.concatenate([o[:, 2 * part * tb:(2 * part + 1) * tb],
                                      o[:, (2 * part + 1) * tb:(2 * part + 2) * tb]], axis=0)
            o_ref[i * tb:(i + 1) * tb, 128 * j:128 * (j + 1)] = pair_t.T


def _swa(sq, skv, sinks_e, layer, bsz, seq, tq):
    m = bsz * seq
    nt = seq // tq
    nq = tq // SWA_BLOCK
    nb = seq // SWA_BLOCK
    return pl.pallas_call(
        functools.partial(_swa_kernel, tq=tq),
        grid=(bsz, nt),
        in_specs=[
            pl.BlockSpec((tq, SWA_WIDTH), lambda b, n: (b * nt + n, 0)),
            pl.BlockSpec((tq, 2 * SWA_KV_WIDTH), lambda b, n: (b * nt + n, 0)),
            pl.BlockSpec((SWA_BLOCK, 2 * SWA_KV_WIDTH),
                         lambda b, n: (b * nb + jnp.maximum(n * nq - 1, 0), 0)),
            pl.BlockSpec((None, SWA_WIDTH // 128, 4 * SWA_BLOCK), lambda b, n: (layer, 0, 0)),
        ],
        out_specs=pl.BlockSpec((tq, SWA_WIDTH), lambda b, n: (b * nt + n, 0)),
        out_shape=jax.ShapeDtypeStruct((m, SWA_WIDTH), F32),
        compiler_params=pltpu.CompilerParams(
            dimension_semantics=("arbitrary", "arbitrary"), vmem_limit_bytes=VMEM_LIMIT),
    )(sq, skv, skv, sinks_e)


def _post_kernel(x_ref, ya_ref, yb_ref, yc_ref, p_ref, wo_ref, nmlp_ref, wup_ref, wdn_ref,
                 nple_ref, wg_ref, wp_ref, nfin_ref, o_ref, *, final, tf):
    x = x_ref[...]
    x = x + _dot(ya_ref[...].astype(BF16), wo_ref[0:WIDTH, :])
    x = x + _dot(yb_ref[...].astype(BF16), wo_ref[WIDTH:2 * WIDTH, :])
    x = x + _dot(yc_ref[...].astype(BF16), wo_ref[2 * WIDTH:, :])
    h = _rms(x, nmlp_ref[...]).astype(BF16)
    acc = jnp.zeros_like(x)
    for f in range(0, D_FF, tf):
        u = jnp.maximum(_dot(h, wup_ref[:, f:f + tf]), 0.0)
        acc = acc + _dot((u * u).astype(BF16), wdn_ref[f:f + tf, :])
    x = x + acc
    gate = _sigmoid(_dot(_rms(x, nple_ref[...]).astype(BF16), wg_ref[...]))
    x = x + gate * _dot(p_ref[...].astype(BF16), wp_ref[...])
    if final:
        x = _rms(x, nfin_ref[...])
    o_ref[...] = x


def _post(x, ya, yb, yc, p, w_out, norm_mlp, w_up, w_down, norm_ple, w_gate, w_proj, norm_final,
          layer, tm, final):
    m = x.shape[0]
    row = lambda n: pl.BlockSpec((tm, n), lambda i: (i, 0))
    lay = lambda a, b: pl.BlockSpec((None, a, b), lambda i: (layer, 0, 0),
                                    pipeline_mode=pl.Buffered(1))
    return pl.pallas_call(
        functools.partial(_post_kernel, final=final, tf=1024),
        grid=(m // tm,),
        in_specs=[
            row(D_MODEL), row(WIDTH), row(WIDTH), row(SWA_WIDTH),
            pl.BlockSpec((None, tm, PLE_DIM), lambda i: (layer, i, 0)),
            lay(D_MODEL, D_MODEL), lay(1, D_MODEL), lay(D_MODEL, D_FF), lay(D_FF, D_MODEL),
            lay(1, D_MODEL), lay(D_MODEL, D_MODEL), lay(PLE_DIM, D_MODEL),
            pl.BlockSpec((1, D_MODEL), lambda i: (0, 0)),
        ],
        out_specs=row(D_MODEL),
        out_shape=jax.ShapeDtypeStruct((m, D_MODEL), F32),
        compiler_params=pltpu.CompilerParams(
            dimension_semantics=("arbitrary",), vmem_limit_bytes=VMEM_LIMIT),
    )(x, ya, yb, yc, p, w_out, norm_mlp, w_up, w_down, norm_ple, w_gate, w_proj, norm_final)


def _pack_w_in(w_in):
    depth = w_in.shape[0]
    pad = jnp.zeros((depth, D_MODEL, GATE_LANES - 4 * HEADS), w_in.dtype)
    return jnp.concatenate(
        [w_in[..., 0:1024], w_in[..., 1032:2056], w_in[..., 2064:2832],
         w_in[..., 1024:1032], w_in[..., 2056:2064], pad], axis=-1).astype(BF16)


def _gate_row(depth, pieces):
    out = jnp.zeros((depth, 1, GATE_LANES), F32)
    for first_col, vals in pieces:
        out = out.at[:, 0, first_col:first_col + HEADS].set(vals.astype(F32))
    return out


def _rope_tables(positions):
    half = ROPE_DIM // 2
    inv_freq = ROPE_THETA ** (-jnp.arange(0, ROPE_DIM, 2, dtype=F32) / ROPE_DIM)
    ang = positions.astype(F32).reshape(-1)[:, None] * inv_freq
    cos, sin = jnp.cos(ang), jnp.sin(ang)
    n = ang.shape[0]
    rest = HEAD_DIM - ROPE_DIM
    cos_t = jnp.concatenate([cos, cos, jnp.ones((n, rest), F32)], axis=-1)
    sin_a = jnp.concatenate([-sin, jnp.zeros((n, half + rest), F32)], axis=-1)
    sin_b = jnp.concatenate([jnp.zeros((n, half), F32), sin, jnp.zeros((n, rest), F32)], axis=-1)
    return tuple(jnp.concatenate([a, a], axis=-1) for a in (cos_t, sin_a, sin_b))


def kernel(x, p, positions, w_in, conv_w, gdn_a_log, gdn_dt_bias, gdn_norm, mlstm_i_bias,
           mlstm_f_bias, mlstm_norm, attn_sinks, w_out, norm_mix, norm_mlp, w_up, w_down,
           norm_ple, w_ple_gate, w_ple_proj, norm_final):
    bsz, seq, d = x.shape
    depth = w_in.shape[0]
    m = bsz * seq
    tm = min(512, seq)
    t_mix = min(512, seq)
    nb_mix = 4 if bsz % 4 == 0 else 1

    cos_t, sin_a, sin_b = _rope_tables(positions)
    w_packed = _pack_w_in(w_in)
    gdn_par = jnp.concatenate([_gate_row(depth, [(HEADS, gdn_a_log)]),
                               _gate_row(depth, [(HEADS, gdn_dt_bias)])], axis=1)
    mlstm_par = _gate_row(depth, [(2 * HEADS, mlstm_i_bias), (3 * HEADS, mlstm_f_bias)])
    gdn_nw = jnp.tile(gdn_norm.astype(F32), (1, HEADS))[:, None, :]
    mlstm_nw = mlstm_norm.astype(F32)[:, None, :]
    sinks_e = jnp.repeat(attn_sinks.astype(F32), 2 * SWA_BLOCK, axis=-1).reshape(
        depth, SWA_WIDTH // 128, 4 * SWA_BLOCK)
    row3 = lambda a: a.astype(F32)[:, None, :]
    wo_b, wup_b, wdn_b = w_out.astype(BF16), w_up.astype(BF16), w_down.astype(BF16)
    wg_b, wp_b = w_ple_gate.astype(BF16), w_ple_proj.astype(BF16)
    nmix, nmlp, nple = row3(norm_mix), row3(norm_mlp), row3(norm_ple)
    nfin = norm_final.astype(F32)[None, :]
    p2 = p.reshape(depth, m, PLE_DIM)

    xf = x.reshape(m, d)
    for i in range(depth):
        gqkv, gz, mqkvo, sq, skv, gates = _proj(xf, nmix, w_packed, cos_t, sin_a, sin_b,
                                                conv_w.astype(F32), i, tm, seq)
        ya = _gdn(gqkv, gz, gates, gdn_par, gdn_nw, i, bsz, seq, t_mix, nb_mix)
        yb = _mlstm(mqkvo, gates, mlstm_par, mlstm_nw, i, bsz, seq, t_mix, nb_mix)
        yc = _swa(sq, skv, sinks_e, i, bsz, seq, min(256, seq))
        xf = _post(xf, ya, yb, yc, p2, wo_b, nmlp, wup_b, wdn_b, nple, wg_b, wp_b, nfin,
                   i, tm, final=(i == depth - 1))
    return xf.reshape(bsz, seq, d)
```

```python
import functools

import jax
import jax.numpy as jnp
from jax import lax
from jax.experimental import pallas as pl
from jax.experimental.pallas import tpu as pltpu

F32 = jnp.float32
BF16 = jnp.bfloat16

D_MODEL = 1024
DEPTH = 4
HEAD_DIM = 64
PLE_DIM = 256
D_FF = 4 * D_MODEL
EPS = 1e-6
HEADS = 4
WIDTH = HEADS * HEAD_DIM
PAIR = 2 * HEAD_DIM
CHUNK = 64
CONV_WIDTH = 4
GATE_SOFTCAP = 15.0
SWA_WIDTH = 512
SWA_KV_WIDTH = 128
SWA_BLOCK = 128
ROPE_DIM = 16
ROPE_THETA = 500000.0
GATE_LANES = 128
IN_COLS_PACKED = 2944
OFF_GQKV, OFF_GZ, OFF_M, OFF_SQ, OFF_SKV, OFF_GATES = 0, 768, 1024, 2048, 2560, 2816

VMEM_LIMIT = 56 * 1024 * 1024


def _dot(a, b):
    return jnp.dot(a, b, preferred_element_type=F32)


def _dot_nt(a, b):
    return lax.dot_general(a, b, (((1,), (1,)), ((), ())), preferred_element_type=F32)


def _dot_tn(a, b):
    return lax.dot_general(a, b, (((0,), (0,)), ((), ())), preferred_element_type=F32)


def _split2(x):
    hi = x.astype(BF16)
    lo = (x - hi.astype(F32)).astype(BF16)
    return hi, lo


def _dot_sel(x, sel):
    hi, lo = _split2(x)
    return _dot(hi, sel) + _dot(lo, sel)


def _chunk_cumsum(x):
    pos = _iota2(x.shape, 0) & (CHUNK - 1)
    sh = 1
    while sh < CHUNK:
        x = x + jnp.where(pos >= sh, pltpu.roll(x, sh, axis=0), 0.0)
        sh *= 2
    return x


def _sigmoid(x):
    return 1.0 / (1.0 + jnp.exp(-x))


def _softplus(x):
    return jnp.maximum(x, 0.0) + jnp.log1p(jnp.exp(-jnp.abs(x)))


def _rms(x, g):
    return x * lax.rsqrt(jnp.mean(x * x, axis=-1, keepdims=True) + EPS) * g


def _iota2(shape, dim):
    return lax.broadcasted_iota(jnp.int32, shape, dim)


def _pairs(x):
    return [x[:, p * PAIR:(p + 1) * PAIR] for p in range(WIDTH // PAIR)]


def _pair_mask():
    return (_iota2((PAIR, PAIR), 0) >> 6) == (_iota2((PAIR, PAIR), 1) >> 6)


def _pair_diag(x):
    low = _iota2(x.shape, 1) < HEAD_DIM
    zero = jnp.zeros((), x.dtype)
    return jnp.concatenate([jnp.where(low, x, zero), jnp.where(low, zero, x)], axis=0)


def _heads_mm(a, b):
    return jnp.concatenate([_dot(ap, _pair_diag(bp)) for ap, bp in zip(_pairs(a), _pairs(b))], axis=1)


def _heads_mm_nt(a, b):
    return jnp.concatenate([_dot_nt(ap, _pair_diag(bp)) for ap, bp in zip(_pairs(a), _pairs(b))],
                           axis=1)


def _heads_sum(x, pair_ones):
    return jnp.concatenate([_dot(xp, pair_ones) for xp in _pairs(x)], axis=1)


def _expander(first_col):
    r = _iota2((GATE_LANES, WIDTH), 0)
    c = _iota2((GATE_LANES, WIDTH), 1)
    return (r == first_col + (c >> 6)).astype(BF16)


def _rope(x, cos_t, sin_a, sin_b, reps):
    n = x.shape[-1]
    if reps > 1:
        cos_t = jnp.concatenate([cos_t] * reps, axis=-1)
        sin_a = jnp.concatenate([sin_a] * reps, axis=-1)
        sin_b = jnp.concatenate([sin_b] * reps, axis=-1)
    half = ROPE_DIM // 2
    x_up = pltpu.roll(x, n - half, axis=1)
    x_dn = pltpu.roll(x, half, axis=1)
    return x * cos_t + x_up * sin_a + x_dn * sin_b


def _proj_kernel(x_ref, g_ref, w_ref, cos_ref, sa_ref, sb_ref, cw_ref, ones_ref,
                 gqkv_ref, gz_ref, m_ref, sq_ref, skv_ref, gates_ref, tail_ref, buf_ref,
                 *, tiles_per_seq):
    t = x_ref.shape[0]

    @pl.when(pl.program_id(0) % tiles_per_seq == 0)
    def _():
        tail_ref[...] = jnp.zeros_like(tail_ref)

    hb = _rms(x_ref[...], g_ref[...]).astype(BF16)
    buf_ref[0:8, :] = tail_ref[...]
    buf_ref[8:8 + t, :] = _dot(hb, w_ref[:, OFF_GQKV:OFF_GZ])
    tail_ref[...] = buf_ref[t:t + 8, :]
    cw = cw_ref[...]
    pair_ones = ones_ref[...]

    def conv_silu(lo, hi):
        acc = buf_ref[5:5 + t, lo:hi] * cw[0:1, lo:hi]
        for j in range(1, CONV_WIDTH):
            acc = acc + buf_ref[5 + j:5 + j + t, lo:hi] * cw[j:j + 1, lo:hi]
        return acc * _sigmoid(acc)

    def l2n(v):
        return v * lax.rsqrt(_heads_sum((v * v).astype(BF16), pair_ones) + EPS)

    gz_ref[...] = _dot(hb, w_ref[:, OFF_GZ:OFF_M])
    q = conv_silu(0, WIDTH)
    m_ref[:, 0:2 * WIDTH] = _dot(hb, w_ref[:, OFF_M:OFF_M + 2 * WIDTH])
    gqkv_ref[:, 0:WIDTH] = l2n(q) * (HEAD_DIM ** -0.5)
    k = conv_silu(WIDTH, 2 * WIDTH)
    m_ref[:, 2 * WIDTH:4 * WIDTH] = _dot(hb, w_ref[:, OFF_M + 2 * WIDTH:OFF_SQ])
    gqkv_ref[:, WIDTH:2 * WIDTH] = l2n(k)
    gates_ref[...] = _dot(hb, w_ref[:, OFF_GATES:IN_COLS_PACKED])
    cos_t, sin_a, sin_b = cos_ref[...], sa_ref[...], sb_ref[...]
    sq = _dot(hb, w_ref[:, OFF_SQ:OFF_SKV])
    gqkv_ref[:, 2 * WIDTH:3 * WIDTH] = conv_silu(2 * WIDTH, 3 * WIDTH)
    sq_ref[...] = _rope(sq, cos_t, sin_a, sin_b, SWA_WIDTH // 128) * (HEAD_DIM ** -0.5)
    kv = _dot(hb, w_ref[:, OFF_SKV:OFF_GATES])
    skv_ref[:, 0:SWA_KV_WIDTH] = _rope(kv[:, 0:SWA_KV_WIDTH], cos_t, sin_a, sin_b, 1)
    skv_ref[:, SWA_KV_WIDTH:] = kv[:, SWA_KV_WIDTH:]


def _proj(x, norm_w, w_packed, cos_t, sin_a, sin_b, conv_w, layer, tm, seq):
    m = x.shape[0]
    row = lambda n: pl.BlockSpec((tm, n), lambda i: (i, 0))
    widths = (768, 256, 1024, SWA_WIDTH, 2 * SWA_KV_WIDTH, GATE_LANES)
    return pl.pallas_call(
        functools.partial(_proj_kernel, tiles_per_seq=seq // tm),
        grid=(m // tm,),
        in_specs=[
            row(D_MODEL),
            pl.BlockSpec((None, 1, D_MODEL), lambda i: (layer, 0, 0)),
            pl.BlockSpec((None, D_MODEL, IN_COLS_PACKED), lambda i: (layer, 0, 0),
                         pipeline_mode=pl.Buffered(1)),
            row(128), row(128), row(128),
            pl.BlockSpec((None, CONV_WIDTH, 3 * WIDTH), lambda i: (layer, 0, 0)),
            pl.BlockSpec((PAIR, PAIR), lambda i: (0, 0)),
        ],
        out_specs=[row(n) for n in widths],
        out_shape=[jax.ShapeDtypeStruct((m, n), F32) for n in widths],
        scratch_shapes=[
            pltpu.VMEM((8, 3 * WIDTH), F32),
            pltpu.VMEM((tm + 8, 3 * WIDTH), F32),
        ],
        compiler_params=pltpu.CompilerParams(
            dimension_semantics=("arbitrary",), vmem_limit_bytes=VMEM_LIMIT),
    )(x, norm_w, w_packed, cos_t, sin_a, sin_b, conv_w, _pair_mask().astype(BF16))


def _gdn_kernel(qkv_ref, z_ref, gt_ref, par_ref, nw_ref, exp_ref, ones_ref, o_ref,
                state_ref, beta_s, gc_s, u_s, w_s, qd_s, kd_s, qkd_s, *, t, group, nb):
    c = CHUNK

    @pl.when(pl.program_id(1) == 0)
    def _():
        state_ref[...] = jnp.zeros_like(state_ref)

    pair_mask = _pair_mask()
    pair_ones = ones_ref[...]

    def front(a, carry):
        gates = gt_ref[a]
        par = par_ref[...]
        beta = _sigmoid(gates)
        g = -jnp.exp(par[0:1, :]) * _softplus(gates + par[1:2, :])
        gc = _chunk_cumsum(g)
        beta_s[a] = _dot_sel(beta, exp_ref[0])
        gc_s[a] = _dot_sel(gc, exp_ref[1])
        return carry

    lax.fori_loop(0, nb, front, 0)

    row = _iota2((c, WIDTH), 0)
    col = _iota2((c, WIDTH), 1) & (c - 1)
    causal = col <= row
    strict = col < row
    eye = col == row
    nw = nw_ref[...]

    def mm(a, b):
        return _heads_mm(a.astype(BF16), b.astype(BF16))

    groups_per_seq = t // c // group

    def prepare_group(idx, carry):
        a = idx // groups_per_seq
        gi = idx - a * groups_per_seq
        qkv_a, beta_a, gc_a = qkv_ref.at[a], beta_s.at[a], gc_s.at[a]
        u_a, w_a, qd_a, kd_a, qkd_a = u_s.at[a], w_s.at[a], qd_s.at[a], kd_s.at[a], qkd_s.at[a]
        n = range(group)
        rs = [pl.ds(pl.multiple_of((gi * group + j) * c, c), c) for j in n]
        q = [qkv_a[r, 0:WIDTH] for r in rs]
        k = [qkv_a[r, WIDTH:2 * WIDTH] for r in rs]
        gc = [gc_a[r, :] for r in rs]
        g_last = [gc_a[pl.ds((gi * group + j) * c + c - 1, 1), :] for j in n]
        kb = [k[j] * beta_a[rs[j], :] for j in n]
        kkqk = [_heads_mm_nt(jnp.concatenate([kb[j], q[j]], axis=0).astype(BF16), k[j].astype(BF16))
                for j in n]
        gc_row = [jnp.sum(jnp.where(eye, gc[j], 0.0), axis=0, keepdims=True) for j in n]
        decay = [jnp.exp(jnp.where(causal, gc[j] - gc_row[j], -jnp.inf)) for j in n]
        nm = [jnp.where(strict, kkqk[j][0:c] * decay[j], 0.0) for j in n]
        for j in n:
            qkd_a[rs[j], :] = (kkqk[j][c:2 * c] * decay[j]).astype(BF16)
        x = [eye.astype(F32) - jnp.where((row >> 1) == (col >> 1), nm[j], 0.0) for j in n]
        for lb in range(1, 6):
            off = ((row >> (lb + 1)) == (col >> (lb + 1))) & ((row >> lb) != (col >> lb))
            t1 = [mm(jnp.where(off, nm[j], 0.0), x[j]) for j in n]
            t2 = [mm(x[j], t1[j]) for j in n]
            x = [x[j] - t2[j] for j in n]
        xb = [x[j].astype(BF16) for j in n]
        egc = [jnp.exp(gc[j]) for j in n]
        for j in n:
            u_a[rs[j], :] = _heads_mm(
                xb[j], (qkv_a[rs[j], 2 * WIDTH:3 * WIDTH] * beta_a[rs[j], :]).astype(BF16))
        for j in n:
            w_a[rs[j], :] = _heads_mm(xb[j], (kb[j] * egc[j]).astype(BF16)).astype(BF16)
        for j in n:
            qd_a[rs[j], :] = (q[j] * egc[j]).astype(BF16)
            kd_a[rs[j], :] = (k[j] * jnp.exp(g_last[j] - gc[j])).astype(BF16)
        return carry

    lax.fori_loop(0, nb * groups_per_seq, prepare_group, 0)

    def chunk_step(i, carry):
        n = range(nb)
        r = pl.ds(pl.multiple_of(i * c, c), c)
        pairs = range(WIDTH // PAIR)
        g_last = [gc_s[a, pl.ds(i * c + c - 1, 1), :] for a in n]
        s = [[state_ref[a, p] for p in pairs] for a in n]
        lhs = [_pairs(jnp.concatenate([w_s[a, r, :], qd_s[a, r, :]], axis=0)) for a in n]
        ws = [jnp.concatenate([_dot(lhs[a][p], s[a][p].astype(BF16)) for p in pairs], axis=1)
              for a in n]
        v_new = [(u_s[a, r, :] - ws[a][0:c]).astype(BF16) for a in n]
        upd = [[_dot_tn(kd, vn) for kd, vn in zip(_pairs(kd_s[a, r, :]), _pairs(v_new[a]))] for a in n]
        for a in n:
            decay_end = _pairs(jnp.exp(g_last[a]))
            for p in pairs:
                state_ref[a, p] = s[a][p] * decay_end[p] + jnp.where(pair_mask, upd[a][p], 0.0)
        o = [ws[a][c:2 * c] + _heads_mm(qkd_s[a, r, :], v_new[a]) for a in n]
        ms = [_heads_sum((o[a] * o[a]).astype(BF16), pair_ones) * (1.0 / HEAD_DIM) for a in n]
        for a in n:
            z = z_ref[a, r, :]
            o_ref[a, r, :] = o[a] * lax.rsqrt(ms[a] + EPS) * nw * (z * _sigmoid(z))
        return carry

    lax.fori_loop(0, t // c, chunk_step, 0)


def _gdn(gqkv, gz, gates, par, norm_w, layer, bsz, seq, t, nb):
    row = lambda n: pl.BlockSpec((nb, t, n), lambda b, j: (b, j, 0))
    lay = lambda a, b2: pl.BlockSpec((None, a, b2), lambda b, j: (layer, 0, 0))
    const = lambda shape: pl.BlockSpec(shape, lambda b, j: (0,) * len(shape))
    seq3 = lambda a: a.reshape(bsz, seq, a.shape[-1])
    f32_buf = pltpu.VMEM((nb, t, WIDTH), F32)
    bf16_buf = pltpu.VMEM((nb, t, WIDTH), BF16)
    out = pl.pallas_call(
        functools.partial(_gdn_kernel, t=t, group=min(8, t // CHUNK), nb=nb),
        grid=(bsz // nb, seq // t),
        in_specs=[row(3 * WIDTH), row(WIDTH), row(GATE_LANES),
                  lay(2, GATE_LANES), lay(1, WIDTH),
                  const((2, GATE_LANES, WIDTH)), const((PAIR, PAIR))],
        out_specs=row(WIDTH),
        out_shape=jax.ShapeDtypeStruct((bsz, seq, WIDTH), F32),
        scratch_shapes=[
            pltpu.VMEM((nb, WIDTH // PAIR, PAIR, PAIR), F32),
            f32_buf, f32_buf,
            f32_buf,
            bf16_buf, bf16_buf, bf16_buf, bf16_buf,
        ],
        compiler_params=pltpu.CompilerParams(
            dimension_semantics=("arbitrary", "arbitrary"), vmem_limit_bytes=VMEM_LIMIT),
    )(seq3(gqkv), seq3(gz), seq3(gates), par, norm_w,
      jnp.stack([_expander(0), _expander(HEADS)]), _pair_mask().astype(BF16))
    return out.reshape(bsz * seq, WIDTH)


def _mlstm_kernel(m_ref, gt_ref, par_ref, nw_ref, exp_ref, ones_ref, o_ref,
                  cn_ref, mx_ref, ig_s, b_s, cm_s, *, t, nb):
    c = CHUNK

    @pl.when(pl.program_id(1) == 0)
    def _():
        cn_ref[...] = jnp.zeros_like(cn_ref)
        mx_ref[...] = jnp.zeros_like(mx_ref)

    pair_mask = _pair_mask()
    state_mask = jnp.concatenate([pair_mask, pair_mask], axis=1)
    pair_ones = ones_ref[...]

    def front(a, carry):
        pre = gt_ref[a] + par_ref[...]
        capped = GATE_SOFTCAP * jnp.tanh(pre * (1.0 / GATE_SOFTCAP))
        log_f = -_softplus(-capped)
        b_all = _chunk_cumsum(log_f)
        ig = _dot_sel(capped, exp_ref[0])
        b = _dot_sel(b_all, exp_ref[1])
        pos = _iota2((t, WIDTH), 0) & (c - 1)
        cm = ig - b
        for sh in (1, 2, 4, 8, 16, 32):
            cm = jnp.where(pos >= sh, jnp.maximum(cm, pltpu.roll(cm, sh, axis=0)), cm)
        ig_s[a] = ig
        b_s[a] = b
        cm_s[a] = cm
        return carry

    lax.fori_loop(0, nb, front, 0)

    row = _iota2((c, WIDTH), 0)
    col = _iota2((c, WIDTH), 1) & (c - 1)
    causal = col <= row
    eye = col == row
    nw = nw_ref[...]
    ones = jnp.ones((c, PAIR), BF16)

    def chunk_step(i, carry):
        n = range(nb)
        pairs = range(WIDTH // PAIR)
        r = pl.ds(pl.multiple_of(i * c, c), c)
        last = pl.ds(i * c + c - 1, 1)
        qb = [m_ref[a, r, 0:WIDTH].astype(BF16) for a in n]
        k = [m_ref[a, r, WIDTH:2 * WIDTH] * (HEAD_DIM ** -0.5) for a in n]
        vb = [m_ref[a, r, 2 * WIDTH:3 * WIDTH].astype(BF16) for a in n]
        ig = [ig_s[a, r, :] for a in n]
        b = [b_s[a, r, :] for a in n]
        m_intra = [b[a] + cm_s[a, r, :] for a in n]
        b_last = [b_s[a, last, :] for a in n]
        m_chunk = [b_last[a] + cm_s[a, last, :] for a in n]
        qk = [_heads_mm_nt(qb[a], k[a].astype(BF16)) for a in n]
        gate_row = [jnp.sum(jnp.where(eye, ig[a] - b[a], 0.0), axis=0, keepdims=True) for a in n]
        qk = [qk[a] * jnp.exp(jnp.where(causal, b[a] + gate_row[a], -jnp.inf) - m_intra[a]) for a in n]
        qk_parts = [_split2(qk[a]) for a in n]
        num_intra = [_heads_mm(qk_parts[a][0], vb[a]) for a in n]
        den_intra = [_heads_sum(qk_parts[a][0], pair_ones) + _heads_sum(qk_parts[a][1], pair_ones)
                     for a in n]
        m_prev = [mx_ref[a] for a in n]
        cn = [[cn_ref[a, p] for p in pairs] for a in n]
        inter = [[_dot(qp, cn[a][p].astype(BF16)) for p, qp in enumerate(_pairs(qb[a]))] for a in n]
        ke = [(k[a] * jnp.exp(b_last[a] - b[a] + ig[a] - m_chunk[a])).astype(BF16) for a in n]
        own = [[_dot_tn(kp, jnp.concatenate([vp, ones], axis=1))
                for kp, vp in zip(_pairs(ke[a]), _pairs(vb[a]))] for a in n]
        for a in n:
            m_new = jnp.maximum(b_last[a] + m_prev[a], m_chunk[a])
            s_old = _pairs(jnp.exp(b_last[a] + m_prev[a] - m_new))
            s_new = _pairs(jnp.exp(m_chunk[a] - m_new))
            for p in pairs:
                cn_ref[a, p] = (jnp.concatenate([s_old[p], s_old[p]], axis=1) * cn[a][p]
                                + jnp.concatenate([s_new[p], s_new[p]], axis=1)
                                * jnp.where(state_mask, own[a][p], 0.0))
            mx_ref[a] = m_new
        h = []
        for a in n:
            pre_m = b[a] + m_prev[a]
            m_t = jnp.maximum(pre_m, m_intra[a])
            s_inter = jnp.exp(pre_m - m_t)
            s_intra = jnp.exp(m_intra[a] - m_t)
            num_inter = jnp.concatenate([inter[a][p][:, 0:PAIR] for p in pairs], axis=1)
            den_inter = jnp.concatenate([inter[a][p][:, PAIR:2 * PAIR] for p in pairs], axis=1)
            num = s_inter * num_inter + s_intra * num_intra[a]
            den = s_inter * den_inter + s_intra * den_intra[a]
            h.append(num / jnp.maximum(jnp.abs(den), jnp.exp(-m_t)))
        ms = [_heads_sum((h[a] * h[a]).astype(BF16), pair_ones) * (1.0 / HEAD_DIM) for a in n]
        for a in n:
            o_ref[a, r, :] = (h[a] * lax.rsqrt(ms[a] + EPS) * nw
                              * _sigmoid(m_ref[a, r, 3 * WIDTH:4 * WIDTH]))
        return carry

    lax.fori_loop(0, t // c, chunk_step, 0)


def _mlstm(mqkvo, gates, par, norm_w, layer, bsz, seq, t, nb):
    row = lambda n: pl.BlockSpec((nb, t, n), lambda b, j: (b, j, 0))
    lay = lambda a, b2: pl.BlockSpec((None, a, b2), lambda b, j: (layer, 0, 0))
    const = lambda shape: pl.BlockSpec(shape, lambda b, j: (0,) * len(shape))
    seq3 = lambda a: a.reshape(bsz, seq, a.shape[-1])
    f32_buf = pltpu.VMEM((nb, t, WIDTH), F32)
    out = pl.pallas_call(
        functools.partial(_mlstm_kernel, t=t, nb=nb),
        grid=(bsz // nb, seq // t),
        in_specs=[row(4 * WIDTH), row(GATE_LANES), lay(1, GATE_LANES), lay(1, WIDTH),
                  const((2, GATE_LANES, WIDTH)), const((PAIR, PAIR))],
        out_specs=row(WIDTH),
        out_shape=jax.ShapeDtypeStruct((bsz, seq, WIDTH), F32),
        scratch_shapes=[
            pltpu.VMEM((nb, WIDTH // PAIR, PAIR, 2 * PAIR), F32),
            pltpu.VMEM((nb, 1, WIDTH), F32),
            f32_buf, f32_buf, f32_buf,
        ],
        compiler_params=pltpu.CompilerParams(
            dimension_semantics=("arbitrary", "arbitrary"), vmem_limit_bytes=VMEM_LIMIT),
    )(seq3(mqkvo), seq3(gates), par, norm_w,
      jnp.stack([_expander(2 * HEADS), _expander(3 * HEADS)]),
      _pair_mask().astype(BF16))
    return out.reshape(bsz * seq, WIDTH)


def _swa_kernel(q_ref, kvc_ref, kvp_ref, sink_ref, o_ref, *, tq):
    tb = SWA_BLOCK
    kvw = SWA_KV_WIDTH
    nq = tq // tb
    first = pl.program_id(1) == 0
    kcat = jnp.concatenate([kvp_ref[:, 0:kvw], kvc_ref[:, 0:kvw]], axis=0)
    vcat = jnp.concatenate([kvp_ref[:, kvw:2 * kvw], kvc_ref[:, kvw:2 * kvw]], axis=0)
    v_t = vcat.T.astype(BF16)
    kr = pltpu.roll(kcat, HEAD_DIM, axis=1)
    lo = _iota2((tb + tq, kvw), 1) < HEAD_DIM

    def place(x_lo, x_hi):
        return jnp.where(lo, x_lo, 0.0).astype(BF16), jnp.where(lo, 0.0, x_hi).astype(BF16)

    k_placed = (place(kcat, kr), place(kr, kcat))
    ki = _iota2((2 * tb, tb), 0)
    qi = _iota2((2 * tb, tb), 1)
    in_window = (ki > qi) & (ki <= qi + tb)

    chains = [(i, g) for i in range(nq) for g in range(2)]
    s_t = []
    for i, g in chains:
        keys = jnp.concatenate([k_placed[g][0][i * tb:(i + 2) * tb],
                                k_placed[g][1][i * tb:(i + 2) * tb]], axis=0)
        qs = jnp.concatenate([q_ref[i * tb:(i + 1) * tb, 256 * g:256 * g + 128],
                              q_ref[i * tb:(i + 1) * tb, 256 * g + 128:256 * g + 256]], axis=0)
        s_t.append(_dot_nt(keys, qs.astype(BF16)))
    p_all, inv_all = [], []
    for (i, g), sc in zip(chains, s_t):
        mask = in_window & (ki >= jnp.where(first, tb, 0)) if i == 0 else in_window
        ps, invs = [], []
        for part in range(2):
            for e in range(2):
                sink = sink_ref[2 * g + part:2 * g + part + 1, e * 2 * tb:e * 2 * tb + 1]
                se = jnp.where(mask, sc[e * 2 * tb:(e + 1) * 2 * tb, part * tb:(part + 1) * tb],
                               -jnp.inf)
                mx = jnp.maximum(jnp.max(se, axis=0, keepdims=True), sink)
                p = jnp.exp(se - mx)
                ps.append(p.astype(BF16))
                invs.append(1.0 / (jnp.sum(p, axis=0, keepdims=True) + jnp.exp(sink - mx)))
        p_all.append(jnp.concatenate(ps, axis=-1))
        inv_all.append(jnp.concatenate(invs, axis=-1))
    out_t = [_dot(v_t[g * HEAD_DIM:(g + 1) * HEAD_DIM, i * tb:(i + 2) * tb], p) * inv
             for (i, g), p, inv in zip(chains, p_all, inv_all)]
    for (i, g), o in zip(chains, out_t):
        for part in range(2):
            j = 2 * g + part
            pair_t = jnp.concatenate([o[:, 2 * part * tb:(2 * part + 1) * tb],
                                      o[:, (2 * part + 1) * tb:(2 * part + 2) * tb]], axis=0)
            o_ref[i * tb:(i + 1) * tb, 128 * j:128 * (j + 1)] = pair_t.T


def _swa(sq, skv, sinks_e, layer, bsz, seq, tq):
    m = bsz * seq
    nt = seq // tq
    nq = tq // SWA_BLOCK
    nb = seq // SWA_BLOCK
    return pl.pallas_call(
        functools.partial(_swa_kernel, tq=tq),
        grid=(bsz, nt),
        in_specs=[
            pl.BlockSpec((tq, SWA_WIDTH), lambda b, n: (b * nt + n, 0)),
            pl.BlockSpec((tq, 2 * SWA_KV_WIDTH), lambda b, n: (b * nt + n, 0)),
            pl.BlockSpec((SWA_BLOCK, 2 * SWA_KV_WIDTH),
                         lambda b, n: (b * nb + jnp.maximum(n * nq - 1, 0), 0)),
            pl.BlockSpec((None, SWA_WIDTH // 128, 4 * SWA_BLOCK), lambda b, n: (layer, 0, 0)),
        ],
        out_specs=pl.BlockSpec((tq, SWA_WIDTH), lambda b, n: (b * nt + n, 0)),
        out_shape=jax.ShapeDtypeStruct((m, SWA_WIDTH), F32),
        compiler_params=pltpu.CompilerParams(
            dimension_semantics=("arbitrary", "arbitrary"), vmem_limit_bytes=VMEM_LIMIT),
    )(sq, skv, skv, sinks_e)


def _post_kernel(x_ref, ya_ref, yb_ref, yc_ref, p_ref, wo_ref, nmlp_ref, wup_ref, wdn_ref,
                 nple_ref, wg_ref, wp_ref, nfin_ref, o_ref, *, final, tf):
    x = x_ref[...]
    x = x + _dot(ya_ref[...].astype(BF16), wo_ref[0:WIDTH, :])
    x = x + _dot(yb_ref[...].astype(BF16), wo_ref[WIDTH:2 * WIDTH, :])
    x = x + _dot(yc_ref[...].astype(BF16), wo_ref[2 * WIDTH:, :])
    h = _rms(x, nmlp_ref[...]).astype(BF16)
    acc = jnp.zeros_like(x)
    for f in range(0, D_FF, tf):
        u = jnp.maximum(_dot(h, wup_ref[:, f:f + tf]), 0.0)
        acc = acc + _dot((u * u).astype(BF16), wdn_ref[f:f + tf, :])
    x = x + acc
    gate = _sigmoid(_dot(_rms(x, nple_ref[...]).astype(BF16), wg_ref[...]))
    x = x + gate * _dot(p_ref[...].astype(BF16), wp_ref[...])
    if final:
        x = _rms(x, nfin_ref[...])
    o_ref[...] = x


def _post(x, ya, yb, yc, p, w_out, norm_mlp, w_up, w_down, norm_ple, w_gate, w_proj, norm_final,
          layer, tm, final):
    m = x.shape[0]
    row = lambda n: pl.BlockSpec((tm, n), lambda i: (i, 0))
    lay = lambda a, b: pl.BlockSpec((None, a, b), lambda i: (layer, 0, 0),
                                    pipeline_mode=pl.Buffered(1))
    return pl.pallas_call(
        functools.partial(_post_kernel, final=final, tf=1024),
        grid=(m // tm,),
        in_specs=[
            row(D_MODEL), row(WIDTH), row(WIDTH), row(SWA_WIDTH),
            pl.BlockSpec((None, tm, PLE_DIM), lambda i: (layer, i, 0)),
            lay(D_MODEL, D_MODEL), lay(1, D_MODEL), lay(D_MODEL, D_FF), lay(D_FF, D_MODEL),
            lay(1, D_MODEL), lay(D_MODEL, D_MODEL), lay(PLE_DIM, D_MODEL),
            pl.BlockSpec((1, D_MODEL), lambda i: (0, 0)),
        ],
        out_specs=row(D_MODEL),
        out_shape=jax.ShapeDtypeStruct((m, D_MODEL), F32),
        compiler_params=pltpu.CompilerParams(
            dimension_semantics=("arbitrary",), vmem_limit_bytes=VMEM_LIMIT),
    )(x, ya, yb, yc, p, w_out, norm_mlp, w_up, w_down, norm_ple, w_gate, w_proj, norm_final)


def _pack_w_in(w_in):
    depth = w_in.shape[0]
    pad = jnp.zeros((depth, D_MODEL, GATE_LANES - 4 * HEADS), w_in.dtype)
    return jnp.concatenate(
        [w_in[..., 0:1024], w_in[..., 1032:2056], w_in[..., 2064:2832],
         w_in[..., 1024:1032], w_in[..., 2056:2064], pad], axis=-1).astype(BF16)


def _gate_row(depth, pieces):
    out = jnp.zeros((depth, 1, GATE_LANES), F32)
    for first_col, vals in pieces:
        out = out.at[:, 0, first_col:first_col + HEADS].set(vals.astype(F32))
    return out


def _rope_tables(positions):
    half = ROPE_DIM // 2
    inv_freq = ROPE_THETA ** (-jnp.arange(0, ROPE_DIM, 2, dtype=F32) / ROPE_DIM)
    ang = positions.astype(F32).reshape(-1)[:, None] * inv_freq
    cos, sin = jnp.cos(ang), jnp.sin(ang)
    n = ang.shape[0]
    rest = HEAD_DIM - ROPE_DIM
    cos_t = jnp.concatenate([cos, cos, jnp.ones((n, rest), F32)], axis=-1)
    sin_a = jnp.concatenate([-sin, jnp.zeros((n, half + rest), F32)], axis=-1)
    sin_b = jnp.concatenate([jnp.zeros((n, half), F32), sin, jnp.zeros((n, rest), F32)], axis=-1)
    return tuple(jnp.concatenate([a, a], axis=-1) for a in (cos_t, sin_a, sin_b))


def kernel(x, p, positions, w_in, conv_w, gdn_a_log, gdn_dt_bias, gdn_norm, mlstm_i_bias,
           mlstm_f_bias, mlstm_norm, attn_sinks, w_out, norm_mix, norm_mlp, w_up, w_down,
           norm_ple, w_ple_gate, w_ple_proj, norm_final):
    bsz, seq, d = x.shape
    depth = w_in.shape[0]
    m = bsz * seq
    tm = min(512, seq)
    t_mix = min(512, seq)
    nb_mix = 4 if bsz % 4 == 0 else 1

    cos_t, sin_a, sin_b = _rope_tables(positions)
    w_packed = _pack_w_in(w_in)
    gdn_par = jnp.concatenate([_gate_row(depth, [(HEADS, gdn_a_log)]),
                               _gate_row(depth, [(HEADS, gdn_dt_bias)])], axis=1)
    mlstm_par = _gate_row(depth, [(2 * HEADS, mlstm_i_bias), (3 * HEADS, mlstm_f_bias)])
    gdn_nw = jnp.tile(gdn_norm.astype(F32), (1, HEADS))[:, None, :]
    mlstm_nw = mlstm_norm.astype(F32)[:, None, :]
    sinks_e = jnp.repeat(attn_sinks.astype(F32), 2 * SWA_BLOCK, axis=-1).reshape(
        depth, SWA_WIDTH // 128, 4 * SWA_BLOCK)
    row3 = lambda a: a.astype(F32)[:, None, :]
    wo_b, wup_b, wdn_b = w_out.astype(BF16), w_up.astype(BF16), w_down.astype(BF16)
    wg_b, wp_b = w_ple_gate.astype(BF16), w_ple_proj.astype(BF16)
    nmix, nmlp, nple = row3(norm_mix), row3(norm_mlp), row3(norm_ple)
    nfin = norm_final.astype(F32)[None, :]
    p2 = p.reshape(depth, m, PLE_DIM)

    xf = x.reshape(m, d)
    for i in range(depth):
        gqkv, gz, mqkvo, sq, skv, gates = _proj(xf, nmix, w_packed, cos_t, sin_a, sin_b,
                                                conv_w.astype(F32), i, tm, seq)
        ya = _gdn(gqkv, gz, gates, gdn_par, gdn_nw, i, bsz, seq, t_mix, nb_mix)
        yb = _mlstm(mqkvo, gates, mlstm_par, mlstm_nw, i, bsz, seq, t_mix, nb_mix)
        yc = _swa(sq, skv, sinks_e, i, bsz, seq, min(256, seq))
        xf = _post(xf, ya, yb, yc, p2, wo_b, nmlp, wup_b, wdn_b, nple, wg_b, wp_b, nfin,
                   i, tm, final=(i == depth - 1))
    return xf.reshape(bsz, seq, d)
```

```python
import functools

import jax
import jax.numpy as jnp
from jax import lax
from jax.experimental import pallas as pl
from jax.experimental.pallas import tpu as pltpu

F32 = jnp.float32
BF16 = jnp.bfloat16

D_MODEL = 1024
DEPTH = 4
HEAD_DIM = 64
PLE_DIM = 256
D_FF = 4 * D_MODEL
EPS = 1e-6
HEADS = 4
WIDTH = HEADS * HEAD_DIM
PAIR = 2 * HEAD_DIM
CHUNK = 64
CONV_WIDTH = 4
GATE_SOFTCAP = 15.0
SWA_WIDTH = 512
SWA_KV_WIDTH = 128
SWA_BLOCK = 128
ROPE_DIM = 16
ROPE_THETA = 500000.0
GATE_LANES = 128
IN_COLS_PACKED = 2944
OFF_GQKV, OFF_GZ, OFF_M, OFF_SQ, OFF_SKV, OFF_GATES = 0, 768, 1024, 2048, 2560, 2816

VMEM_LIMIT = 56 * 1024 * 1024


def _dot(a, b):
    return jnp.dot(a, b, preferred_element_type=F32)


def _dot_nt(a, b):
    return lax.dot_general(a, b, (((1,), (1,)), ((), ())), preferred_element_type=F32)


def _dot_tn(a, b):
    return lax.dot_general(a, b, (((0,), (0,)), ((), ())), preferred_element_type=F32)


def _split2(x):
    hi = x.astype(BF16)
    lo = (x - hi.astype(F32)).astype(BF16)
    return hi, lo


def _dot_sel(x, sel):
    hi, lo = _split2(x)
    return _dot(hi, sel) + _dot(lo, sel)


def _chunk_cumsum(x):
    pos = _iota2(x.shape, 0) & (CHUNK - 1)
    sh = 1
    while sh < CHUNK:
        x = x + jnp.where(pos >= sh, pltpu.roll(x, sh, axis=0), 0.0)
        sh *= 2
    return x


def _sigmoid(x):
    return 1.0 / (1.0 + jnp.exp(-x))


def _softplus(x):
    return jnp.maximum(x, 0.0) + jnp.log1p(jnp.exp(-jnp.abs(x)))


def _rms(x, g):
    return x * lax.rsqrt(jnp.mean(x * x, axis=-1, keepdims=True) + EPS) * g


def _rms_scale(x):
    return lax.rsqrt(jnp.mean(x * x, axis=-1, keepdims=True) + EPS)


def _iota2(shape, dim):
    return lax.broadcasted_iota(jnp.int32, shape, dim)


def _pairs(x):
    return [x[:, p * PAIR:(p + 1) * PAIR] for p in range(WIDTH // PAIR)]


def _pair_mask():
    return (_iota2((PAIR, PAIR), 0) >> 6) == (_iota2((PAIR, PAIR), 1) >> 6)


def _pair_diag(x):
    low = _iota2(x.shape, 1) < HEAD_DIM
    zero = jnp.zeros((), x.dtype)
    return jnp.concatenate([jnp.where(low, x, zero), jnp.where(low, zero, x)], axis=0)


def _heads_mm(a, b):
    return jnp.concatenate([_dot(ap, _pair_diag(bp)) for ap, bp in zip(_pairs(a), _pairs(b))], axis=1)


def _heads_mm_nt(a, b):
    return jnp.concatenate([_dot_nt(ap, _pair_diag(bp)) for ap, bp in zip(_pairs(a), _pairs(b))],
                           axis=1)


def _heads_sum(x, pair_ones):
    return jnp.concatenate([_dot(xp, pair_ones) for xp in _pairs(x)], axis=1)


def _expander(first_col):
    r = _iota2((GATE_LANES, WIDTH), 0)
    c = _iota2((GATE_LANES, WIDTH), 1)
    return (r == first_col + (c >> 6)).astype(BF16)


def _rope(x, cos_t, sin_t, reps):
    n = x.shape[-1]
    if reps > 1:
        cos_t = jnp.concatenate([cos_t] * reps, axis=-1)
        sin_t = jnp.concatenate([sin_t] * reps, axis=-1)
    half = ROPE_DIM // 2
    x_up = pltpu.roll(x, n - half, axis=1)
    x_dn = pltpu.roll(x, half, axis=1)
    first_half = (_iota2(x.shape, 1) & (HEAD_DIM - 1)) < half
    return x * cos_t + jnp.where(first_half, x_up, x_dn) * sin_t


def _proj_kernel(x_ref, g_ref, w_ref, cos_ref, sin_ref, cw_ref, ones_ref,
                 gqkv_ref, gz_ref, m_ref, sq_ref, skv_ref, gates_ref, tail_ref, buf_ref,
                 *, tiles_per_seq):
    t = x_ref.shape[0]

    @pl.when(pl.program_id(0) % tiles_per_seq == 0)
    def _():
        tail_ref[...] = jnp.zeros_like(tail_ref)

    x = x_ref[...]
    hb = (x * g_ref[...]).astype(BF16)
    scale = _rms_scale(x)

    def proj(lo, hi):
        return scale * _dot(hb, w_ref[:, lo:hi])

    buf_ref[0:8, :] = tail_ref[...]
    buf_ref[8:8 + t, :] = proj(OFF_GQKV, OFF_GZ)
    tail_ref[...] = buf_ref[t:t + 8, :]
    cw = cw_ref[...]
    pair_ones = ones_ref[...]

    def conv_silu(lo, hi):
        acc = buf_ref[5:5 + t, lo:hi] * cw[0:1, lo:hi]
        for j in range(1, CONV_WIDTH):
            acc = acc + buf_ref[5 + j:5 + j + t, lo:hi] * cw[j:j + 1, lo:hi]
        return acc * _sigmoid(acc)

    def l2n(v):
        return v * lax.rsqrt(_heads_sum((v * v).astype(BF16), pair_ones) + EPS)

    cos_t, sin_t = cos_ref[...], sin_ref[...]
    sq = proj(OFF_SQ, OFF_SKV)
    q = conv_silu(0, WIDTH)
    kv = proj(OFF_SKV, OFF_GATES)
    sq_ref[...] = _rope(sq, cos_t, sin_t, SWA_WIDTH // 128) * (HEAD_DIM ** -0.5)
    gqkv_ref[:, 0:WIDTH] = l2n(q) * (HEAD_DIM ** -0.5)
    m_ref[:, 0:2 * WIDTH] = proj(OFF_M, OFF_M + 2 * WIDTH)
    skv_ref[:, 0:SWA_KV_WIDTH] = _rope(kv[:, 0:SWA_KV_WIDTH], cos_t, sin_t, 1)
    skv_ref[:, SWA_KV_WIDTH:] = kv[:, SWA_KV_WIDTH:]
    k = conv_silu(WIDTH, 2 * WIDTH)
    m_ref[:, 2 * WIDTH:4 * WIDTH] = proj(OFF_M + 2 * WIDTH, OFF_SQ)
    gqkv_ref[:, WIDTH:2 * WIDTH] = l2n(k)
    gates_ref[...] = proj(OFF_GATES, IN_COLS_PACKED)
    gqkv_ref[:, 2 * WIDTH:3 * WIDTH] = conv_silu(2 * WIDTH, 3 * WIDTH)
    gz_ref[...] = proj(OFF_GZ, OFF_M)


def _proj(x, norm_w, w_packed, cos_t, sin_t, conv_w, layer, tm, seq):
    m = x.shape[0]
    row = lambda n: pl.BlockSpec((tm, n), lambda i: (i, 0))
    widths = (768, 256, 1024, SWA_WIDTH, 2 * SWA_KV_WIDTH, GATE_LANES)
    return pl.pallas_call(
        functools.partial(_proj_kernel, tiles_per_seq=seq // tm),
        grid=(m // tm,),
        in_specs=[
            row(D_MODEL),
            pl.BlockSpec((None, 1, D_MODEL), lambda i: (layer, 0, 0)),
            pl.BlockSpec((None, D_MODEL, IN_COLS_PACKED), lambda i: (layer, 0, 0),
                         pipeline_mode=pl.Buffered(1)),
            row(128), row(128),
            pl.BlockSpec((None, CONV_WIDTH, 3 * WIDTH), lambda i: (layer, 0, 0)),
            pl.BlockSpec((PAIR, PAIR), lambda i: (0, 0)),
        ],
        out_specs=[row(n) for n in widths],
        out_shape=[jax.ShapeDtypeStruct((m, n), F32) for n in widths],
        scratch_shapes=[
            pltpu.VMEM((8, 3 * WIDTH), F32),
            pltpu.VMEM((tm + 8, 3 * WIDTH), F32),
        ],
        compiler_params=pltpu.CompilerParams(
            dimension_semantics=("arbitrary",), vmem_limit_bytes=VMEM_LIMIT),
    )(x, norm_w, w_packed, cos_t, sin_t, conv_w, _pair_mask().astype(BF16))


def _gdn_kernel(qkv_ref, z_ref, gt_ref, par_ref, nw_ref, exp_ref, ones_ref, o_ref,
                state_ref, beta_s, gc_s, u_s, w_s, qd_s, kd_s, qkd_s, *, t, group, nb):
    c = CHUNK

    @pl.when(pl.program_id(1) == 0)
    def _():
        state_ref[...] = jnp.zeros_like(state_ref)

    pair_mask = _pair_mask()
    pair_ones = ones_ref[...]

    def front(a, carry):
        gates = gt_ref[a]
        par = par_ref[...]
        beta = _sigmoid(gates)
        g = -jnp.exp(par[0:1, :]) * _softplus(gates + par[1:2, :])
        gc = _chunk_cumsum(g)
        beta_s[a] = _dot_sel(beta, exp_ref[0])
        gc_s[a] = _dot_sel(gc, exp_ref[1])
        return carry

    lax.fori_loop(0, nb, front, 0)

    row = _iota2((c, WIDTH), 0)
    col = _iota2((c, WIDTH), 1) & (c - 1)
    causal = col <= row
    strict = col < row
    eye = col == row
    nw = nw_ref[...]

    def mm(a, b):
        return _heads_mm(a.astype(BF16), b.astype(BF16))

    groups_per_seq = t // c // group

    def prepare_group(idx, carry):
        a = idx // groups_per_seq
        gi = idx - a * groups_per_seq
        qkv_a, beta_a, gc_a = qkv_ref.at[a], beta_s.at[a], gc_s.at[a]
        u_a, w_a, qd_a, kd_a, qkd_a = u_s.at[a], w_s.at[a], qd_s.at[a], kd_s.at[a], qkd_s.at[a]
        n = range(group)
        rs = [pl.ds(pl.multiple_of((gi * group + j) * c, c), c) for j in n]
        q = [qkv_a[r, 0:WIDTH] for r in rs]
        k = [qkv_a[r, WIDTH:2 * WIDTH] for r in rs]
        gc = [gc_a[r, :] for r in rs]
        g_last = [gc_a[pl.ds((gi * group + j) * c + c - 1, 1), :] for j in n]
        kb = [k[j] * beta_a[rs[j], :] for j in n]
        kkqk = [_heads_mm_nt(jnp.concatenate([kb[j], q[j]], axis=0).astype(BF16), k[j].astype(BF16))
                for j in n]
        gc_row = [jnp.sum(jnp.where(eye, gc[j], 0.0), axis=0, keepdims=True) for j in n]
        decay = [jnp.exp(jnp.where(causal, gc[j] - gc_row[j], -jnp.inf)) for j in n]
        nm = [jnp.where(strict, kkqk[j][0:c] * decay[j], 0.0) for j in n]
        for j in n:
            qkd_a[rs[j], :] = (kkqk[j][c:2 * c] * decay[j]).astype(BF16)
        x = [eye.astype(F32) - jnp.where((row >> 1) == (col >> 1), nm[j], 0.0) for j in n]
        for lb in range(1, 6):
            off = ((row >> (lb + 1)) == (col >> (lb + 1))) & ((row >> lb) != (col >> lb))
            t1 = [mm(jnp.where(off, nm[j], 0.0), x[j]) for j in n]
            t2 = [mm(x[j], t1[j]) for j in n]
            x = [x[j] - t2[j] for j in n]
        xb = [x[j].astype(BF16) for j in n]
        egc = [jnp.exp(gc[j]) for j in n]
        for j in n:
            u_a[rs[j], :] = _heads_mm(
                xb[j], (qkv_a[rs[j], 2 * WIDTH:3 * WIDTH] * beta_a[rs[j], :]).astype(BF16))
        for j in n:
            w_a[rs[j], :] = _heads_mm(xb[j], (kb[j] * egc[j]).astype(BF16)).astype(BF16)
        for j in n:
            qd_a[rs[j], :] = (q[j] * egc[j]).astype(BF16)
            kd_a[rs[j], :] = (k[j] * jnp.exp(g_last[j] - gc[j])).astype(BF16)
        return carry

    lax.fori_loop(0, nb * groups_per_seq, prepare_group, 0)

    def chunk_step(i, carry):
        n = range(nb)
        r = pl.ds(pl.multiple_of(i * c, c), c)
        pairs = range(WIDTH // PAIR)
        g_last = [gc_s[a, pl.ds(i * c + c - 1, 1), :] for a in n]
        s = [[state_ref[a, p] for p in pairs] for a in n]
        lhs = [_pairs(jnp.concatenate([w_s[a, r, :], qd_s[a, r, :]], axis=0)) for a in n]
        ws = [jnp.concatenate([_dot(lhs[a][p], s[a][p].astype(BF16)) for p in pairs], axis=1)
              for a in n]
        v_new = [(u_s[a, r, :] - ws[a][0:c]).astype(BF16) for a in n]
        upd = [[_dot_tn(kd, vn) for kd, vn in zip(_pairs(kd_s[a, r, :]), _pairs(v_new[a]))] for a in n]
        for a in n:
            decay_end = _pairs(jnp.exp(g_last[a]))
            for p in pairs:
                state_ref[a, p] = s[a][p] * decay_end[p] + jnp.where(pair_mask, upd[a][p], 0.0)
        o = [ws[a][c:2 * c] + _heads_mm(qkd_s[a, r, :], v_new[a]) for a in n]
        ms = [_heads_sum((o[a] * o[a]).astype(BF16), pair_ones) * (1.0 / HEAD_DIM) for a in n]
        for a in n:
            z = z_ref[a, r, :]
            o_ref[a, r, :] = o[a] * lax.rsqrt(ms[a] + EPS) * nw * (z * _sigmoid(z))
        return carry

    lax.fori_loop(0, t // c, chunk_step, 0)


def _gdn(gqkv, gz, gates, par, norm_w, layer, bsz, seq, t, nb):
    row = lambda n: pl.BlockSpec((nb, t, n), lambda b, j: (b, j, 0))
    lay = lambda a, b2: pl.BlockSpec((None, a, b2), lambda b, j: (layer, 0, 0))
    const = lambda shape: pl.BlockSpec(shape, lambda b, j: (0,) * len(shape))
    seq3 = lambda a: a.reshape(bsz, seq, a.shape[-1])
    f32_buf = pltpu.VMEM((nb, t, WIDTH), F32)
    bf16_buf = pltpu.VMEM((nb, t, WIDTH), BF16)
    out = pl.pallas_call(
        functools.partial(_gdn_kernel, t=t, group=min(8, t // CHUNK), nb=nb),
        grid=(bsz // nb, seq // t),
        in_specs=[row(3 * WIDTH), row(WIDTH), row(GATE_LANES),
                  lay(2, GATE_LANES), lay(1, WIDTH),
                  const((2, GATE_LANES, WIDTH)), const((PAIR, PAIR))],
        out_specs=row(WIDTH),
        out_shape=jax.ShapeDtypeStruct((bsz, seq, WIDTH), F32),
        scratch_shapes=[
            pltpu.VMEM((nb, WIDTH // PAIR, PAIR, PAIR), F32),
            f32_buf, f32_buf,
            f32_buf,
            bf16_buf, bf16_buf, bf16_buf, bf16_buf,
        ],
        compiler_params=pltpu.CompilerParams(
            dimension_semantics=("arbitrary", "arbitrary"), vmem_limit_bytes=VMEM_LIMIT),
    )(seq3(gqkv), seq3(gz), seq3(gates), par, norm_w,
      jnp.stack([_expander(0), _expander(HEADS)]), _pair_mask().astype(BF16))
    return out.reshape(bsz * seq, WIDTH)


def _mlstm_kernel(m_ref, gt_ref, par_ref, nw_ref, exp_ref, ones_ref, o_ref,
                  cn_ref, mx_ref, ig_s, b_s, cm_s, *, t, nb):
    c = CHUNK

    @pl.when(pl.program_id(1) == 0)
    def _():
        cn_ref[...] = jnp.zeros_like(cn_ref)
        mx_ref[...] = jnp.zeros_like(mx_ref)

    pair_mask = _pair_mask()
    state_mask = jnp.concatenate([pair_mask, pair_mask], axis=1)
    pair_ones = ones_ref[...]

    def front(a, carry):
        pre = gt_ref[a] + par_ref[...]
        capped = GATE_SOFTCAP * jnp.tanh(pre * (1.0 / GATE_SOFTCAP))
        log_f = -_softplus(-capped)
        b_all = _chunk_cumsum(log_f)
        ig = _dot_sel(capped, exp_ref[0])
        b = _dot_sel(b_all, exp_ref[1])
        pos = _iota2((t, WIDTH), 0) & (c - 1)
        cm = ig - b
        for sh in (1, 2, 4, 8, 16, 32):
            cm = jnp.where(pos >= sh, jnp.maximum(cm, pltpu.roll(cm, sh, axis=0)), cm)
        ig_s[a] = ig
        b_s[a] = b
        cm_s[a] = cm
        return carry

    lax.fori_loop(0, nb, front, 0)

    row = _iota2((c, WIDTH), 0)
    col = _iota2((c, WIDTH), 1) & (c - 1)
    causal = col <= row
    eye = col == row
    nw = nw_ref[...]
    ones = jnp.ones((c, PAIR), BF16)

    def chunk_step(i, carry):
        n = range(nb)
        pairs = range(WIDTH // PAIR)
        r = pl.ds(pl.multiple_of(i * c, c), c)
        last = pl.ds(i * c + c - 1, 1)
        qb = [m_ref[a, r, 0:WIDTH].astype(BF16) for a in n]
        k = [m_ref[a, r, WIDTH:2 * WIDTH] * (HEAD_DIM ** -0.5) for a in n]
        vb = [m_ref[a, r, 2 * WIDTH:3 * WIDTH].astype(BF16) for a in n]
        ig = [ig_s[a, r, :] for a in n]
        b = [b_s[a, r, :] for a in n]
        m_intra = [b[a] + cm_s[a, r, :] for a in n]
        b_last = [b_s[a, last, :] for a in n]
        m_chunk = [b_last[a] + cm_s[a, last, :] for a in n]
        qk = [_heads_mm_nt(qb[a], k[a].astype(BF16)) for a in n]
        gate_row = [jnp.sum(jnp.where(eye, ig[a] - b[a], 0.0), axis=0, keepdims=True) for a in n]
        qk = [qk[a] * jnp.exp(jnp.where(causal, b[a] + gate_row[a], -jnp.inf) - m_intra[a]) for a in n]
        qk_parts = [_split2(qk[a]) for a in n]
        num_intra = [_heads_mm(qk_parts[a][0], vb[a]) for a in n]
        den_intra = [_heads_sum(qk_parts[a][0], pair_ones) + _heads_sum(qk_parts[a][1], pair_ones)
                     for a in n]
        m_prev = [mx_ref[a] for a in n]
        cn = [[cn_ref[a, p] for p in pairs] for a in n]
        inter = [[_dot(qp, cn[a][p].astype(BF16)) for p, qp in enumerate(_pairs(qb[a]))] for a in n]
        ke = [(k[a] * jnp.exp(b_last[a] - b[a] + ig[a] - m_chunk[a])).astype(BF16) for a in n]
        own = [[_dot_tn(kp, jnp.concatenate([vp, ones], axis=1))
                for kp, vp in zip(_pairs(ke[a]), _pairs(vb[a]))] for a in n]
        for a in n:
            m_new = jnp.maximum(b_last[a] + m_prev[a], m_chunk[a])
            s_old = _pairs(jnp.exp(b_last[a] + m_prev[a] - m_new))
            s_new = _pairs(jnp.exp(m_chunk[a] - m_new))
            for p in pairs:
                cn_ref[a, p] = (jnp.concatenate([s_old[p], s_old[p]], axis=1) * cn[a][p]
                                + jnp.concatenate([s_new[p], s_new[p]], axis=1)
                                * jnp.where(state_mask, own[a][p], 0.0))
            mx_ref[a] = m_new
        h = []
        for a in n:
            pre_m = b[a] + m_prev[a]
            m_t = jnp.maximum(pre_m, m_intra[a])
            s_inter = jnp.exp(pre_m - m_t)
            s_intra = jnp.exp(m_intra[a] - m_t)
            num_inter = jnp.concatenate([inter[a][p][:, 0:PAIR] for p in pairs], axis=1)
            den_inter = jnp.concatenate([inter[a][p][:, PAIR:2 * PAIR] for p in pairs], axis=1)
            num = s_inter * num_inter + s_intra * num_intra[a]
            den = s_inter * den_inter + s_intra * den_intra[a]
            h.append(num / jnp.maximum(jnp.abs(den), jnp.exp(-m_t)))
        ms = [_heads_sum((h[a] * h[a]).astype(BF16), pair_ones) * (1.0 / HEAD_DIM) for a in n]
        for a in n:
            o_ref[a, r, :] = (h[a] * lax.rsqrt(ms[a] + EPS) * nw
                              * _sigmoid(m_ref[a, r, 3 * WIDTH:4 * WIDTH]))
        return carry

    lax.fori_loop(0, t // c, chunk_step, 0)


def _mlstm(mqkvo, gates, par, norm_w, layer, bsz, seq, t, nb):
    row = lambda n: pl.BlockSpec((nb, t, n), lambda b, j: (b, j, 0))
    lay = lambda a, b2: pl.BlockSpec((None, a, b2), lambda b, j: (layer, 0, 0))
    const = lambda shape: pl.BlockSpec(shape, lambda b, j: (0,) * len(shape))
    seq3 = lambda a: a.reshape(bsz, seq, a.shape[-1])
    f32_buf = pltpu.VMEM((nb, t, WIDTH), F32)
    out = pl.pallas_call(
        functools.partial(_mlstm_kernel, t=t, nb=nb),
        grid=(bsz // nb, seq // t),
        in_specs=[row(4 * WIDTH), row(GATE_LANES), lay(1, GATE_LANES), lay(1, WIDTH),
                  const((2, GATE_LANES, WIDTH)), const((PAIR, PAIR))],
        out_specs=row(WIDTH),
        out_shape=jax.ShapeDtypeStruct((bsz, seq, WIDTH), F32),
        scratch_shapes=[
            pltpu.VMEM((nb, WIDTH // PAIR, PAIR, 2 * PAIR), F32),
            pltpu.VMEM((nb, 1, WIDTH), F32),
            f32_buf, f32_buf, f32_buf,
        ],
        compiler_params=pltpu.CompilerParams(
            dimension_semantics=("arbitrary", "arbitrary"), vmem_limit_bytes=VMEM_LIMIT),
    )(seq3(mqkvo), seq3(gates), par, norm_w,
      jnp.stack([_expander(2 * HEADS), _expander(3 * HEADS)]),
      _pair_mask().astype(BF16))
    return out.reshape(bsz * seq, WIDTH)


def _swa_kernel(q_ref, kvc_ref, kvp_ref, sink_ref, o_ref, *, tq):
    tb = SWA_BLOCK
    kvw = SWA_KV_WIDTH
    nq = tq // tb
    first = pl.program_id(1) == 0
    kcat = jnp.concatenate([kvp_ref[:, 0:kvw], kvc_ref[:, 0:kvw]], axis=0)
    vcat = jnp.concatenate([kvp_ref[:, kvw:2 * kvw], kvc_ref[:, kvw:2 * kvw]], axis=0)
    v_t = vcat.T.astype(BF16)
    kr = pltpu.roll(kcat, HEAD_DIM, axis=1)
    lo = _iota2((tb + tq, kvw), 1) < HEAD_DIM

    def place(x_lo, x_hi):
        return jnp.where(lo, x_lo, 0.0).astype(BF16), jnp.where(lo, 0.0, x_hi).astype(BF16)

    k_placed = (place(kcat, kr), place(kr, kcat))
    ki = _iota2((2 * tb, tb), 0)
    qi = _iota2((2 * tb, tb), 1)
    in_window = (ki > qi) & (ki <= qi + tb)

    chains = [(i, g) for i in range(nq) for g in range(2)]
    s_t = []
    for i, g in chains:
        keys = jnp.concatenate([k_placed[g][0][i * tb:(i + 2) * tb],
                                k_placed[g][1][i * tb:(i + 2) * tb]], axis=0)
        qs = jnp.concatenate([q_ref[i * tb:(i + 1) * tb, 256 * g:256 * g + 128],
                              q_ref[i * tb:(i + 1) * tb, 256 * g + 128:256 * g + 256]], axis=0)
        s_t.append(_dot_nt(keys, qs.astype(BF16)))
    p_all, inv_all = [], []
    for (i, g), sc in zip(chains, s_t):
        mask = in_window & (ki >= jnp.where(first, tb, 0)) if i == 0 else in_window
        ps, invs = [], []
        for part in range(2):
            for e in range(2):
                sink = sink_ref[2 * g + part:2 * g + part + 1, e * 2 * tb:e * 2 * tb + 1]
                se = jnp.where(mask, sc[e * 2 * tb:(e + 1) * 2 * tb, part * tb:(part + 1) * tb],
                               -jnp.inf)
                mx = jnp.maximum(jnp.max(se, axis=0, keepdims=True), sink)
                p = jnp.exp(se - mx)
                ps.append(p.astype(BF16))
                invs.append(1.0 / (jnp.sum(p, axis=0, keepdims=True) + jnp.exp(sink - mx)))
        p_all.append(jnp.concatenate(ps, axis=-1))
        inv_all.append(jnp.concatenate(invs, axis=-1))
    out_t = [_dot(v_t[g * HEAD_DIM:(g + 1) * HEAD_DIM, i * tb:(i + 2) * tb], p) * inv
             for (i, g), p, inv in zip(chains, p_all, inv_all)]
    for (i, g), o in zip(chains, out_t):
        for part in range(2):
            j = 2 * g + part
            pair_t = jnp.concatenate([o[:, 2 * part * tb:(2 * part + 1) * tb],
                                      o[:, (2 * part + 1) * tb:(2 * part + 2) * tb]], axis=0)
            o_ref[i * tb:(i + 1) * tb, 128 * j:128 * (j + 1)] = pair_t.T


def _swa(sq, skv, sinks_e, layer, bsz, seq, tq):
    m = bsz * seq
    nt = seq // tq
    nq = tq // SWA_BLOCK
    nb = seq // SWA_BLOCK
    return pl.pallas_call(
        functools.partial(_swa_kernel, tq=tq),
        grid=(bsz, nt),
        in_specs=[
            pl.BlockSpec((tq, SWA_WIDTH), lambda b, n: (b * nt + n, 0)),
            pl.BlockSpec((tq, 2 * SWA_KV_WIDTH), lambda b, n: (b * nt + n, 0)),
            pl.BlockSpec((SWA_BLOCK, 2 * SWA_KV_WIDTH),
                         lambda b, n: (b * nb + jnp.maximum(n * nq - 1, 0), 0)),
            pl.BlockSpec((None, SWA_WIDTH // 128, 4 * SWA_BLOCK), lambda b, n: (layer, 0, 0)),
        ],
        out_specs=pl.BlockSpec((tq, SWA_WIDTH), lambda b, n: (b * nt + n, 0)),
        out_shape=jax.ShapeDtypeStruct((m, SWA_WIDTH), F32),
        compiler_params=pltpu.CompilerParams(
            dimension_semantics=("arbitrary", "arbitrary"), vmem_limit_bytes=VMEM_LIMIT),
    )(sq, skv, skv, sinks_e)


def _post_kernel(x_ref, ya_ref, yb_ref, yc_ref, p_ref, wo_ref, nmlp_ref, wup_ref, wdn_ref,
                 nple_ref, wg_ref, wp_ref, nfin_ref, o_ref, *, final, tf):
    x = x_ref[...]
    x = x + _dot(ya_ref[...].astype(BF16), wo_ref[0:WIDTH, :])
    x = x + _dot(yb_ref[...].astype(BF16), wo_ref[WIDTH:2 * WIDTH, :])
    x = x + _dot(yc_ref[...].astype(BF16), wo_ref[2 * WIDTH:, :])
    r = _rms_scale(x)
    h = (x * nmlp_ref[...]).astype(BF16)
    acc = jnp.zeros_like(x)
    for f in range(0, D_FF, tf):
        u = jnp.maximum(_dot(h, wup_ref[:, f:f + tf]), 0.0)
        acc = acc + _dot((u * u).astype(BF16), wdn_ref[f:f + tf, :])
    x = x + (r * r) * acc
    gate = _sigmoid(_rms_scale(x) * _dot((x * nple_ref[...]).astype(BF16), wg_ref[...]))
    x = x + gate * _dot(p_ref[...].astype(BF16), wp_ref[...])
    if final:
        x = _rms(x, nfin_ref[...])
    o_ref[...] = x


def _post(x, ya, yb, yc, p, w_out, norm_mlp, w_up, w_down, norm_ple, w_gate, w_proj, norm_final,
          layer, tm, final):
    m = x.shape[0]
    row = lambda n: pl.BlockSpec((tm, n), lambda i: (i, 0))
    lay = lambda a, b: pl.BlockSpec((None, a, b), lambda i: (layer, 0, 0),
                                    pipeline_mode=pl.Buffered(1))
    return pl.pallas_call(
        functools.partial(_post_kernel, final=final, tf=1024),
        grid=(m // tm,),
        in_specs=[
            row(D_MODEL), row(WIDTH), row(WIDTH), row(SWA_WIDTH),
            pl.BlockSpec((None, tm, PLE_DIM), lambda i: (layer, i, 0)),
            lay(D_MODEL, D_MODEL), lay(1, D_MODEL), lay(D_MODEL, D_FF), lay(D_FF, D_MODEL),
            lay(1, D_MODEL), lay(D_MODEL, D_MODEL), lay(PLE_DIM, D_MODEL),
            pl.BlockSpec((1, D_MODEL), lambda i: (0, 0)),
        ],
        out_specs=row(D_MODEL),
        out_shape=jax.ShapeDtypeStruct((m, D_MODEL), F32),
        compiler_params=pltpu.CompilerParams(
            dimension_semantics=("arbitrary",), vmem_limit_bytes=VMEM_LIMIT),
    )(x, ya, yb, yc, p, w_out, norm_mlp, w_up, w_down, norm_ple, w_gate, w_proj, norm_final)


def _pack_w_in(w_in):
    depth = w_in.shape[0]
    pad = jnp.zeros((depth, D_MODEL, GATE_LANES - 4 * HEADS), w_in.dtype)
    return jnp.concatenate(
        [w_in[..., 0:1024], w_in[..., 1032:2056], w_in[..., 2064:2832],
         w_in[..., 1024:1032], w_in[..., 2056:2064], pad], axis=-1).astype(BF16)


def _gate_row(depth, pieces):
    out = jnp.zeros((depth, 1, GATE_LANES), F32)
    for first_col, vals in pieces:
        out = out.at[:, 0, first_col:first_col + HEADS].set(vals.astype(F32))
    return out


def _rope_tables(positions):
    half = ROPE_DIM // 2
    inv_freq = ROPE_THETA ** (-jnp.arange(0, ROPE_DIM, 2, dtype=F32) / ROPE_DIM)
    dim = jnp.arange(128) % HEAD_DIM
    ang = positions.astype(F32).reshape(-1)[:, None] * inv_freq[dim % half][None, :]
    cos_t = jnp.where(dim < ROPE_DIM, jnp.cos(ang), 1.0)
    sin_t = jnp.where(dim < half, -jnp.sin(ang), jnp.where(dim < ROPE_DIM, jnp.sin(ang), 0.0))
    return cos_t, sin_t


def kernel(x, p, positions, w_in, conv_w, gdn_a_log, gdn_dt_bias, gdn_norm, mlstm_i_bias,
           mlstm_f_bias, mlstm_norm, attn_sinks, w_out, norm_mix, norm_mlp, w_up, w_down,
           norm_ple, w_ple_gate, w_ple_proj, norm_final):
    bsz, seq, d = x.shape
    depth = w_in.shape[0]
    m = bsz * seq
    tm = min(512, seq)
    t_mix = min(512, seq)
    nb_mix = 4 if bsz % 4 == 0 else 1

    cos_t, sin_t = _rope_tables(positions)
    w_packed = _pack_w_in(w_in)
    gdn_par = jnp.concatenate([_gate_row(depth, [(HEADS, gdn_a_log)]),
                               _gate_row(depth, [(HEADS, gdn_dt_bias)])], axis=1)
    mlstm_par = _gate_row(depth, [(2 * HEADS, mlstm_i_bias), (3 * HEADS, mlstm_f_bias)])
    gdn_nw = jnp.tile(gdn_norm.astype(F32), (1, HEADS))[:, None, :]
    mlstm_nw = mlstm_norm.astype(F32)[:, None, :]
    sinks_e = jnp.repeat(attn_sinks.astype(F32), 2 * SWA_BLOCK, axis=-1).reshape(
        depth, SWA_WIDTH // 128, 4 * SWA_BLOCK)
    row3 = lambda a: a.astype(F32)[:, None, :]
    wo_b, wup_b, wdn_b = w_out.astype(BF16), w_up.astype(BF16), w_down.astype(BF16)
    wg_b, wp_b = w_ple_gate.astype(BF16), w_ple_proj.astype(BF16)
    nmix, nmlp, nple = row3(norm_mix), row3(norm_mlp), row3(norm_ple)
    nfin = norm_final.astype(F32)[None, :]
    p2 = p.reshape(depth, m, PLE_DIM)

    xf = x.reshape(m, d)
    for i in range(depth):
        gqkv, gz, mqkvo, sq, skv, gates = _proj(xf, nmix, w_packed, cos_t, sin_t,
                                                conv_w.astype(F32), i, tm, seq)
        ya = _gdn(gqkv, gz, gates, gdn_par, gdn_nw, i, bsz, seq, t_mix, nb_mix)
        yb = _mlstm(mqkvo, gates, mlstm_par, mlstm_nw, i, bsz, seq, t_mix, nb_mix)
        yc = _swa(sq, skv, sinks_e, i, bsz, seq, min(512, seq))
        xf = _post(xf, ya, yb, yc, p2, wo_b, nmlp, wup_b, wdn_b, nple, wg_b, wp_b, nfin,
                   i, tm, final=(i == depth - 1))
    return xf.reshape(bsz, seq, d)
```

```python
import functools

import jax
import jax.numpy as jnp
from jax import lax
from jax.experimental import pallas as pl
from jax.experimental.pallas import tpu as pltpu

F32 = jnp.float32
BF16 = jnp.bfloat16

D_MODEL = 1024
DEPTH = 4
HEAD_DIM = 64
PLE_DIM = 256
D_FF = 4 * D_MODEL
EPS = 1e-6
HEADS = 4
WIDTH = HEADS * HEAD_DIM
PAIR = 2 * HEAD_DIM
CHUNK = 64
CONV_WIDTH = 4
GATE_SOFTCAP = 15.0
SWA_WIDTH = 512
SWA_KV_WIDTH = 128
SWA_BLOCK = 128
ROPE_DIM = 16
ROPE_THETA = 500000.0
GATE_LANES = 128
IN_COLS_PACKED = 2944
OFF_GQKV, OFF_GZ, OFF_M, OFF_SQ, OFF_SKV, OFF_GATES = 0, 768, 1024, 2048, 2560, 2816

VMEM_LIMIT = 56 * 1024 * 1024


def _dot(a, b):
    return jnp.dot(a, b, preferred_element_type=F32)


def _dot_nt(a, b):
    return lax.dot_general(a, b, (((1,), (1,)), ((), ())), preferred_element_type=F32)


def _dot_tn(a, b):
    return lax.dot_general(a, b, (((0,), (0,)), ((), ())), preferred_element_type=F32)


def _split2(x):
    hi = x.astype(BF16)
    lo = (x - hi.astype(F32)).astype(BF16)
    return hi, lo


def _dot_sel(x, sel):
    hi, lo = _split2(x)
    return _dot(hi, sel) + _dot(lo, sel)


def _chunk_cumsum(x):
    pos = _iota2(x.shape, 0) & (CHUNK - 1)
    sh = 1
    while sh < CHUNK:
        x = x + jnp.where(pos >= sh, pltpu.roll(x, sh, axis=0), 0.0)
        sh *= 2
    return x


def _sigmoid(x):
    return 1.0 / (1.0 + jnp.exp(-x))


def _softplus(x):
    return jnp.maximum(x, 0.0) + jnp.log1p(jnp.exp(-jnp.abs(x)))


def _rms(x, g):
    return x * lax.rsqrt(jnp.mean(x * x, axis=-1, keepdims=True) + EPS) * g


def _rms_scale(x):
    return lax.rsqrt(jnp.mean(x * x, axis=-1, keepdims=True) + EPS)


def _iota2(shape, dim):
    return lax.broadcasted_iota(jnp.int32, shape, dim)


def _pairs(x):
    return [x[:, p * PAIR:(p + 1) * PAIR] for p in range(WIDTH // PAIR)]


def _pair_mask():
    return (_iota2((PAIR, PAIR), 0) >> 6) == (_iota2((PAIR, PAIR), 1) >> 6)


def _pair_diag(x):
    low = _iota2(x.shape, 1) < HEAD_DIM
    zero = jnp.zeros((), x.dtype)
    return jnp.concatenate([jnp.where(low, x, zero), jnp.where(low, zero, x)], axis=0)


def _heads_mm(a, b):
    return jnp.concatenate([_dot(ap, _pair_diag(bp)) for ap, bp in zip(_pairs(a), _pairs(b))], axis=1)


def _heads_mm_nt(a, b):
    return jnp.concatenate([_dot_nt(ap, _pair_diag(bp)) for ap, bp in zip(_pairs(a), _pairs(b))],
                           axis=1)


def _heads_sum(x, pair_ones):
    return jnp.concatenate([_dot(xp, pair_ones) for xp in _pairs(x)], axis=1)


def _expander(first_col):
    r = _iota2((GATE_LANES, WIDTH), 0)
    c = _iota2((GATE_LANES, WIDTH), 1)
    return (r == first_col + (c >> 6)).astype(BF16)


def _rope(x, cos_t, sin_t, reps):
    n = x.shape[-1]
    if reps > 1:
        cos_t = jnp.concatenate([cos_t] * reps, axis=-1)
        sin_t = jnp.concatenate([sin_t] * reps, axis=-1)
    half = ROPE_DIM // 2
    x_up = pltpu.roll(x, n - half, axis=1)
    x_dn = pltpu.roll(x, half, axis=1)
    first_half = (_iota2(x.shape, 1) & (HEAD_DIM - 1)) < half
    return x * cos_t + jnp.where(first_half, x_up, x_dn) * sin_t


def _proj_kernel(x_ref, g_ref, w_ref, cos_ref, sin_ref, cw_ref, ones_ref,
                 gqkv_ref, gz_ref, m_ref, sq_ref, skv_ref, gates_ref, tail_ref, buf_ref,
                 *, tiles_per_seq):
    t = x_ref.shape[0]

    @pl.when(pl.program_id(0) % tiles_per_seq == 0)
    def _():
        tail_ref[...] = jnp.zeros_like(tail_ref)

    x = x_ref[...]
    hb = (x * g_ref[...]).astype(BF16)
    scale = _rms_scale(x)

    def proj(lo, hi):
        return scale * _dot(hb, w_ref[:, lo:hi])

    buf_ref[0:8, :] = tail_ref[...]
    buf_ref[8:8 + t, :] = proj(OFF_GQKV, OFF_GZ)
    tail_ref[...] = buf_ref[t:t + 8, :]
    cw = cw_ref[...]
    pair_ones = ones_ref[...]

    def conv_silu(lo, hi):
        acc = buf_ref[5:5 + t, lo:hi] * cw[0:1, lo:hi]
        for j in range(1, CONV_WIDTH):
            acc = acc + buf_ref[5 + j:5 + j + t, lo:hi] * cw[j:j + 1, lo:hi]
        return acc * _sigmoid(acc)

    def l2n(v):
        return v * lax.rsqrt(_heads_sum((v * v).astype(BF16), pair_ones) + EPS)

    cos_t, sin_t = cos_ref[...], sin_ref[...]
    sq = proj(OFF_SQ, OFF_SKV)
    q = conv_silu(0, WIDTH)
    kv = proj(OFF_SKV, OFF_GATES)
    sq_ref[...] = _rope(sq, cos_t, sin_t, SWA_WIDTH // 128) * (HEAD_DIM ** -0.5)
    gqkv_ref[:, 0:WIDTH] = l2n(q) * (HEAD_DIM ** -0.5)
    m_ref[:, 0:2 * WIDTH] = proj(OFF_M, OFF_M + 2 * WIDTH)
    skv_ref[:, 0:SWA_KV_WIDTH] = _rope(kv[:, 0:SWA_KV_WIDTH], cos_t, sin_t, 1)
    skv_ref[:, SWA_KV_WIDTH:] = kv[:, SWA_KV_WIDTH:]
    k = conv_silu(WIDTH, 2 * WIDTH)
    m_ref[:, 2 * WIDTH:4 * WIDTH] = proj(OFF_M + 2 * WIDTH, OFF_SQ)
    gqkv_ref[:, WIDTH:2 * WIDTH] = l2n(k)
    gates_ref[...] = proj(OFF_GATES, IN_COLS_PACKED)
    gqkv_ref[:, 2 * WIDTH:3 * WIDTH] = conv_silu(2 * WIDTH, 3 * WIDTH)
    gz_ref[...] = proj(OFF_GZ, OFF_M)


def _proj(x, norm_w, w_packed, cos_t, sin_t, conv_w, layer, tm, seq):
    m = x.shape[0]
    row = lambda n: pl.BlockSpec((tm, n), lambda i: (i, 0))
    widths = (768, 256, 1024, SWA_WIDTH, 2 * SWA_KV_WIDTH, GATE_LANES)
    return pl.pallas_call(
        functools.partial(_proj_kernel, tiles_per_seq=seq // tm),
        grid=(m // tm,),
        in_specs=[
            row(D_MODEL),
            pl.BlockSpec((None, 1, D_MODEL), lambda i: (layer, 0, 0)),
            pl.BlockSpec((None, D_MODEL, IN_COLS_PACKED), lambda i: (layer, 0, 0),
                         pipeline_mode=pl.Buffered(1)),
            row(128), row(128),
            pl.BlockSpec((None, CONV_WIDTH, 3 * WIDTH), lambda i: (layer, 0, 0)),
            pl.BlockSpec((PAIR, PAIR), lambda i: (0, 0)),
        ],
        out_specs=[row(n) for n in widths],
        out_shape=[jax.ShapeDtypeStruct((m, n), F32) for n in widths],
        scratch_shapes=[
            pltpu.VMEM((8, 3 * WIDTH), F32),
            pltpu.VMEM((tm + 8, 3 * WIDTH), F32),
        ],
        compiler_params=pltpu.CompilerParams(
            dimension_semantics=("arbitrary",), vmem_limit_bytes=VMEM_LIMIT),
    )(x, norm_w, w_packed, cos_t, sin_t, conv_w, _pair_mask().astype(BF16))


def _gdn_kernel(qkv_ref, z_ref, gt_ref, par_ref, nw_ref, exp_ref, ones_ref, o_ref,
                state_ref, beta_s, gc_s, u_s, w_s, qd_s, kd_s, qkd_s, *, t, group, nb):
    c = CHUNK

    @pl.when(pl.program_id(1) == 0)
    def _():
        state_ref[...] = jnp.zeros_like(state_ref)

    pair_ones = ones_ref[...]

    def gate_rows(a, lo, hi):
        gates = gt_ref[a, lo:hi, :]
        par = par_ref[...]
        beta = _sigmoid(gates)
        g = -jnp.exp(par[0:1, :]) * _softplus(gates + par[1:2, :])
        beta_s[a, lo:hi, :] = _dot_sel(beta, exp_ref[0])
        gc_s[a, lo:hi, :] = _dot_sel(_chunk_cumsum(g), exp_ref[1])

    row = _iota2((c, WIDTH), 0)
    col = _iota2((c, WIDTH), 1) & (c - 1)
    causal = col <= row
    strict = col < row
    eye = col == row
    nw = nw_ref[...]

    def mm(a, b):
        return _heads_mm(a.astype(BF16), b.astype(BF16))

    def prepare(a, fillers):
        fillers = list(fillers)

        def fill():
            if fillers:
                fillers.pop(0)()

        n = range(t // c)
        rs = [pl.ds(j * c, c) for j in n]
        q = [qkv_ref[a, r, 0:WIDTH] for r in rs]
        k = [qkv_ref[a, r, WIDTH:2 * WIDTH] for r in rs]
        gc = [gc_s[a, r, :] for r in rs]
        g_last = [gc_s[a, pl.ds(j * c + c - 1, 1), :] for j in n]
        kb = [k[j] * beta_s[a, rs[j], :] for j in n]
        kkqk = [_heads_mm_nt(jnp.concatenate([kb[j], q[j]], axis=0).astype(BF16), k[j].astype(BF16))
                for j in n]
        fill()
        gc_row = [jnp.sum(jnp.where(eye, gc[j], 0.0), axis=0, keepdims=True) for j in n]
        decay = [jnp.exp(jnp.where(causal, gc[j] - gc_row[j], -jnp.inf)) for j in n]
        nm = [jnp.where(strict, kkqk[j][0:c] * decay[j], 0.0) for j in n]
        for j in n:
            qkd_s[a, rs[j], :] = (kkqk[j][c:2 * c] * decay[j]).astype(BF16)
        x = [eye.astype(F32) - jnp.where((row >> 1) == (col >> 1), nm[j], 0.0) for j in n]
        for lb in range(1, 6):
            off = ((row >> (lb + 1)) == (col >> (lb + 1))) & ((row >> lb) != (col >> lb))
            t1 = [mm(jnp.where(off, nm[j], 0.0), x[j]) for j in n]
            fill()
            t2 = [mm(x[j], t1[j]) for j in n]
            if lb % 2 == 0:
                fill()
            x = [x[j] - t2[j] for j in n]
        xb = [x[j].astype(BF16) for j in n]
        egc = [jnp.exp(gc[j]) for j in n]
        for j in n:
            u_s[a, rs[j], :] = _heads_mm(
                xb[j], (qkv_ref[a, rs[j], 2 * WIDTH:3 * WIDTH] * beta_s[a, rs[j], :]).astype(BF16))
        for j in n:
            w_s[a, rs[j], :] = _heads_mm(xb[j], (kb[j] * egc[j]).astype(BF16)).astype(BF16)
        for j in n:
            qd_s[a, rs[j], :] = (q[j] * egc[j]).astype(BF16)
            kd_s[a, rs[j], :] = (k[j] * jnp.exp(g_last[j] - gc[j])).astype(BF16)
        while fillers:
            fill()

    gate_rows(0, 0, t)
    for a in range(nb):
        nxt = [functools.partial(gate_rows, a + 1, j * c, (j + 1) * c) for j in range(t // c)]
        prepare(a, nxt if a + 1 < nb else [])

    def chunk_step(i, carry):
        n = range(nb)
        r = pl.ds(pl.multiple_of(i * c, c), c)
        pairs = range(WIDTH // PAIR)
        g_last = [gc_s[a, pl.ds(i * c + c - 1, 1), :] for a in n]
        s = [[state_ref[a, p] for p in pairs] for a in n]
        lhs = [_pairs(jnp.concatenate([w_s[a, r, :], qd_s[a, r, :]], axis=0)) for a in n]
        ws = [jnp.concatenate([_dot(lhs[a][p], s[a][p].astype(BF16)) for p in pairs], axis=1)
              for a in n]
        v_new = [(u_s[a, r, :] - ws[a][0:c]).astype(BF16) for a in n]
        v_diag = [[_pair_diag(vn) for vn in _pairs(v_new[a])] for a in n]
        upd = [[_dot_tn(_pair_diag(kd), v_diag[a][p]) for p, kd in enumerate(_pairs(kd_s[a, r, :]))]
               for a in n]
        for a in n:
            decay_end = _pairs(jnp.exp(g_last[a]))
            for p in pairs:
                state_ref[a, p] = s[a][p] * decay_end[p] + upd[a][p]
        o = [ws[a][c:2 * c] + jnp.concatenate(
            [_dot(qp, v_diag[a][p]) for p, qp in enumerate(_pairs(qkd_s[a, r, :]))], axis=1) for a in n]
        ms = [_heads_sum((o[a] * o[a]).astype(BF16), pair_ones) * (1.0 / HEAD_DIM) for a in n]
        for a in n:
            z = z_ref[a, r, :]
            o_ref[a, r, :] = o[a] * lax.rsqrt(ms[a] + EPS) * nw * (z * _sigmoid(z))
        return carry

    lax.fori_loop(0, t // c, chunk_step, 0)


def _gdn(gqkv, gz, gates, par, norm_w, layer, bsz, seq, t, nb):
    row = lambda n: pl.BlockSpec((nb, t, n), lambda b, j: (b, j, 0))
    lay = lambda a, b2: pl.BlockSpec((None, a, b2), lambda b, j: (layer, 0, 0))
    const = lambda shape: pl.BlockSpec(shape, lambda b, j: (0,) * len(shape))
    seq3 = lambda a: a.reshape(bsz, seq, a.shape[-1])
    f32_buf = pltpu.VMEM((nb, t, WIDTH), F32)
    bf16_buf = pltpu.VMEM((nb, t, WIDTH), BF16)
    out = pl.pallas_call(
        functools.partial(_gdn_kernel, t=t, group=min(8, t // CHUNK), nb=nb),
        grid=(bsz // nb, seq // t),
        in_specs=[row(3 * WIDTH), row(WIDTH), row(GATE_LANES),
                  lay(2, GATE_LANES), lay(1, WIDTH),
                  const((2, GATE_LANES, WIDTH)), const((PAIR, PAIR))],
        out_specs=row(WIDTH),
        out_shape=jax.ShapeDtypeStruct((bsz, seq, WIDTH), F32),
        scratch_shapes=[
            pltpu.VMEM((nb, WIDTH // PAIR, PAIR, PAIR), F32),
            f32_buf, f32_buf,
            f32_buf,
            bf16_buf, bf16_buf, bf16_buf, bf16_buf,
        ],
        compiler_params=pltpu.CompilerParams(
            dimension_semantics=("arbitrary", "arbitrary"), vmem_limit_bytes=VMEM_LIMIT),
    )(seq3(gqkv), seq3(gz), seq3(gates), par, norm_w,
      jnp.stack([_expander(0), _expander(HEADS)]), _pair_mask().astype(BF16))
    return out.reshape(bsz * seq, WIDTH)


def _mlstm_kernel(m_ref, gt_ref, par_ref, nw_ref, exp_ref, ones_ref, o_ref,
                  cn_ref, mx_ref, ig_s, b_s, cm_s, *, t, nb):
    c = CHUNK

    @pl.when(pl.program_id(1) == 0)
    def _():
        cn_ref[...] = jnp.zeros_like(cn_ref)
        mx_ref[...] = jnp.zeros_like(mx_ref)

    pair_ones = ones_ref[...]

    def front(a, carry):
        pre = gt_ref[a] + par_ref[...]
        capped = GATE_SOFTCAP * jnp.tanh(pre * (1.0 / GATE_SOFTCAP))
        log_f = -_softplus(-capped)
        b_all = _chunk_cumsum(log_f)
        ig = _dot_sel(capped, exp_ref[0])
        b = _dot_sel(b_all, exp_ref[1])
        pos = _iota2((t, WIDTH), 0) & (c - 1)
        cm = ig - b
        for sh in (1, 2, 4, 8, 16, 32):
            cm = jnp.where(pos >= sh, jnp.maximum(cm, pltpu.roll(cm, sh, axis=0)), cm)
        ig_s[a] = ig
        b_s[a] = b
        cm_s[a] = cm
        return carry

    lax.fori_loop(0, nb, front, 0)

    row = _iota2((c, WIDTH), 0)
    col = _iota2((c, WIDTH), 1) & (c - 1)
    causal = col <= row
    eye = col == row
    nw = nw_ref[...]

    def chunk_step(i, carry):
        n = range(nb)
        pairs = range(WIDTH // PAIR)
        r = pl.ds(pl.multiple_of(i * c, c), c)
        last = pl.ds(i * c + c - 1, 1)
        qb = [m_ref[a, r, 0:WIDTH].astype(BF16) for a in n]
        k = [m_ref[a, r, WIDTH:2 * WIDTH] * (HEAD_DIM ** -0.5) for a in n]
        vb = [m_ref[a, r, 2 * WIDTH:3 * WIDTH].astype(BF16) for a in n]
        ig = [ig_s[a, r, :] for a in n]
        b = [b_s[a, r, :] for a in n]
        m_intra = [b[a] + cm_s[a, r, :] for a in n]
        b_last = [b_s[a, last, :] for a in n]
        m_chunk = [b_last[a] + cm_s[a, last, :] for a in n]
        qk = [_heads_mm_nt(qb[a], k[a].astype(BF16)) for a in n]
        gate_row = [jnp.sum(jnp.where(eye, ig[a] - b[a], 0.0), axis=0, keepdims=True) for a in n]
        qk = [qk[a] * jnp.exp(jnp.where(causal, b[a] + gate_row[a], -jnp.inf) - m_intra[a]) for a in n]
        qk_parts = [_split2(qk[a]) for a in n]
        v_diag = [[_pair_diag(vp) for vp in _pairs(vb[a])] for a in n]
        num_intra = [jnp.concatenate([_dot(qp, v_diag[a][p])
                                      for p, qp in enumerate(_pairs(qk_parts[a][0]))], axis=1) for a in n]
        den_intra = [_heads_sum(qk_parts[a][0], pair_ones) + _heads_sum(qk_parts[a][1], pair_ones)
                     for a in n]
        m_prev = [mx_ref[a] for a in n]
        cn = [[cn_ref[a, p] for p in pairs] for a in n]
        inter = [[_dot(qp, cn[a][p].astype(BF16)) for p, qp in enumerate(_pairs(qb[a]))] for a in n]
        ke = [(k[a] * jnp.exp(b_last[a] - b[a] + ig[a] - m_chunk[a])).astype(BF16) for a in n]
        own = [[_dot_tn(_pair_diag(kp), jnp.concatenate([v_diag[a][p], pair_ones], axis=1))
                for p, kp in enumerate(_pairs(ke[a]))] for a in n]
        for a in n:
            m_new = jnp.maximum(b_last[a] + m_prev[a], m_chunk[a])
            s_old = _pairs(jnp.exp(b_last[a] + m_prev[a] - m_new))
            s_new = _pairs(jnp.exp(m_chunk[a] - m_new))
            for p in pairs:
                cn_ref[a, p] = (jnp.concatenate([s_old[p], s_old[p]], axis=1) * cn[a][p]
                                + jnp.concatenate([s_new[p], s_new[p]], axis=1) * own[a][p])
            mx_ref[a] = m_new
        h = []
        for a in n:
            pre_m = b[a] + m_prev[a]
            m_t = jnp.maximum(pre_m, m_intra[a])
            s_inter = jnp.exp(pre_m - m_t)
            s_intra = jnp.exp(m_intra[a] - m_t)
            num_inter = jnp.concatenate([inter[a][p][:, 0:PAIR] for p in pairs], axis=1)
            den_inter = jnp.concatenate([inter[a][p][:, PAIR:2 * PAIR] for p in pairs], axis=1)
            num = s_inter * num_inter + s_intra * num_intra[a]
            den = s_inter * den_inter + s_intra * den_intra[a]
            h.append(num / jnp.maximum(jnp.abs(den), jnp.exp(-m_t)))
        ms = [_heads_sum((h[a] * h[a]).astype(BF16), pair_ones) * (1.0 / HEAD_DIM) for a in n]
        for a in n:
            o_ref[a, r, :] = (h[a] * lax.rsqrt(ms[a] + EPS) * nw
                              * _sigmoid(m_ref[a, r, 3 * WIDTH:4 * WIDTH]))
        return carry

    lax.fori_loop(0, t // c, chunk_step, 0)


def _mlstm(mqkvo, gates, par, norm_w, layer, bsz, seq, t, nb):
    row = lambda n: pl.BlockSpec((nb, t, n), lambda b, j: (b, j, 0))
    lay = lambda a, b2: pl.BlockSpec((None, a, b2), lambda b, j: (layer, 0, 0))
    const = lambda shape: pl.BlockSpec(shape, lambda b, j: (0,) * len(shape))
    seq3 = lambda a: a.reshape(bsz, seq, a.shape[-1])
    f32_buf = pltpu.VMEM((nb, t, WIDTH), F32)
    out = pl.pallas_call(
        functools.partial(_mlstm_kernel, t=t, nb=nb),
        grid=(bsz // nb, seq // t),
        in_specs=[row(4 * WIDTH), row(GATE_LANES), lay(1, GATE_LANES), lay(1, WIDTH),
                  const((2, GATE_LANES, WIDTH)), const((PAIR, PAIR))],
        out_specs=row(WIDTH),
        out_shape=jax.ShapeDtypeStruct((bsz, seq, WIDTH), F32),
        scratch_shapes=[
            pltpu.VMEM((nb, WIDTH // PAIR, PAIR, 2 * PAIR), F32),
            pltpu.VMEM((nb, 1, WIDTH), F32),
            f32_buf, f32_buf, f32_buf,
        ],
        compiler_params=pltpu.CompilerParams(
            dimension_semantics=("arbitrary", "arbitrary"), vmem_limit_bytes=VMEM_LIMIT),
    )(seq3(mqkvo), seq3(gates), par, norm_w,
      jnp.stack([_expander(2 * HEADS), _expander(3 * HEADS)]),
      _pair_mask().astype(BF16))
    return out.reshape(bsz * seq, WIDTH)


def _swa_kernel(q_ref, kvc_ref, kvp_ref, sink_ref, o_ref, *, tq):
    tb = SWA_BLOCK
    kvw = SWA_KV_WIDTH
    nq = tq // tb
    first = pl.program_id(1) == 0
    kcat = jnp.concatenate([kvp_ref[:, 0:kvw], kvc_ref[:, 0:kvw]], axis=0)
    vcat = jnp.concatenate([kvp_ref[:, kvw:2 * kvw], kvc_ref[:, kvw:2 * kvw]], axis=0)
    v_t = vcat.T.astype(BF16)
    kr = pltpu.roll(kcat, HEAD_DIM, axis=1)
    lo = _iota2((tb + tq, kvw), 1) < HEAD_DIM

    def place(x_lo, x_hi):
        return jnp.where(lo, x_lo, 0.0).astype(BF16), jnp.where(lo, 0.0, x_hi).astype(BF16)

    k_placed = (place(kcat, kr), place(kr, kcat))
    ki = _iota2((2 * tb, tb), 0)
    qi = _iota2((2 * tb, tb), 1)
    in_window = (ki > qi) & (ki <= qi + tb)

    chains = [(i, g) for i in range(nq) for g in range(2)]
    s_t = []
    for i, g in chains:
        keys = jnp.concatenate([k_placed[g][0][i * tb:(i + 2) * tb],
                                k_placed[g][1][i * tb:(i + 2) * tb]], axis=0)
        qs = jnp.concatenate([q_ref[i * tb:(i + 1) * tb, 256 * g:256 * g + 128],
                              q_ref[i * tb:(i + 1) * tb, 256 * g + 128:256 * g + 256]], axis=0)
        s_t.append(_dot_nt(keys, qs.astype(BF16)))
    p_all, inv_all = [], []
    for (i, g), sc in zip(chains, s_t):
        mask = in_window & (ki >= jnp.where(first, tb, 0)) if i == 0 else in_window
        ps, invs = [], []
        for part in range(2):
            for e in range(2):
                sink = sink_ref[2 * g + part:2 * g + part + 1, e * 2 * tb:e * 2 * tb + 1]
                se = jnp.where(mask, sc[e * 2 * tb:(e + 1) * 2 * tb, part * tb:(part + 1) * tb],
                               -jnp.inf)
                mx = jnp.maximum(jnp.max(se, axis=0, keepdims=True), sink)
                p = jnp.exp(se - mx)
                ps.append(p.astype(BF16))
                invs.append(1.0 / (jnp.sum(p, axis=0, keepdims=True) + jnp.exp(sink - mx)))
        p_all.append(jnp.concatenate(ps, axis=-1))
        inv_all.append(jnp.concatenate(invs, axis=-1))
    out_t = [_dot(v_t[g * HEAD_DIM:(g + 1) * HEAD_DIM, i * tb:(i + 2) * tb], p) * inv
             for (i, g), p, inv in zip(chains, p_all, inv_all)]
    for (i, g), o in zip(chains, out_t):
        for part in range(2):
            j = 2 * g + part
            pair_t = jnp.concatenate([o[:, 2 * part * tb:(2 * part + 1) * tb],
                                      o[:, (2 * part + 1) * tb:(2 * part + 2) * tb]], axis=0)
            o_ref[i * tb:(i + 1) * tb, 128 * j:128 * (j + 1)] = pair_t.T


def _swa(sq, skv, sinks_e, layer, bsz, seq, tq):
    m = bsz * seq
    nt = seq // tq
    nq = tq // SWA_BLOCK
    nb = seq // SWA_BLOCK
    return pl.pallas_call(
        functools.partial(_swa_kernel, tq=tq),
        grid=(bsz, nt),
        in_specs=[
            pl.BlockSpec((tq, SWA_WIDTH), lambda b, n: (b * nt + n, 0)),
            pl.BlockSpec((tq, 2 * SWA_KV_WIDTH), lambda b, n: (b * nt + n, 0)),
            pl.BlockSpec((SWA_BLOCK, 2 * SWA_KV_WIDTH),
                         lambda b, n: (b * nb + jnp.maximum(n * nq - 1, 0), 0)),
            pl.BlockSpec((None, SWA_WIDTH // 128, 4 * SWA_BLOCK), lambda b, n: (layer, 0, 0)),
        ],
        out_specs=pl.BlockSpec((tq, SWA_WIDTH), lambda b, n: (b * nt + n, 0)),
        out_shape=jax.ShapeDtypeStruct((m, SWA_WIDTH), F32),
        compiler_params=pltpu.CompilerParams(
            dimension_semantics=("arbitrary", "arbitrary"), vmem_limit_bytes=VMEM_LIMIT),
    )(sq, skv, skv, sinks_e)


def _post_kernel(x_ref, ya_ref, yb_ref, yc_ref, p_ref, wo_ref, nmlp_ref, wup_ref, wdn_ref,
                 nple_ref, wg_ref, wp_ref, nfin_ref, o_ref, *, final, tf):
    x = x_ref[...]
    x = x + _dot(ya_ref[...].astype(BF16), wo_ref[0:WIDTH, :])
    x = x + _dot(yb_ref[...].astype(BF16), wo_ref[WIDTH:2 * WIDTH, :])
    x = x + _dot(yc_ref[...].astype(BF16), wo_ref[2 * WIDTH:, :])
    r = _rms_scale(x)
    h = (x * nmlp_ref[...]).astype(BF16)
    acc = jnp.zeros_like(x)
    for f in range(0, D_FF, tf):
        u = jnp.maximum(_dot(h, wup_ref[:, f:f + tf]), 0.0)
        acc = acc + _dot((u * u).astype(BF16), wdn_ref[f:f + tf, :])
    x = x + (r * r) * acc
    gate = _sigmoid(_rms_scale(x) * _dot((x * nple_ref[...]).astype(BF16), wg_ref[...]))
    x = x + gate * _dot(p_ref[...].astype(BF16), wp_ref[...])
    if final:
        x = _rms(x, nfin_ref[...])
    o_ref[...] = x


def _post(x, ya, yb, yc, p, w_out, norm_mlp, w_up, w_down, norm_ple, w_gate, w_proj, norm_final,
          layer, tm, final):
    m = x.shape[0]
    row = lambda n: pl.BlockSpec((tm, n), lambda i: (i, 0))
    lay = lambda a, b: pl.BlockSpec((None, a, b), lambda i: (layer, 0, 0),
                                    pipeline_mode=pl.Buffered(1))
    return pl.pallas_call(
        functools.partial(_post_kernel, final=final, tf=1024),
        grid=(m // tm,),
        in_specs=[
            row(D_MODEL), row(WIDTH), row(WIDTH), row(SWA_WIDTH),
            pl.BlockSpec((None, tm, PLE_DIM), lambda i: (layer, i, 0)),
            lay(D_MODEL, D_MODEL), lay(1, D_MODEL), lay(D_MODEL, D_FF), lay(D_FF, D_MODEL),
            lay(1, D_MODEL), lay(D_MODEL, D_MODEL), lay(PLE_DIM, D_MODEL),
            pl.BlockSpec((1, D_MODEL), lambda i: (0, 0)),
        ],
        out_specs=row(D_MODEL),
        out_shape=jax.ShapeDtypeStruct((m, D_MODEL), F32),
        compiler_params=pltpu.CompilerParams(
            dimension_semantics=("arbitrary",), vmem_limit_bytes=VMEM_LIMIT),
    )(x, ya, yb, yc, p, w_out, norm_mlp, w_up, w_down, norm_ple, w_gate, w_proj, norm_final)


def _pack_w_in(w_in):
    depth = w_in.shape[0]
    pad = jnp.zeros((depth, D_MODEL, GATE_LANES - 4 * HEADS), w_in.dtype)
    return jnp.concatenate(
        [w_in[..., 0:1024], w_in[..., 1032:2056], w_in[..., 2064:2832],
         w_in[..., 1024:1032], w_in[..., 2056:2064], pad], axis=-1).astype(BF16)


def _gate_row(depth, pieces):
    out = jnp.zeros((depth, 1, GATE_LANES), F32)
    for first_col, vals in pieces:
        out = out.at[:, 0, first_col:first_col + HEADS].set(vals.astype(F32))
    return out


def _rope_tables(positions):
    half = ROPE_DIM // 2
    inv_freq = ROPE_THETA ** (-jnp.arange(0, ROPE_DIM, 2, dtype=F32) / ROPE_DIM)
    dim = jnp.arange(128) % HEAD_DIM
    ang = positions.astype(F32).reshape(-1)[:, None] * inv_freq[dim % half][None, :]
    cos_t = jnp.where(dim < ROPE_DIM, jnp.cos(ang), 1.0)
    sin_t = jnp.where(dim < half, -jnp.sin(ang), jnp.where(dim < ROPE_DIM, jnp.sin(ang), 0.0))
    return cos_t, sin_t


def kernel(x, p, positions, w_in, conv_w, gdn_a_log, gdn_dt_bias, gdn_norm, mlstm_i_bias,
           mlstm_f_bias, mlstm_norm, attn_sinks, w_out, norm_mix, norm_mlp, w_up, w_down,
           norm_ple, w_ple_gate, w_ple_proj, norm_final):
    bsz, seq, d = x.shape
    depth = w_in.shape[0]
    m = bsz * seq
    tm = min(512, seq)
    t_mix = min(512, seq)
    nb_mix = 4 if bsz % 4 == 0 else 1

    cos_t, sin_t = _rope_tables(positions)
    w_packed = _pack_w_in(w_in)
    gdn_par = jnp.concatenate([_gate_row(depth, [(HEADS, gdn_a_log)]),
                               _gate_row(depth, [(HEADS, gdn_dt_bias)])], axis=1)
    mlstm_par = _gate_row(depth, [(2 * HEADS, mlstm_i_bias), (3 * HEADS, mlstm_f_bias)])
    gdn_nw = jnp.tile(gdn_norm.astype(F32), (1, HEADS))[:, None, :]
    mlstm_nw = mlstm_norm.astype(F32)[:, None, :]
    sinks_e = jnp.repeat(attn_sinks.astype(F32), 2 * SWA_BLOCK, axis=-1).reshape(
        depth, SWA_WIDTH // 128, 4 * SWA_BLOCK)
    row3 = lambda a: a.astype(F32)[:, None, :]
    wo_b, wup_b, wdn_b = w_out.astype(BF16), w_up.astype(BF16), w_down.astype(BF16)
    wg_b, wp_b = w_ple_gate.astype(BF16), w_ple_proj.astype(BF16)
    nmix, nmlp, nple = row3(norm_mix), row3(norm_mlp), row3(norm_ple)
    nfin = norm_final.astype(F32)[None, :]
    p2 = p.reshape(depth, m, PLE_DIM)

    xf = x.reshape(m, d)
    for i in range(depth):
        gqkv, gz, mqkvo, sq, skv, gates = _proj(xf, nmix, w_packed, cos_t, sin_t,
                                                conv_w.astype(F32), i, tm, seq)
        ya = _gdn(gqkv, gz, gates, gdn_par, gdn_nw, i, bsz, seq, t_mix, nb_mix)
        yb = _mlstm(mqkvo, gates, mlstm_par, mlstm_nw, i, bsz, seq, t_mix, nb_mix)
        yc = _swa(sq, skv, sinks_e, i, bsz, seq, min(512, seq))
        xf = _post(xf, ya, yb, yc, p2, wo_b, nmlp, wup_b, wdn_b, nple, wg_b, wp_b, nfin,
                   i, tm, final=(i == depth - 1))
    return xf.reshape(bsz, seq, d)
```

```python
import functools

import jax
import jax.numpy as jnp
from jax import lax
from jax.experimental import pallas as pl
from jax.experimental.pallas import tpu as pltpu

F32 = jnp.float32
BF16 = jnp.bfloat16

D_MODEL = 1024
DEPTH = 4
HEAD_DIM = 64
PLE_DIM = 256
D_FF = 4 * D_MODEL
EPS = 1e-6
HEADS = 4
WIDTH = HEADS * HEAD_DIM
PAIR = 2 * HEAD_DIM
CHUNK = 64
CONV_WIDTH = 4
GATE_SOFTCAP = 15.0
SWA_WIDTH = 512
SWA_KV_WIDTH = 128
SWA_BLOCK = 128
ROPE_DIM = 16
ROPE_THETA = 500000.0
GATE_LANES = 128
IN_COLS_PACKED = 2944
OFF_GQKV, OFF_GZ, OFF_M, OFF_SQ, OFF_SKV, OFF_GATES = 0, 768, 1024, 2048, 2560, 2816

VMEM_LIMIT = 56 * 1024 * 1024


def _dot(a, b):
    return jnp.dot(a, b, preferred_element_type=F32)


def _dot_nt(a, b):
    return lax.dot_general(a, b, (((1,), (1,)), ((), ())), preferred_element_type=F32)


def _dot_tn(a, b):
    return lax.dot_general(a, b, (((0,), (0,)), ((), ())), preferred_element_type=F32)


def _split2(x):
    hi = x.astype(BF16)
    lo = (x - hi.astype(F32)).astype(BF16)
    return hi, lo


def _dot_sel(x, sel):
    hi, lo = _split2(x)
    return _dot(hi, sel) + _dot(lo, sel)


def _chunk_cumsum(x):
    pos = _iota2(x.shape, 0) & (CHUNK - 1)
    sh = 1
    while sh < CHUNK:
        x = x + jnp.where(pos >= sh, pltpu.roll(x, sh, axis=0), 0.0)
        sh *= 2
    return x


def _sigmoid(x):
    return 1.0 / (1.0 + jnp.exp(-x))


def _softplus(x):
    return jnp.maximum(x, 0.0) + jnp.log1p(jnp.exp(-jnp.abs(x)))


def _rms(x, g):
    return x * lax.rsqrt(jnp.mean(x * x, axis=-1, keepdims=True) + EPS) * g


def _rms_scale(x):
    return lax.rsqrt(jnp.mean(x * x, axis=-1, keepdims=True) + EPS)


def _iota2(shape, dim):
    return lax.broadcasted_iota(jnp.int32, shape, dim)


def _pairs(x):
    return [x[:, p * PAIR:(p + 1) * PAIR] for p in range(WIDTH // PAIR)]


def _pair_mask():
    return (_iota2((PAIR, PAIR), 0) >> 6) == (_iota2((PAIR, PAIR), 1) >> 6)


def _pair_diag(x):
    low = _iota2(x.shape, 1) < HEAD_DIM
    zero = jnp.zeros((), x.dtype)
    return jnp.concatenate([jnp.where(low, x, zero), jnp.where(low, zero, x)], axis=0)


def _heads_mm(a, b):
    return jnp.concatenate([_dot(ap, _pair_diag(bp)) for ap, bp in zip(_pairs(a), _pairs(b))], axis=1)


def _heads_mm_nt(a, b):
    return jnp.concatenate([_dot_nt(ap, _pair_diag(bp)) for ap, bp in zip(_pairs(a), _pairs(b))],
                           axis=1)


def _heads_sum(x, pair_ones):
    return jnp.concatenate([_dot(xp, pair_ones) for xp in _pairs(x)], axis=1)


def _expander(first_col):
    r = _iota2((GATE_LANES, WIDTH), 0)
    c = _iota2((GATE_LANES, WIDTH), 1)
    return (r == first_col + (c >> 6)).astype(BF16)


def _rope(x, cos_t, sin_t, reps):
    n = x.shape[-1]
    if reps > 1:
        cos_t = jnp.concatenate([cos_t] * reps, axis=-1)
        sin_t = jnp.concatenate([sin_t] * reps, axis=-1)
    half = ROPE_DIM // 2
    x_up = pltpu.roll(x, n - half, axis=1)
    x_dn = pltpu.roll(x, half, axis=1)
    first_half = (_iota2(x.shape, 1) & (HEAD_DIM - 1)) < half
    return x * cos_t + jnp.where(first_half, x_up, x_dn) * sin_t


def _proj_kernel(x_ref, g_ref, w_ref, cos_ref, sin_ref, cw_ref, ones_ref,
                 gqkv_ref, gz_ref, m_ref, sq_ref, skv_ref, gates_ref, tail_ref, buf_ref,
                 *, tiles_per_seq):
    t = x_ref.shape[0]

    @pl.when(pl.program_id(0) % tiles_per_seq == 0)
    def _():
        tail_ref[...] = jnp.zeros_like(tail_ref)

    x = x_ref[...]
    hb = (x * g_ref[...]).astype(BF16)
    scale = _rms_scale(x)

    def proj(lo, hi):
        return scale * _dot(hb, w_ref[:, lo:hi])

    buf_ref[0:8, :] = tail_ref[...]
    buf_ref[8:8 + t, :] = proj(OFF_GQKV, OFF_GZ)
    tail_ref[...] = buf_ref[t:t + 8, :]
    cw = cw_ref[...]
    pair_ones = ones_ref[...]

    def conv_silu(lo, hi):
        acc = buf_ref[5:5 + t, lo:hi] * cw[0:1, lo:hi]
        for j in range(1, CONV_WIDTH):
            acc = acc + buf_ref[5 + j:5 + j + t, lo:hi] * cw[j:j + 1, lo:hi]
        return acc * _sigmoid(acc)

    def l2n(v):
        return v * lax.rsqrt(_heads_sum((v * v).astype(BF16), pair_ones) + EPS)

    cos_t, sin_t = cos_ref[...], sin_ref[...]
    sq = proj(OFF_SQ, OFF_SKV)
    q = conv_silu(0, WIDTH)
    kv = proj(OFF_SKV, OFF_GATES)
    sq_ref[...] = _rope(sq, cos_t, sin_t, SWA_WIDTH // 128) * (HEAD_DIM ** -0.5)
    gqkv_ref[:, 0:WIDTH] = l2n(q) * (HEAD_DIM ** -0.5)
    m_ref[:, 0:2 * WIDTH] = proj(OFF_M, OFF_M + 2 * WIDTH)
    skv_ref[:, 0:SWA_KV_WIDTH] = _rope(kv[:, 0:SWA_KV_WIDTH], cos_t, sin_t, 1)
    skv_ref[:, SWA_KV_WIDTH:] = kv[:, SWA_KV_WIDTH:]
    k = conv_silu(WIDTH, 2 * WIDTH)
    m_ref[:, 2 * WIDTH:4 * WIDTH] = proj(OFF_M + 2 * WIDTH, OFF_SQ)
    gqkv_ref[:, WIDTH:2 * WIDTH] = l2n(k)
    gates_ref[...] = proj(OFF_GATES, IN_COLS_PACKED)
    gqkv_ref[:, 2 * WIDTH:3 * WIDTH] = conv_silu(2 * WIDTH, 3 * WIDTH)
    gz_ref[...] = proj(OFF_GZ, OFF_M)


def _proj(x, norm_w, w_packed, cos_t, sin_t, conv_w, layer, tm, seq):
    m = x.shape[0]
    row = lambda n: pl.BlockSpec((tm, n), lambda i: (i, 0))
    widths = (768, 256, 1024, SWA_WIDTH, 2 * SWA_KV_WIDTH, GATE_LANES)
    return pl.pallas_call(
        functools.partial(_proj_kernel, tiles_per_seq=seq // tm),
        grid=(m // tm,),
        in_specs=[
            row(D_MODEL),
            pl.BlockSpec((None, 1, D_MODEL), lambda i: (layer, 0, 0)),
            pl.BlockSpec((None, D_MODEL, IN_COLS_PACKED), lambda i: (layer, 0, 0),
                         pipeline_mode=pl.Buffered(1)),
            row(128), row(128),
            pl.BlockSpec((None, CONV_WIDTH, 3 * WIDTH), lambda i: (layer, 0, 0)),
            pl.BlockSpec((PAIR, PAIR), lambda i: (0, 0)),
        ],
        out_specs=[row(n) for n in widths],
        out_shape=[jax.ShapeDtypeStruct((m, n), F32) for n in widths],
        scratch_shapes=[
            pltpu.VMEM((8, 3 * WIDTH), F32),
            pltpu.VMEM((tm + 8, 3 * WIDTH), F32),
        ],
        compiler_params=pltpu.CompilerParams(
            dimension_semantics=("arbitrary",), vmem_limit_bytes=VMEM_LIMIT),
    )(x, norm_w, w_packed, cos_t, sin_t, conv_w, _pair_mask().astype(BF16))


def _gdn_kernel(qkv_ref, z_ref, gt_ref, par_ref, nw_ref, exp_ref, ones_ref, o_ref,
                state_ref, beta_s, gc_s, u_s, w_s, qd_s, kd_s, qkd_s, *, t, group, nb):
    c = CHUNK

    @pl.when(pl.program_id(1) == 0)
    def _():
        state_ref[...] = jnp.zeros_like(state_ref)

    pair_ones = ones_ref[...]

    def gate_rows(a, lo, hi):
        gates = gt_ref[a, lo:hi, :]
        par = par_ref[...]
        beta = _sigmoid(gates)
        g = -jnp.exp(par[0:1, :]) * _softplus(gates + par[1:2, :])
        beta_s[a, lo:hi, :] = _dot_sel(beta, exp_ref[0])
        gc_s[a, lo:hi, :] = _dot_sel(_chunk_cumsum(g), exp_ref[1])

    row = _iota2((c, WIDTH), 0)
    col = _iota2((c, WIDTH), 1) & (c - 1)
    causal = col <= row
    strict = col < row
    eye = col == row
    nw = nw_ref[...]

    def mm(a, b):
        return _heads_mm(a.astype(BF16), b.astype(BF16))

    def prepare(items, fillers):
        fillers = list(fillers)

        def fill():
            if fillers:
                fillers.pop(0)()

        n = range(len(items))
        sq = [a for a, _ in items]
        rs = [pl.ds(j * c, c) for _, j in items]
        q = [qkv_ref[sq[j], rs[j], 0:WIDTH] for j in n]
        k = [qkv_ref[sq[j], rs[j], WIDTH:2 * WIDTH] for j in n]
        gc = [gc_s[sq[j], rs[j], :] for j in n]
        g_last = [gc_s[a, pl.ds(j * c + c - 1, 1), :] for a, j in items]
        kb = [k[j] * beta_s[sq[j], rs[j], :] for j in n]
        kkqk = [_heads_mm_nt(jnp.concatenate([kb[j], q[j]], axis=0).astype(BF16), k[j].astype(BF16))
                for j in n]
        fill()
        gc_row = [jnp.sum(jnp.where(eye, gc[j], 0.0), axis=0, keepdims=True) for j in n]
        decay = [jnp.exp(jnp.where(causal, gc[j] - gc_row[j], -jnp.inf)) for j in n]
        nm = [jnp.where(strict, kkqk[j][0:c] * decay[j], 0.0) for j in n]
        for j in n:
            qkd_s[sq[j], rs[j], :] = (kkqk[j][c:2 * c] * decay[j]).astype(BF16)
        x = [eye.astype(F32) - jnp.where((row >> 1) == (col >> 1), nm[j], 0.0) for j in n]
        for lb in range(1, 6):
            off = ((row >> (lb + 1)) == (col >> (lb + 1))) & ((row >> lb) != (col >> lb))
            t1 = [mm(jnp.where(off, nm[j], 0.0), x[j]) for j in n]
            fill()
            t2 = [mm(x[j], t1[j]) for j in n]
            if lb % 2 == 0:
                fill()
            x = [x[j] - t2[j] for j in n]
        xb = [x[j].astype(BF16) for j in n]
        egc = [jnp.exp(gc[j]) for j in n]
        for j in n:
            u_s[sq[j], rs[j], :] = _heads_mm(
                xb[j], (qkv_ref[sq[j], rs[j], 2 * WIDTH:3 * WIDTH] * beta_s[sq[j], rs[j], :]).astype(BF16))
        for j in n:
            w_s[sq[j], rs[j], :] = _heads_mm(xb[j], (kb[j] * egc[j]).astype(BF16)).astype(BF16)
        for j in n:
            qd_s[sq[j], rs[j], :] = (q[j] * egc[j]).astype(BF16)
            kd_s[sq[j], rs[j], :] = (k[j] * jnp.exp(g_last[j] - gc[j])).astype(BF16)
        while fillers:
            fill()

    chunks = [(a, j) for a in range(nb) for j in range(t // c)]
    groups = [chunks[i:i + group] for i in range(0, len(chunks), group)]
    for a, j in groups[0]:
        gate_rows(a, j * c, (j + 1) * c)
    for gi, items in enumerate(groups):
        nxt = groups[gi + 1] if gi + 1 < len(groups) else []
        prepare(items, [functools.partial(gate_rows, a, j * c, (j + 1) * c) for a, j in nxt])

    def chunk_step(i, carry):
        n = range(nb)
        r = pl.ds(pl.multiple_of(i * c, c), c)
        pairs = range(WIDTH // PAIR)
        g_last = [gc_s[a, pl.ds(i * c + c - 1, 1), :] for a in n]
        s = [[state_ref[a, p] for p in pairs] for a in n]
        lhs = [_pairs(jnp.concatenate([w_s[a, r, :], qd_s[a, r, :]], axis=0)) for a in n]
        ws = [jnp.concatenate([_dot(lhs[a][p], s[a][p].astype(BF16)) for p in pairs], axis=1)
              for a in n]
        v_new = [(u_s[a, r, :] - ws[a][0:c]).astype(BF16) for a in n]
        v_diag = [[_pair_diag(vn) for vn in _pairs(v_new[a])] for a in n]
        upd = [[_dot_tn(_pair_diag(kd), v_diag[a][p]) for p, kd in enumerate(_pairs(kd_s[a, r, :]))]
               for a in n]
        for a in n:
            decay_end = _pairs(jnp.exp(g_last[a]))
            for p in pairs:
                state_ref[a, p] = s[a][p] * decay_end[p] + upd[a][p]
        o = [ws[a][c:2 * c] + jnp.concatenate(
            [_dot(qp, v_diag[a][p]) for p, qp in enumerate(_pairs(qkd_s[a, r, :]))], axis=1) for a in n]
        ms = [_heads_sum((o[a] * o[a]).astype(BF16), pair_ones) * (1.0 / HEAD_DIM) for a in n]
        for a in n:
            z = z_ref[a, r, :]
            o_ref[a, r, :] = o[a] * lax.rsqrt(ms[a] + EPS) * nw * (z * _sigmoid(z))
        return carry

    lax.fori_loop(0, t // c, chunk_step, 0)


def _gdn(gqkv, gz, gates, par, norm_w, layer, bsz, seq, t, nb):
    row = lambda n: pl.BlockSpec((nb, t, n), lambda b, j: (b, j, 0))
    lay = lambda a, b2: pl.BlockSpec((None, a, b2), lambda b, j: (layer, 0, 0))
    const = lambda shape: pl.BlockSpec(shape, lambda b, j: (0,) * len(shape))
    seq3 = lambda a: a.reshape(bsz, seq, a.shape[-1])
    f32_buf = pltpu.VMEM((nb, t, WIDTH), F32)
    bf16_buf = pltpu.VMEM((nb, t, WIDTH), BF16)
    out = pl.pallas_call(
        functools.partial(_gdn_kernel, t=t, group=8, nb=nb),
        grid=(bsz // nb, seq // t),
        in_specs=[row(3 * WIDTH), row(WIDTH), row(GATE_LANES),
                  lay(2, GATE_LANES), lay(1, WIDTH),
                  const((2, GATE_LANES, WIDTH)), const((PAIR, PAIR))],
        out_specs=row(WIDTH),
        out_shape=jax.ShapeDtypeStruct((bsz, seq, WIDTH), F32),
        scratch_shapes=[
            pltpu.VMEM((nb, WIDTH // PAIR, PAIR, PAIR), F32),
            f32_buf, f32_buf,
            f32_buf,
            bf16_buf, bf16_buf, bf16_buf, bf16_buf,
        ],
        compiler_params=pltpu.CompilerParams(
            dimension_semantics=("arbitrary", "arbitrary"), vmem_limit_bytes=VMEM_LIMIT),
    )(seq3(gqkv), seq3(gz), seq3(gates), par, norm_w,
      jnp.stack([_expander(0), _expander(HEADS)]), _pair_mask().astype(BF16))
    return out.reshape(bsz * seq, WIDTH)


def _mlstm_kernel(m_ref, gt_ref, par_ref, nw_ref, exp_ref, ones_ref, o_ref,
                  cn_ref, mx_ref, ig_s, b_s, cm_s, *, t, nb):
    c = CHUNK

    @pl.when(pl.program_id(1) == 0)
    def _():
        cn_ref[...] = jnp.zeros_like(cn_ref)
        mx_ref[...] = jnp.zeros_like(mx_ref)

    pair_ones = ones_ref[...]

    def front(a, carry):
        pre = gt_ref[a] + par_ref[...]
        capped = GATE_SOFTCAP * jnp.tanh(pre * (1.0 / GATE_SOFTCAP))
        log_f = -_softplus(-capped)
        b_all = _chunk_cumsum(log_f)
        ig = _dot_sel(capped, exp_ref[0])
        b = _dot_sel(b_all, exp_ref[1])
        pos = _iota2((t, WIDTH), 0) & (c - 1)
        cm = ig - b
        for sh in (1, 2, 4, 8, 16, 32):
            cm = jnp.where(pos >= sh, jnp.maximum(cm, pltpu.roll(cm, sh, axis=0)), cm)
        ig_s[a] = ig
        b_s[a] = b
        cm_s[a] = cm
        return carry

    lax.fori_loop(0, nb, front, 0)

    row = _iota2((c, WIDTH), 0)
    col = _iota2((c, WIDTH), 1) & (c - 1)
    causal = col <= row
    eye = col == row
    nw = nw_ref[...]

    def chunk_step(i, carry):
        n = range(nb)
        pairs = range(WIDTH // PAIR)
        r = pl.ds(pl.multiple_of(i * c, c), c)
        last = pl.ds(i * c + c - 1, 1)
        qb = [m_ref[a, r, 0:WIDTH].astype(BF16) for a in n]
        k = [m_ref[a, r, WIDTH:2 * WIDTH] * (HEAD_DIM ** -0.5) for a in n]
        vb = [m_ref[a, r, 2 * WIDTH:3 * WIDTH].astype(BF16) for a in n]
        ig = [ig_s[a, r, :] for a in n]
        b = [b_s[a, r, :] for a in n]
        m_intra = [b[a] + cm_s[a, r, :] for a in n]
        b_last = [b_s[a, last, :] for a in n]
        m_chunk = [b_last[a] + cm_s[a, last, :] for a in n]
        qk = [_heads_mm_nt(qb[a], k[a].astype(BF16)) for a in n]
        gate_row = [jnp.sum(jnp.where(eye, ig[a] - b[a], 0.0), axis=0, keepdims=True) for a in n]
        qk = [qk[a] * jnp.exp(jnp.where(causal, b[a] + gate_row[a], -jnp.inf) - m_intra[a]) for a in n]
        v_ones = [[jnp.concatenate([_pair_diag(vp), pair_ones], axis=1) for vp in _pairs(vb[a])]
                  for a in n]
        qk_parts = [_split2(qk[a]) for a in n]
        intra = [[_dot(qp, v_ones[a][p]) for p, qp in enumerate(_pairs(qk_parts[a][0]))] for a in n]
        num_intra = [jnp.concatenate([intra[a][p][:, 0:PAIR] for p in pairs], axis=1) for a in n]
        den_intra = [jnp.concatenate([intra[a][p][:, PAIR:2 * PAIR] for p in pairs], axis=1)
                     + _heads_sum(qk_parts[a][1], pair_ones) for a in n]
        m_prev = [mx_ref[a] for a in n]
        cn = [[cn_ref[a, p] for p in pairs] for a in n]
        inter = [[_dot(qp, cn[a][p].astype(BF16)) for p, qp in enumerate(_pairs(qb[a]))] for a in n]
        ke = [(k[a] * jnp.exp(b_last[a] - b[a] + ig[a] - m_chunk[a])).astype(BF16) for a in n]
        own = [[_dot_tn(_pair_diag(kp), v_ones[a][p]) for p, kp in enumerate(_pairs(ke[a]))]
               for a in n]
        for a in n:
            m_new = jnp.maximum(b_last[a] + m_prev[a], m_chunk[a])
            s_old = _pairs(jnp.exp(b_last[a] + m_prev[a] - m_new))
            s_new = _pairs(jnp.exp(m_chunk[a] - m_new))
            for p in pairs:
                cn_ref[a, p] = (jnp.concatenate([s_old[p], s_old[p]], axis=1) * cn[a][p]
                                + jnp.concatenate([s_new[p], s_new[p]], axis=1) * own[a][p])
            mx_ref[a] = m_new
        h = []
        for a in n:
            pre_m = b[a] + m_prev[a]
            m_t = jnp.maximum(pre_m, m_intra[a])
            s_inter = jnp.exp(pre_m - m_t)
            s_intra = jnp.exp(m_intra[a] - m_t)
            num_inter = jnp.concatenate([inter[a][p][:, 0:PAIR] for p in pairs], axis=1)
            den_inter = jnp.concatenate([inter[a][p][:, PAIR:2 * PAIR] for p in pairs], axis=1)
            num = s_inter * num_inter + s_intra * num_intra[a]
            den = s_inter * den_inter + s_intra * den_intra[a]
            h.append(num / jnp.maximum(jnp.abs(den), jnp.exp(-m_t)))
        ms = [_heads_sum((h[a] * h[a]).astype(BF16), pair_ones) * (1.0 / HEAD_DIM) for a in n]
        for a in n:
            o_ref[a, r, :] = (h[a] * lax.rsqrt(ms[a] + EPS) * nw
                              * _sigmoid(m_ref[a, r, 3 * WIDTH:4 * WIDTH]))
        return carry

    lax.fori_loop(0, t // c, chunk_step, 0)


def _mlstm(mqkvo, gates, par, norm_w, layer, bsz, seq, t, nb):
    row = lambda n: pl.BlockSpec((nb, t, n), lambda b, j: (b, j, 0))
    lay = lambda a, b2: pl.BlockSpec((None, a, b2), lambda b, j: (layer, 0, 0))
    const = lambda shape: pl.BlockSpec(shape, lambda b, j: (0,) * len(shape))
    seq3 = lambda a: a.reshape(bsz, seq, a.shape[-1])
    f32_buf = pltpu.VMEM((nb, t, WIDTH), F32)
    out = pl.pallas_call(
        functools.partial(_mlstm_kernel, t=t, nb=nb),
        grid=(bsz // nb, seq // t),
        in_specs=[row(4 * WIDTH), row(GATE_LANES), lay(1, GATE_LANES), lay(1, WIDTH),
                  const((2, GATE_LANES, WIDTH)), const((PAIR, PAIR))],
        out_specs=row(WIDTH),
        out_shape=jax.ShapeDtypeStruct((bsz, seq, WIDTH), F32),
        scratch_shapes=[
            pltpu.VMEM((nb, WIDTH // PAIR, PAIR, 2 * PAIR), F32),
            pltpu.VMEM((nb, 1, WIDTH), F32),
            f32_buf, f32_buf, f32_buf,
        ],
        compiler_params=pltpu.CompilerParams(
            dimension_semantics=("arbitrary", "arbitrary"), vmem_limit_bytes=VMEM_LIMIT),
    )(seq3(mqkvo), seq3(gates), par, norm_w,
      jnp.stack([_expander(2 * HEADS), _expander(3 * HEADS)]),
      _pair_mask().astype(BF16))
    return out.reshape(bsz * seq, WIDTH)


def _swa_kernel(q_ref, kvc_ref, kvp_ref, sink_ref, o_ref, *, tq):
    tb = SWA_BLOCK
    kvw = SWA_KV_WIDTH
    nq = tq // tb
    first = pl.program_id(1) == 0
    kcat = jnp.concatenate([kvp_ref[:, 0:kvw], kvc_ref[:, 0:kvw]], axis=0)
    vcat = jnp.concatenate([kvp_ref[:, kvw:2 * kvw], kvc_ref[:, kvw:2 * kvw]], axis=0)
    v_t = vcat.T.astype(BF16)
    kr = pltpu.roll(kcat, HEAD_DIM, axis=1)
    lo = _iota2((tb + tq, kvw), 1) < HEAD_DIM

    def place(x_lo, x_hi):
        return jnp.where(lo, x_lo, 0.0).astype(BF16), jnp.where(lo, 0.0, x_hi).astype(BF16)

    k_placed = (place(kcat, kr), place(kr, kcat))
    ki = _iota2((2 * tb, tb), 0)
    qi = _iota2((2 * tb, tb), 1)
    in_window = (ki > qi) & (ki <= qi + tb)

    chains = [(i, g) for i in range(nq) for g in range(2)]
    s_t = []
    for i, g in chains:
        keys = jnp.concatenate([k_placed[g][0][i * tb:(i + 2) * tb],
                                k_placed[g][1][i * tb:(i + 2) * tb]], axis=0)
        qs = jnp.concatenate([q_ref[i * tb:(i + 1) * tb, 256 * g:256 * g + 128],
                              q_ref[i * tb:(i + 1) * tb, 256 * g + 128:256 * g + 256]], axis=0)
        s_t.append(_dot_nt(keys, qs.astype(BF16)))
    p_all, inv_all = [], []
    for (i, g), sc in zip(chains, s_t):
        mask = in_window & (ki >= jnp.where(first, tb, 0)) if i == 0 else in_window
        ps, invs = [], []
        for part in range(2):
            for e in range(2):
                sink = sink_ref[2 * g + part:2 * g + part + 1, e * 2 * tb:e * 2 * tb + 1]
                se = jnp.where(mask, sc[e * 2 * tb:(e + 1) * 2 * tb, part * tb:(part + 1) * tb],
                               -jnp.inf)
                mx = jnp.maximum(jnp.max(se, axis=0, keepdims=True), sink)
                p = jnp.exp(se - mx)
                ps.append(p.astype(BF16))
                invs.append(1.0 / (jnp.sum(p, axis=0, keepdims=True) + jnp.exp(sink - mx)))
        p_all.append(jnp.concatenate(ps, axis=-1))
        inv_all.append(jnp.concatenate(invs, axis=-1))
    out_t = [_dot(v_t[g * HEAD_DIM:(g + 1) * HEAD_DIM, i * tb:(i + 2) * tb], p) * inv
             for (i, g), p, inv in zip(chains, p_all, inv_all)]
    for (i, g), o in zip(chains, out_t):
        for part in range(2):
            j = 2 * g + part
            pair_t = jnp.concatenate([o[:, 2 * part * tb:(2 * part + 1) * tb],
                                      o[:, (2 * part + 1) * tb:(2 * part + 2) * tb]], axis=0)
            o_ref[i * tb:(i + 1) * tb, 128 * j:128 * (j + 1)] = pair_t.T


def _swa(sq, skv, sinks_e, layer, bsz, seq, tq):
    m = bsz * seq
    nt = seq // tq
    nq = tq // SWA_BLOCK
    nb = seq // SWA_BLOCK
    return pl.pallas_call(
        functools.partial(_swa_kernel, tq=tq),
        grid=(bsz, nt),
        in_specs=[
            pl.BlockSpec((tq, SWA_WIDTH), lambda b, n: (b * nt + n, 0)),
            pl.BlockSpec((tq, 2 * SWA_KV_WIDTH), lambda b, n: (b * nt + n, 0)),
            pl.BlockSpec((SWA_BLOCK, 2 * SWA_KV_WIDTH),
                         lambda b, n: (b * nb + jnp.maximum(n * nq - 1, 0), 0)),
            pl.BlockSpec((None, SWA_WIDTH // 128, 4 * SWA_BLOCK), lambda b, n: (layer, 0, 0)),
        ],
        out_specs=pl.BlockSpec((tq, SWA_WIDTH), lambda b, n: (b * nt + n, 0)),
        out_shape=jax.ShapeDtypeStruct((m, SWA_WIDTH), F32),
        compiler_params=pltpu.CompilerParams(
            dimension_semantics=("arbitrary", "arbitrary"), vmem_limit_bytes=VMEM_LIMIT),
    )(sq, skv, skv, sinks_e)


def _post_kernel(x_ref, ya_ref, yb_ref, yc_ref, p_ref, wo_ref, nmlp_ref, wup_ref, wdn_ref,
                 nple_ref, wg_ref, wp_ref, nfin_ref, o_ref, *, final, tf):
    x = x_ref[...]
    x = x + _dot(ya_ref[...].astype(BF16), wo_ref[0:WIDTH, :])
    x = x + _dot(yb_ref[...].astype(BF16), wo_ref[WIDTH:2 * WIDTH, :])
    x = x + _dot(yc_ref[...].astype(BF16), wo_ref[2 * WIDTH:, :])
    r = _rms_scale(x)
    h = (x * nmlp_ref[...]).astype(BF16)
    acc = jnp.zeros_like(x)
    for f in range(0, D_FF, tf):
        u = jnp.maximum(_dot(h, wup_ref[:, f:f + tf]), 0.0)
        acc = acc + _dot((u * u).astype(BF16), wdn_ref[f:f + tf, :])
    x = x + (r * r) * acc
    gate = _sigmoid(_rms_scale(x) * _dot((x * nple_ref[...]).astype(BF16), wg_ref[...]))
    x = x + gate * _dot(p_ref[...].astype(BF16), wp_ref[...])
    if final:
        x = _rms(x, nfin_ref[...])
    o_ref[...] = x


def _post(x, ya, yb, yc, p, w_out, norm_mlp, w_up, w_down, norm_ple, w_gate, w_proj, norm_final,
          layer, tm, final):
    m = x.shape[0]
    row = lambda n: pl.BlockSpec((tm, n), lambda i: (i, 0))
    lay = lambda a, b: pl.BlockSpec((None, a, b), lambda i: (layer, 0, 0),
                                    pipeline_mode=pl.Buffered(1))
    return pl.pallas_call(
        functools.partial(_post_kernel, final=final, tf=1024),
        grid=(m // tm,),
        in_specs=[
            row(D_MODEL), row(WIDTH), row(WIDTH), row(SWA_WIDTH),
            pl.BlockSpec((None, tm, PLE_DIM), lambda i: (layer, i, 0)),
            lay(D_MODEL, D_MODEL), lay(1, D_MODEL), lay(D_MODEL, D_FF), lay(D_FF, D_MODEL),
            lay(1, D_MODEL), lay(D_MODEL, D_MODEL), lay(PLE_DIM, D_MODEL),
            pl.BlockSpec((1, D_MODEL), lambda i: (0, 0)),
        ],
        out_specs=row(D_MODEL),
        out_shape=jax.ShapeDtypeStruct((m, D_MODEL), F32),
        compiler_params=pltpu.CompilerParams(
            dimension_semantics=("arbitrary",), vmem_limit_bytes=VMEM_LIMIT),
    )(x, ya, yb, yc, p, w_out, norm_mlp, w_up, w_down, norm_ple, w_gate, w_proj, norm_final)


def _pack_w_in(w_in):
    depth = w_in.shape[0]
    pad = jnp.zeros((depth, D_MODEL, GATE_LANES - 4 * HEADS), w_in.dtype)
    return jnp.concatenate(
        [w_in[..., 0:1024], w_in[..., 1032:2056], w_in[..., 2064:2832],
         w_in[..., 1024:1032], w_in[..., 2056:2064], pad], axis=-1).astype(BF16)


def _gate_row(depth, pieces):
    out = jnp.zeros((depth, 1, GATE_LANES), F32)
    for first_col, vals in pieces:
        out = out.at[:, 0, first_col:first_col + HEADS].set(vals.astype(F32))
    return out


def _rope_tables(positions):
    half = ROPE_DIM // 2
    inv_freq = ROPE_THETA ** (-jnp.arange(0, ROPE_DIM, 2, dtype=F32) / ROPE_DIM)
    dim = jnp.arange(128) % HEAD_DIM
    ang = positions.astype(F32).reshape(-1)[:, None] * inv_freq[dim % half][None, :]
    cos_t = jnp.where(dim < ROPE_DIM, jnp.cos(ang), 1.0)
    sin_t = jnp.where(dim < half, -jnp.sin(ang), jnp.where(dim < ROPE_DIM, jnp.sin(ang), 0.0))
    return cos_t, sin_t


def kernel(x, p, positions, w_in, conv_w, gdn_a_log, gdn_dt_bias, gdn_norm, mlstm_i_bias,
           mlstm_f_bias, mlstm_norm, attn_sinks, w_out, norm_mix, norm_mlp, w_up, w_down,
           norm_ple, w_ple_gate, w_ple_proj, norm_final):
    bsz, seq, d = x.shape
    depth = w_in.shape[0]
    m = bsz * seq
    tm = min(512, seq)
    t_mix = min(512, seq)
    nb_mix = 4 if bsz % 4 == 0 else 1
    nb_gdn = 8 if bsz % 8 == 0 else nb_mix
    t_gdn = min(2048 // nb_gdn, seq)

    cos_t, sin_t = _rope_tables(positions)
    w_packed = _pack_w_in(w_in)
    gdn_par = jnp.concatenate([_gate_row(depth, [(HEADS, gdn_a_log)]),
                               _gate_row(depth, [(HEADS, gdn_dt_bias)])], axis=1)
    mlstm_par = _gate_row(depth, [(2 * HEADS, mlstm_i_bias), (3 * HEADS, mlstm_f_bias)])
    gdn_nw = jnp.tile(gdn_norm.astype(F32), (1, HEADS))[:, None, :]
    mlstm_nw = mlstm_norm.astype(F32)[:, None, :]
    sinks_e = jnp.repeat(attn_sinks.astype(F32), 2 * SWA_BLOCK, axis=-1).reshape(
        depth, SWA_WIDTH // 128, 4 * SWA_BLOCK)
    row3 = lambda a: a.astype(F32)[:, None, :]
    wo_b, wup_b, wdn_b = w_out.astype(BF16), w_up.astype(BF16), w_down.astype(BF16)
    wg_b, wp_b = w_ple_gate.astype(BF16), w_ple_proj.astype(BF16)
    nmix, nmlp, nple = row3(norm_mix), row3(norm_mlp), row3(norm_ple)
    nfin = norm_final.astype(F32)[None, :]
    p2 = p.reshape(depth, m, PLE_DIM)

    xf = x.reshape(m, d)
    for i in range(depth):
        gqkv, gz, mqkvo, sq, skv, gates = _proj(xf, nmix, w_packed, cos_t, sin_t,
                                                conv_w.astype(F32), i, tm, seq)
        ya = _gdn(gqkv, gz, gates, gdn_par, gdn_nw, i, bsz, seq, t_gdn, nb_gdn)
        yb = _mlstm(mqkvo, gates, mlstm_par, mlstm_nw, i, bsz, seq, t_mix, nb_mix)
        yc = _swa(sq, skv, sinks_e, i, bsz, seq, min(512, seq))
        xf = _post(xf, ya, yb, yc, p2, wo_b, nmlp, wup_b, wdn_b, nple, wg_b, wp_b, nfin,
                   i, tm, final=(i == depth - 1))
    return xf.reshape(bsz, seq, d)
```

```python
import functools

import jax
import jax.numpy as jnp
from jax import lax
from jax.experimental import pallas as pl
from jax.experimental.pallas import tpu as pltpu

F32 = jnp.float32
BF16 = jnp.bfloat16

D_MODEL = 1024
DEPTH = 4
HEAD_DIM = 64
PLE_DIM = 256
D_FF = 4 * D_MODEL
EPS = 1e-6
HEADS = 4
WIDTH = HEADS * HEAD_DIM
PAIR = 2 * HEAD_DIM
CHUNK = 64
CONV_WIDTH = 4
GATE_SOFTCAP = 15.0
SWA_WIDTH = 512
SWA_KV_WIDTH = 128
SWA_BLOCK = 128
ROPE_DIM = 16
ROPE_THETA = 500000.0
GATE_LANES = 128
IN_COLS_PACKED = 2944
OFF_GQKV, OFF_GZ, OFF_M, OFF_SQ, OFF_SKV, OFF_GATES = 0, 768, 1024, 2048, 2560, 2816

VMEM_LIMIT = 56 * 1024 * 1024


def _dot(a, b):
    return jnp.dot(a, b, preferred_element_type=F32)


def _dot_nt(a, b):
    return lax.dot_general(a, b, (((1,), (1,)), ((), ())), preferred_element_type=F32)


def _dot_tn(a, b):
    return lax.dot_general(a, b, (((0,), (0,)), ((), ())), preferred_element_type=F32)


def _split2(x):
    hi = x.astype(BF16)
    lo = (x - hi.astype(F32)).astype(BF16)
    return hi, lo


def _dot_sel(x, sel):
    hi, lo = _split2(x)
    return _dot(hi, sel) + _dot(lo, sel)


def _chunk_cumsum(x):
    pos = _iota2(x.shape, 0) & (CHUNK - 1)
    sh = 1
    while sh < CHUNK:
        x = x + jnp.where(pos >= sh, pltpu.roll(x, sh, axis=0), 0.0)
        sh *= 2
    return x


def _sigmoid(x):
    return 1.0 / (1.0 + jnp.exp(-x))


def _softplus(x):
    return jnp.maximum(x, 0.0) + jnp.log1p(jnp.exp(-jnp.abs(x)))


def _rms(x, g):
    return x * lax.rsqrt(jnp.mean(x * x, axis=-1, keepdims=True) + EPS) * g


def _rms_scale(x):
    return lax.rsqrt(jnp.mean(x * x, axis=-1, keepdims=True) + EPS)


def _iota2(shape, dim):
    return lax.broadcasted_iota(jnp.int32, shape, dim)


def _pairs(x):
    return [x[:, p * PAIR:(p + 1) * PAIR] for p in range(WIDTH // PAIR)]


def _pair_mask():
    return (_iota2((PAIR, PAIR), 0) >> 6) == (_iota2((PAIR, PAIR), 1) >> 6)


def _pair_diag(x):
    low = _iota2(x.shape, 1) < HEAD_DIM
    zero = jnp.zeros((), x.dtype)
    return jnp.concatenate([jnp.where(low, x, zero), jnp.where(low, zero, x)], axis=0)


def _heads_mm(a, b):
    return jnp.concatenate([_dot(ap, _pair_diag(bp)) for ap, bp in zip(_pairs(a), _pairs(b))], axis=1)


def _heads_mm_nt(a, b):
    return jnp.concatenate([_dot_nt(ap, _pair_diag(bp)) for ap, bp in zip(_pairs(a), _pairs(b))],
                           axis=1)


def _heads_sum(x, pair_ones):
    return jnp.concatenate([_dot(xp, pair_ones) for xp in _pairs(x)], axis=1)


def _expander(first_col):
    r = _iota2((GATE_LANES, WIDTH), 0)
    c = _iota2((GATE_LANES, WIDTH), 1)
    return (r == first_col + (c >> 6)).astype(BF16)


def _rope(x, cos_t, sin_t, reps):
    n = x.shape[-1]
    if reps > 1:
        cos_t = jnp.concatenate([cos_t] * reps, axis=-1)
        sin_t = jnp.concatenate([sin_t] * reps, axis=-1)
    half = ROPE_DIM // 2
    x_up = pltpu.roll(x, n - half, axis=1)
    x_dn = pltpu.roll(x, half, axis=1)
    first_half = (_iota2(x.shape, 1) & (HEAD_DIM - 1)) < half
    return x * cos_t + jnp.where(first_half, x_up, x_dn) * sin_t


def _proj_kernel(x_ref, g_ref, w_ref, cos_ref, sin_ref, cw_ref, ones_ref,
                 gqkv_ref, gz_ref, m_ref, sq_ref, skv_ref, gates_ref, tail_ref, buf_ref,
                 *, tiles_per_seq):
    t = x_ref.shape[0]

    @pl.when(pl.program_id(0) % tiles_per_seq == 0)
    def _():
        tail_ref[...] = jnp.zeros_like(tail_ref)

    x = x_ref[...]
    hb = (x * g_ref[...]).astype(BF16)
    scale = _rms_scale(x)

    def proj(lo, hi):
        return scale * _dot(hb, w_ref[:, lo:hi])

    buf_ref[0:8, :] = tail_ref[...]
    buf_ref[8:8 + t, :] = proj(OFF_GQKV, OFF_GZ)
    tail_ref[...] = buf_ref[t:t + 8, :]
    cw = cw_ref[...]
    pair_ones = ones_ref[...]

    def conv_silu(lo, hi):
        acc = buf_ref[5:5 + t, lo:hi] * cw[0:1, lo:hi]
        for j in range(1, CONV_WIDTH):
            acc = acc + buf_ref[5 + j:5 + j + t, lo:hi] * cw[j:j + 1, lo:hi]
        return acc * _sigmoid(acc)

    def l2n(v):
        return v * lax.rsqrt(_heads_sum((v * v).astype(BF16), pair_ones) + EPS)

    cos_t, sin_t = cos_ref[...], sin_ref[...]
    sq = proj(OFF_SQ, OFF_SKV)
    q = conv_silu(0, WIDTH)
    kv = proj(OFF_SKV, OFF_GATES)
    sq_ref[...] = _rope(sq, cos_t, sin_t, SWA_WIDTH // 128) * (HEAD_DIM ** -0.5)
    gqkv_ref[:, 0:WIDTH] = l2n(q) * (HEAD_DIM ** -0.5)
    m_ref[:, 0:2 * WIDTH] = proj(OFF_M, OFF_M + 2 * WIDTH)
    skv_ref[:, 0:SWA_KV_WIDTH] = _rope(kv[:, 0:SWA_KV_WIDTH], cos_t, sin_t, 1)
    skv_ref[:, SWA_KV_WIDTH:] = kv[:, SWA_KV_WIDTH:]
    k = conv_silu(WIDTH, 2 * WIDTH)
    m_ref[:, 2 * WIDTH:4 * WIDTH] = proj(OFF_M + 2 * WIDTH, OFF_SQ)
    gqkv_ref[:, WIDTH:2 * WIDTH] = l2n(k)
    gates_ref[...] = proj(OFF_GATES, IN_COLS_PACKED)
    gqkv_ref[:, 2 * WIDTH:3 * WIDTH] = conv_silu(2 * WIDTH, 3 * WIDTH)
    gz_ref[...] = proj(OFF_GZ, OFF_M)


def _proj(x, norm_w, w_packed, cos_t, sin_t, conv_w, layer, tm, seq):
    m = x.shape[0]
    row = lambda n: pl.BlockSpec((tm, n), lambda i: (i, 0))
    widths = (768, 256, 1024, SWA_WIDTH, 2 * SWA_KV_WIDTH, GATE_LANES)
    return pl.pallas_call(
        functools.partial(_proj_kernel, tiles_per_seq=seq // tm),
        grid=(m // tm,),
        in_specs=[
            row(D_MODEL),
            pl.BlockSpec((None, 1, D_MODEL), lambda i: (layer, 0, 0)),
            pl.BlockSpec((None, D_MODEL, IN_COLS_PACKED), lambda i: (layer, 0, 0),
                         pipeline_mode=pl.Buffered(1)),
            row(128), row(128),
            pl.BlockSpec((None, CONV_WIDTH, 3 * WIDTH), lambda i: (layer, 0, 0)),
            pl.BlockSpec((PAIR, PAIR), lambda i: (0, 0)),
        ],
        out_specs=[row(n) for n in widths],
        out_shape=[jax.ShapeDtypeStruct((m, n), F32) for n in widths],
        scratch_shapes=[
            pltpu.VMEM((8, 3 * WIDTH), F32),
            pltpu.VMEM((tm + 8, 3 * WIDTH), F32),
        ],
        compiler_params=pltpu.CompilerParams(
            dimension_semantics=("arbitrary",), vmem_limit_bytes=VMEM_LIMIT),
    )(x, norm_w, w_packed, cos_t, sin_t, conv_w, _pair_mask().astype(BF16))


def _gdn_kernel(qkv_ref, z_ref, gt_ref, par_ref, nw_ref, exp_ref, ones_ref, o_ref,
                state_ref, beta_s, gc_s, u_s, w_s, qd_s, kd_s, qkd_s, *, t, group, nb):
    c = CHUNK

    @pl.when(pl.program_id(1) == 0)
    def _():
        state_ref[...] = jnp.zeros_like(state_ref)

    pair_ones = ones_ref[...]

    def gate_rows(a, lo, hi):
        gates = gt_ref[a, lo:hi, :]
        par = par_ref[...]
        beta = _sigmoid(gates)
        g = -jnp.exp(par[0:1, :]) * _softplus(gates + par[1:2, :])
        beta_s[a, lo:hi, :] = _dot_sel(beta, exp_ref[0])
        gc_s[a, lo:hi, :] = _dot_sel(_chunk_cumsum(g), exp_ref[1])

    row = _iota2((c, WIDTH), 0)
    col = _iota2((c, WIDTH), 1) & (c - 1)
    causal = col <= row
    strict = col < row
    eye = col == row
    nw = nw_ref[...]

    def mm(a, b):
        return _heads_mm(a.astype(BF16), b.astype(BF16))

    def prepare(items, fillers):
        fillers = list(fillers)

        def fill():
            if fillers:
                fillers.pop(0)()

        n = range(len(items))
        sq = [a for a, _ in items]
        rs = [pl.ds(j * c, c) for _, j in items]
        q = [qkv_ref[sq[j], rs[j], 0:WIDTH] for j in n]
        k = [qkv_ref[sq[j], rs[j], WIDTH:2 * WIDTH] for j in n]
        gc = [gc_s[sq[j], rs[j], :] for j in n]
        g_last = [gc_s[a, pl.ds(j * c + c - 1, 1), :] for a, j in items]
        kb = [k[j] * beta_s[sq[j], rs[j], :] for j in n]
        kkqk = [_heads_mm_nt(jnp.concatenate([kb[j], q[j]], axis=0).astype(BF16), k[j].astype(BF16))
                for j in n]
        fill()
        gc_row = [jnp.sum(jnp.where(eye, gc[j], 0.0), axis=0, keepdims=True) for j in n]
        decay = [jnp.exp(jnp.where(causal, gc[j] - gc_row[j], -jnp.inf)) for j in n]
        nm = [jnp.where(strict, kkqk[j][0:c] * decay[j], 0.0) for j in n]
        for j in n:
            qkd_s[sq[j], rs[j], :] = (kkqk[j][c:2 * c] * decay[j]).astype(BF16)
        x = [eye.astype(F32) - jnp.where((row >> 1) == (col >> 1), nm[j], 0.0) for j in n]
        for lb in range(1, 6):
            off = ((row >> (lb + 1)) == (col >> (lb + 1))) & ((row >> lb) != (col >> lb))
            t1 = [mm(jnp.where(off, nm[j], 0.0), x[j]) for j in n]
            fill()
            t2 = [mm(x[j], t1[j]) for j in n]
            if lb % 2 == 0:
                fill()
            x = [x[j] - t2[j] for j in n]
        xb = [x[j].astype(BF16) for j in n]
        egc = [jnp.exp(gc[j]) for j in n]
        for j in n:
            u_s[sq[j], rs[j], :] = _heads_mm(
                xb[j], (qkv_ref[sq[j], rs[j], 2 * WIDTH:3 * WIDTH] * beta_s[sq[j], rs[j], :]).astype(BF16))
        for j in n:
            w_s[sq[j], rs[j], :] = _heads_mm(xb[j], (kb[j] * egc[j]).astype(BF16)).astype(BF16)
        for j in n:
            qd_s[sq[j], rs[j], :] = (q[j] * egc[j]).astype(BF16)
            kd_s[sq[j], rs[j], :] = (k[j] * jnp.exp(g_last[j] - gc[j])).astype(BF16)
        while fillers:
            fill()

    chunks = [(a, j) for a in range(nb) for j in range(t // c)]
    groups = [chunks[i:i + group] for i in range(0, len(chunks), group)]
    for a, j in groups[0]:
        gate_rows(a, j * c, (j + 1) * c)
    for gi, items in enumerate(groups):
        nxt = groups[gi + 1] if gi + 1 < len(groups) else []
        prepare(items, [functools.partial(gate_rows, a, j * c, (j + 1) * c) for a, j in nxt])

    def chunk_step(i, carry):
        n = range(nb)
        r = pl.ds(pl.multiple_of(i * c, c), c)
        pairs = range(WIDTH // PAIR)
        g_last = [gc_s[a, pl.ds(i * c + c - 1, 1), :] for a in n]
        s = [[state_ref[a, p] for p in pairs] for a in n]
        lhs = [_pairs(jnp.concatenate([w_s[a, r, :], qd_s[a, r, :]], axis=0)) for a in n]
        ws = [jnp.concatenate([_dot(lhs[a][p], s[a][p].astype(BF16)) for p in pairs], axis=1)
              for a in n]
        v_new = [(u_s[a, r, :] - ws[a][0:c]).astype(BF16) for a in n]
        v_diag = [[_pair_diag(vn) for vn in _pairs(v_new[a])] for a in n]
        upd = [[_dot_tn(_pair_diag(kd), v_diag[a][p]) for p, kd in enumerate(_pairs(kd_s[a, r, :]))]
               for a in n]
        for a in n:
            decay_end = _pairs(jnp.exp(g_last[a]))
            for p in pairs:
                state_ref[a, p] = s[a][p] * decay_end[p] + upd[a][p]
        o = [ws[a][c:2 * c] + jnp.concatenate(
            [_dot(qp, v_diag[a][p]) for p, qp in enumerate(_pairs(qkd_s[a, r, :]))], axis=1) for a in n]
        ms = [_heads_sum((o[a] * o[a]).astype(BF16), pair_ones) * (1.0 / HEAD_DIM) for a in n]
        for a in n:
            z = z_ref[a, r, :]
            o_ref[a, r, :] = o[a] * lax.rsqrt(ms[a] + EPS) * nw * (z * _sigmoid(z))
        return carry

    lax.fori_loop(0, t // c, chunk_step, 0)


def _gdn(gqkv, gz, gates, par, norm_w, layer, bsz, seq, t, nb):
    row = lambda n: pl.BlockSpec((nb, t, n), lambda b, j: (b, j, 0))
    lay = lambda a, b2: pl.BlockSpec((None, a, b2), lambda b, j: (layer, 0, 0))
    const = lambda shape: pl.BlockSpec(shape, lambda b, j: (0,) * len(shape))
    seq3 = lambda a: a.reshape(bsz, seq, a.shape[-1])
    f32_buf = pltpu.VMEM((nb, t, WIDTH), F32)
    bf16_buf = pltpu.VMEM((nb, t, WIDTH), BF16)
    out = pl.pallas_call(
        functools.partial(_gdn_kernel, t=t, group=8, nb=nb),
        grid=(bsz // nb, seq // t),
        in_specs=[row(3 * WIDTH), row(WIDTH), row(GATE_LANES),
                  lay(2, GATE_LANES), lay(1, WIDTH),
                  const((2, GATE_LANES, WIDTH)), const((PAIR, PAIR))],
        out_specs=row(WIDTH),
        out_shape=jax.ShapeDtypeStruct((bsz, seq, WIDTH), F32),
        scratch_shapes=[
            pltpu.VMEM((nb, WIDTH // PAIR, PAIR, PAIR), F32),
            f32_buf, f32_buf,
            f32_buf,
            bf16_buf, bf16_buf, bf16_buf, bf16_buf,
        ],
        compiler_params=pltpu.CompilerParams(
            dimension_semantics=("arbitrary", "arbitrary"), vmem_limit_bytes=VMEM_LIMIT),
    )(seq3(gqkv), seq3(gz), seq3(gates), par, norm_w,
      jnp.stack([_expander(0), _expander(HEADS)]), _pair_mask().astype(BF16))
    return out.reshape(bsz * seq, WIDTH)


def _mlstm_kernel(m_ref, gt_ref, par_ref, nw_ref, exp_ref, ones_ref, o_ref,
                  cn_ref, mx_ref, ig_s, b_s, cm_s, *, t, nb):
    c = CHUNK

    @pl.when(pl.program_id(1) == 0)
    def _():
        cn_ref[...] = jnp.zeros_like(cn_ref)
        mx_ref[...] = jnp.zeros_like(mx_ref)

    pair_ones = ones_ref[...]

    def front(a, carry):
        pre = gt_ref[a] + par_ref[...]
        capped = GATE_SOFTCAP * jnp.tanh(pre * (1.0 / GATE_SOFTCAP))
        log_f = -_softplus(-capped)
        b_all = _chunk_cumsum(log_f)
        ig = _dot_sel(capped, exp_ref[0])
        b = _dot_sel(b_all, exp_ref[1])
        pos = _iota2((t, WIDTH), 0) & (c - 1)
        cm = ig - b
        for sh in (1, 2, 4, 8, 16, 32):
            cm = jnp.where(pos >= sh, jnp.maximum(cm, pltpu.roll(cm, sh, axis=0)), cm)
        ig_s[a] = ig
        b_s[a] = b
        cm_s[a] = cm
        return carry

    lax.fori_loop(0, nb, front, 0)

    row = _iota2((c, WIDTH), 0)
    col = _iota2((c, WIDTH), 1) & (c - 1)
    causal = col <= row
    eye = col == row
    nw = nw_ref[...]

    def chunk_step(i, carry):
        n = range(nb)
        pairs = range(WIDTH // PAIR)
        r = pl.ds(pl.multiple_of(i * c, c), c)
        last = pl.ds(i * c + c - 1, 1)
        qb = [m_ref[a, r, 0:WIDTH].astype(BF16) for a in n]
        k = [m_ref[a, r, WIDTH:2 * WIDTH] * (HEAD_DIM ** -0.5) for a in n]
        vb = [m_ref[a, r, 2 * WIDTH:3 * WIDTH].astype(BF16) for a in n]
        ig = [ig_s[a, r, :] for a in n]
        b = [b_s[a, r, :] for a in n]
        m_intra = [b[a] + cm_s[a, r, :] for a in n]
        b_last = [b_s[a, last, :] for a in n]
        m_chunk = [b_last[a] + cm_s[a, last, :] for a in n]
        qk = [_heads_mm_nt(qb[a], k[a].astype(BF16)) for a in n]
        gate_row = [jnp.sum(jnp.where(eye, ig[a] - b[a], 0.0), axis=0, keepdims=True) for a in n]
        qk = [qk[a] * jnp.exp(jnp.where(causal, b[a] + gate_row[a], -jnp.inf) - m_intra[a]) for a in n]
        v_ones = [[jnp.concatenate([_pair_diag(vp), pair_ones], axis=1) for vp in _pairs(vb[a])]
                  for a in n]
        qk_parts = [_split2(qk[a]) for a in n]
        intra = [[_dot(qp, v_ones[a][p]) for p, qp in enumerate(_pairs(qk_parts[a][0]))] for a in n]
        num_intra = [jnp.concatenate([intra[a][p][:, 0:PAIR] for p in pairs], axis=1) for a in n]
        den_intra = [jnp.concatenate([intra[a][p][:, PAIR:2 * PAIR] for p in pairs], axis=1)
                     + _heads_sum(qk_parts[a][1], pair_ones) for a in n]
        m_prev = [mx_ref[a] for a in n]
        cn = [[cn_ref[a, p] for p in pairs] for a in n]
        inter = [[_dot(qp, cn[a][p].astype(BF16)) for p, qp in enumerate(_pairs(qb[a]))] for a in n]
        ke = [(k[a] * jnp.exp(b_last[a] - b[a] + ig[a] - m_chunk[a])).astype(BF16) for a in n]
        own = [[_dot_tn(_pair_diag(kp), v_ones[a][p]) for p, kp in enumerate(_pairs(ke[a]))]
               for a in n]
        for a in n:
            m_new = jnp.maximum(b_last[a] + m_prev[a], m_chunk[a])
            s_old = _pairs(jnp.exp(b_last[a] + m_prev[a] - m_new))
            s_new = _pairs(jnp.exp(m_chunk[a] - m_new))
            for p in pairs:
                cn_ref[a, p] = (jnp.concatenate([s_old[p], s_old[p]], axis=1) * cn[a][p]
                                + jnp.concatenate([s_new[p], s_new[p]], axis=1) * own[a][p])
            mx_ref[a] = m_new
        h = []
        for a in n:
            pre_m = b[a] + m_prev[a]
            m_t = jnp.maximum(pre_m, m_intra[a])
            s_inter = jnp.exp(pre_m - m_t)
            s_intra = jnp.exp(m_intra[a] - m_t)
            num_inter = jnp.concatenate([inter[a][p][:, 0:PAIR] for p in pairs], axis=1)
            den_inter = jnp.concatenate([inter[a][p][:, PAIR:2 * PAIR] for p in pairs], axis=1)
            num = s_inter * num_inter + s_intra * num_intra[a]
            den = s_inter * den_inter + s_intra * den_intra[a]
            h.append(num / jnp.maximum(jnp.abs(den), jnp.exp(-m_t)))
        ms = [_heads_sum((h[a] * h[a]).astype(BF16), pair_ones) * (1.0 / HEAD_DIM) for a in n]
        for a in n:
            o_ref[a, r, :] = (h[a] * lax.rsqrt(ms[a] + EPS) * nw
                              * _sigmoid(m_ref[a, r, 3 * WIDTH:4 * WIDTH]))
        return carry

    lax.fori_loop(0, t // c, chunk_step, 0)


def _mlstm(mqkvo, gates, par, norm_w, layer, bsz, seq, t, nb):
    row = lambda n: pl.BlockSpec((nb, t, n), lambda b, j: (b, j, 0))
    lay = lambda a, b2: pl.BlockSpec((None, a, b2), lambda b, j: (layer, 0, 0))
    const = lambda shape: pl.BlockSpec(shape, lambda b, j: (0,) * len(shape))
    seq3 = lambda a: a.reshape(bsz, seq, a.shape[-1])
    f32_buf = pltpu.VMEM((nb, t, WIDTH), F32)
    out = pl.pallas_call(
        functools.partial(_mlstm_kernel, t=t, nb=nb),
        grid=(bsz // nb, seq // t),
        in_specs=[row(4 * WIDTH), row(GATE_LANES), lay(1, GATE_LANES), lay(1, WIDTH),
                  const((2, GATE_LANES, WIDTH)), const((PAIR, PAIR))],
        out_specs=row(WIDTH),
        out_shape=jax.ShapeDtypeStruct((bsz, seq, WIDTH), F32),
        scratch_shapes=[
            pltpu.VMEM((nb, WIDTH // PAIR, PAIR, 2 * PAIR), F32),
            pltpu.VMEM((nb, 1, WIDTH), F32),
            f32_buf, f32_buf, f32_buf,
        ],
        compiler_params=pltpu.CompilerParams(
            dimension_semantics=("arbitrary", "arbitrary"), vmem_limit_bytes=VMEM_LIMIT),
    )(seq3(mqkvo), seq3(gates), par, norm_w,
      jnp.stack([_expander(2 * HEADS), _expander(3 * HEADS)]),
      _pair_mask().astype(BF16))
    return out.reshape(bsz * seq, WIDTH)


def _swa_stages(q_ref, kvc_ref, kvp_ref, sink_ref, o_ref, first, tq):
    tb = SWA_BLOCK
    kvw = SWA_KV_WIDTH
    nq = tq // tb
    chains = [(i, g) for i in range(nq) for g in range(2)]
    st = {}

    def setup():
        kcat = jnp.concatenate([kvp_ref[:, 0:kvw], kvc_ref[:, 0:kvw]], axis=0)
        vcat = jnp.concatenate([kvp_ref[:, kvw:2 * kvw], kvc_ref[:, kvw:2 * kvw]], axis=0)
        st["v_t"] = vcat.T.astype(BF16)
        kr = pltpu.roll(kcat, HEAD_DIM, axis=1)
        lo = _iota2((tb + tq, kvw), 1) < HEAD_DIM

        def place(x_lo, x_hi):
            return jnp.where(lo, x_lo, 0.0).astype(BF16), jnp.where(lo, 0.0, x_hi).astype(BF16)

        st["k"] = (place(kcat, kr), place(kr, kcat))
        st["ki"] = _iota2((2 * tb, tb), 0)
        qi = _iota2((2 * tb, tb), 1)
        st["in_window"] = (st["ki"] > qi) & (st["ki"] <= qi + tb)

    def scores(ci):
        i, g = chains[ci]
        keys = jnp.concatenate([st["k"][g][0][i * tb:(i + 2) * tb],
                                st["k"][g][1][i * tb:(i + 2) * tb]], axis=0)
        qs = jnp.concatenate([q_ref[i * tb:(i + 1) * tb, 256 * g:256 * g + 128],
                              q_ref[i * tb:(i + 1) * tb, 256 * g + 128:256 * g + 256]], axis=0)
        st["s", ci] = _dot_nt(keys, qs.astype(BF16))

    def probs(ci):
        i, g = chains[ci]
        sc = st.pop(("s", ci))
        mask = st["in_window"]
        if i == 0:
            mask = mask & (st["ki"] >= jnp.where(first, tb, 0))
        ps, invs = [], []
        for part in range(2):
            for e in range(2):
                sink = sink_ref[2 * g + part:2 * g + part + 1, e * 2 * tb:e * 2 * tb + 1]
                se = jnp.where(mask, sc[e * 2 * tb:(e + 1) * 2 * tb, part * tb:(part + 1) * tb],
                               -jnp.inf)
                mx = jnp.maximum(jnp.max(se, axis=0, keepdims=True), sink)
                p = jnp.exp(se - mx)
                ps.append(p.astype(BF16))
                invs.append(1.0 / (jnp.sum(p, axis=0, keepdims=True) + jnp.exp(sink - mx)))
        st["p", ci] = jnp.concatenate(ps, axis=-1)
        st["inv", ci] = jnp.concatenate(invs, axis=-1)

    def attend(ci):
        i, g = chains[ci]
        o = _dot(st["v_t"][g * HEAD_DIM:(g + 1) * HEAD_DIM, i * tb:(i + 2) * tb],
                 st.pop(("p", ci))) * st.pop(("inv", ci))
        for part in range(2):
            j = 2 * g + part
            pair_t = jnp.concatenate([o[:, 2 * part * tb:(2 * part + 1) * tb],
                                      o[:, (2 * part + 1) * tb:(2 * part + 2) * tb]], axis=0)
            o_ref[i * tb:(i + 1) * tb, 128 * j:128 * (j + 1)] = pair_t.T

    n = range(len(chains))
    return ([setup] + [functools.partial(scores, ci) for ci in n]
            + [functools.partial(probs, ci) for ci in n] + [functools.partial(attend, ci) for ci in n])


def _post_kernel(x_ref, ya_ref, yb_ref, p_ref, wo_ref, nmlp_ref, wup_ref, wdn_ref,
                 nple_ref, wg_ref, wp_ref, nfin_ref,
                 sq0_ref, skv0_ref, sqn_ref, skvn_ref, kvpn_ref, sink_ref,
                 o_ref, yc_ref, *, final, tf, tiles_per_seq):
    t = x_ref.shape[0]
    step = pl.program_id(0)

    @pl.when(step == 0)
    def _():
        for stage in _swa_stages(sq0_ref, skv0_ref, kvpn_ref, sink_ref, yc_ref, True, t):
            stage()

    x = x_ref[...]
    x = x + _dot(ya_ref[...].astype(BF16), wo_ref[0:WIDTH, :])
    x = x + _dot(yb_ref[...].astype(BF16), wo_ref[WIDTH:2 * WIDTH, :])
    x = x + _dot(yc_ref[...].astype(BF16), wo_ref[2 * WIDTH:, :])
    stages = _swa_stages(sqn_ref, skvn_ref, kvpn_ref, sink_ref, yc_ref,
                         (step + 1) % tiles_per_seq == 0, t)
    slots = 2 * (D_FF // tf)
    per_slot = -(-len(stages) // slots)

    def fill():
        for _ in range(per_slot):
            if stages:
                stages.pop(0)()

    r = _rms_scale(x)
    h = (x * nmlp_ref[...]).astype(BF16)
    acc = jnp.zeros_like(x)
    for f in range(0, D_FF, tf):
        u = jnp.maximum(_dot(h, wup_ref[:, f:f + tf]), 0.0)
        fill()
        acc = acc + _dot((u * u).astype(BF16), wdn_ref[f:f + tf, :])
        fill()
    x = x + (r * r) * acc
    gate = _sigmoid(_rms_scale(x) * _dot((x * nple_ref[...]).astype(BF16), wg_ref[...]))
    x = x + gate * _dot(p_ref[...].astype(BF16), wp_ref[...])
    if final:
        x = _rms(x, nfin_ref[...])
    o_ref[...] = x


def _post(x, ya, yb, sq, skv, sinks_e, p, w_out, norm_mlp, w_up, w_down, norm_ple, w_gate, w_proj,
          norm_final, layer, tm, seq, final):
    m = x.shape[0]
    nt = m // tm
    blocks_per_tile = tm // SWA_BLOCK
    row = lambda n: pl.BlockSpec((tm, n), lambda i: (i, 0))
    first = lambda n: pl.BlockSpec((tm, n), lambda i: (0, 0))
    nxt = lambda n: pl.BlockSpec((tm, n), lambda i: (jnp.minimum(i + 1, nt - 1), 0))
    lay = lambda a, b: pl.BlockSpec((None, a, b), lambda i: (layer, 0, 0),
                                    pipeline_mode=pl.Buffered(1))
    return pl.pallas_call(
        functools.partial(_post_kernel, final=final, tf=512, tiles_per_seq=seq // tm),
        grid=(nt,),
        in_specs=[
            row(D_MODEL), row(WIDTH), row(WIDTH),
            pl.BlockSpec((None, tm, PLE_DIM), lambda i: (layer, i, 0)),
            lay(D_MODEL, D_MODEL), lay(1, D_MODEL), lay(D_MODEL, D_FF), lay(D_FF, D_MODEL),
            lay(1, D_MODEL), lay(D_MODEL, D_MODEL), lay(PLE_DIM, D_MODEL),
            pl.BlockSpec((1, D_MODEL), lambda i: (0, 0)),
            first(SWA_WIDTH), first(2 * SWA_KV_WIDTH), nxt(SWA_WIDTH), nxt(2 * SWA_KV_WIDTH),
            pl.BlockSpec((SWA_BLOCK, 2 * SWA_KV_WIDTH),
                         lambda i: ((i + 1) * blocks_per_tile - 1, 0)),
            pl.BlockSpec((None, SWA_WIDTH // 128, 4 * SWA_BLOCK), lambda i: (layer, 0, 0)),
        ],
        out_specs=row(D_MODEL),
        out_shape=jax.ShapeDtypeStruct((m, D_MODEL), F32),
        scratch_shapes=[pltpu.VMEM((tm, SWA_WIDTH), F32)],
        compiler_params=pltpu.CompilerParams(
            dimension_semantics=("arbitrary",), vmem_limit_bytes=VMEM_LIMIT),
    )(x, ya, yb, p, w_out, norm_mlp, w_up, w_down, norm_ple, w_gate, w_proj, norm_final,
      sq, skv, sq, skv, skv, sinks_e)


def _pack_w_in(w_in):
    depth = w_in.shape[0]
    pad = jnp.zeros((depth, D_MODEL, GATE_LANES - 4 * HEADS), w_in.dtype)
    return jnp.concatenate(
        [w_in[..., 0:1024], w_in[..., 1032:2056], w_in[..., 2064:2832],
         w_in[..., 1024:1032], w_in[..., 2056:2064], pad], axis=-1).astype(BF16)


def _gate_row(depth, pieces):
    out = jnp.zeros((depth, 1, GATE_LANES), F32)
    for first_col, vals in pieces:
        out = out.at[:, 0, first_col:first_col + HEADS].set(vals.astype(F32))
    return out


def _rope_tables(positions):
    half = ROPE_DIM // 2
    inv_freq = ROPE_THETA ** (-jnp.arange(0, ROPE_DIM, 2, dtype=F32) / ROPE_DIM)
    dim = jnp.arange(128) % HEAD_DIM
    ang = positions.astype(F32).reshape(-1)[:, None] * inv_freq[dim % half][None, :]
    cos_t = jnp.where(dim < ROPE_DIM, jnp.cos(ang), 1.0)
    sin_t = jnp.where(dim < half, -jnp.sin(ang), jnp.where(dim < ROPE_DIM, jnp.sin(ang), 0.0))
    return cos_t, sin_t


def kernel(x, p, positions, w_in, conv_w, gdn_a_log, gdn_dt_bias, gdn_norm, mlstm_i_bias,
           mlstm_f_bias, mlstm_norm, attn_sinks, w_out, norm_mix, norm_mlp, w_up, w_down,
           norm_ple, w_ple_gate, w_ple_proj, norm_final):
    bsz, seq, d = x.shape
    depth = w_in.shape[0]
    m = bsz * seq
    tm = min(512, seq)
    t_mix = min(512, seq)
    nb_mix = 4 if bsz % 4 == 0 else 1
    nb_gdn = 8 if bsz % 8 == 0 else nb_mix
    t_gdn = min(2048 // nb_gdn, seq)

    cos_t, sin_t = _rope_tables(positions)
    w_packed = _pack_w_in(w_in)
    gdn_par = jnp.concatenate([_gate_row(depth, [(HEADS, gdn_a_log)]),
                               _gate_row(depth, [(HEADS, gdn_dt_bias)])], axis=1)
    mlstm_par = _gate_row(depth, [(2 * HEADS, mlstm_i_bias), (3 * HEADS, mlstm_f_bias)])
    gdn_nw = jnp.tile(gdn_norm.astype(F32), (1, HEADS))[:, None, :]
    mlstm_nw = mlstm_norm.astype(F32)[:, None, :]
    sinks_e = jnp.repeat(attn_sinks.astype(F32), 2 * SWA_BLOCK, axis=-1).reshape(
        depth, SWA_WIDTH // 128, 4 * SWA_BLOCK)
    row3 = lambda a: a.astype(F32)[:, None, :]
    wo_b, wup_b, wdn_b = w_out.astype(BF16), w_up.astype(BF16), w_down.astype(BF16)
    wg_b, wp_b = w_ple_gate.astype(BF16), w_ple_proj.astype(BF16)
    nmix, nmlp, nple = row3(norm_mix), row3(norm_mlp), row3(norm_ple)
    nfin = norm_final.astype(F32)[None, :]
    p2 = p.reshape(depth, m, PLE_DIM)

    xf = x.reshape(m, d)
    for i in range(depth):
        gqkv, gz, mqkvo, sq, skv, gates = _proj(xf, nmix, w_packed, cos_t, sin_t,
                                                conv_w.astype(F32), i, tm, seq)
        ya = _gdn(gqkv, gz, gates, gdn_par, gdn_nw, i, bsz, seq, t_gdn, nb_gdn)
        yb = _mlstm(mqkvo, gates, mlstm_par, mlstm_nw, i, bsz, seq, t_mix, nb_mix)
        xf = _post(xf, ya, yb, sq, skv, sinks_e, p2, wo_b, nmlp, wup_b, wdn_b, nple, wg_b, wp_b, nfin,
                   i, tm, seq, final=(i == depth - 1))
    return xf.reshape(bsz, seq, d)
```

```python
import functools

import jax
import jax.numpy as jnp
from jax import lax
from jax.experimental import pallas as pl
from jax.experimental.pallas import tpu as pltpu

F32 = jnp.float32
BF16 = jnp.bfloat16

D_MODEL = 1024
DEPTH = 4
HEAD_DIM = 64
PLE_DIM = 256
D_FF = 4 * D_MODEL
EPS = 1e-6
HEADS = 4
WIDTH = HEADS * HEAD_DIM
PAIR = 2 * HEAD_DIM
CHUNK = 64
CONV_WIDTH = 4
GATE_SOFTCAP = 15.0
SWA_WIDTH = 512
SWA_KV_WIDTH = 128
SWA_BLOCK = 128
ROPE_DIM = 16
ROPE_THETA = 500000.0
GATE_LANES = 128
IN_COLS_PACKED = 2944
OFF_GQKV, OFF_GZ, OFF_M, OFF_SQ, OFF_SKV, OFF_GATES = 0, 768, 1024, 2048, 2560, 2816

VMEM_LIMIT = 56 * 1024 * 1024


def _dot(a, b):
    return jnp.dot(a, b, preferred_element_type=F32)


def _dot_nt(a, b):
    return lax.dot_general(a, b, (((1,), (1,)), ((), ())), preferred_element_type=F32)


def _dot_tn(a, b):
    return lax.dot_general(a, b, (((0,), (0,)), ((), ())), preferred_element_type=F32)


def _split2(x):
    hi = x.astype(BF16)
    lo = (x - hi.astype(F32)).astype(BF16)
    return hi, lo


def _dot_sel(x, sel):
    hi, lo = _split2(x)
    return _dot(hi, sel) + _dot(lo, sel)


def _chunk_cumsum(x):
    pos = _iota2(x.shape, 0) & (CHUNK - 1)
    sh = 1
    while sh < CHUNK:
        x = x + jnp.where(pos >= sh, pltpu.roll(x, sh, axis=0), 0.0)
        sh *= 2
    return x


def _sigmoid(x):
    return 1.0 / (1.0 + jnp.exp(-x))


def _softplus(x):
    return jnp.maximum(x, 0.0) + jnp.log1p(jnp.exp(-jnp.abs(x)))


def _rms(x, g):
    return x * lax.rsqrt(jnp.mean(x * x, axis=-1, keepdims=True) + EPS) * g


def _rms_scale(x):
    return lax.rsqrt(jnp.mean(x * x, axis=-1, keepdims=True) + EPS)


def _iota2(shape, dim):
    return lax.broadcasted_iota(jnp.int32, shape, dim)


def _pairs(x):
    return [x[:, p * PAIR:(p + 1) * PAIR] for p in range(WIDTH // PAIR)]


def _pair_mask():
    return (_iota2((PAIR, PAIR), 0) >> 6) == (_iota2((PAIR, PAIR), 1) >> 6)


def _pair_diag(x):
    low = _iota2(x.shape, 1) < HEAD_DIM
    zero = jnp.zeros((), x.dtype)
    return jnp.concatenate([jnp.where(low, x, zero), jnp.where(low, zero, x)], axis=0)


def _heads_mm(a, b):
    return jnp.concatenate([_dot(ap, _pair_diag(bp)) for ap, bp in zip(_pairs(a), _pairs(b))], axis=1)


def _heads_mm_nt(a, b):
    return jnp.concatenate([_dot_nt(ap, _pair_diag(bp)) for ap, bp in zip(_pairs(a), _pairs(b))],
                           axis=1)


def _heads_sum(x, pair_ones):
    return jnp.concatenate([_dot(xp, pair_ones) for xp in _pairs(x)], axis=1)


def _expander(first_col):
    r = _iota2((GATE_LANES, WIDTH), 0)
    c = _iota2((GATE_LANES, WIDTH), 1)
    return (r == first_col + (c >> 6)).astype(BF16)


def _rope(x, cos_t, sin_t, reps):
    n = x.shape[-1]
    if reps > 1:
        cos_t = jnp.concatenate([cos_t] * reps, axis=-1)
        sin_t = jnp.concatenate([sin_t] * reps, axis=-1)
    half = ROPE_DIM // 2
    x_up = pltpu.roll(x, n - half, axis=1)
    x_dn = pltpu.roll(x, half, axis=1)
    first_half = (_iota2(x.shape, 1) & (HEAD_DIM - 1)) < half
    return x * cos_t + jnp.where(first_half, x_up, x_dn) * sin_t


def _repack_w_in(w_in_ref, w_ref):
    rows = 128

    def body(i, carry):
        r = pl.ds(pl.multiple_of(i * rows, rows), rows)
        w_ref[r, 0:1024] = w_in_ref[r, 0:1024].astype(BF16)
        w_ref[r, 1024:2048] = w_in_ref[r, 1032:2056].astype(BF16)
        w_ref[r, 2048:2816] = w_in_ref[r, 2064:2832].astype(BF16)
        lane = _iota2((rows, GATE_LANES), 1)
        gates = jnp.where(lane < 2 * HEADS, w_in_ref[r, 1024:1152],
                          jnp.where(lane < 4 * HEADS, w_in_ref[r, 2048:2176], 0.0))
        w_ref[r, OFF_GATES:IN_COLS_PACKED] = gates.astype(BF16)
        return carry

    lax.fori_loop(0, D_MODEL // rows, body, 0)


def _proj_kernel(x_ref, g_ref, w_in_ref, cos_ref, sin_ref, cw_ref, ones_ref,
                 gqkv_ref, gz_ref, m_ref, sq_ref, skv_ref, gates_ref, tail_ref, buf_ref, w_ref,
                 *, tiles_per_seq):
    t = x_ref.shape[0]

    @pl.when(pl.program_id(0) == 0)
    def _():
        _repack_w_in(w_in_ref, w_ref)

    @pl.when(pl.program_id(0) % tiles_per_seq == 0)
    def _():
        tail_ref[...] = jnp.zeros_like(tail_ref)

    x = x_ref[...]
    hb = (x * g_ref[...]).astype(BF16)
    scale = _rms_scale(x)

    def proj(lo, hi):
        return scale * _dot(hb, w_ref[:, lo:hi])

    buf_ref[0:8, :] = tail_ref[...]
    buf_ref[8:8 + t, :] = proj(OFF_GQKV, OFF_GZ)
    tail_ref[...] = buf_ref[t:t + 8, :]
    cw = cw_ref[...]
    pair_ones = ones_ref[...]

    def conv_silu(lo, hi):
        acc = buf_ref[5:5 + t, lo:hi] * cw[0:1, lo:hi]
        for j in range(1, CONV_WIDTH):
            acc = acc + buf_ref[5 + j:5 + j + t, lo:hi] * cw[j:j + 1, lo:hi]
        return acc * _sigmoid(acc)

    def l2n(v):
        return v * lax.rsqrt(_heads_sum((v * v).astype(BF16), pair_ones) + EPS)

    cos_t, sin_t = cos_ref[...], sin_ref[...]
    sq = proj(OFF_SQ, OFF_SKV)
    q = conv_silu(0, WIDTH)
    kv = proj(OFF_SKV, OFF_GATES)
    sq_ref[...] = _rope(sq, cos_t, sin_t, SWA_WIDTH // 128) * (HEAD_DIM ** -0.5)
    gqkv_ref[:, 0:WIDTH] = l2n(q) * (HEAD_DIM ** -0.5)
    m_ref[:, 0:2 * WIDTH] = proj(OFF_M, OFF_M + 2 * WIDTH)
    skv_ref[:, 0:SWA_KV_WIDTH] = _rope(kv[:, 0:SWA_KV_WIDTH], cos_t, sin_t, 1)
    skv_ref[:, SWA_KV_WIDTH:] = kv[:, SWA_KV_WIDTH:]
    k = conv_silu(WIDTH, 2 * WIDTH)
    m_ref[:, 2 * WIDTH:4 * WIDTH] = proj(OFF_M + 2 * WIDTH, OFF_SQ)
    gqkv_ref[:, WIDTH:2 * WIDTH] = l2n(k)
    gates_ref[...] = proj(OFF_GATES, IN_COLS_PACKED)
    gqkv_ref[:, 2 * WIDTH:3 * WIDTH] = conv_silu(2 * WIDTH, 3 * WIDTH)
    gz_ref[...] = proj(OFF_GZ, OFF_M)


def _proj(x, norm_w, w_in, cos_t, sin_t, conv_w, layer, tm, seq):
    m = x.shape[0]
    row = lambda n: pl.BlockSpec((tm, n), lambda i: (i, 0))
    widths = (768, 256, 1024, SWA_WIDTH, 2 * SWA_KV_WIDTH, GATE_LANES)
    return pl.pallas_call(
        functools.partial(_proj_kernel, tiles_per_seq=seq // tm),
        grid=(m // tm,),
        in_specs=[
            row(D_MODEL),
            pl.BlockSpec((None, 1, D_MODEL), lambda i: (layer, 0, 0)),
            pl.BlockSpec((None, D_MODEL, w_in.shape[-1]), lambda i: (layer, 0, 0),
                         pipeline_mode=pl.Buffered(1)),
            row(128), row(128),
            pl.BlockSpec((None, CONV_WIDTH, 3 * WIDTH), lambda i: (layer, 0, 0)),
            pl.BlockSpec((PAIR, PAIR), lambda i: (0, 0)),
        ],
        out_specs=[row(n) for n in widths],
        out_shape=[jax.ShapeDtypeStruct((m, n), F32) for n in widths],
        scratch_shapes=[
            pltpu.VMEM((8, 3 * WIDTH), F32),
            pltpu.VMEM((tm + 8, 3 * WIDTH), F32),
            pltpu.VMEM((D_MODEL, IN_COLS_PACKED), BF16),
        ],
        compiler_params=pltpu.CompilerParams(
            dimension_semantics=("arbitrary",), vmem_limit_bytes=VMEM_LIMIT),
    )(x, norm_w, w_in, cos_t, sin_t, conv_w, _pair_mask().astype(BF16))


def _gdn_kernel(qkv_ref, z_ref, gt_ref, par_ref, nw_ref, exp_ref, ones_ref, o_ref,
                state_ref, beta_s, gc_s, u_s, w_s, qd_s, kd_s, qkd_s, *, t, group, nb):
    c = CHUNK

    @pl.when(pl.program_id(1) == 0)
    def _():
        state_ref[...] = jnp.zeros_like(state_ref)

    pair_ones = ones_ref[...]

    def gate_rows(a, lo, hi):
        gates = gt_ref[a, lo:hi, :]
        par = par_ref[...]
        beta = _sigmoid(gates)
        g = -jnp.exp(par[0:1, :]) * _softplus(gates + par[1:2, :])
        beta_s[a, lo:hi, :] = _dot_sel(beta, exp_ref[0])
        gc_s[a, lo:hi, :] = _dot_sel(_chunk_cumsum(g), exp_ref[1])

    row = _iota2((c, WIDTH), 0)
    col = _iota2((c, WIDTH), 1) & (c - 1)
    causal = col <= row
    strict = col < row
    eye = col == row
    nw = nw_ref[...]

    def mm(a, b):
        return _heads_mm(a.astype(BF16), b.astype(BF16))

    def prepare(items, fillers):
        fillers = list(fillers)

        def fill():
            if fillers:
                fillers.pop(0)()

        n = range(len(items))
        sq = [a for a, _ in items]
        rs = [pl.ds(j * c, c) for _, j in items]
        q = [qkv_ref[sq[j], rs[j], 0:WIDTH] for j in n]
        k = [qkv_ref[sq[j], rs[j], WIDTH:2 * WIDTH] for j in n]
        gc = [gc_s[sq[j], rs[j], :] for j in n]
        g_last = [gc_s[a, pl.ds(j * c + c - 1, 1), :] for a, j in items]
        kb = [k[j] * beta_s[sq[j], rs[j], :] for j in n]
        kkqk = [_heads_mm_nt(jnp.concatenate([kb[j], q[j]], axis=0).astype(BF16), k[j].astype(BF16))
                for j in n]
        fill()
        gc_row = [jnp.sum(jnp.where(eye, gc[j], 0.0), axis=0, keepdims=True) for j in n]
        decay = [jnp.exp(jnp.where(causal, gc[j] - gc_row[j], -jnp.inf)) for j in n]
        nm = [jnp.where(strict, kkqk[j][0:c] * decay[j], 0.0) for j in n]
        for j in n:
            qkd_s[sq[j], rs[j], :] = (kkqk[j][c:2 * c] * decay[j]).astype(BF16)
        x = [eye.astype(F32) - jnp.where((row >> 1) == (col >> 1), nm[j], 0.0) for j in n]
        for lb in range(1, 6):
            off = ((row >> (lb + 1)) == (col >> (lb + 1))) & ((row >> lb) != (col >> lb))
            t1 = [mm(jnp.where(off, nm[j], 0.0), x[j]) for j in n]
            fill()
            t2 = [mm(x[j], t1[j]) for j in n]
            if lb % 2 == 0:
                fill()
            x = [x[j] - t2[j] for j in n]
        xb = [x[j].astype(BF16) for j in n]
        egc = [jnp.exp(gc[j]) for j in n]
        for j in n:
            u_s[sq[j], rs[j], :] = _heads_mm(
                xb[j], (qkv_ref[sq[j], rs[j], 2 * WIDTH:3 * WIDTH] * beta_s[sq[j], rs[j], :]).astype(BF16))
        for j in n:
            w_s[sq[j], rs[j], :] = _heads_mm(xb[j], (kb[j] * egc[j]).astype(BF16)).astype(BF16)
        for j in n:
            qd_s[sq[j], rs[j], :] = (q[j] * egc[j]).astype(BF16)
            kd_s[sq[j], rs[j], :] = (k[j] * jnp.exp(g_last[j] - gc[j])).astype(BF16)
        while fillers:
            fill()

    chunks = [(a, j) for a in range(nb) for j in range(t // c)]
    groups = [chunks[i:i + group] for i in range(0, len(chunks), group)]
    for a, j in groups[0]:
        gate_rows(a, j * c, (j + 1) * c)
    for gi, items in enumerate(groups):
        nxt = groups[gi + 1] if gi + 1 < len(groups) else []
        prepare(items, [functools.partial(gate_rows, a, j * c, (j + 1) * c) for a, j in nxt])

    def chunk_step(i, carry):
        n = range(nb)
        r = pl.ds(pl.multiple_of(i * c, c), c)
        pairs = range(WIDTH // PAIR)
        g_last = [gc_s[a, pl.ds(i * c + c - 1, 1), :] for a in n]
        s = [[state_ref[a, p] for p in pairs] for a in n]
        lhs = [_pairs(jnp.concatenate([w_s[a, r, :], qd_s[a, r, :]], axis=0)) for a in n]
        ws = [jnp.concatenate([_dot(lhs[a][p], s[a][p].astype(BF16)) for p in pairs], axis=1)
              for a in n]
        v_new = [(u_s[a, r, :] - ws[a][0:c]).astype(BF16) for a in n]
        v_diag = [[_pair_diag(vn) for vn in _pairs(v_new[a])] for a in n]
        upd = [[_dot_tn(_pair_diag(kd), v_diag[a][p]) for p, kd in enumerate(_pairs(kd_s[a, r, :]))]
               for a in n]
        for a in n:
            decay_end = _pairs(jnp.exp(g_last[a]))
            for p in pairs:
                state_ref[a, p] = s[a][p] * decay_end[p] + upd[a][p]
        o = [ws[a][c:2 * c] + jnp.concatenate(
            [_dot(qp, v_diag[a][p]) for p, qp in enumerate(_pairs(qkd_s[a, r, :]))], axis=1) for a in n]
        ms = [_heads_sum((o[a] * o[a]).astype(BF16), pair_ones) * (1.0 / HEAD_DIM) for a in n]
        for a in n:
            z = z_ref[a, r, :]
            o_ref[a, r, :] = o[a] * lax.rsqrt(ms[a] + EPS) * nw * (z * _sigmoid(z))
        return carry

    lax.fori_loop(0, t // c, chunk_step, 0)


def _gdn(gqkv, gz, gates, par, norm_w, layer, bsz, seq, t, nb):
    row = lambda n: pl.BlockSpec((nb, t, n), lambda b, j: (b, j, 0))
    lay = lambda a, b2: pl.BlockSpec((None, a, b2), lambda b, j: (layer, 0, 0))
    const = lambda shape: pl.BlockSpec(shape, lambda b, j: (0,) * len(shape))
    seq3 = lambda a: a.reshape(bsz, seq, a.shape[-1])
    f32_buf = pltpu.VMEM((nb, t, WIDTH), F32)
    bf16_buf = pltpu.VMEM((nb, t, WIDTH), BF16)
    out = pl.pallas_call(
        functools.partial(_gdn_kernel, t=t, group=8, nb=nb),
        grid=(bsz // nb, seq // t),
        in_specs=[row(3 * WIDTH), row(WIDTH), row(GATE_LANES),
                  lay(2, GATE_LANES), lay(1, WIDTH),
                  const((2, GATE_LANES, WIDTH)), const((PAIR, PAIR))],
        out_specs=row(WIDTH),
        out_shape=jax.ShapeDtypeStruct((bsz, seq, WIDTH), F32),
        scratch_shapes=[
            pltpu.VMEM((nb, WIDTH // PAIR, PAIR, PAIR), F32),
            f32_buf, f32_buf,
            f32_buf,
            bf16_buf, bf16_buf, bf16_buf, bf16_buf,
        ],
        compiler_params=pltpu.CompilerParams(
            dimension_semantics=("arbitrary", "arbitrary"), vmem_limit_bytes=VMEM_LIMIT),
    )(seq3(gqkv), seq3(gz), seq3(gates), par, norm_w,
      jnp.stack([_expander(0), _expander(HEADS)]), _pair_mask().astype(BF16))
    return out.reshape(bsz * seq, WIDTH)


def _mlstm_kernel(m_ref, gt_ref, par_ref, nw_ref, exp_ref, ones_ref, o_ref,
                  cn_ref, mx_ref, ig_s, b_s, cm_s, *, t, nb):
    c = CHUNK

    @pl.when(pl.program_id(1) == 0)
    def _():
        cn_ref[...] = jnp.zeros_like(cn_ref)
        mx_ref[...] = jnp.zeros_like(mx_ref)

    pair_ones = ones_ref[...]

    def front(a, carry):
        pre = gt_ref[a] + par_ref[...]
        capped = GATE_SOFTCAP * jnp.tanh(pre * (1.0 / GATE_SOFTCAP))
        log_f = -_softplus(-capped)
        b_all = _chunk_cumsum(log_f)
        ig = _dot_sel(capped, exp_ref[0])
        b = _dot_sel(b_all, exp_ref[1])
        pos = _iota2((t, WIDTH), 0) & (c - 1)
        cm = ig - b
        for sh in (1, 2, 4, 8, 16, 32):
            cm = jnp.where(pos >= sh, jnp.maximum(cm, pltpu.roll(cm, sh, axis=0)), cm)
        ig_s[a] = ig
        b_s[a] = b
        cm_s[a] = cm
        return carry

    lax.fori_loop(0, nb, front, 0)

    row = _iota2((c, WIDTH), 0)
    col = _iota2((c, WIDTH), 1) & (c - 1)
    causal = col <= row
    eye = col == row
    nw = nw_ref[...]

    def chunk_step(i, carry):
        n = range(nb)
        pairs = range(WIDTH // PAIR)
        r = pl.ds(pl.multiple_of(i * c, c), c)
        last = pl.ds(i * c + c - 1, 1)
        qb = [m_ref[a, r, 0:WIDTH].astype(BF16) for a in n]
        k = [m_ref[a, r, WIDTH:2 * WIDTH] * (HEAD_DIM ** -0.5) for a in n]
        vb = [m_ref[a, r, 2 * WIDTH:3 * WIDTH].astype(BF16) for a in n]
        ig = [ig_s[a, r, :] for a in n]
        b = [b_s[a, r, :] for a in n]
        m_intra = [b[a] + cm_s[a, r, :] for a in n]
        b_last = [b_s[a, last, :] for a in n]
        m_chunk = [b_last[a] + cm_s[a, last, :] for a in n]
        qk = [_heads_mm_nt(qb[a], k[a].astype(BF16)) for a in n]
        gate_row = [jnp.sum(jnp.where(eye, ig[a] - b[a], 0.0), axis=0, keepdims=True) for a in n]
        qk = [qk[a] * jnp.exp(jnp.where(causal, b[a] + gate_row[a], -jnp.inf) - m_intra[a]) for a in n]
        v_ones = [[jnp.concatenate([_pair_diag(vp), pair_ones], axis=1) for vp in _pairs(vb[a])]
                  for a in n]
        qk_parts = [_split2(qk[a]) for a in n]
        intra = [[_dot(qp, v_ones[a][p]) for p, qp in enumerate(_pairs(qk_parts[a][0]))] for a in n]
        num_intra = [jnp.concatenate([intra[a][p][:, 0:PAIR] for p in pairs], axis=1) for a in n]
        den_intra = [jnp.concatenate([intra[a][p][:, PAIR:2 * PAIR] for p in pairs], axis=1)
                     + _heads_sum(qk_parts[a][1], pair_ones) for a in n]
        m_prev = [mx_ref[a] for a in n]
        cn = [[cn_ref[a, p] for p in pairs] for a in n]
        inter = [[_dot(qp, cn[a][p].astype(BF16)) for p, qp in enumerate(_pairs(qb[a]))] for a in n]
        ke = [(k[a] * jnp.exp(b_last[a] - b[a] + ig[a] - m_chunk[a])).astype(BF16) for a in n]
        own = [[_dot_tn(_pair_diag(kp), v_ones[a][p]) for p, kp in enumerate(_pairs(ke[a]))]
               for a in n]
        for a in n:
            m_new = jnp.maximum(b_last[a] + m_prev[a], m_chunk[a])
            s_old = _pairs(jnp.exp(b_last[a] + m_prev[a] - m_new))
            s_new = _pairs(jnp.exp(m_chunk[a] - m_new))
            for p in pairs:
                cn_ref[a, p] = (jnp.concatenate([s_old[p], s_old[p]], axis=1) * cn[a][p]
                                + jnp.concatenate([s_new[p], s_new[p]], axis=1) * own[a][p])
            mx_ref[a] = m_new
        h = []
        for a in n:
            pre_m = b[a] + m_prev[a]
            m_t = jnp.maximum(pre_m, m_intra[a])
            s_inter = jnp.exp(pre_m - m_t)
            s_intra = jnp.exp(m_intra[a] - m_t)
            num_inter = jnp.concatenate([inter[a][p][:, 0:PAIR] for p in pairs], axis=1)
            den_inter = jnp.concatenate([inter[a][p][:, PAIR:2 * PAIR] for p in pairs], axis=1)
            num = s_inter * num_inter + s_intra * num_intra[a]
            den = s_inter * den_inter + s_intra * den_intra[a]
            h.append(num / jnp.maximum(jnp.abs(den), jnp.exp(-m_t)))
        ms = [_heads_sum((h[a] * h[a]).astype(BF16), pair_ones) * (1.0 / HEAD_DIM) for a in n]
        for a in n:
            o_ref[a, r, :] = (h[a] * lax.rsqrt(ms[a] + EPS) * nw
                              * _sigmoid(m_ref[a, r, 3 * WIDTH:4 * WIDTH]))
        return carry

    lax.fori_loop(0, t // c, chunk_step, 0)


def _mlstm(mqkvo, gates, par, norm_w, layer, bsz, seq, t, nb):
    row = lambda n: pl.BlockSpec((nb, t, n), lambda b, j: (b, j, 0))
    lay = lambda a, b2: pl.BlockSpec((None, a, b2), lambda b, j: (layer, 0, 0))
    const = lambda shape: pl.BlockSpec(shape, lambda b, j: (0,) * len(shape))
    seq3 = lambda a: a.reshape(bsz, seq, a.shape[-1])
    f32_buf = pltpu.VMEM((nb, t, WIDTH), F32)
    out = pl.pallas_call(
        functools.partial(_mlstm_kernel, t=t, nb=nb),
        grid=(bsz // nb, seq // t),
        in_specs=[row(4 * WIDTH), row(GATE_LANES), lay(1, GATE_LANES), lay(1, WIDTH),
                  const((2, GATE_LANES, WIDTH)), const((PAIR, PAIR))],
        out_specs=row(WIDTH),
        out_shape=jax.ShapeDtypeStruct((bsz, seq, WIDTH), F32),
        scratch_shapes=[
            pltpu.VMEM((nb, WIDTH // PAIR, PAIR, 2 * PAIR), F32),
            pltpu.VMEM((nb, 1, WIDTH), F32),
            f32_buf, f32_buf, f32_buf,
        ],
        compiler_params=pltpu.CompilerParams(
            dimension_semantics=("arbitrary", "arbitrary"), vmem_limit_bytes=VMEM_LIMIT),
    )(seq3(mqkvo), seq3(gates), par, norm_w,
      jnp.stack([_expander(2 * HEADS), _expander(3 * HEADS)]),
      _pair_mask().astype(BF16))
    return out.reshape(bsz * seq, WIDTH)


def _swa_stages(q_ref, kvc_ref, kvp_ref, sink_ref, o_ref, first, tq):
    tb = SWA_BLOCK
    kvw = SWA_KV_WIDTH
    nq = tq // tb
    chains = [(i, g) for i in range(nq) for g in range(2)]
    st = {}

    def setup():
        kcat = jnp.concatenate([kvp_ref[:, 0:kvw], kvc_ref[:, 0:kvw]], axis=0)
        vcat = jnp.concatenate([kvp_ref[:, kvw:2 * kvw], kvc_ref[:, kvw:2 * kvw]], axis=0)
        st["v_t"] = vcat.T.astype(BF16)
        kr = pltpu.roll(kcat, HEAD_DIM, axis=1)
        lo = _iota2((tb + tq, kvw), 1) < HEAD_DIM

        def place(x_lo, x_hi):
            return jnp.where(lo, x_lo, 0.0).astype(BF16), jnp.where(lo, 0.0, x_hi).astype(BF16)

        st["k"] = (place(kcat, kr), place(kr, kcat))
        st["ki"] = _iota2((2 * tb, tb), 0)
        qi = _iota2((2 * tb, tb), 1)
        st["in_window"] = (st["ki"] > qi) & (st["ki"] <= qi + tb)

    def scores(ci):
        i, g = chains[ci]
        keys = jnp.concatenate([st["k"][g][0][i * tb:(i + 2) * tb],
                                st["k"][g][1][i * tb:(i + 2) * tb]], axis=0)
        qs = jnp.concatenate([q_ref[i * tb:(i + 1) * tb, 256 * g:256 * g + 128],
                              q_ref[i * tb:(i + 1) * tb, 256 * g + 128:256 * g + 256]], axis=0)
        st["s", ci] = _dot_nt(keys, qs.astype(BF16))

    def probs(ci):
        i, g = chains[ci]
        sc = st.pop(("s", ci))
        mask = st["in_window"]
        if i == 0:
            mask = mask & (st["ki"] >= jnp.where(first, tb, 0))
        ps, invs = [], []
        for part in range(2):
            for e in range(2):
                sink = sink_ref[2 * g + part:2 * g + part + 1, e * 2 * tb:e * 2 * tb + 1]
                se = jnp.where(mask, sc[e * 2 * tb:(e + 1) * 2 * tb, part * tb:(part + 1) * tb],
                               -jnp.inf)
                mx = jnp.maximum(jnp.max(se, axis=0, keepdims=True), sink)
                p = jnp.exp(se - mx)
                ps.append(p.astype(BF16))
                invs.append(1.0 / (jnp.sum(p, axis=0, keepdims=True) + jnp.exp(sink - mx)))
        st["p", ci] = jnp.concatenate(ps, axis=-1)
        st["inv", ci] = jnp.concatenate(invs, axis=-1)

    def attend(ci):
        i, g = chains[ci]
        o = _dot(st["v_t"][g * HEAD_DIM:(g + 1) * HEAD_DIM, i * tb:(i + 2) * tb],
                 st.pop(("p", ci))) * st.pop(("inv", ci))
        for part in range(2):
            j = 2 * g + part
            pair_t = jnp.concatenate([o[:, 2 * part * tb:(2 * part + 1) * tb],
                                      o[:, (2 * part + 1) * tb:(2 * part + 2) * tb]], axis=0)
            o_ref[i * tb:(i + 1) * tb, 128 * j:128 * (j + 1)] = pair_t.T

    n = len(chains)
    stages = [setup]
    for ci in range(n + 2):
        stages += [functools.partial(f, cj) for f, cj in ((scores, ci), (probs, ci - 1), (attend, ci - 2))
                   if 0 <= cj < n]
    return stages


def _post_kernel(x_ref, ya_ref, yb_ref, p_ref, wo_ref, nmlp_ref, wup_ref, wdn_ref,
                 nple_ref, wg_ref, wp_ref, nfin_ref,
                 sq0_ref, skv0_ref, sqn_ref, skvn_ref, kvpn_ref, sink_ref,
                 o_ref, yc_ref, *, final, tf, tiles_per_seq):
    t = x_ref.shape[0]
    step = pl.program_id(0)

    @pl.when(step == 0)
    def _():
        for stage in _swa_stages(sq0_ref, skv0_ref, kvpn_ref, sink_ref, yc_ref, True, t):
            stage()

    x = x_ref[...]
    x = x + _dot(ya_ref[...].astype(BF16), wo_ref[0:WIDTH, :])
    x = x + _dot(yb_ref[...].astype(BF16), wo_ref[WIDTH:2 * WIDTH, :])
    x = x + _dot(yc_ref[...].astype(BF16), wo_ref[2 * WIDTH:, :])
    stages = _swa_stages(sqn_ref, skvn_ref, kvpn_ref, sink_ref, yc_ref,
                         (step + 1) % tiles_per_seq == 0, t)
    slots = 2 * (D_FF // tf)
    per_slot = -(-len(stages) // slots)

    def fill():
        for _ in range(per_slot):
            if stages:
                stages.pop(0)()

    r = _rms_scale(x)
    h = (x * nmlp_ref[...]).astype(BF16)
    acc = jnp.zeros_like(x)
    for f in range(0, D_FF, tf):
        u = jnp.maximum(_dot(h, wup_ref[:, f:f + tf]), 0.0)
        fill()
        acc = acc + _dot((u * u).astype(BF16), wdn_ref[f:f + tf, :])
        fill()
    x = x + (r * r) * acc
    gate = _sigmoid(_rms_scale(x) * _dot((x * nple_ref[...]).astype(BF16), wg_ref[...]))
    x = x + gate * _dot(p_ref[...].astype(BF16), wp_ref[...])
    if final:
        x = _rms(x, nfin_ref[...])
    o_ref[...] = x


def _post(x, ya, yb, sq, skv, sinks_e, p, w_out, norm_mlp, w_up, w_down, norm_ple, w_gate, w_proj,
          norm_final, layer, tm, seq, final):
    m = x.shape[0]
    nt = m // tm
    blocks_per_tile = tm // SWA_BLOCK
    row = lambda n: pl.BlockSpec((tm, n), lambda i: (i, 0))
    first = lambda n: pl.BlockSpec((tm, n), lambda i: (0, 0))
    nxt = lambda n: pl.BlockSpec((tm, n), lambda i: (jnp.minimum(i + 1, nt - 1), 0))
    lay = lambda a, b: pl.BlockSpec((None, a, b), lambda i: (layer, 0, 0),
                                    pipeline_mode=pl.Buffered(1))
    return pl.pallas_call(
        functools.partial(_post_kernel, final=final, tf=512, tiles_per_seq=seq // tm),
        grid=(nt,),
        in_specs=[
            row(D_MODEL), row(WIDTH), row(WIDTH),
            pl.BlockSpec((None, tm, PLE_DIM), lambda i: (layer, i, 0)),
            lay(D_MODEL, D_MODEL), lay(1, D_MODEL), lay(D_MODEL, D_FF), lay(D_FF, D_MODEL),
            lay(1, D_MODEL), lay(D_MODEL, D_MODEL), lay(PLE_DIM, D_MODEL),
            pl.BlockSpec((1, D_MODEL), lambda i: (0, 0)),
            first(SWA_WIDTH), first(2 * SWA_KV_WIDTH), nxt(SWA_WIDTH), nxt(2 * SWA_KV_WIDTH),
            pl.BlockSpec((SWA_BLOCK, 2 * SWA_KV_WIDTH),
                         lambda i: ((i + 1) * blocks_per_tile - 1, 0)),
            pl.BlockSpec((None, SWA_WIDTH // 128, 4 * SWA_BLOCK), lambda i: (layer, 0, 0)),
        ],
        out_specs=row(D_MODEL),
        out_shape=jax.ShapeDtypeStruct((m, D_MODEL), F32),
        scratch_shapes=[pltpu.VMEM((tm, SWA_WIDTH), F32)],
        compiler_params=pltpu.CompilerParams(
            dimension_semantics=("arbitrary",), vmem_limit_bytes=VMEM_LIMIT),
    )(x, ya, yb, p, w_out, norm_mlp, w_up, w_down, norm_ple, w_gate, w_proj, norm_final,
      sq, skv, sq, skv, skv, sinks_e)


def _gate_row(depth, pieces):
    out = jnp.zeros((depth, 1, GATE_LANES), F32)
    for first_col, vals in pieces:
        out = out.at[:, 0, first_col:first_col + HEADS].set(vals.astype(F32))
    return out


def _rope_tables(positions):
    half = ROPE_DIM // 2
    inv_freq = ROPE_THETA ** (-jnp.arange(0, ROPE_DIM, 2, dtype=F32) / ROPE_DIM)
    dim = jnp.arange(128) % HEAD_DIM
    ang = positions.astype(F32).reshape(-1)[:, None] * inv_freq[dim % half][None, :]
    cos_t = jnp.where(dim < ROPE_DIM, jnp.cos(ang), 1.0)
    sin_t = jnp.where(dim < half, -jnp.sin(ang), jnp.where(dim < ROPE_DIM, jnp.sin(ang), 0.0))
    return cos_t, sin_t


def kernel(x, p, positions, w_in, conv_w, gdn_a_log, gdn_dt_bias, gdn_norm, mlstm_i_bias,
           mlstm_f_bias, mlstm_norm, attn_sinks, w_out, norm_mix, norm_mlp, w_up, w_down,
           norm_ple, w_ple_gate, w_ple_proj, norm_final):
    bsz, seq, d = x.shape
    depth = w_in.shape[0]
    m = bsz * seq
    tm = min(512, seq)
    t_mix = min(512, seq)
    nb_mix = 4 if bsz % 4 == 0 else 1
    nb_gdn = 8 if bsz % 8 == 0 else nb_mix
    t_gdn = min(2048 // nb_gdn, seq)

    cos_t, sin_t = _rope_tables(positions)
    gdn_par = jnp.concatenate([_gate_row(depth, [(HEADS, gdn_a_log)]),
                               _gate_row(depth, [(HEADS, gdn_dt_bias)])], axis=1)
    mlstm_par = _gate_row(depth, [(2 * HEADS, mlstm_i_bias), (3 * HEADS, mlstm_f_bias)])
    gdn_nw = jnp.tile(gdn_norm.astype(F32), (1, HEADS))[:, None, :]
    mlstm_nw = mlstm_norm.astype(F32)[:, None, :]
    sinks_e = jnp.repeat(attn_sinks.astype(F32), 2 * SWA_BLOCK, axis=-1).reshape(
        depth, SWA_WIDTH // 128, 4 * SWA_BLOCK)
    row3 = lambda a: a.astype(F32)[:, None, :]
    wo_b, wup_b, wdn_b = w_out.astype(BF16), w_up.astype(BF16), w_down.astype(BF16)
    wg_b, wp_b = w_ple_gate.astype(BF16), w_ple_proj.astype(BF16)
    nmix, nmlp, nple = row3(norm_mix), row3(norm_mlp), row3(norm_ple)
    nfin = norm_final.astype(F32)[None, :]
    p2 = p.reshape(depth, m, PLE_DIM)

    xf = x.reshape(m, d)
    for i in range(depth):
        gqkv, gz, mqkvo, sq, skv, gates = _proj(xf, nmix, w_in.astype(F32), cos_t, sin_t,
                                                conv_w.astype(F32), i, tm, seq)
        ya = _gdn(gqkv, gz, gates, gdn_par, gdn_nw, i, bsz, seq, t_gdn, nb_gdn)
        yb = _mlstm(mqkvo, gates, mlstm_par, mlstm_nw, i, bsz, seq, t_mix, nb_mix)
        xf = _post(xf, ya, yb, sq, skv, sinks_e, p2, wo_b, nmlp, wup_b, wdn_b, nple, wg_b, wp_b, nfin,
                   i, tm, seq, final=(i == depth - 1))
    return xf.reshape(bsz, seq, d)
```

```python
import functools

import jax
import jax.numpy as jnp
from jax import lax
from jax.experimental import pallas as pl
from jax.experimental.pallas import tpu as pltpu

F32 = jnp.float32
BF16 = jnp.bfloat16

D_MODEL = 1024
DEPTH = 4
HEAD_DIM = 64
PLE_DIM = 256
D_FF = 4 * D_MODEL
EPS = 1e-6
HEADS = 4
WIDTH = HEADS * HEAD_DIM
PAIR = 2 * HEAD_DIM
CHUNK = 64
CONV_WIDTH = 4
GATE_SOFTCAP = 15.0
SWA_WIDTH = 512
SWA_KV_WIDTH = 128
SWA_BLOCK = 128
ROPE_DIM = 16
ROPE_THETA = 500000.0
GATE_LANES = 128
IN_COLS_PACKED = 2944
OFF_GQKV, OFF_GZ, OFF_M, OFF_SQ, OFF_SKV, OFF_GATES = 0, 768, 1024, 2048, 2560, 2816

VMEM_LIMIT = 56 * 1024 * 1024


def _dot(a, b):
    return jnp.dot(a, b, preferred_element_type=F32)


def _dot_nt(a, b):
    return lax.dot_general(a, b, (((1,), (1,)), ((), ())), preferred_element_type=F32)


def _dot_tn(a, b):
    return lax.dot_general(a, b, (((0,), (0,)), ((), ())), preferred_element_type=F32)


def _split2(x):
    hi = x.astype(BF16)
    lo = (x - hi.astype(F32)).astype(BF16)
    return hi, lo


def _dot_sel(x, sel):
    hi, lo = _split2(x)
    return _dot(hi, sel) + _dot(lo, sel)


def _chunk_cumsum(x):
    pos = _iota2(x.shape, 0) & (CHUNK - 1)
    sh = 1
    while sh < CHUNK:
        x = x + jnp.where(pos >= sh, pltpu.roll(x, sh, axis=0), 0.0)
        sh *= 2
    return x


def _sigmoid(x):
    return 1.0 / (1.0 + jnp.exp(-x))


def _softplus(x):
    return jnp.maximum(x, 0.0) + jnp.log1p(jnp.exp(-jnp.abs(x)))


def _rms(x, g):
    return x * lax.rsqrt(jnp.mean(x * x, axis=-1, keepdims=True) + EPS) * g


def _rms_scale(x):
    return lax.rsqrt(jnp.mean(x * x, axis=-1, keepdims=True) + EPS)


def _iota2(shape, dim):
    return lax.broadcasted_iota(jnp.int32, shape, dim)


def _pairs(x):
    return [x[:, p * PAIR:(p + 1) * PAIR] for p in range(WIDTH // PAIR)]


def _pair_mask():
    return (_iota2((PAIR, PAIR), 0) >> 6) == (_iota2((PAIR, PAIR), 1) >> 6)


def _pair_diag(x):
    low = _iota2(x.shape, 1) < HEAD_DIM
    zero = jnp.zeros((), x.dtype)
    return jnp.concatenate([jnp.where(low, x, zero), jnp.where(low, zero, x)], axis=0)


def _heads_mm(a, b):
    return jnp.concatenate([_dot(ap, _pair_diag(bp)) for ap, bp in zip(_pairs(a), _pairs(b))], axis=1)


def _heads_mm_nt(a, b):
    return jnp.concatenate([_dot_nt(ap, _pair_diag(bp)) for ap, bp in zip(_pairs(a), _pairs(b))],
                           axis=1)


def _heads_sum(x, pair_ones):
    return jnp.concatenate([_dot(xp, pair_ones) for xp in _pairs(x)], axis=1)


def _expander(first_col):
    r = _iota2((GATE_LANES, WIDTH), 0)
    c = _iota2((GATE_LANES, WIDTH), 1)
    return (r == first_col + (c >> 6)).astype(BF16)


def _rope(x, cos_t, sin_t, reps):
    n = x.shape[-1]
    if reps > 1:
        cos_t = jnp.concatenate([cos_t] * reps, axis=-1)
        sin_t = jnp.concatenate([sin_t] * reps, axis=-1)
    half = ROPE_DIM // 2
    x_up = pltpu.roll(x, n - half, axis=1)
    x_dn = pltpu.roll(x, half, axis=1)
    first_half = (_iota2(x.shape, 1) & (HEAD_DIM - 1)) < half
    return x * cos_t + jnp.where(first_half, x_up, x_dn) * sin_t


def _repack_w_in(w_in_ref, w_ref):
    rows = 128
    scale = HEAD_DIM ** -0.5
    lane_m = _iota2((1, 4 * WIDTH), 1)
    scale_m = jnp.where((lane_m >= WIDTH) & (lane_m < 2 * WIDTH), scale, 1.0)
    scale_s = jnp.where(_iota2((1, SWA_WIDTH + 2 * SWA_KV_WIDTH), 1) < SWA_WIDTH, scale, 1.0)

    def body(i, carry):
        r = pl.ds(pl.multiple_of(i * rows, rows), rows)
        w_ref[r, 0:1024] = w_in_ref[r, 0:1024].astype(BF16)
        w_ref[r, 1024:2048] = (w_in_ref[r, 1032:2056] * scale_m).astype(BF16)
        w_ref[r, 2048:2816] = (w_in_ref[r, 2064:2832] * scale_s).astype(BF16)
        lane = _iota2((rows, GATE_LANES), 1)
        gates = jnp.where(lane < 2 * HEADS, w_in_ref[r, 1024:1152],
                          jnp.where(lane < 4 * HEADS, w_in_ref[r, 2048:2176], 0.0))
        w_ref[r, OFF_GATES:IN_COLS_PACKED] = gates.astype(BF16)
        return carry

    lax.fori_loop(0, D_MODEL // rows, body, 0)


def _proj_kernel(x_ref, g_ref, w_in_ref, cos_ref, sin_ref, cw_ref, ones_ref,
                 gqkv_ref, gz_ref, m_ref, sq_ref, skv_ref, gates_ref, tail_ref, buf_ref, w_ref,
                 *, tiles_per_seq):
    t = x_ref.shape[0]

    @pl.when(pl.program_id(0) == 0)
    def _():
        _repack_w_in(w_in_ref, w_ref)

    @pl.when(pl.program_id(0) % tiles_per_seq == 0)
    def _():
        tail_ref[...] = jnp.zeros_like(tail_ref)

    x = x_ref[...]
    hb = (x * g_ref[...]).astype(BF16)
    scale = _rms_scale(x)

    def proj(lo, hi):
        return scale * _dot(hb, w_ref[:, lo:hi])

    buf_ref[0:8, :] = tail_ref[...]
    buf_ref[8:8 + t, :] = proj(OFF_GQKV, OFF_GZ)
    tail_ref[...] = buf_ref[t:t + 8, :]
    cw = cw_ref[...]
    pair_ones = ones_ref[...]

    def conv_silu(lo, hi):
        acc = buf_ref[5:5 + t, lo:hi] * cw[0:1, lo:hi]
        for j in range(1, CONV_WIDTH):
            acc = acc + buf_ref[5 + j:5 + j + t, lo:hi] * cw[j:j + 1, lo:hi]
        return acc * _sigmoid(acc)

    def l2n(v):
        return v * lax.rsqrt(_heads_sum((v * v).astype(BF16), pair_ones) + EPS)

    cos_t, sin_t = cos_ref[...], sin_ref[...]
    sq = proj(OFF_SQ, OFF_SKV)
    q = conv_silu(0, WIDTH)
    kv = proj(OFF_SKV, OFF_GATES)
    sq_ref[...] = _rope(sq, cos_t, sin_t, SWA_WIDTH // 128)
    gqkv_ref[:, 0:WIDTH] = l2n(q) * (HEAD_DIM ** -0.5)
    m_ref[:, 0:2 * WIDTH] = proj(OFF_M, OFF_M + 2 * WIDTH)
    skv_ref[:, 0:SWA_KV_WIDTH] = _rope(kv[:, 0:SWA_KV_WIDTH], cos_t, sin_t, 1)
    skv_ref[:, SWA_KV_WIDTH:] = kv[:, SWA_KV_WIDTH:]
    k = conv_silu(WIDTH, 2 * WIDTH)
    m_ref[:, 2 * WIDTH:4 * WIDTH] = proj(OFF_M + 2 * WIDTH, OFF_SQ)
    gqkv_ref[:, WIDTH:2 * WIDTH] = l2n(k)
    gates_ref[...] = proj(OFF_GATES, IN_COLS_PACKED)
    gqkv_ref[:, 2 * WIDTH:3 * WIDTH] = conv_silu(2 * WIDTH, 3 * WIDTH)
    gz_ref[...] = proj(OFF_GZ, OFF_M)


def _proj(x, norm_w, w_in, cos_t, sin_t, conv_w, layer, tm, seq):
    m = x.shape[0]
    row = lambda n: pl.BlockSpec((tm, n), lambda i: (i, 0))
    widths = (768, 256, 1024, SWA_WIDTH, 2 * SWA_KV_WIDTH, GATE_LANES)
    return pl.pallas_call(
        functools.partial(_proj_kernel, tiles_per_seq=seq // tm),
        grid=(m // tm,),
        in_specs=[
            row(D_MODEL),
            pl.BlockSpec((None, 1, D_MODEL), lambda i: (layer, 0, 0)),
            pl.BlockSpec((None, D_MODEL, w_in.shape[-1]), lambda i: (layer, 0, 0),
                         pipeline_mode=pl.Buffered(1)),
            row(128), row(128),
            pl.BlockSpec((None, CONV_WIDTH, 3 * WIDTH), lambda i: (layer, 0, 0)),
            pl.BlockSpec((PAIR, PAIR), lambda i: (0, 0)),
        ],
        out_specs=[row(n) for n in widths],
        out_shape=[jax.ShapeDtypeStruct((m, n), F32) for n in widths],
        scratch_shapes=[
            pltpu.VMEM((8, 3 * WIDTH), F32),
            pltpu.VMEM((tm + 8, 3 * WIDTH), F32),
            pltpu.VMEM((D_MODEL, IN_COLS_PACKED), BF16),
        ],
        compiler_params=pltpu.CompilerParams(
            dimension_semantics=("arbitrary",), vmem_limit_bytes=VMEM_LIMIT),
    )(x, norm_w, w_in, cos_t, sin_t, conv_w, _pair_mask().astype(BF16))


def _gdn_kernel(qkv_ref, z_ref, gt_ref, par_ref, nw_ref, exp_ref, ones_ref, o_ref,
                state_ref, beta_s, gc_s, u_s, w_s, qd_s, kd_s, qkd_s, *, t, group, nb):
    c = CHUNK

    @pl.when(pl.program_id(1) == 0)
    def _():
        state_ref[...] = jnp.zeros_like(state_ref)

    pair_ones = ones_ref[...]

    def gate_rows(a, lo, hi):
        gates = gt_ref[a, lo:hi, :]
        par = par_ref[...]
        beta = _sigmoid(gates)
        g = -jnp.exp(par[0:1, :]) * _softplus(gates + par[1:2, :])
        beta_s[a, lo:hi, :] = _dot_sel(beta, exp_ref[0])
        gc_s[a, lo:hi, :] = _dot_sel(_chunk_cumsum(g), exp_ref[1])

    row = _iota2((c, WIDTH), 0)
    col = _iota2((c, WIDTH), 1) & (c - 1)
    causal = col <= row
    strict = col < row
    eye = col == row
    nw = nw_ref[...]

    def mm(a, b):
        return _heads_mm(a.astype(BF16), b.astype(BF16))

    def prepare(items, fillers):
        fillers = list(fillers)

        def fill():
            if fillers:
                fillers.pop(0)()

        n = range(len(items))
        sq = [a for a, _ in items]
        rs = [pl.ds(j * c, c) for _, j in items]
        q = [qkv_ref[sq[j], rs[j], 0:WIDTH] for j in n]
        k = [qkv_ref[sq[j], rs[j], WIDTH:2 * WIDTH] for j in n]
        gc = [gc_s[sq[j], rs[j], :] for j in n]
        g_last = [gc_s[a, pl.ds(j * c + c - 1, 1), :] for a, j in items]
        kb = [k[j] * beta_s[sq[j], rs[j], :] for j in n]
        kkqk = [_heads_mm_nt(jnp.concatenate([kb[j], q[j]], axis=0).astype(BF16), k[j].astype(BF16))
                for j in n]
        fill()
        gc_row = [jnp.sum(jnp.where(eye, gc[j], 0.0), axis=0, keepdims=True) for j in n]
        decay = [jnp.exp(jnp.where(causal, gc[j] - gc_row[j], -jnp.inf)) for j in n]
        nm = [jnp.where(strict, kkqk[j][0:c] * decay[j], 0.0) for j in n]
        for j in n:
            qkd_s[sq[j], rs[j], :] = (kkqk[j][c:2 * c] * decay[j]).astype(BF16)
        x = [eye.astype(F32) - jnp.where((row >> 1) == (col >> 1), nm[j], 0.0) for j in n]
        for lb in range(1, 6):
            off = ((row >> (lb + 1)) == (col >> (lb + 1))) & ((row >> lb) != (col >> lb))
            t1 = [mm(jnp.where(off, nm[j], 0.0), x[j]) for j in n]
            fill()
            t2 = [mm(x[j], t1[j]) for j in n]
            if lb % 2 == 0:
                fill()
            x = [x[j] - t2[j] for j in n]
        xb = [x[j].astype(BF16) for j in n]
        egc = [jnp.exp(gc[j]) for j in n]
        for j in n:
            u_s[sq[j], rs[j], :] = _heads_mm(
                xb[j], (qkv_ref[sq[j], rs[j], 2 * WIDTH:3 * WIDTH] * beta_s[sq[j], rs[j], :]).astype(BF16))
        for j in n:
            w_s[sq[j], rs[j], :] = _heads_mm(xb[j], (kb[j] * egc[j]).astype(BF16)).astype(BF16)
        for j in n:
            qd_s[sq[j], rs[j], :] = (q[j] * egc[j]).astype(BF16)
            kd_s[sq[j], rs[j], :] = (k[j] * jnp.exp(g_last[j] - gc[j])).astype(BF16)
        while fillers:
            fill()

    chunks = [(a, j) for a in range(nb) for j in range(t // c)]
    groups = [chunks[i:i + group] for i in range(0, len(chunks), group)]
    for a, j in groups[0]:
        gate_rows(a, j * c, (j + 1) * c)
    for gi, items in enumerate(groups):
        nxt = groups[gi + 1] if gi + 1 < len(groups) else []
        prepare(items, [functools.partial(gate_rows, a, j * c, (j + 1) * c) for a, j in nxt])

    def chunk_step(i, carry):
        n = range(nb)
        r = pl.ds(pl.multiple_of(i * c, c), c)
        pairs = range(WIDTH // PAIR)
        g_last = [gc_s[a, pl.ds(i * c + c - 1, 1), :] for a in n]
        s = [[state_ref[a, p] for p in pairs] for a in n]
        lhs = [_pairs(jnp.concatenate([w_s[a, r, :], qd_s[a, r, :]], axis=0)) for a in n]
        ws = [jnp.concatenate([_dot(lhs[a][p], s[a][p].astype(BF16)) for p in pairs], axis=1)
              for a in n]
        v_new = [(u_s[a, r, :] - ws[a][0:c]).astype(BF16) for a in n]
        v_diag = [[_pair_diag(vn) for vn in _pairs(v_new[a])] for a in n]
        upd = [[_dot_tn(_pair_diag(kd), v_diag[a][p]) for p, kd in enumerate(_pairs(kd_s[a, r, :]))]
               for a in n]
        for a in n:
            decay_end = _pairs(jnp.exp(g_last[a]))
            for p in pairs:
                state_ref[a, p] = s[a][p] * decay_end[p] + upd[a][p]
        o = [ws[a][c:2 * c] + jnp.concatenate(
            [_dot(qp, v_diag[a][p]) for p, qp in enumerate(_pairs(qkd_s[a, r, :]))], axis=1) for a in n]
        ms = [_heads_sum((o[a] * o[a]).astype(BF16), pair_ones) * (1.0 / HEAD_DIM) for a in n]
        for a in n:
            z = z_ref[a, r, :]
            o_ref[a, r, :] = o[a] * lax.rsqrt(ms[a] + EPS) * nw * (z * _sigmoid(z))
        return carry

    lax.fori_loop(0, t // c, chunk_step, 0)


def _gdn(gqkv, gz, gates, par, norm_w, layer, bsz, seq, t, nb):
    row = lambda n: pl.BlockSpec((nb, t, n), lambda b, j: (b, j, 0))
    lay = lambda a, b2: pl.BlockSpec((None, a, b2), lambda b, j: (layer, 0, 0))
    const = lambda shape: pl.BlockSpec(shape, lambda b, j: (0,) * len(shape))
    seq3 = lambda a: a.reshape(bsz, seq, a.shape[-1])
    f32_buf = pltpu.VMEM((nb, t, WIDTH), F32)
    bf16_buf = pltpu.VMEM((nb, t, WIDTH), BF16)
    out = pl.pallas_call(
        functools.partial(_gdn_kernel, t=t, group=8, nb=nb),
        grid=(bsz // nb, seq // t),
        in_specs=[row(3 * WIDTH), row(WIDTH), row(GATE_LANES),
                  lay(2, GATE_LANES), lay(1, WIDTH),
                  const((2, GATE_LANES, WIDTH)), const((PAIR, PAIR))],
        out_specs=row(WIDTH),
        out_shape=jax.ShapeDtypeStruct((bsz, seq, WIDTH), F32),
        scratch_shapes=[
            pltpu.VMEM((nb, WIDTH // PAIR, PAIR, PAIR), F32),
            f32_buf, f32_buf,
            f32_buf,
            bf16_buf, bf16_buf, bf16_buf, bf16_buf,
        ],
        compiler_params=pltpu.CompilerParams(
            dimension_semantics=("arbitrary", "arbitrary"), vmem_limit_bytes=VMEM_LIMIT),
    )(seq3(gqkv), seq3(gz), seq3(gates), par, norm_w,
      jnp.stack([_expander(0), _expander(HEADS)]), _pair_mask().astype(BF16))
    return out.reshape(bsz * seq, WIDTH)


def _mlstm_kernel(m_ref, gt_ref, par_ref, nw_ref, exp_ref, ones_ref, o_ref,
                  cn_ref, mx_ref, ig_s, b_s, cm_s, *, t, nb):
    c = CHUNK

    @pl.when(pl.program_id(1) == 0)
    def _():
        cn_ref[...] = jnp.zeros_like(cn_ref)
        mx_ref[...] = jnp.zeros_like(mx_ref)

    pair_ones = ones_ref[...]

    def front(a, carry):
        pre = gt_ref[a] + par_ref[...]
        capped = GATE_SOFTCAP * jnp.tanh(pre * (1.0 / GATE_SOFTCAP))
        log_f = -_softplus(-capped)
        b_all = _chunk_cumsum(log_f)
        ig = _dot_sel(capped, exp_ref[0])
        b = _dot_sel(b_all, exp_ref[1])
        pos = _iota2((t, WIDTH), 0) & (c - 1)
        cm = ig - b
        for sh in (1, 2, 4, 8, 16, 32):
            cm = jnp.where(pos >= sh, jnp.maximum(cm, pltpu.roll(cm, sh, axis=0)), cm)
        ig_s[a] = ig
        b_s[a] = b
        cm_s[a] = cm
        return carry

    lax.fori_loop(0, nb, front, 0)

    row = _iota2((c, WIDTH), 0)
    col = _iota2((c, WIDTH), 1) & (c - 1)
    causal = col <= row
    eye = col == row
    nw = nw_ref[...]

    def chunk_step(i, carry):
        n = range(nb)
        pairs = range(WIDTH // PAIR)
        r = pl.ds(pl.multiple_of(i * c, c), c)
        last = pl.ds(i * c + c - 1, 1)
        qb = [m_ref[a, r, 0:WIDTH].astype(BF16) for a in n]
        k = [m_ref[a, r, WIDTH:2 * WIDTH] for a in n]
        vb = [m_ref[a, r, 2 * WIDTH:3 * WIDTH].astype(BF16) for a in n]
        ig = [ig_s[a, r, :] for a in n]
        b = [b_s[a, r, :] for a in n]
        m_intra = [b[a] + cm_s[a, r, :] for a in n]
        b_last = [b_s[a, last, :] for a in n]
        m_chunk = [b_last[a] + cm_s[a, last, :] for a in n]
        qk = [_heads_mm_nt(qb[a], k[a].astype(BF16)) for a in n]
        gate_row = [jnp.sum(jnp.where(eye, ig[a] - b[a], 0.0), axis=0, keepdims=True) for a in n]
        qk = [qk[a] * jnp.exp(jnp.where(causal, b[a] + gate_row[a], -jnp.inf) - m_intra[a]) for a in n]
        v_ones = [[jnp.concatenate([_pair_diag(vp), pair_ones], axis=1) for vp in _pairs(vb[a])]
                  for a in n]
        qk_parts = [_split2(qk[a]) for a in n]
        intra = [[_dot(qp, v_ones[a][p]) for p, qp in enumerate(_pairs(qk_parts[a][0]))] for a in n]
        num_intra = [jnp.concatenate([intra[a][p][:, 0:PAIR] for p in pairs], axis=1) for a in n]
        den_intra = [jnp.concatenate([intra[a][p][:, PAIR:2 * PAIR] for p in pairs], axis=1)
                     + _heads_sum(qk_parts[a][1], pair_ones) for a in n]
        m_prev = [mx_ref[a] for a in n]
        cn = [[cn_ref[a, p] for p in pairs] for a in n]
        inter = [[_dot(qp, cn[a][p].astype(BF16)) for p, qp in enumerate(_pairs(qb[a]))] for a in n]
        ke = [(k[a] * jnp.exp(b_last[a] - b[a] + ig[a] - m_chunk[a])).astype(BF16) for a in n]
        own = [[_dot_tn(_pair_diag(kp), v_ones[a][p]) for p, kp in enumerate(_pairs(ke[a]))]
               for a in n]
        for a in n:
            m_new = jnp.maximum(b_last[a] + m_prev[a], m_chunk[a])
            s_old = _pairs(jnp.exp(b_last[a] + m_prev[a] - m_new))
            s_new = _pairs(jnp.exp(m_chunk[a] - m_new))
            for p in pairs:
                cn_ref[a, p] = (jnp.concatenate([s_old[p], s_old[p]], axis=1) * cn[a][p]
                                + jnp.concatenate([s_new[p], s_new[p]], axis=1) * own[a][p])
            mx_ref[a] = m_new
        h = []
        for a in n:
            pre_m = b[a] + m_prev[a]
            m_t = jnp.maximum(pre_m, m_intra[a])
            s_inter = jnp.exp(pre_m - m_t)
            s_intra = jnp.exp(m_intra[a] - m_t)
            num_inter = jnp.concatenate([inter[a][p][:, 0:PAIR] for p in pairs], axis=1)
            den_inter = jnp.concatenate([inter[a][p][:, PAIR:2 * PAIR] for p in pairs], axis=1)
            num = s_inter * num_inter + s_intra * num_intra[a]
            den = s_inter * den_inter + s_intra * den_intra[a]
            h.append(num / jnp.maximum(jnp.abs(den), jnp.exp(-m_t)))
        ms = [_heads_sum((h[a] * h[a]).astype(BF16), pair_ones) * (1.0 / HEAD_DIM) for a in n]
        for a in n:
            o_ref[a, r, :] = (h[a] * lax.rsqrt(ms[a] + EPS) * nw
                              * _sigmoid(m_ref[a, r, 3 * WIDTH:4 * WIDTH]))
        return carry

    lax.fori_loop(0, t // c, chunk_step, 0)


def _mlstm(mqkvo, gates, par, norm_w, layer, bsz, seq, t, nb):
    row = lambda n: pl.BlockSpec((nb, t, n), lambda b, j: (b, j, 0))
    lay = lambda a, b2: pl.BlockSpec((None, a, b2), lambda b, j: (layer, 0, 0))
    const = lambda shape: pl.BlockSpec(shape, lambda b, j: (0,) * len(shape))
    seq3 = lambda a: a.reshape(bsz, seq, a.shape[-1])
    f32_buf = pltpu.VMEM((nb, t, WIDTH), F32)
    out = pl.pallas_call(
        functools.partial(_mlstm_kernel, t=t, nb=nb),
        grid=(bsz // nb, seq // t),
        in_specs=[row(4 * WIDTH), row(GATE_LANES), lay(1, GATE_LANES), lay(1, WIDTH),
                  const((2, GATE_LANES, WIDTH)), const((PAIR, PAIR))],
        out_specs=row(WIDTH),
        out_shape=jax.ShapeDtypeStruct((bsz, seq, WIDTH), F32),
        scratch_shapes=[
            pltpu.VMEM((nb, WIDTH // PAIR, PAIR, 2 * PAIR), F32),
            pltpu.VMEM((nb, 1, WIDTH), F32),
            f32_buf, f32_buf, f32_buf,
        ],
        compiler_params=pltpu.CompilerParams(
            dimension_semantics=("arbitrary", "arbitrary"), vmem_limit_bytes=VMEM_LIMIT),
    )(seq3(mqkvo), seq3(gates), par, norm_w,
      jnp.stack([_expander(2 * HEADS), _expander(3 * HEADS)]),
      _pair_mask().astype(BF16))
    return out.reshape(bsz * seq, WIDTH)


def _swa_stages(q_ref, kvc_ref, kvp_ref, sink_ref, o_ref, first, tq):
    tb = SWA_BLOCK
    kvw = SWA_KV_WIDTH
    nq = tq // tb
    chains = [(i, g) for i in range(nq) for g in range(2)]
    st = {}

    def setup():
        kcat = jnp.concatenate([kvp_ref[:, 0:kvw], kvc_ref[:, 0:kvw]], axis=0)
        vcat = jnp.concatenate([kvp_ref[:, kvw:2 * kvw], kvc_ref[:, kvw:2 * kvw]], axis=0)
        st["v_t"] = vcat.T.astype(BF16)
        kr = pltpu.roll(kcat, HEAD_DIM, axis=1)
        lo = _iota2((tb + tq, kvw), 1) < HEAD_DIM

        def place(x_lo, x_hi):
            return jnp.where(lo, x_lo, 0.0).astype(BF16), jnp.where(lo, 0.0, x_hi).astype(BF16)

        st["k"] = (place(kcat, kr), place(kr, kcat))
        st["ki"] = _iota2((2 * tb, tb), 0)
        qi = _iota2((2 * tb, tb), 1)
        st["in_window"] = (st["ki"] > qi) & (st["ki"] <= qi + tb)

    def scores(ci):
        i, g = chains[ci]
        keys = jnp.concatenate([st["k"][g][0][i * tb:(i + 2) * tb],
                                st["k"][g][1][i * tb:(i + 2) * tb]], axis=0)
        qs = jnp.concatenate([q_ref[i * tb:(i + 1) * tb, 256 * g:256 * g + 128],
                              q_ref[i * tb:(i + 1) * tb, 256 * g + 128:256 * g + 256]], axis=0)
        st["s", ci] = _dot_nt(keys, qs.astype(BF16))

    def probs(ci):
        i, g = chains[ci]
        sc = st.pop(("s", ci))
        mask = st["in_window"]
        if i == 0:
            mask = mask & (st["ki"] >= jnp.where(first, tb, 0))
        ps, invs = [], []
        for part in range(2):
            for e in range(2):
                sink = sink_ref[2 * g + part:2 * g + part + 1, e * 2 * tb:e * 2 * tb + 1]
                se = jnp.where(mask, sc[e * 2 * tb:(e + 1) * 2 * tb, part * tb:(part + 1) * tb],
                               -jnp.inf)
                mx = jnp.maximum(jnp.max(se, axis=0, keepdims=True), sink)
                p = jnp.exp(se - mx)
                ps.append(p.astype(BF16))
                invs.append(1.0 / (jnp.sum(p, axis=0, keepdims=True) + jnp.exp(sink - mx)))
        st["p", ci] = jnp.concatenate(ps, axis=-1)
        st["inv", ci] = jnp.concatenate(invs, axis=-1)

    def attend(ci):
        i, g = chains[ci]
        o = _dot(st["v_t"][g * HEAD_DIM:(g + 1) * HEAD_DIM, i * tb:(i + 2) * tb],
                 st.pop(("p", ci))) * st.pop(("inv", ci))
        for part in range(2):
            j = 2 * g + part
            pair_t = jnp.concatenate([o[:, 2 * part * tb:(2 * part + 1) * tb],
                                      o[:, (2 * part + 1) * tb:(2 * part + 2) * tb]], axis=0)
            o_ref[i * tb:(i + 1) * tb, 128 * j:128 * (j + 1)] = pair_t.T

    n = len(chains)
    stages = [setup]
    for ci in range(n + 2):
        stages += [functools.partial(f, cj) for f, cj in ((scores, ci), (probs, ci - 1), (attend, ci - 2))
                   if 0 <= cj < n]
    return stages


def _post_kernel(x_ref, ya_ref, yb_ref, p_ref, wo_ref, nmlp_ref, wup_ref, wdn_ref,
                 nple_ref, wg_ref, wp_ref, nfin_ref,
                 sq0_ref, skv0_ref, sqn_ref, skvn_ref, kvpn_ref, sink_ref,
                 o_ref, yc_ref, *, final, tf, tiles_per_seq):
    t = x_ref.shape[0]
    step = pl.program_id(0)

    @pl.when(step == 0)
    def _():
        for stage in _swa_stages(sq0_ref, skv0_ref, kvpn_ref, sink_ref, yc_ref, True, t):
            stage()

    x = x_ref[...]
    x = x + _dot(ya_ref[...].astype(BF16), wo_ref[0:WIDTH, :])
    x = x + _dot(yb_ref[...].astype(BF16), wo_ref[WIDTH:2 * WIDTH, :])
    x = x + _dot(yc_ref[...].astype(BF16), wo_ref[2 * WIDTH:, :])
    stages = _swa_stages(sqn_ref, skvn_ref, kvpn_ref, sink_ref, yc_ref,
                         (step + 1) % tiles_per_seq == 0, t)
    slots = 2 * (D_FF // tf)
    per_slot = -(-len(stages) // slots)

    def fill():
        for _ in range(per_slot):
            if stages:
                stages.pop(0)()

    r = _rms_scale(x)
    h = (x * nmlp_ref[...]).astype(BF16)
    acc = jnp.zeros_like(x)
    for f in range(0, D_FF, tf):
        u = jnp.maximum(_dot(h, wup_ref[:, f:f + tf]), 0.0)
        fill()
        acc = acc + _dot((u * u).astype(BF16), wdn_ref[f:f + tf, :])
        fill()
    x = x + (r * r) * acc
    gate = _sigmoid(_rms_scale(x) * _dot((x * nple_ref[...]).astype(BF16), wg_ref[...]))
    x = x + gate * _dot(p_ref[...].astype(BF16), wp_ref[...])
    if final:
        x = _rms(x, nfin_ref[...])
    o_ref[...] = x


def _post(x, ya, yb, sq, skv, sinks_e, p, w_out, norm_mlp, w_up, w_down, norm_ple, w_gate, w_proj,
          norm_final, layer, tm, seq, final):
    m = x.shape[0]
    nt = m // tm
    blocks_per_tile = tm // SWA_BLOCK
    row = lambda n: pl.BlockSpec((tm, n), lambda i: (i, 0))
    first = lambda n: pl.BlockSpec((tm, n), lambda i: (0, 0))
    nxt = lambda n: pl.BlockSpec((tm, n), lambda i: (jnp.minimum(i + 1, nt - 1), 0))
    lay = lambda a, b: pl.BlockSpec((None, a, b), lambda i: (layer, 0, 0),
                                    pipeline_mode=pl.Buffered(1))
    return pl.pallas_call(
        functools.partial(_post_kernel, final=final, tf=512, tiles_per_seq=seq // tm),
        grid=(nt,),
        in_specs=[
            row(D_MODEL), row(WIDTH), row(WIDTH),
            pl.BlockSpec((None, tm, PLE_DIM), lambda i: (layer, i, 0)),
            lay(D_MODEL, D_MODEL), lay(1, D_MODEL), lay(D_MODEL, D_FF), lay(D_FF, D_MODEL),
            lay(1, D_MODEL), lay(D_MODEL, D_MODEL), lay(PLE_DIM, D_MODEL),
            pl.BlockSpec((1, D_MODEL), lambda i: (0, 0)),
            first(SWA_WIDTH), first(2 * SWA_KV_WIDTH), nxt(SWA_WIDTH), nxt(2 * SWA_KV_WIDTH),
            pl.BlockSpec((SWA_BLOCK, 2 * SWA_KV_WIDTH),
                         lambda i: ((i + 1) * blocks_per_tile - 1, 0)),
            pl.BlockSpec((None, SWA_WIDTH // 128, 4 * SWA_BLOCK), lambda i: (layer, 0, 0)),
        ],
        out_specs=row(D_MODEL),
        out_shape=jax.ShapeDtypeStruct((m, D_MODEL), F32),
        scratch_shapes=[pltpu.VMEM((tm, SWA_WIDTH), F32)],
        compiler_params=pltpu.CompilerParams(
            dimension_semantics=("arbitrary",), vmem_limit_bytes=VMEM_LIMIT),
    )(x, ya, yb, p, w_out, norm_mlp, w_up, w_down, norm_ple, w_gate, w_proj, norm_final,
      sq, skv, sq, skv, skv, sinks_e)


def _gate_row(depth, pieces):
    out = jnp.zeros((depth, 1, GATE_LANES), F32)
    for first_col, vals in pieces:
        out = out.at[:, 0, first_col:first_col + HEADS].set(vals.astype(F32))
    return out


def _rope_tables(positions):
    half = ROPE_DIM // 2
    inv_freq = ROPE_THETA ** (-jnp.arange(0, ROPE_DIM, 2, dtype=F32) / ROPE_DIM)
    dim = jnp.arange(128) % HEAD_DIM
    ang = positions.astype(F32).reshape(-1)[:, None] * inv_freq[dim % half][None, :]
    cos_t = jnp.where(dim < ROPE_DIM, jnp.cos(ang), 1.0)
    sin_t = jnp.where(dim < half, -jnp.sin(ang), jnp.where(dim < ROPE_DIM, jnp.sin(ang), 0.0))
    return cos_t, sin_t


def kernel(x, p, positions, w_in, conv_w, gdn_a_log, gdn_dt_bias, gdn_norm, mlstm_i_bias,
           mlstm_f_bias, mlstm_norm, attn_sinks, w_out, norm_mix, norm_mlp, w_up, w_down,
           norm_ple, w_ple_gate, w_ple_proj, norm_final):
    bsz, seq, d = x.shape
    depth = w_in.shape[0]
    m = bsz * seq
    tm = min(512, seq)
    t_mix = min(512, seq)
    nb_mix = 4 if bsz % 4 == 0 else 1
    nb_gdn = 8 if bsz % 8 == 0 else nb_mix
    t_gdn = min(2048 // nb_gdn, seq)

    cos_t, sin_t = _rope_tables(positions)
    gdn_par = jnp.concatenate([_gate_row(depth, [(HEADS, gdn_a_log)]),
                               _gate_row(depth, [(HEADS, gdn_dt_bias)])], axis=1)
    mlstm_par = _gate_row(depth, [(2 * HEADS, mlstm_i_bias), (3 * HEADS, mlstm_f_bias)])
    gdn_nw = jnp.tile(gdn_norm.astype(F32), (1, HEADS))[:, None, :]
    mlstm_nw = mlstm_norm.astype(F32)[:, None, :]
    sinks_e = jnp.repeat(attn_sinks.astype(F32), 2 * SWA_BLOCK, axis=-1).reshape(
        depth, SWA_WIDTH // 128, 4 * SWA_BLOCK)
    row3 = lambda a: a.astype(F32)[:, None, :]
    wo_b, wup_b, wdn_b = w_out.astype(BF16), w_up.astype(BF16), w_down.astype(BF16)
    wg_b, wp_b = w_ple_gate.astype(BF16), w_ple_proj.astype(BF16)
    nmix, nmlp, nple = row3(norm_mix), row3(norm_mlp), row3(norm_ple)
    nfin = norm_final.astype(F32)[None, :]
    p2 = p.reshape(depth, m, PLE_DIM)

    xf = x.reshape(m, d)
    for i in range(depth):
        gqkv, gz, mqkvo, sq, skv, gates = _proj(xf, nmix, w_in.astype(F32), cos_t, sin_t,
                                                conv_w.astype(F32), i, tm, seq)
        ya = _gdn(gqkv, gz, gates, gdn_par, gdn_nw, i, bsz, seq, t_gdn, nb_gdn)
        yb = _mlstm(mqkvo, gates, mlstm_par, mlstm_nw, i, bsz, seq, t_mix, nb_mix)
        xf = _post(xf, ya, yb, sq, skv, sinks_e, p2, wo_b, nmlp, wup_b, wdn_b, nple, wg_b, wp_b, nfin,
                   i, tm, seq, final=(i == depth - 1))
    return xf.reshape(bsz, seq, d)
```

```python
import functools
import itertools

import jax
import jax.numpy as jnp
from jax import lax
from jax.experimental import pallas as pl
from jax.experimental.pallas import tpu as pltpu

F32 = jnp.float32
BF16 = jnp.bfloat16

D_MODEL = 1024
DEPTH = 4
HEAD_DIM = 64
PLE_DIM = 256
D_FF = 4 * D_MODEL
EPS = 1e-6
HEADS = 4
WIDTH = HEADS * HEAD_DIM
PAIR = 2 * HEAD_DIM
CHUNK = 64
CONV_WIDTH = 4
GATE_SOFTCAP = 15.0
SWA_WIDTH = 512
SWA_KV_WIDTH = 128
SWA_BLOCK = 128
ROPE_DIM = 16
ROPE_THETA = 500000.0
GATE_LANES = 128
IN_COLS_PACKED = 2944
OFF_GQKV, OFF_GZ, OFF_M, OFF_SQ, OFF_SKV, OFF_GATES = 0, 768, 1024, 2048, 2560, 2816

VMEM_LIMIT = 56 * 1024 * 1024


def _dot(a, b):
    return jnp.dot(a, b, preferred_element_type=F32)


def _dot_nt(a, b):
    return lax.dot_general(a, b, (((1,), (1,)), ((), ())), preferred_element_type=F32)


def _dot_tn(a, b):
    return lax.dot_general(a, b, (((0,), (0,)), ((), ())), preferred_element_type=F32)


def _split2(x):
    hi = x.astype(BF16)
    lo = (x - hi.astype(F32)).astype(BF16)
    return hi, lo


def _dot_sel(x, sel):
    hi, lo = _split2(x)
    return _dot(hi, sel) + _dot(lo, sel)


def _chunk_cumsum(x):
    pos = _iota2(x.shape, 0) & (CHUNK - 1)
    sh = 1
    while sh < CHUNK:
        x = x + jnp.where(pos >= sh, pltpu.roll(x, sh, axis=0), 0.0)
        sh *= 2
    return x


def _sigmoid(x):
    return 1.0 / (1.0 + jnp.exp(-x))


def _softplus(x):
    return jnp.maximum(x, 0.0) + jnp.log1p(jnp.exp(-jnp.abs(x)))


def _rms(x, g):
    return x * lax.rsqrt(jnp.mean(x * x, axis=-1, keepdims=True) + EPS) * g


def _rms_scale(x):
    return lax.rsqrt(jnp.mean(x * x, axis=-1, keepdims=True) + EPS)


def _iota2(shape, dim):
    return lax.broadcasted_iota(jnp.int32, shape, dim)


def _pairs(x):
    return [x[:, p * PAIR:(p + 1) * PAIR] for p in range(WIDTH // PAIR)]


def _pair_mask():
    return (_iota2((PAIR, PAIR), 0) >> 6) == (_iota2((PAIR, PAIR), 1) >> 6)


def _pair_diag(x):
    low = _iota2(x.shape, 1) < HEAD_DIM
    zero = jnp.zeros((), x.dtype)
    return jnp.concatenate([jnp.where(low, x, zero), jnp.where(low, zero, x)], axis=0)


def _heads_mm(a, b):
    return jnp.concatenate([_dot(ap, _pair_diag(bp)) for ap, bp in zip(_pairs(a), _pairs(b))], axis=1)


def _heads_mm_nt(a, b):
    return jnp.concatenate([_dot_nt(ap, _pair_diag(bp)) for ap, bp in zip(_pairs(a), _pairs(b))],
                           axis=1)


def _heads_sum(x, pair_ones):
    return jnp.concatenate([_dot(xp, pair_ones) for xp in _pairs(x)], axis=1)


def _expander(first_col):
    r = _iota2((GATE_LANES, WIDTH), 0)
    c = _iota2((GATE_LANES, WIDTH), 1)
    return (r == first_col + (c >> 6)).astype(BF16)


def _rope(x, cos_t, sin_t, reps):
    n = x.shape[-1]
    if reps > 1:
        cos_t = jnp.concatenate([cos_t] * reps, axis=-1)
        sin_t = jnp.concatenate([sin_t] * reps, axis=-1)
    half = ROPE_DIM // 2
    x_up = pltpu.roll(x, n - half, axis=1)
    x_dn = pltpu.roll(x, half, axis=1)
    first_half = (_iota2(x.shape, 1) & (HEAD_DIM - 1)) < half
    return x * cos_t + jnp.where(first_half, x_up, x_dn) * sin_t


def _repack_w_in(w_in_ref, w_ref):
    rows = 128
    scale = HEAD_DIM ** -0.5
    lane_m = _iota2((1, 4 * WIDTH), 1)
    scale_m = jnp.where((lane_m >= WIDTH) & (lane_m < 2 * WIDTH), scale, 1.0)
    scale_s = jnp.where(_iota2((1, SWA_WIDTH + 2 * SWA_KV_WIDTH), 1) < SWA_WIDTH, scale, 1.0)

    def body(i, carry):
        r = pl.ds(pl.multiple_of(i * rows, rows), rows)
        w_ref[r, 0:1024] = w_in_ref[r, 0:1024].astype(BF16)
        w_ref[r, 1024:2048] = (w_in_ref[r, 1032:2056] * scale_m).astype(BF16)
        w_ref[r, 2048:2816] = (w_in_ref[r, 2064:2832] * scale_s).astype(BF16)
        lane = _iota2((rows, GATE_LANES), 1)
        gates = jnp.where(lane < 2 * HEADS, w_in_ref[r, 1024:1152],
                          jnp.where(lane < 4 * HEADS, w_in_ref[r, 2048:2176], 0.0))
        w_ref[r, OFF_GATES:IN_COLS_PACKED] = gates.astype(BF16)
        return carry

    lax.fori_loop(0, D_MODEL // rows, body, 0)


def _proj_kernel(x_ref, g_ref, w_in_ref, cos_ref, sin_ref, cw_ref, ones_ref,
                 gqkv_ref, gz_ref, m_ref, sq_ref, skv_ref, gates_ref, tail_ref, buf_ref, w_ref,
                 *, tiles_per_seq):
    t = x_ref.shape[0]

    @pl.when(pl.program_id(0) == 0)
    def _():
        _repack_w_in(w_in_ref, w_ref)

    @pl.when(pl.program_id(0) % tiles_per_seq == 0)
    def _():
        tail_ref[...] = jnp.zeros_like(tail_ref)

    x = x_ref[...]
    hb = (x * g_ref[...]).astype(BF16)
    scale = _rms_scale(x)

    def proj(lo, hi):
        return scale * _dot(hb, w_ref[:, lo:hi])

    a = proj(OFF_GQKV, OFF_M)
    buf_ref[0:8, :] = tail_ref[...]
    buf_ref[8:8 + t, :] = a[:, 0:OFF_GZ]
    tail_ref[...] = buf_ref[t:t + 8, :]
    gz_ref[...] = a[:, OFF_GZ:OFF_M]
    cw = cw_ref[...]
    pair_ones = ones_ref[...]

    def conv_silu(lo, hi):
        acc = buf_ref[5:5 + t, lo:hi] * cw[0:1, lo:hi]
        for j in range(1, CONV_WIDTH):
            acc = acc + buf_ref[5 + j:5 + j + t, lo:hi] * cw[j:j + 1, lo:hi]
        return acc * _sigmoid(acc)

    def l2n(v):
        return v * lax.rsqrt(_heads_sum((v * v).astype(BF16), pair_ones) + EPS)

    cos_t, sin_t = cos_ref[...], sin_ref[...]
    s = proj(OFF_SQ, IN_COLS_PACKED)
    q = conv_silu(0, WIDTH)
    gqkv_ref[:, 0:WIDTH] = l2n(q) * (HEAD_DIM ** -0.5)
    m_ref[:, 0:2 * WIDTH] = proj(OFF_M, OFF_M + 2 * WIDTH)
    sq_ref[...] = _rope(s[:, 0:SWA_WIDTH], cos_t, sin_t, SWA_WIDTH // 128)
    skv_ref[:, 0:SWA_KV_WIDTH] = _rope(s[:, SWA_WIDTH:SWA_WIDTH + SWA_KV_WIDTH], cos_t, sin_t, 1)
    skv_ref[:, SWA_KV_WIDTH:] = s[:, SWA_WIDTH + SWA_KV_WIDTH:SWA_WIDTH + 2 * SWA_KV_WIDTH]
    gates_ref[...] = s[:, SWA_WIDTH + 2 * SWA_KV_WIDTH:]
    k = conv_silu(WIDTH, 2 * WIDTH)
    gqkv_ref[:, WIDTH:2 * WIDTH] = l2n(k)
    gqkv_ref[:, 2 * WIDTH:3 * WIDTH] = conv_silu(2 * WIDTH, 3 * WIDTH)
    m_ref[:, 2 * WIDTH:4 * WIDTH] = proj(OFF_M + 2 * WIDTH, OFF_SQ)


def _proj(x, norm_w, w_in, cos_t, sin_t, conv_w, layer, tm, seq):
    m = x.shape[0]
    row = lambda n: pl.BlockSpec((tm, n), lambda i: (i, 0))
    widths = (768, 256, 1024, SWA_WIDTH, 2 * SWA_KV_WIDTH, GATE_LANES)
    return pl.pallas_call(
        functools.partial(_proj_kernel, tiles_per_seq=seq // tm),
        grid=(m // tm,),
        in_specs=[
            row(D_MODEL),
            pl.BlockSpec((None, 1, D_MODEL), lambda i: (layer, 0, 0)),
            pl.BlockSpec((None, D_MODEL, w_in.shape[-1]), lambda i: (layer, 0, 0),
                         pipeline_mode=pl.Buffered(1)),
            row(128), row(128),
            pl.BlockSpec((None, CONV_WIDTH, 3 * WIDTH), lambda i: (layer, 0, 0)),
            pl.BlockSpec((PAIR, PAIR), lambda i: (0, 0)),
        ],
        out_specs=[row(n) for n in widths],
        out_shape=[jax.ShapeDtypeStruct((m, n), F32) for n in widths],
        scratch_shapes=[
            pltpu.VMEM((8, 3 * WIDTH), F32),
            pltpu.VMEM((tm + 8, 3 * WIDTH), F32),
            pltpu.VMEM((D_MODEL, IN_COLS_PACKED), BF16),
        ],
        compiler_params=pltpu.CompilerParams(
            dimension_semantics=("arbitrary",), vmem_limit_bytes=VMEM_LIMIT),
    )(x, norm_w, w_in, cos_t, sin_t, conv_w, _pair_mask().astype(BF16))


def _gdn_kernel(qkv_ref, z_ref, gt_ref, par_ref, nw_ref, exp_ref, ones_ref, o_ref,
                state_ref, beta_s, gc_s, u_s, w_s, qd_s, kd_s, qkd_s, *, t, group, nb):
    c = CHUNK

    @pl.when(pl.program_id(1) == 0)
    def _():
        state_ref[...] = jnp.zeros_like(state_ref)

    pair_ones = ones_ref[...]

    def gate_rows(a, lo, hi):
        gates = gt_ref[a, lo:hi, :]
        par = par_ref[...]
        beta = _sigmoid(gates)
        g = -jnp.exp(par[0:1, :]) * _softplus(gates + par[1:2, :])
        beta_s[a, lo:hi, :] = _dot_sel(beta, exp_ref[0])
        gc_s[a, lo:hi, :] = _dot_sel(_chunk_cumsum(g), exp_ref[1])

    row = _iota2((c, WIDTH), 0)
    col = _iota2((c, WIDTH), 1) & (c - 1)
    causal = col <= row
    strict = col < row
    eye = col == row
    nw = nw_ref[...]

    def mm(a, b):
        return _heads_mm(a.astype(BF16), b.astype(BF16))

    def prepare(items, fillers):
        fillers = list(fillers)

        def fill():
            if fillers:
                fillers.pop(0)()

        n = range(len(items))
        sq = [a for a, _ in items]
        rs = [pl.ds(j * c, c) for _, j in items]
        q = [qkv_ref[sq[j], rs[j], 0:WIDTH] for j in n]
        k = [qkv_ref[sq[j], rs[j], WIDTH:2 * WIDTH] for j in n]
        gc = [gc_s[sq[j], rs[j], :] for j in n]
        g_last = [gc_s[a, pl.ds(j * c + c - 1, 1), :] for a, j in items]
        kb = [k[j] * beta_s[sq[j], rs[j], :] for j in n]
        kkqk = [_heads_mm_nt(jnp.concatenate([kb[j], q[j]], axis=0).astype(BF16), k[j].astype(BF16))
                for j in n]
        fill()
        gc_row = [jnp.sum(jnp.where(eye, gc[j], 0.0), axis=0, keepdims=True) for j in n]
        decay = [jnp.exp(jnp.where(causal, gc[j] - gc_row[j], -jnp.inf)) for j in n]
        nm = [jnp.where(strict, kkqk[j][0:c] * decay[j], 0.0) for j in n]
        for j in n:
            qkd_s[sq[j], rs[j], :] = (kkqk[j][c:2 * c] * decay[j]).astype(BF16)
        x = [eye.astype(F32) - jnp.where((row >> 1) == (col >> 1), nm[j], 0.0) for j in n]
        for lb in range(1, 6):
            off = ((row >> (lb + 1)) == (col >> (lb + 1))) & ((row >> lb) != (col >> lb))
            t1 = [mm(jnp.where(off, nm[j], 0.0), x[j]) for j in n]
            fill()
            t2 = [mm(x[j], t1[j]) for j in n]
            if lb % 2 == 0:
                fill()
            x = [x[j] - t2[j] for j in n]
        xb = [x[j].astype(BF16) for j in n]
        egc = [jnp.exp(gc[j]) for j in n]
        for j in n:
            u_s[sq[j], rs[j], :] = _heads_mm(
                xb[j], (qkv_ref[sq[j], rs[j], 2 * WIDTH:3 * WIDTH] * beta_s[sq[j], rs[j], :]).astype(BF16))
        for j in n:
            w_s[sq[j], rs[j], :] = _heads_mm(xb[j], (kb[j] * egc[j]).astype(BF16)).astype(BF16)
        for j in n:
            qd_s[sq[j], rs[j], :] = (q[j] * egc[j]).astype(BF16)
            kd_s[sq[j], rs[j], :] = (k[j] * jnp.exp(g_last[j] - gc[j])).astype(BF16)
        while fillers:
            fill()

    chunks = [(a, j) for a in range(nb) for j in range(t // c)]
    groups = [chunks[i:i + group] for i in range(0, len(chunks), group)]
    for a, j in groups[0]:
        gate_rows(a, j * c, (j + 1) * c)
    for gi, items in enumerate(groups):
        nxt = groups[gi + 1] if gi + 1 < len(groups) else []
        prepare(items, [functools.partial(gate_rows, a, j * c, (j + 1) * c) for a, j in nxt])

    def chunk_step(i, carry):
        n = range(nb)
        r = pl.ds(pl.multiple_of(i * c, c), c)
        pairs = range(WIDTH // PAIR)
        g_last = [gc_s[a, pl.ds(i * c + c - 1, 1), :] for a in n]
        s = [[state_ref[a, p] for p in pairs] for a in n]
        lhs = [_pairs(jnp.concatenate([w_s[a, r, :], qd_s[a, r, :]], axis=0)) for a in n]
        ws = [jnp.concatenate([_dot(lhs[a][p], s[a][p].astype(BF16)) for p in pairs], axis=1)
              for a in n]
        v_new = [(u_s[a, r, :] - ws[a][0:c]).astype(BF16) for a in n]
        v_diag = [[_pair_diag(vn) for vn in _pairs(v_new[a])] for a in n]
        upd = [[_dot_tn(_pair_diag(kd), v_diag[a][p]) for p, kd in enumerate(_pairs(kd_s[a, r, :]))]
               for a in n]
        for a in n:
            decay_end = _pairs(jnp.exp(g_last[a]))
            for p in pairs:
                state_ref[a, p] = s[a][p] * decay_end[p] + upd[a][p]
        o = [ws[a][c:2 * c] + jnp.concatenate(
            [_dot(qp, v_diag[a][p]) for p, qp in enumerate(_pairs(qkd_s[a, r, :]))], axis=1) for a in n]
        ms = [_heads_sum((o[a] * o[a]).astype(BF16), pair_ones) * (1.0 / HEAD_DIM) for a in n]
        for a in n:
            z = z_ref[a, r, :]
            o_ref[a, r, :] = o[a] * lax.rsqrt(ms[a] + EPS) * nw * (z * _sigmoid(z))
        return carry

    lax.fori_loop(0, t // c, chunk_step, 0)


def _gdn(gqkv, gz, gates, par, norm_w, layer, bsz, seq, t, nb):
    row = lambda n: pl.BlockSpec((nb, t, n), lambda b, j: (b, j, 0))
    lay = lambda a, b2: pl.BlockSpec((None, a, b2), lambda b, j: (layer, 0, 0))
    const = lambda shape: pl.BlockSpec(shape, lambda b, j: (0,) * len(shape))
    seq3 = lambda a: a.reshape(bsz, seq, a.shape[-1])
    f32_buf = pltpu.VMEM((nb, t, WIDTH), F32)
    bf16_buf = pltpu.VMEM((nb, t, WIDTH), BF16)
    out = pl.pallas_call(
        functools.partial(_gdn_kernel, t=t, group=8, nb=nb),
        grid=(bsz // nb, seq // t),
        in_specs=[row(3 * WIDTH), row(WIDTH), row(GATE_LANES),
                  lay(2, GATE_LANES), lay(1, WIDTH),
                  const((2, GATE_LANES, WIDTH)), const((PAIR, PAIR))],
        out_specs=row(WIDTH),
        out_shape=jax.ShapeDtypeStruct((bsz, seq, WIDTH), F32),
        scratch_shapes=[
            pltpu.VMEM((nb, WIDTH // PAIR, PAIR, PAIR), F32),
            f32_buf, f32_buf,
            f32_buf,
            bf16_buf, bf16_buf, bf16_buf, bf16_buf,
        ],
        compiler_params=pltpu.CompilerParams(
            dimension_semantics=("arbitrary", "arbitrary"), vmem_limit_bytes=VMEM_LIMIT),
    )(seq3(gqkv), seq3(gz), seq3(gates), par, norm_w,
      jnp.stack([_expander(0), _expander(HEADS)]), _pair_mask().astype(BF16))
    return out.reshape(bsz * seq, WIDTH)


def _mlstm_kernel(m_ref, gt_ref, par_ref, nw_ref, exp_ref, ones_ref, o_ref,
                  cn_ref, mx_ref, ig_s, b_s, cm_s, *, t, nb):
    c = CHUNK

    @pl.when(pl.program_id(1) == 0)
    def _():
        cn_ref[...] = jnp.zeros_like(cn_ref)
        mx_ref[...] = jnp.zeros_like(mx_ref)

    pair_ones = ones_ref[...]

    def front(a, carry):
        pre = gt_ref[a] + par_ref[...]
        capped = GATE_SOFTCAP * jnp.tanh(pre * (1.0 / GATE_SOFTCAP))
        log_f = -_softplus(-capped)
        b_all = _chunk_cumsum(log_f)
        ig = _dot_sel(capped, exp_ref[0])
        b = _dot_sel(b_all, exp_ref[1])
        pos = _iota2((t, WIDTH), 0) & (c - 1)
        cm = ig - b
        for sh in (1, 2, 4, 8, 16, 32):
            cm = jnp.where(pos >= sh, jnp.maximum(cm, pltpu.roll(cm, sh, axis=0)), cm)
        ig_s[a] = ig
        b_s[a] = b
        cm_s[a] = cm
        return carry

    lax.fori_loop(0, nb, front, 0)

    row = _iota2((c, WIDTH), 0)
    col = _iota2((c, WIDTH), 1) & (c - 1)
    causal = col <= row
    eye = col == row
    nw = nw_ref[...]

    def chunk_stages(i):
        n = range(nb)
        pairs = range(WIDTH // PAIR)
        r = pl.ds(pl.multiple_of(i * c, c), c)
        last = pl.ds(i * c + c - 1, 1)
        qb = [m_ref[a, r, 0:WIDTH].astype(BF16) for a in n]
        k = [m_ref[a, r, WIDTH:2 * WIDTH] for a in n]
        vb = [m_ref[a, r, 2 * WIDTH:3 * WIDTH].astype(BF16) for a in n]
        ig = [ig_s[a, r, :] for a in n]
        b = [b_s[a, r, :] for a in n]
        m_intra = [b[a] + cm_s[a, r, :] for a in n]
        b_last = [b_s[a, last, :] for a in n]
        m_chunk = [b_last[a] + cm_s[a, last, :] for a in n]
        qk = [_heads_mm_nt(qb[a], k[a].astype(BF16)) for a in n]
        yield
        gate_row = [jnp.sum(jnp.where(eye, ig[a] - b[a], 0.0), axis=0, keepdims=True) for a in n]
        qk = [qk[a] * jnp.exp(jnp.where(causal, b[a] + gate_row[a], -jnp.inf) - m_intra[a]) for a in n]
        yield
        v_ones = [[jnp.concatenate([_pair_diag(vp), pair_ones], axis=1) for vp in _pairs(vb[a])]
                  for a in n]
        qk_parts = [_split2(qk[a]) for a in n]
        intra = [[_dot(qp, v_ones[a][p]) for p, qp in enumerate(_pairs(qk_parts[a][0]))] for a in n]
        yield
        num_intra = [jnp.concatenate([intra[a][p][:, 0:PAIR] for p in pairs], axis=1) for a in n]
        den_intra = [jnp.concatenate([intra[a][p][:, PAIR:2 * PAIR] for p in pairs], axis=1)
                     + _heads_sum(qk_parts[a][1], pair_ones) for a in n]
        ke = [(k[a] * jnp.exp(b_last[a] - b[a] + ig[a] - m_chunk[a])).astype(BF16) for a in n]
        own = [[_dot_tn(_pair_diag(kp), v_ones[a][p]) for p, kp in enumerate(_pairs(ke[a]))]
               for a in n]
        yield
        m_prev = [mx_ref[a] for a in n]
        cn = [[cn_ref[a, p] for p in pairs] for a in n]
        inter = [[_dot(qp, cn[a][p].astype(BF16)) for p, qp in enumerate(_pairs(qb[a]))] for a in n]
        yield
        for a in n:
            m_new = jnp.maximum(b_last[a] + m_prev[a], m_chunk[a])
            s_old = _pairs(jnp.exp(b_last[a] + m_prev[a] - m_new))
            s_new = _pairs(jnp.exp(m_chunk[a] - m_new))
            for p in pairs:
                cn_ref[a, p] = (jnp.concatenate([s_old[p], s_old[p]], axis=1) * cn[a][p]
                                + jnp.concatenate([s_new[p], s_new[p]], axis=1) * own[a][p])
            mx_ref[a] = m_new
        yield
        h = []
        for a in n:
            pre_m = b[a] + m_prev[a]
            m_t = jnp.maximum(pre_m, m_intra[a])
            s_inter = jnp.exp(pre_m - m_t)
            s_intra = jnp.exp(m_intra[a] - m_t)
            num_inter = jnp.concatenate([inter[a][p][:, 0:PAIR] for p in pairs], axis=1)
            den_inter = jnp.concatenate([inter[a][p][:, PAIR:2 * PAIR] for p in pairs], axis=1)
            num = s_inter * num_inter + s_intra * num_intra[a]
            den = s_inter * den_inter + s_intra * den_intra[a]
            h.append(num / jnp.maximum(jnp.abs(den), jnp.exp(-m_t)))
        yield
        ms = [_heads_sum((h[a] * h[a]).astype(BF16), pair_ones) * (1.0 / HEAD_DIM) for a in n]
        yield
        for a in n:
            o_ref[a, r, :] = (h[a] * lax.rsqrt(ms[a] + EPS) * nw
                              * _sigmoid(m_ref[a, r, 3 * WIDTH:4 * WIDTH]))

    state_free = 4

    def chunk_pair(ii, carry):
        first, second = chunk_stages(2 * ii), chunk_stages(2 * ii + 1)
        for _ in range(state_free):
            next(first)
        for _ in itertools.zip_longest(first, itertools.islice(second, state_free)):
            pass
        for _ in second:
            pass
        return carry

    assert (t // c) % 2 == 0
    lax.fori_loop(0, t // c // 2, chunk_pair, 0)


def _mlstm(mqkvo, gates, par, norm_w, layer, bsz, seq, t, nb):
    row = lambda n: pl.BlockSpec((nb, t, n), lambda b, j: (b, j, 0))
    lay = lambda a, b2: pl.BlockSpec((None, a, b2), lambda b, j: (layer, 0, 0))
    const = lambda shape: pl.BlockSpec(shape, lambda b, j: (0,) * len(shape))
    seq3 = lambda a: a.reshape(bsz, seq, a.shape[-1])
    f32_buf = pltpu.VMEM((nb, t, WIDTH), F32)
    out = pl.pallas_call(
        functools.partial(_mlstm_kernel, t=t, nb=nb),
        grid=(bsz // nb, seq // t),
        in_specs=[row(4 * WIDTH), row(GATE_LANES), lay(1, GATE_LANES), lay(1, WIDTH),
                  const((2, GATE_LANES, WIDTH)), const((PAIR, PAIR))],
        out_specs=row(WIDTH),
        out_shape=jax.ShapeDtypeStruct((bsz, seq, WIDTH), F32),
        scratch_shapes=[
            pltpu.VMEM((nb, WIDTH // PAIR, PAIR, 2 * PAIR), F32),
            pltpu.VMEM((nb, 1, WIDTH), F32),
            f32_buf, f32_buf, f32_buf,
        ],
        compiler_params=pltpu.CompilerParams(
            dimension_semantics=("arbitrary", "arbitrary"), vmem_limit_bytes=VMEM_LIMIT),
    )(seq3(mqkvo), seq3(gates), par, norm_w,
      jnp.stack([_expander(2 * HEADS), _expander(3 * HEADS)]),
      _pair_mask().astype(BF16))
    return out.reshape(bsz * seq, WIDTH)


def _swa_stages(q_ref, kvc_ref, kvp_ref, sink_ref, o_ref, first, tq):
    tb = SWA_BLOCK
    kvw = SWA_KV_WIDTH
    nq = tq // tb
    chains = [(i, g) for i in range(nq) for g in range(2)]
    st = {}

    def setup():
        kcat = jnp.concatenate([kvp_ref[:, 0:kvw], kvc_ref[:, 0:kvw]], axis=0)
        vcat = jnp.concatenate([kvp_ref[:, kvw:2 * kvw], kvc_ref[:, kvw:2 * kvw]], axis=0)
        st["v_t"] = vcat.T.astype(BF16)
        kr = pltpu.roll(kcat, HEAD_DIM, axis=1)
        lo = _iota2((tb + tq, kvw), 1) < HEAD_DIM

        def place(x_lo, x_hi):
            return jnp.where(lo, x_lo, 0.0).astype(BF16), jnp.where(lo, 0.0, x_hi).astype(BF16)

        st["k"] = (place(kcat, kr), place(kr, kcat))
        st["ki"] = _iota2((2 * tb, tb), 0)
        qi = _iota2((2 * tb, tb), 1)
        st["in_window"] = (st["ki"] > qi) & (st["ki"] <= qi + tb)

    def scores(ci):
        i, g = chains[ci]
        keys = jnp.concatenate([st["k"][g][0][i * tb:(i + 2) * tb],
                                st["k"][g][1][i * tb:(i + 2) * tb]], axis=0)
        qs = jnp.concatenate([q_ref[i * tb:(i + 1) * tb, 256 * g:256 * g + 128],
                              q_ref[i * tb:(i + 1) * tb, 256 * g + 128:256 * g + 256]], axis=0)
        st["s", ci] = _dot_nt(keys, qs.astype(BF16))

    def probs(ci):
        i, g = chains[ci]
        sc = st.pop(("s", ci))
        mask = st["in_window"]
        if i == 0:
            mask = mask & (st["ki"] >= jnp.where(first, tb, 0))
        ps, invs = [], []
        for part in range(2):
            for e in range(2):
                sink = sink_ref[2 * g + part:2 * g + part + 1, e * 2 * tb:e * 2 * tb + 1]
                se = jnp.where(mask, sc[e * 2 * tb:(e + 1) * 2 * tb, part * tb:(part + 1) * tb],
                               -jnp.inf)
                mx = jnp.maximum(jnp.max(se, axis=0, keepdims=True), sink)
                p = jnp.exp(se - mx)
                ps.append(p.astype(BF16))
                invs.append(1.0 / (jnp.sum(p, axis=0, keepdims=True) + jnp.exp(sink - mx)))
        st["p", ci] = jnp.concatenate(ps, axis=-1)
        st["inv", ci] = jnp.concatenate(invs, axis=-1)

    def attend(ci):
        i, g = chains[ci]
        o = _dot(st["v_t"][g * HEAD_DIM:(g + 1) * HEAD_DIM, i * tb:(i + 2) * tb],
                 st.pop(("p", ci))) * st.pop(("inv", ci))
        for part in range(2):
            j = 2 * g + part
            pair_t = jnp.concatenate([o[:, 2 * part * tb:(2 * part + 1) * tb],
                                      o[:, (2 * part + 1) * tb:(2 * part + 2) * tb]], axis=0)
            o_ref[i * tb:(i + 1) * tb, 128 * j:128 * (j + 1)] = pair_t.T

    n = len(chains)
    stages = [setup]
    for ci in range(n + 2):
        stages += [functools.partial(f, cj) for f, cj in ((scores, ci), (probs, ci - 1), (attend, ci - 2))
                   if 0 <= cj < n]
    return stages


def _post_kernel(x_ref, ya_ref, yb_ref, p_ref, wo_ref, nmlp_ref, wup_ref, wdn_ref,
                 nple_ref, wg_ref, wp_ref, nfin_ref,
                 sq0_ref, skv0_ref, sqn_ref, skvn_ref, kvpn_ref, sink_ref,
                 o_ref, yc_ref, *, final, tf, tiles_per_seq):
    t = x_ref.shape[0]
    step = pl.program_id(0)

    @pl.when(step == 0)
    def _():
        for stage in _swa_stages(sq0_ref, skv0_ref, kvpn_ref, sink_ref, yc_ref, True, t):
            stage()

    x = x_ref[...]
    x = x + _dot(ya_ref[...].astype(BF16), wo_ref[0:WIDTH, :])
    x = x + _dot(yb_ref[...].astype(BF16), wo_ref[WIDTH:2 * WIDTH, :])
    x = x + _dot(yc_ref[...].astype(BF16), wo_ref[2 * WIDTH:, :])
    stages = _swa_stages(sqn_ref, skvn_ref, kvpn_ref, sink_ref, yc_ref,
                         (step + 1) % tiles_per_seq == 0, t)
    slots = 2 * (D_FF // tf)
    per_slot = -(-len(stages) // slots)

    def fill():
        for _ in range(per_slot):
            if stages:
                stages.pop(0)()

    r = _rms_scale(x)
    h = (x * nmlp_ref[...]).astype(BF16)
    acc = jnp.zeros_like(x)
    for f in range(0, D_FF, tf):
        u = jnp.maximum(_dot(h, wup_ref[:, f:f + tf]), 0.0)
        fill()
        acc = acc + _dot((u * u).astype(BF16), wdn_ref[f:f + tf, :])
        fill()
    x = x + (r * r) * acc
    gate = _sigmoid(_rms_scale(x) * _dot((x * nple_ref[...]).astype(BF16), wg_ref[...]))
    x = x + gate * _dot(p_ref[...].astype(BF16), wp_ref[...])
    if final:
        x = _rms(x, nfin_ref[...])
    o_ref[...] = x


def _post(x, ya, yb, sq, skv, sinks_e, p, w_out, norm_mlp, w_up, w_down, norm_ple, w_gate, w_proj,
          norm_final, layer, tm, seq, final):
    m = x.shape[0]
    nt = m // tm
    blocks_per_tile = tm // SWA_BLOCK
    row = lambda n: pl.BlockSpec((tm, n), lambda i: (i, 0))
    first = lambda n: pl.BlockSpec((tm, n), lambda i: (0, 0))
    nxt = lambda n: pl.BlockSpec((tm, n), lambda i: (jnp.minimum(i + 1, nt - 1), 0))
    lay = lambda a, b: pl.BlockSpec((None, a, b), lambda i: (layer, 0, 0),
                                    pipeline_mode=pl.Buffered(1))
    return pl.pallas_call(
        functools.partial(_post_kernel, final=final, tf=512, tiles_per_seq=seq // tm),
        grid=(nt,),
        in_specs=[
            row(D_MODEL), row(WIDTH), row(WIDTH),
            pl.BlockSpec((None, tm, PLE_DIM), lambda i: (layer, i, 0)),
            lay(D_MODEL, D_MODEL), lay(1, D_MODEL), lay(D_MODEL, D_FF), lay(D_FF, D_MODEL),
            lay(1, D_MODEL), lay(D_MODEL, D_MODEL), lay(PLE_DIM, D_MODEL),
            pl.BlockSpec((1, D_MODEL), lambda i: (0, 0)),
            first(SWA_WIDTH), first(2 * SWA_KV_WIDTH), nxt(SWA_WIDTH), nxt(2 * SWA_KV_WIDTH),
            pl.BlockSpec((SWA_BLOCK, 2 * SWA_KV_WIDTH),
                         lambda i: ((i + 1) * blocks_per_tile - 1, 0)),
            pl.BlockSpec((None, SWA_WIDTH // 128, 4 * SWA_BLOCK), lambda i: (layer, 0, 0)),
        ],
        out_specs=row(D_MODEL),
        out_shape=jax.ShapeDtypeStruct((m, D_MODEL), F32),
        scratch_shapes=[pltpu.VMEM((tm, SWA_WIDTH), F32)],
        compiler_params=pltpu.CompilerParams(
            dimension_semantics=("arbitrary",), vmem_limit_bytes=VMEM_LIMIT),
    )(x, ya, yb, p, w_out, norm_mlp, w_up, w_down, norm_ple, w_gate, w_proj, norm_final,
      sq, skv, sq, skv, skv, sinks_e)


def _gate_row(depth, pieces):
    out = jnp.zeros((depth, 1, GATE_LANES), F32)
    for first_col, vals in pieces:
        out = out.at[:, 0, first_col:first_col + HEADS].set(vals.astype(F32))
    return out


def _rope_tables(positions):
    half = ROPE_DIM // 2
    inv_freq = ROPE_THETA ** (-jnp.arange(0, ROPE_DIM, 2, dtype=F32) / ROPE_DIM)
    dim = jnp.arange(128) % HEAD_DIM
    ang = positions.astype(F32).reshape(-1)[:, None] * inv_freq[dim % half][None, :]
    cos_t = jnp.where(dim < ROPE_DIM, jnp.cos(ang), 1.0)
    sin_t = jnp.where(dim < half, -jnp.sin(ang), jnp.where(dim < ROPE_DIM, jnp.sin(ang), 0.0))
    return cos_t, sin_t


def kernel(x, p, positions, w_in, conv_w, gdn_a_log, gdn_dt_bias, gdn_norm, mlstm_i_bias,
           mlstm_f_bias, mlstm_norm, attn_sinks, w_out, norm_mix, norm_mlp, w_up, w_down,
           norm_ple, w_ple_gate, w_ple_proj, norm_final):
    bsz, seq, d = x.shape
    depth = w_in.shape[0]
    m = bsz * seq
    tm = min(512, seq)
    t_mix = min(512, seq)
    nb_mix = 4 if bsz % 4 == 0 else 1
    nb_gdn = 8 if bsz % 8 == 0 else nb_mix
    t_gdn = min(2048 // nb_gdn, seq)

    cos_t, sin_t = _rope_tables(positions)
    gdn_par = jnp.concatenate([_gate_row(depth, [(HEADS, gdn_a_log)]),
                               _gate_row(depth, [(HEADS, gdn_dt_bias)])], axis=1)
    mlstm_par = _gate_row(depth, [(2 * HEADS, mlstm_i_bias), (3 * HEADS, mlstm_f_bias)])
    gdn_nw = jnp.tile(gdn_norm.astype(F32), (1, HEADS))[:, None, :]
    mlstm_nw = mlstm_norm.astype(F32)[:, None, :]
    sinks_e = jnp.repeat(attn_sinks.astype(F32), 2 * SWA_BLOCK, axis=-1).reshape(
        depth, SWA_WIDTH // 128, 4 * SWA_BLOCK)
    row3 = lambda a: a.astype(F32)[:, None, :]
    wo_b, wup_b, wdn_b = w_out.astype(BF16), w_up.astype(BF16), w_down.astype(BF16)
    wg_b, wp_b = w_ple_gate.astype(BF16), w_ple_proj.astype(BF16)
    nmix, nmlp, nple = row3(norm_mix), row3(norm_mlp), row3(norm_ple)
    nfin = norm_final.astype(F32)[None, :]
    p2 = p.reshape(depth, m, PLE_DIM)

    xf = x.reshape(m, d)
    for i in range(depth):
        gqkv, gz, mqkvo, sq, skv, gates = _proj(xf, nmix, w_in.astype(F32), cos_t, sin_t,
                                                conv_w.astype(F32), i, tm, seq)
        ya = _gdn(gqkv, gz, gates, gdn_par, gdn_nw, i, bsz, seq, t_gdn, nb_gdn)
        yb = _mlstm(mqkvo, gates, mlstm_par, mlstm_nw, i, bsz, seq, t_mix, nb_mix)
        xf = _post(xf, ya, yb, sq, skv, sinks_e, p2, wo_b, nmlp, wup_b, wdn_b, nple, wg_b, wp_b, nfin,
                   i, tm, seq, final=(i == depth - 1))
    return xf.reshape(bsz, seq, d)
```

```python
import functools
import itertools

import jax
import jax.numpy as jnp
from jax import lax
from jax.experimental import pallas as pl
from jax.experimental.pallas import tpu as pltpu

F32 = jnp.float32
BF16 = jnp.bfloat16

D_MODEL = 1024
DEPTH = 4
HEAD_DIM = 64
PLE_DIM = 256
D_FF = 4 * D_MODEL
EPS = 1e-6
HEADS = 4
WIDTH = HEADS * HEAD_DIM
PAIR = 2 * HEAD_DIM
CHUNK = 64
CONV_WIDTH = 4
GATE_SOFTCAP = 15.0
SWA_WIDTH = 512
SWA_KV_WIDTH = 128
SWA_BLOCK = 128
ROPE_DIM = 16
ROPE_THETA = 500000.0
GATE_LANES = 128
IN_COLS_PACKED = 2944
OFF_GQKV, OFF_GZ, OFF_M, OFF_SQ, OFF_SKV, OFF_GATES = 0, 768, 1024, 2048, 2560, 2816
IN_A = 4 * WIDTH
IN_M0 = IN_A + 2 * HEADS
IN_G2 = IN_M0 + 4 * WIDTH
IN_S0 = IN_G2 + 2 * HEADS
IN_S1 = IN_S0 + SWA_WIDTH + 2 * SWA_KV_WIDTH

VMEM_LIMIT = 56 * 1024 * 1024


def _dot(a, b):
    return jnp.dot(a, b, preferred_element_type=F32)


def _dot_nt(a, b):
    return lax.dot_general(a, b, (((1,), (1,)), ((), ())), preferred_element_type=F32)


def _dot_tn(a, b):
    return lax.dot_general(a, b, (((0,), (0,)), ((), ())), preferred_element_type=F32)


def _split2(x):
    hi = x.astype(BF16)
    lo = (x - hi.astype(F32)).astype(BF16)
    return hi, lo


def _dot_sel(x, sel):
    hi, lo = _split2(x)
    return _dot(hi, sel) + _dot(lo, sel)


def _chunk_cumsum(x):
    pos = _iota2(x.shape, 0) & (CHUNK - 1)
    sh = 1
    while sh < CHUNK:
        x = x + jnp.where(pos >= sh, pltpu.roll(x, sh, axis=0), 0.0)
        sh *= 2
    return x


def _sigmoid(x):
    return 1.0 / (1.0 + jnp.exp(-x))


def _softplus(x):
    return jnp.maximum(x, 0.0) + jnp.log1p(jnp.exp(-jnp.abs(x)))


def _rms(x, g):
    return x * lax.rsqrt(jnp.mean(x * x, axis=-1, keepdims=True) + EPS) * g


def _rms_scale(x):
    return lax.rsqrt(jnp.mean(x * x, axis=-1, keepdims=True) + EPS)


def _iota2(shape, dim):
    return lax.broadcasted_iota(jnp.int32, shape, dim)


def _pairs(x):
    return [x[:, p * PAIR:(p + 1) * PAIR] for p in range(WIDTH // PAIR)]


def _pair_mask():
    return (_iota2((PAIR, PAIR), 0) >> 6) == (_iota2((PAIR, PAIR), 1) >> 6)


def _pair_diag(x):
    low = _iota2(x.shape, 1) < HEAD_DIM
    zero = jnp.zeros((), x.dtype)
    return jnp.concatenate([jnp.where(low, x, zero), jnp.where(low, zero, x)], axis=0)


def _heads_mm(a, b):
    return jnp.concatenate([_dot(ap, _pair_diag(bp)) for ap, bp in zip(_pairs(a), _pairs(b))], axis=1)


def _heads_mm_nt(a, b):
    return jnp.concatenate([_dot_nt(ap, _pair_diag(bp)) for ap, bp in zip(_pairs(a), _pairs(b))],
                           axis=1)


def _heads_sum(x, pair_ones):
    return jnp.concatenate([_dot(xp, pair_ones) for xp in _pairs(x)], axis=1)


def _expander(first_col):
    r = _iota2((GATE_LANES, WIDTH), 0)
    c = _iota2((GATE_LANES, WIDTH), 1)
    return (r == first_col + (c >> 6)).astype(BF16)


def _rope(x, cos_t, sin_t, reps):
    n = x.shape[-1]
    if reps > 1:
        cos_t = jnp.concatenate([cos_t] * reps, axis=-1)
        sin_t = jnp.concatenate([sin_t] * reps, axis=-1)
    half = ROPE_DIM // 2
    x_up = pltpu.roll(x, n - half, axis=1)
    x_dn = pltpu.roll(x, half, axis=1)
    first_half = (_iota2(x.shape, 1) & (HEAD_DIM - 1)) < half
    return x * cos_t + jnp.where(first_half, x_up, x_dn) * sin_t


def _repack_w_in(w_in_ref, w_ref):
    rows = 128
    scale = HEAD_DIM ** -0.5
    lane_m = _iota2((1, 4 * WIDTH), 1)
    scale_m = jnp.where((lane_m >= WIDTH) & (lane_m < 2 * WIDTH), scale, 1.0)
    scale_s = jnp.where(_iota2((1, SWA_WIDTH + 2 * SWA_KV_WIDTH), 1) < SWA_WIDTH, scale, 1.0)

    def body(i, carry):
        r = pl.ds(pl.multiple_of(i * rows, rows), rows)
        w_ref[r, OFF_GQKV:OFF_M] = w_in_ref[r, 0:IN_A].astype(BF16)
        w_ref[r, OFF_M:OFF_SQ] = (w_in_ref[r, IN_M0:IN_G2] * scale_m).astype(BF16)
        w_ref[r, OFF_SQ:OFF_GATES] = (w_in_ref[r, IN_S0:IN_S1] * scale_s).astype(BF16)
        lane = _iota2((rows, GATE_LANES), 1)
        g2_tile = IN_G2 - 2 * HEADS
        gates = jnp.where(lane < 2 * HEADS, w_in_ref[r, IN_A:IN_A + GATE_LANES],
                          jnp.where(lane < 4 * HEADS, w_in_ref[r, g2_tile:g2_tile + GATE_LANES], 0.0))
        w_ref[r, OFF_GATES:IN_COLS_PACKED] = gates.astype(BF16)
        return carry

    lax.fori_loop(0, D_MODEL // rows, body, 0)


def _proj_kernel(x_ref, g_ref, w_in_ref, cos_ref, sin_ref, cw_ref, ones_ref,
                 gqkv_ref, gz_ref, m_ref, sq_ref, skv_ref, gates_ref, tail_ref, buf_ref, w_ref,
                 *, tiles_per_seq):
    t = x_ref.shape[0]

    @pl.when(pl.program_id(0) == 0)
    def _():
        _repack_w_in(w_in_ref, w_ref)

    @pl.when(pl.program_id(0) % tiles_per_seq == 0)
    def _():
        tail_ref[...] = jnp.zeros_like(tail_ref)

    x = x_ref[...]
    hb = (x * g_ref[...]).astype(BF16)
    scale = _rms_scale(x)

    def proj(lo, hi):
        return scale * _dot(hb, w_ref[:, lo:hi])

    a = proj(OFF_GQKV, OFF_M)
    buf_ref[0:8, :] = tail_ref[...]
    buf_ref[8:8 + t, :] = a[:, 0:OFF_GZ]
    tail_ref[...] = buf_ref[t:t + 8, :]
    gz_ref[...] = a[:, OFF_GZ:OFF_M]
    cw = cw_ref[...]
    pair_ones = ones_ref[...]

    def conv_silu(lo, hi):
        acc = buf_ref[5:5 + t, lo:hi] * cw[0:1, lo:hi]
        for j in range(1, CONV_WIDTH):
            acc = acc + buf_ref[5 + j:5 + j + t, lo:hi] * cw[j:j + 1, lo:hi]
        return acc * _sigmoid(acc)

    def l2n(v):
        return v * lax.rsqrt(_heads_sum((v * v).astype(BF16), pair_ones) + EPS)

    cos_t, sin_t = cos_ref[...], sin_ref[...]
    s = proj(OFF_SQ, IN_COLS_PACKED)
    q = conv_silu(0, WIDTH)
    gqkv_ref[:, 0:WIDTH] = l2n(q) * (HEAD_DIM ** -0.5)
    m_ref[:, 0:2 * WIDTH] = proj(OFF_M, OFF_M + 2 * WIDTH)
    sq_ref[...] = _rope(s[:, 0:SWA_WIDTH], cos_t, sin_t, SWA_WIDTH // 128)
    skv_ref[:, 0:SWA_KV_WIDTH] = _rope(s[:, SWA_WIDTH:SWA_WIDTH + SWA_KV_WIDTH], cos_t, sin_t, 1)
    skv_ref[:, SWA_KV_WIDTH:] = s[:, SWA_WIDTH + SWA_KV_WIDTH:SWA_WIDTH + 2 * SWA_KV_WIDTH]
    gates_ref[...] = s[:, SWA_WIDTH + 2 * SWA_KV_WIDTH:]
    k = conv_silu(WIDTH, 2 * WIDTH)
    gqkv_ref[:, WIDTH:2 * WIDTH] = l2n(k)
    gqkv_ref[:, 2 * WIDTH:3 * WIDTH] = conv_silu(2 * WIDTH, 3 * WIDTH)
    m_ref[:, 2 * WIDTH:4 * WIDTH] = proj(OFF_M + 2 * WIDTH, OFF_SQ)


def _proj(x, norm_w, w_in, cos_t, sin_t, conv_w, layer, tm, seq):
    m = x.shape[0]
    row = lambda n: pl.BlockSpec((tm, n), lambda i: (i, 0))
    widths = (768, 256, 1024, SWA_WIDTH, 2 * SWA_KV_WIDTH, GATE_LANES)
    return pl.pallas_call(
        functools.partial(_proj_kernel, tiles_per_seq=seq // tm),
        grid=(m // tm,),
        in_specs=[
            row(D_MODEL),
            pl.BlockSpec((None, 1, D_MODEL), lambda i: (layer, 0, 0)),
            pl.BlockSpec((None, D_MODEL, w_in.shape[-1]), lambda i: (layer, 0, 0),
                         pipeline_mode=pl.Buffered(1)),
            row(128), row(128),
            pl.BlockSpec((None, CONV_WIDTH, 3 * WIDTH), lambda i: (layer, 0, 0)),
            pl.BlockSpec((PAIR, PAIR), lambda i: (0, 0)),
        ],
        out_specs=[row(n) for n in widths],
        out_shape=[jax.ShapeDtypeStruct((m, n), F32) for n in widths],
        scratch_shapes=[
            pltpu.VMEM((8, 3 * WIDTH), F32),
            pltpu.VMEM((tm + 8, 3 * WIDTH), F32),
            pltpu.VMEM((D_MODEL, IN_COLS_PACKED), BF16),
        ],
        compiler_params=pltpu.CompilerParams(
            dimension_semantics=("arbitrary",), vmem_limit_bytes=VMEM_LIMIT),
    )(x, norm_w, w_in, cos_t, sin_t, conv_w, _pair_mask().astype(BF16))


def _gdn_kernel(qkv_ref, z_ref, gt_ref, par_ref, nw_ref, exp_ref, ones_ref, o_ref,
                state_ref, beta_s, gc_s, u_s, w_s, qd_s, kd_s, qkd_s, *, t, group, nb):
    c = CHUNK

    @pl.when(pl.program_id(1) == 0)
    def _():
        state_ref[...] = jnp.zeros_like(state_ref)

    pair_ones = ones_ref[...]

    def gate_rows(a, lo, hi):
        gates = gt_ref[a, lo:hi, :]
        par = par_ref[...]
        beta = _sigmoid(gates)
        g = -jnp.exp(par[0:1, :]) * _softplus(gates + par[1:2, :])
        beta_s[a, lo:hi, :] = _dot_sel(beta, exp_ref[0])
        gc_s[a, lo:hi, :] = _dot_sel(_chunk_cumsum(g), exp_ref[1])

    row = _iota2((c, WIDTH), 0)
    col = _iota2((c, WIDTH), 1) & (c - 1)
    causal = col <= row
    strict = col < row
    eye = col == row
    nw = nw_ref[...]

    def mm(a, b):
        return _heads_mm(a.astype(BF16), b.astype(BF16))

    def prepare(items, fillers):
        fillers = list(fillers)

        def fill():
            if fillers:
                fillers.pop(0)()

        n = range(len(items))
        sq = [a for a, _ in items]
        rs = [pl.ds(j * c, c) for _, j in items]
        q = [qkv_ref[sq[j], rs[j], 0:WIDTH] for j in n]
        k = [qkv_ref[sq[j], rs[j], WIDTH:2 * WIDTH] for j in n]
        gc = [gc_s[sq[j], rs[j], :] for j in n]
        g_last = [gc_s[a, pl.ds(j * c + c - 1, 1), :] for a, j in items]
        kb = [k[j] * beta_s[sq[j], rs[j], :] for j in n]
        kkqk = [_heads_mm_nt(jnp.concatenate([kb[j], q[j]], axis=0).astype(BF16), k[j].astype(BF16))
                for j in n]
        fill()
        gc_row = [jnp.sum(jnp.where(eye, gc[j], 0.0), axis=0, keepdims=True) for j in n]
        decay = [jnp.exp(jnp.where(causal, gc[j] - gc_row[j], -jnp.inf)) for j in n]
        nm = [jnp.where(strict, kkqk[j][0:c] * decay[j], 0.0) for j in n]
        for j in n:
            qkd_s[sq[j], rs[j], :] = (kkqk[j][c:2 * c] * decay[j]).astype(BF16)
        x = [eye.astype(F32) - jnp.where((row >> 1) == (col >> 1), nm[j], 0.0) for j in n]
        for lb in range(1, 6):
            off = ((row >> (lb + 1)) == (col >> (lb + 1))) & ((row >> lb) != (col >> lb))
            t1 = [mm(jnp.where(off, nm[j], 0.0), x[j]) for j in n]
            fill()
            t2 = [mm(x[j], t1[j]) for j in n]
            if lb % 2 == 0:
                fill()
            x = [x[j] - t2[j] for j in n]
        xb = [x[j].astype(BF16) for j in n]
        egc = [jnp.exp(gc[j]) for j in n]
        for j in n:
            u_s[sq[j], rs[j], :] = _heads_mm(
                xb[j], (qkv_ref[sq[j], rs[j], 2 * WIDTH:3 * WIDTH] * beta_s[sq[j], rs[j], :]).astype(BF16))
        for j in n:
            w_s[sq[j], rs[j], :] = _heads_mm(xb[j], (kb[j] * egc[j]).astype(BF16)).astype(BF16)
        for j in n:
            qd_s[sq[j], rs[j], :] = (q[j] * egc[j]).astype(BF16)
            kd_s[sq[j], rs[j], :] = (k[j] * jnp.exp(g_last[j] - gc[j])).astype(BF16)
        while fillers:
            fill()

    chunks = [(a, j) for a in range(nb) for j in range(t // c)]
    groups = [chunks[i:i + group] for i in range(0, len(chunks), group)]
    for a, j in groups[0]:
        gate_rows(a, j * c, (j + 1) * c)
    for gi, items in enumerate(groups):
        nxt = groups[gi + 1] if gi + 1 < len(groups) else []
        prepare(items, [functools.partial(gate_rows, a, j * c, (j + 1) * c) for a, j in nxt])

    def chunk_stages(i):
        n = range(nb)
        r = pl.ds(pl.multiple_of(i * c, c), c)
        pairs = range(WIDTH // PAIR)
        g_last = [gc_s[a, pl.ds(i * c + c - 1, 1), :] for a in n]
        s = [[state_ref[a, p] for p in pairs] for a in n]
        lhs = [_pairs(jnp.concatenate([w_s[a, r, :], qd_s[a, r, :]], axis=0)) for a in n]
        ws = [jnp.concatenate([_dot(lhs[a][p], s[a][p].astype(BF16)) for p in pairs], axis=1)
              for a in n]
        yield
        v_new = [(u_s[a, r, :] - ws[a][0:c]).astype(BF16) for a in n]
        v_diag = [[_pair_diag(vn) for vn in _pairs(v_new[a])] for a in n]
        upd = [[_dot_tn(_pair_diag(kd), v_diag[a][p]) for p, kd in enumerate(_pairs(kd_s[a, r, :]))]
               for a in n]
        yield
        for a in n:
            decay_end = _pairs(jnp.exp(g_last[a]))
            for p in pairs:
                state_ref[a, p] = s[a][p] * decay_end[p] + upd[a][p]
        yield
        o = [ws[a][c:2 * c] + jnp.concatenate(
            [_dot(qp, v_diag[a][p]) for p, qp in enumerate(_pairs(qkd_s[a, r, :]))], axis=1) for a in n]
        yield
        ms = [_heads_sum((o[a] * o[a]).astype(BF16), pair_ones) * (1.0 / HEAD_DIM) for a in n]
        yield
        for a in n:
            z = z_ref[a, r, :]
            o_ref[a, r, :] = o[a] * lax.rsqrt(ms[a] + EPS) * nw * (z * _sigmoid(z))

    state_stages = 3

    def chunk_pair(ii, carry):
        first, second = chunk_stages(2 * ii), chunk_stages(2 * ii + 1)
        for _ in range(state_stages):
            next(first)
        for _ in itertools.zip_longest(itertools.islice(second, state_stages), first):
            pass
        for _ in second:
            pass
        return carry

    assert (t // c) % 2 == 0
    lax.fori_loop(0, t // c // 2, chunk_pair, 0)


def _gdn(gqkv, gz, gates, par, norm_w, layer, bsz, seq, t, nb):
    row = lambda n: pl.BlockSpec((nb, t, n), lambda b, j: (b, j, 0))
    lay = lambda a, b2: pl.BlockSpec((None, a, b2), lambda b, j: (layer, 0, 0))
    const = lambda shape: pl.BlockSpec(shape, lambda b, j: (0,) * len(shape))
    seq3 = lambda a: a.reshape(bsz, seq, a.shape[-1])
    f32_buf = pltpu.VMEM((nb, t, WIDTH), F32)
    bf16_buf = pltpu.VMEM((nb, t, WIDTH), BF16)
    out = pl.pallas_call(
        functools.partial(_gdn_kernel, t=t, group=8, nb=nb),
        grid=(bsz // nb, seq // t),
        in_specs=[row(3 * WIDTH), row(WIDTH), row(GATE_LANES),
                  lay(2, GATE_LANES), lay(1, WIDTH),
                  const((2, GATE_LANES, WIDTH)), const((PAIR, PAIR))],
        out_specs=row(WIDTH),
        out_shape=jax.ShapeDtypeStruct((bsz, seq, WIDTH), F32),
        scratch_shapes=[
            pltpu.VMEM((nb, WIDTH // PAIR, PAIR, PAIR), F32),
            f32_buf, f32_buf,
            f32_buf,
            bf16_buf, bf16_buf, bf16_buf, bf16_buf,
        ],
        compiler_params=pltpu.CompilerParams(
            dimension_semantics=("arbitrary", "arbitrary"), vmem_limit_bytes=VMEM_LIMIT),
    )(seq3(gqkv), seq3(gz), seq3(gates), par, norm_w,
      jnp.stack([_expander(0), _expander(HEADS)]), _pair_mask().astype(BF16))
    return out.reshape(bsz * seq, WIDTH)


def _mlstm_kernel(m_ref, gt_ref, par_ref, nw_ref, exp_ref, ones_ref, o_ref,
                  cn_ref, mx_ref, ig_s, b_s, cm_s, *, t, nb):
    c = CHUNK

    @pl.when(pl.program_id(1) == 0)
    def _():
        cn_ref[...] = jnp.zeros_like(cn_ref)
        mx_ref[...] = jnp.zeros_like(mx_ref)

    pair_ones = ones_ref[...]

    def front(a, carry):
        pre = gt_ref[a] + par_ref[...]
        capped = GATE_SOFTCAP * jnp.tanh(pre * (1.0 / GATE_SOFTCAP))
        log_f = -_softplus(-capped)
        b_all = _chunk_cumsum(log_f)
        ig = _dot_sel(capped, exp_ref[0])
        b = _dot_sel(b_all, exp_ref[1])
        pos = _iota2((t, WIDTH), 0) & (c - 1)
        cm = ig - b
        for sh in (1, 2, 4, 8, 16, 32):
            cm = jnp.where(pos >= sh, jnp.maximum(cm, pltpu.roll(cm, sh, axis=0)), cm)
        ig_s[a] = ig
        b_s[a] = b
        cm_s[a] = cm
        return carry

    lax.fori_loop(0, nb, front, 0)

    row = _iota2((c, WIDTH), 0)
    col = _iota2((c, WIDTH), 1) & (c - 1)
    causal = col <= row
    eye = col == row
    nw = nw_ref[...]

    def chunk_stages(i):
        n = range(nb)
        pairs = range(WIDTH // PAIR)
        r = pl.ds(pl.multiple_of(i * c, c), c)
        last = pl.ds(i * c + c - 1, 1)
        qb = [m_ref[a, r, 0:WIDTH].astype(BF16) for a in n]
        k = [m_ref[a, r, WIDTH:2 * WIDTH] for a in n]
        vb = [m_ref[a, r, 2 * WIDTH:3 * WIDTH].astype(BF16) for a in n]
        ig = [ig_s[a, r, :] for a in n]
        b = [b_s[a, r, :] for a in n]
        m_intra = [b[a] + cm_s[a, r, :] for a in n]
        b_last = [b_s[a, last, :] for a in n]
        m_chunk = [b_last[a] + cm_s[a, last, :] for a in n]
        qk = [_heads_mm_nt(qb[a], k[a].astype(BF16)) for a in n]
        yield
        gate_row = [jnp.sum(jnp.where(eye, ig[a] - b[a], 0.0), axis=0, keepdims=True) for a in n]
        qk = [qk[a] * jnp.exp(jnp.where(causal, b[a] + gate_row[a], -jnp.inf) - m_intra[a]) for a in n]
        yield
        v_ones = [[jnp.concatenate([_pair_diag(vp), pair_ones], axis=1) for vp in _pairs(vb[a])]
                  for a in n]
        qk_parts = [_split2(qk[a]) for a in n]
        intra = [[_dot(qp, v_ones[a][p]) for p, qp in enumerate(_pairs(qk_parts[a][0]))] for a in n]
        yield
        num_intra = [jnp.concatenate([intra[a][p][:, 0:PAIR] for p in pairs], axis=1) for a in n]
        den_intra = [jnp.concatenate([intra[a][p][:, PAIR:2 * PAIR] for p in pairs], axis=1)
                     + _heads_sum(qk_parts[a][1], pair_ones) for a in n]
        ke = [(k[a] * jnp.exp(b_last[a] - b[a] + ig[a] - m_chunk[a])).astype(BF16) for a in n]
        own = [[_dot_tn(_pair_diag(kp), v_ones[a][p]) for p, kp in enumerate(_pairs(ke[a]))]
               for a in n]
        yield
        m_prev = [mx_ref[a] for a in n]
        cn = [[cn_ref[a, p] for p in pairs] for a in n]
        inter = [[_dot(qp, cn[a][p].astype(BF16)) for p, qp in enumerate(_pairs(qb[a]))] for a in n]
        yield
        for a in n:
            m_new = jnp.maximum(b_last[a] + m_prev[a], m_chunk[a])
            s_old = _pairs(jnp.exp(b_last[a] + m_prev[a] - m_new))
            s_new = _pairs(jnp.exp(m_chunk[a] - m_new))
            for p in pairs:
                cn_ref[a, p] = (jnp.concatenate([s_old[p], s_old[p]], axis=1) * cn[a][p]
                                + jnp.concatenate([s_new[p], s_new[p]], axis=1) * own[a][p])
            mx_ref[a] = m_new
        yield
        h = []
        for a in n:
            pre_m = b[a] + m_prev[a]
            m_t = jnp.maximum(pre_m, m_intra[a])
            s_inter = jnp.exp(pre_m - m_t)
            s_intra = jnp.exp(m_intra[a] - m_t)
            num_inter = jnp.concatenate([inter[a][p][:, 0:PAIR] for p in pairs], axis=1)
            den_inter = jnp.concatenate([inter[a][p][:, PAIR:2 * PAIR] for p in pairs], axis=1)
            num = s_inter * num_inter + s_intra * num_intra[a]
            den = s_inter * den_inter + s_intra * den_intra[a]
            h.append(num / jnp.maximum(jnp.abs(den), jnp.exp(-m_t)))
        yield
        ms = [_heads_sum((h[a] * h[a]).astype(BF16), pair_ones) * (1.0 / HEAD_DIM) for a in n]
        yield
        for a in n:
            o_ref[a, r, :] = (h[a] * lax.rsqrt(ms[a] + EPS) * nw
                              * _sigmoid(m_ref[a, r, 3 * WIDTH:4 * WIDTH]))

    state_free = 4

    def chunk_pair(ii, carry):
        first, second = chunk_stages(2 * ii), chunk_stages(2 * ii + 1)
        for _ in range(state_free):
            next(first)
        for _ in itertools.zip_longest(first, itertools.islice(second, state_free)):
            pass
        for _ in second:
            pass
        return carry

    assert (t // c) % 2 == 0
    lax.fori_loop(0, t // c // 2, chunk_pair, 0)


def _mlstm(mqkvo, gates, par, norm_w, layer, bsz, seq, t, nb):
    row = lambda n: pl.BlockSpec((nb, t, n), lambda b, j: (b, j, 0))
    lay = lambda a, b2: pl.BlockSpec((None, a, b2), lambda b, j: (layer, 0, 0))
    const = lambda shape: pl.BlockSpec(shape, lambda b, j: (0,) * len(shape))
    seq3 = lambda a: a.reshape(bsz, seq, a.shape[-1])
    f32_buf = pltpu.VMEM((nb, t, WIDTH), F32)
    out = pl.pallas_call(
        functools.partial(_mlstm_kernel, t=t, nb=nb),
        grid=(bsz // nb, seq // t),
        in_specs=[row(4 * WIDTH), row(GATE_LANES), lay(1, GATE_LANES), lay(1, WIDTH),
                  const((2, GATE_LANES, WIDTH)), const((PAIR, PAIR))],
        out_specs=row(WIDTH),
        out_shape=jax.ShapeDtypeStruct((bsz, seq, WIDTH), F32),
        scratch_shapes=[
            pltpu.VMEM((nb, WIDTH // PAIR, PAIR, 2 * PAIR), F32),
            pltpu.VMEM((nb, 1, WIDTH), F32),
            f32_buf, f32_buf, f32_buf,
        ],
        compiler_params=pltpu.CompilerParams(
            dimension_semantics=("arbitrary", "arbitrary"), vmem_limit_bytes=VMEM_LIMIT),
    )(seq3(mqkvo), seq3(gates), par, norm_w,
      jnp.stack([_expander(2 * HEADS), _expander(3 * HEADS)]),
      _pair_mask().astype(BF16))
    return out.reshape(bsz * seq, WIDTH)


def _swa_stages(q_ref, kvc_ref, kvp_ref, sink_ref, o_ref, first, tq):
    tb = SWA_BLOCK
    kvw = SWA_KV_WIDTH
    nq = tq // tb
    chains = [(i, g) for i in range(nq) for g in range(2)]
    st = {}

    def setup():
        kcat = jnp.concatenate([kvp_ref[:, 0:kvw], kvc_ref[:, 0:kvw]], axis=0)
        vcat = jnp.concatenate([kvp_ref[:, kvw:2 * kvw], kvc_ref[:, kvw:2 * kvw]], axis=0)
        st["v_t"] = vcat.T.astype(BF16)
        kr = pltpu.roll(kcat, HEAD_DIM, axis=1)
        lo = _iota2((tb + tq, kvw), 1) < HEAD_DIM

        def place(x_lo, x_hi):
            return jnp.where(lo, x_lo, 0.0).astype(BF16), jnp.where(lo, 0.0, x_hi).astype(BF16)

        st["k"] = (place(kcat, kr), place(kr, kcat))
        st["ki"] = _iota2((2 * tb, tb), 0)
        qi = _iota2((2 * tb, tb), 1)
        st["in_window"] = (st["ki"] > qi) & (st["ki"] <= qi + tb)

    def scores(ci):
        i, g = chains[ci]
        keys = jnp.concatenate([st["k"][g][0][i * tb:(i + 2) * tb],
                                st["k"][g][1][i * tb:(i + 2) * tb]], axis=0)
        qs = jnp.concatenate([q_ref[i * tb:(i + 1) * tb, 256 * g:256 * g + 128],
                              q_ref[i * tb:(i + 1) * tb, 256 * g + 128:256 * g + 256]], axis=0)
        st["s", ci] = _dot_nt(keys, qs.astype(BF16))

    def probs(ci):
        i, g = chains[ci]
        sc = st.pop(("s", ci))
        mask = st["in_window"]
        if i == 0:
            mask = mask & (st["ki"] >= jnp.where(first, tb, 0))
        ps, invs = [], []
        for part in range(2):
            for e in range(2):
                sink = sink_ref[2 * g + part:2 * g + part + 1, e * 2 * tb:e * 2 * tb + 1]
                se = jnp.where(mask, sc[e * 2 * tb:(e + 1) * 2 * tb, part * tb:(part + 1) * tb],
                               -jnp.inf)
                mx = jnp.maximum(jnp.max(se, axis=0, keepdims=True), sink)
                p = jnp.exp(se - mx)
                ps.append(p.astype(BF16))
                invs.append(1.0 / (jnp.sum(p, axis=0, keepdims=True) + jnp.exp(sink - mx)))
        st["p", ci] = jnp.concatenate(ps, axis=-1)
        st["inv", ci] = jnp.concatenate(invs, axis=-1)

    def attend(ci):
        i, g = chains[ci]
        o = _dot(st["v_t"][g * HEAD_DIM:(g + 1) * HEAD_DIM, i * tb:(i + 2) * tb],
                 st.pop(("p", ci))) * st.pop(("inv", ci))
        for part in range(2):
            j = 2 * g + part
            pair_t = jnp.concatenate([o[:, 2 * part * tb:(2 * part + 1) * tb],
                                      o[:, (2 * part + 1) * tb:(2 * part + 2) * tb]], axis=0)
            o_ref[i * tb:(i + 1) * tb, 128 * j:128 * (j + 1)] = pair_t.T

    n = len(chains)
    stages = [setup]
    for ci in range(n + 2):
        stages += [functools.partial(f, cj) for f, cj in ((scores, ci), (probs, ci - 1), (attend, ci - 2))
                   if 0 <= cj < n]
    return stages


def _post_kernel(x_ref, ya_ref, yb_ref, p_ref, wo_ref, nmlp_ref, wup_ref, wdn_ref,
                 nple_ref, wg_ref, wp_ref, nfin_ref,
                 sq0_ref, skv0_ref, sqn_ref, skvn_ref, kvpn_ref, sink_ref,
                 o_ref, yc_ref, *, final, tf, tiles_per_seq):
    t = x_ref.shape[0]
    step = pl.program_id(0)

    @pl.when(step == 0)
    def _():
        for stage in _swa_stages(sq0_ref, skv0_ref, kvpn_ref, sink_ref, yc_ref, True, t):
            stage()

    x = x_ref[...]
    x = x + _dot(ya_ref[...].astype(BF16), wo_ref[0:WIDTH, :])
    x = x + _dot(yb_ref[...].astype(BF16), wo_ref[WIDTH:2 * WIDTH, :])
    x = x + _dot(yc_ref[...].astype(BF16), wo_ref[2 * WIDTH:, :])
    stages = _swa_stages(sqn_ref, skvn_ref, kvpn_ref, sink_ref, yc_ref,
                         (step + 1) % tiles_per_seq == 0, t)
    slots = 2 * (D_FF // tf)
    per_slot = -(-len(stages) // slots)

    def fill():
        for _ in range(per_slot):
            if stages:
                stages.pop(0)()

    r = _rms_scale(x)
    h = (x * nmlp_ref[...]).astype(BF16)
    acc = jnp.zeros_like(x)
    for f in range(0, D_FF, tf):
        u = jnp.maximum(_dot(h, wup_ref[:, f:f + tf]), 0.0)
        fill()
        acc = acc + _dot((u * u).astype(BF16), wdn_ref[f:f + tf, :])
        fill()
    x = x + (r * r) * acc
    gate = _sigmoid(_rms_scale(x) * _dot((x * nple_ref[...]).astype(BF16), wg_ref[...]))
    x = x + gate * _dot(p_ref[...].astype(BF16), wp_ref[...])
    if final:
        x = _rms(x, nfin_ref[...])
    o_ref[...] = x


def _post(x, ya, yb, sq, skv, sinks_e, p, w_out, norm_mlp, w_up, w_down, norm_ple, w_gate, w_proj,
          norm_final, layer, tm, seq, final):
    m = x.shape[0]
    nt = m // tm
    blocks_per_tile = tm // SWA_BLOCK
    row = lambda n: pl.BlockSpec((tm, n), lambda i: (i, 0))
    first = lambda n: pl.BlockSpec((tm, n), lambda i: (0, 0))
    nxt = lambda n: pl.BlockSpec((tm, n), lambda i: (jnp.minimum(i + 1, nt - 1), 0))
    lay = lambda a, b: pl.BlockSpec((None, a, b), lambda i: (layer, 0, 0),
                                    pipeline_mode=pl.Buffered(1))
    return pl.pallas_call(
        functools.partial(_post_kernel, final=final, tf=512, tiles_per_seq=seq // tm),
        grid=(nt,),
        in_specs=[
            row(D_MODEL), row(WIDTH), row(WIDTH),
            pl.BlockSpec((None, tm, PLE_DIM), lambda i: (layer, i, 0)),
            lay(D_MODEL, D_MODEL), lay(1, D_MODEL), lay(D_MODEL, D_FF), lay(D_FF, D_MODEL),
            lay(1, D_MODEL), lay(D_MODEL, D_MODEL), lay(PLE_DIM, D_MODEL),
            pl.BlockSpec((1, D_MODEL), lambda i: (0, 0)),
            first(SWA_WIDTH), first(2 * SWA_KV_WIDTH), nxt(SWA_WIDTH), nxt(2 * SWA_KV_WIDTH),
            pl.BlockSpec((SWA_BLOCK, 2 * SWA_KV_WIDTH),
                         lambda i: ((i + 1) * blocks_per_tile - 1, 0)),
            pl.BlockSpec((None, SWA_WIDTH // 128, 4 * SWA_BLOCK), lambda i: (layer, 0, 0)),
        ],
        out_specs=row(D_MODEL),
        out_shape=jax.ShapeDtypeStruct((m, D_MODEL), F32),
        scratch_shapes=[pltpu.VMEM((tm, SWA_WIDTH), F32)],
        compiler_params=pltpu.CompilerParams(
            dimension_semantics=("arbitrary",), vmem_limit_bytes=VMEM_LIMIT),
    )(x, ya, yb, p, w_out, norm_mlp, w_up, w_down, norm_ple, w_gate, w_proj, norm_final,
      sq, skv, sq, skv, skv, sinks_e)


def _gate_row(depth, pieces):
    out = jnp.zeros((depth, 1, GATE_LANES), F32)
    for first_col, vals in pieces:
        out = out.at[:, 0, first_col:first_col + HEADS].set(vals.astype(F32))
    return out


def _rope_tables(positions):
    half = ROPE_DIM // 2
    inv_freq = ROPE_THETA ** (-jnp.arange(0, ROPE_DIM, 2, dtype=F32) / ROPE_DIM)
    dim = jnp.arange(128) % HEAD_DIM
    ang = positions.astype(F32).reshape(-1)[:, None] * inv_freq[dim % half][None, :]
    cos_t = jnp.where(dim < ROPE_DIM, jnp.cos(ang), 1.0)
    sin_t = jnp.where(dim < half, -jnp.sin(ang), jnp.where(dim < ROPE_DIM, jnp.sin(ang), 0.0))
    return cos_t, sin_t


def kernel(x, p, positions, w_in, conv_w, gdn_a_log, gdn_dt_bias, gdn_norm, mlstm_i_bias,
           mlstm_f_bias, mlstm_norm, attn_sinks, w_out, norm_mix, norm_mlp, w_up, w_down,
           norm_ple, w_ple_gate, w_ple_proj, norm_final):
    bsz, seq, d = x.shape
    depth = w_in.shape[0]
    m = bsz * seq
    tm = min(512, seq)
    t_mix = min(512, seq)
    nb_mix = 4 if bsz % 4 == 0 else 1
    nb_gdn = 8 if bsz % 8 == 0 else nb_mix
    t_gdn = min(2048 // nb_gdn, seq)

    cos_t, sin_t = _rope_tables(positions)
    gdn_par = jnp.concatenate([_gate_row(depth, [(HEADS, gdn_a_log)]),
                               _gate_row(depth, [(HEADS, gdn_dt_bias)])], axis=1)
    mlstm_par = _gate_row(depth, [(2 * HEADS, mlstm_i_bias), (3 * HEADS, mlstm_f_bias)])
    gdn_nw = jnp.tile(gdn_norm.astype(F32), (1, HEADS))[:, None, :]
    mlstm_nw = mlstm_norm.astype(F32)[:, None, :]
    sinks_e = jnp.repeat(attn_sinks.astype(F32), 2 * SWA_BLOCK, axis=-1).reshape(
        depth, SWA_WIDTH // 128, 4 * SWA_BLOCK)
    row3 = lambda a: a.astype(F32)[:, None, :]
    wo_b, wup_b, wdn_b = w_out.astype(BF16), w_up.astype(BF16), w_down.astype(BF16)
    wg_b, wp_b = w_ple_gate.astype(BF16), w_ple_proj.astype(BF16)
    nmix, nmlp, nple = row3(norm_mix), row3(norm_mlp), row3(norm_ple)
    nfin = norm_final.astype(F32)[None, :]
    p2 = p.reshape(depth, m, PLE_DIM)

    xf = x.reshape(m, d)
    for i in range(depth):
        gqkv, gz, mqkvo, sq, skv, gates = _proj(xf, nmix, w_in.astype(F32), cos_t, sin_t,
                                                conv_w.astype(F32), i, tm, seq)
        ya = _gdn(gqkv, gz, gates, gdn_par, gdn_nw, i, bsz, seq, t_gdn, nb_gdn)
        yb = _mlstm(mqkvo, gates, mlstm_par, mlstm_nw, i, bsz, seq, t_mix, nb_mix)
        xf = _post(xf, ya, yb, sq, skv, sinks_e, p2, wo_b, nmlp, wup_b, wdn_b, nple, wg_b, wp_b, nfin,
                   i, tm, seq, final=(i == depth - 1))
    return xf.reshape(bsz, seq, d)
```

```python
import functools

import jax
import jax.numpy as jnp
from jax import lax
from jax.experimental import pallas as pl
from jax.experimental.pallas import tpu as pltpu

F32 = jnp.float32
BF16 = jnp.bfloat16

D_MODEL = 1024
DEPTH = 4
HEAD_DIM = 64
PLE_DIM = 256
D_FF = 4 * D_MODEL
EPS = 1e-6
HEADS = 4
WIDTH = HEADS * HEAD_DIM
PAIR = 2 * HEAD_DIM
CHUNK = 64
CONV_WIDTH = 4
GATE_SOFTCAP = 15.0
SWA_WIDTH = 512
SWA_KV_WIDTH = 128
SWA_BLOCK = 128
ROPE_DIM = 16
ROPE_THETA = 500000.0
GATE_LANES = 128
IN_COLS_PACKED = 2944
OFF_GQKV, OFF_GZ, OFF_M, OFF_SQ, OFF_SKV, OFF_GATES = 0, 768, 1024, 2048, 2560, 2816
IN_A = 4 * WIDTH
IN_M0 = IN_A + 2 * HEADS
IN_G2 = IN_M0 + 4 * WIDTH
IN_S0 = IN_G2 + 2 * HEADS
IN_S1 = IN_S0 + SWA_WIDTH + 2 * SWA_KV_WIDTH

VMEM_LIMIT = 56 * 1024 * 1024


def _dot(a, b):
    return jnp.dot(a, b, preferred_element_type=F32)


def _dot_nt(a, b):
    return lax.dot_general(a, b, (((1,), (1,)), ((), ())), preferred_element_type=F32)


def _dot_tn(a, b):
    return lax.dot_general(a, b, (((0,), (0,)), ((), ())), preferred_element_type=F32)


def _split2(x):
    hi = x.astype(BF16)
    lo = (x - hi.astype(F32)).astype(BF16)
    return hi, lo


def _dot_sel(x, sel):
    hi, lo = _split2(x)
    return _dot(hi, sel) + _dot(lo, sel)


def _chunk_cumsum(x):
    pos = _iota2(x.shape, 0) & (CHUNK - 1)
    sh = 1
    while sh < CHUNK:
        x = x + jnp.where(pos >= sh, pltpu.roll(x, sh, axis=0), 0.0)
        sh *= 2
    return x


def _sigmoid(x):
    return 1.0 / (1.0 + jnp.exp(-x))


def _softplus(x):
    return jnp.maximum(x, 0.0) + jnp.log1p(jnp.exp(-jnp.abs(x)))


def _rms(x, g):
    return x * lax.rsqrt(jnp.mean(x * x, axis=-1, keepdims=True) + EPS) * g


def _rms_scale(x):
    return lax.rsqrt(jnp.mean(x * x, axis=-1, keepdims=True) + EPS)


def _iota2(shape, dim):
    return lax.broadcasted_iota(jnp.int32, shape, dim)


def _pairs(x):
    return [x[:, p * PAIR:(p + 1) * PAIR] for p in range(WIDTH // PAIR)]


def _pair_mask():
    return (_iota2((PAIR, PAIR), 0) >> 6) == (_iota2((PAIR, PAIR), 1) >> 6)


def _pair_diag(x):
    low = _iota2(x.shape, 1) < HEAD_DIM
    zero = jnp.zeros((), x.dtype)
    return jnp.concatenate([jnp.where(low, x, zero), jnp.where(low, zero, x)], axis=0)


def _heads_mm(a, b):
    return jnp.concatenate([_dot(ap, _pair_diag(bp)) for ap, bp in zip(_pairs(a), _pairs(b))], axis=1)


def _heads_mm_nt(a, b):
    return jnp.concatenate([_dot_nt(ap, _pair_diag(bp)) for ap, bp in zip(_pairs(a), _pairs(b))],
                           axis=1)


def _heads_sum(x, pair_ones):
    return jnp.concatenate([_dot(xp, pair_ones) for xp in _pairs(x)], axis=1)


def _emit_pipelined(stage_gens, lead):
    pending, live, step = list(stage_gens), [], 0
    while pending or live:
        if pending and step % lead == 0:
            live.append(pending.pop(0))
        for g in list(live):
            if next(g, StopIteration) is StopIteration:
                live.remove(g)
        step += 1


def _expander(first_col):
    r = _iota2((GATE_LANES, WIDTH), 0)
    c = _iota2((GATE_LANES, WIDTH), 1)
    return (r == first_col + (c >> 6)).astype(BF16)


def _rope(x, cos_t, sin_t, reps):
    n = x.shape[-1]
    if reps > 1:
        cos_t = jnp.concatenate([cos_t] * reps, axis=-1)
        sin_t = jnp.concatenate([sin_t] * reps, axis=-1)
    half = ROPE_DIM // 2
    x_up = pltpu.roll(x, n - half, axis=1)
    x_dn = pltpu.roll(x, half, axis=1)
    first_half = (_iota2(x.shape, 1) & (HEAD_DIM - 1)) < half
    return x * cos_t + jnp.where(first_half, x_up, x_dn) * sin_t


def _repack_w_in(w_in_ref, w_ref):
    rows = 128
    scale = HEAD_DIM ** -0.5
    lane_m = _iota2((1, 4 * WIDTH), 1)
    scale_m = jnp.where((lane_m >= WIDTH) & (lane_m < 2 * WIDTH), scale, 1.0)
    scale_s = jnp.where(_iota2((1, SWA_WIDTH + 2 * SWA_KV_WIDTH), 1) < SWA_WIDTH, scale, 1.0)

    def body(i, carry):
        r = pl.ds(pl.multiple_of(i * rows, rows), rows)
        w_ref[r, OFF_GQKV:OFF_M] = w_in_ref[r, 0:IN_A].astype(BF16)
        w_ref[r, OFF_M:OFF_SQ] = (w_in_ref[r, IN_M0:IN_G2] * scale_m).astype(BF16)
        w_ref[r, OFF_SQ:OFF_GATES] = (w_in_ref[r, IN_S0:IN_S1] * scale_s).astype(BF16)
        lane = _iota2((rows, GATE_LANES), 1)
        g2_tile = IN_G2 - 2 * HEADS
        gates = jnp.where(lane < 2 * HEADS, w_in_ref[r, IN_A:IN_A + GATE_LANES],
                          jnp.where(lane < 4 * HEADS, w_in_ref[r, g2_tile:g2_tile + GATE_LANES], 0.0))
        w_ref[r, OFF_GATES:IN_COLS_PACKED] = gates.astype(BF16)
        return carry

    lax.fori_loop(0, D_MODEL // rows, body, 0)


def _proj_kernel(x_ref, g_ref, w_in_ref, cos_ref, sin_ref, cw_ref, ones_ref,
                 gqkv_ref, gz_ref, m_ref, sq_ref, skv_ref, gates_ref, tail_ref, buf_ref, w_ref,
                 *, tiles_per_seq):
    t = x_ref.shape[0]

    @pl.when(pl.program_id(0) == 0)
    def _():
        _repack_w_in(w_in_ref, w_ref)

    @pl.when(pl.program_id(0) % tiles_per_seq == 0)
    def _():
        tail_ref[...] = jnp.zeros_like(tail_ref)

    x = x_ref[...]
    hb = (x * g_ref[...]).astype(BF16)
    scale = _rms_scale(x)

    def proj(lo, hi):
        return scale * _dot(hb, w_ref[:, lo:hi])

    a = proj(OFF_GQKV, OFF_M)
    buf_ref[0:8, :] = tail_ref[...]
    buf_ref[8:8 + t, :] = a[:, 0:OFF_GZ]
    tail_ref[...] = buf_ref[t:t + 8, :]
    gz_ref[...] = a[:, OFF_GZ:OFF_M]
    cw = cw_ref[...]
    pair_ones = ones_ref[...]

    def conv_silu(lo, hi):
        acc = buf_ref[5:5 + t, lo:hi] * cw[0:1, lo:hi]
        for j in range(1, CONV_WIDTH):
            acc = acc + buf_ref[5 + j:5 + j + t, lo:hi] * cw[j:j + 1, lo:hi]
        return acc * _sigmoid(acc)

    def l2n(v):
        return v * lax.rsqrt(_heads_sum((v * v).astype(BF16), pair_ones) + EPS)

    cos_t, sin_t = cos_ref[...], sin_ref[...]
    s = proj(OFF_SQ, IN_COLS_PACKED)
    q = conv_silu(0, WIDTH)
    gqkv_ref[:, 0:WIDTH] = l2n(q) * (HEAD_DIM ** -0.5)
    m_ref[:, 0:2 * WIDTH] = proj(OFF_M, OFF_M + 2 * WIDTH)
    sq_ref[...] = _rope(s[:, 0:SWA_WIDTH], cos_t, sin_t, SWA_WIDTH // 128)
    skv_ref[:, 0:SWA_KV_WIDTH] = _rope(s[:, SWA_WIDTH:SWA_WIDTH + SWA_KV_WIDTH], cos_t, sin_t, 1)
    skv_ref[:, SWA_KV_WIDTH:] = s[:, SWA_WIDTH + SWA_KV_WIDTH:SWA_WIDTH + 2 * SWA_KV_WIDTH]
    gates_ref[...] = s[:, SWA_WIDTH + 2 * SWA_KV_WIDTH:]
    k = conv_silu(WIDTH, 2 * WIDTH)
    gqkv_ref[:, WIDTH:2 * WIDTH] = l2n(k)
    gqkv_ref[:, 2 * WIDTH:3 * WIDTH] = conv_silu(2 * WIDTH, 3 * WIDTH)
    m_ref[:, 2 * WIDTH:4 * WIDTH] = proj(OFF_M + 2 * WIDTH, OFF_SQ)


def _proj(x, norm_w, w_in, cos_t, sin_t, conv_w, layer, tm, seq):
    m = x.shape[0]
    row = lambda n: pl.BlockSpec((tm, n), lambda i: (i, 0))
    widths = (768, 256, 1024, SWA_WIDTH, 2 * SWA_KV_WIDTH, GATE_LANES)
    return pl.pallas_call(
        functools.partial(_proj_kernel, tiles_per_seq=seq // tm),
        grid=(m // tm,),
        in_specs=[
            row(D_MODEL),
            pl.BlockSpec((None, 1, D_MODEL), lambda i: (layer, 0, 0)),
            pl.BlockSpec((None, D_MODEL, w_in.shape[-1]), lambda i: (layer, 0, 0),
                         pipeline_mode=pl.Buffered(1)),
            row(128), row(128),
            pl.BlockSpec((None, CONV_WIDTH, 3 * WIDTH), lambda i: (layer, 0, 0)),
            pl.BlockSpec((PAIR, PAIR), lambda i: (0, 0)),
        ],
        out_specs=[row(n) for n in widths],
        out_shape=[jax.ShapeDtypeStruct((m, n), F32) for n in widths],
        scratch_shapes=[
            pltpu.VMEM((8, 3 * WIDTH), F32),
            pltpu.VMEM((tm + 8, 3 * WIDTH), F32),
            pltpu.VMEM((D_MODEL, IN_COLS_PACKED), BF16),
        ],
        compiler_params=pltpu.CompilerParams(
            dimension_semantics=("arbitrary",), vmem_limit_bytes=VMEM_LIMIT),
    )(x, norm_w, w_in, cos_t, sin_t, conv_w, _pair_mask().astype(BF16))


def _gdn_kernel(qkv_ref, z_ref, gt_ref, par_ref, nw_ref, exp_ref, ones_ref, o_ref,
                state_ref, beta_s, gc_s, u_s, w_s, qd_s, kd_s, qkd_s, *, t, group, nb):
    c = CHUNK

    @pl.when(pl.program_id(1) == 0)
    def _():
        state_ref[...] = jnp.zeros_like(state_ref)

    pair_ones = ones_ref[...]

    def gate_rows(a, lo, hi):
        gates = gt_ref[a, lo:hi, :]
        par = par_ref[...]
        beta = _sigmoid(gates)
        g = -jnp.exp(par[0:1, :]) * _softplus(gates + par[1:2, :])
        beta_s[a, lo:hi, :] = _dot_sel(beta, exp_ref[0])
        gc_s[a, lo:hi, :] = _dot_sel(_chunk_cumsum(g), exp_ref[1])

    row = _iota2((c, WIDTH), 0)
    col = _iota2((c, WIDTH), 1) & (c - 1)
    causal = col <= row
    strict = col < row
    eye = col == row
    nw = nw_ref[...]

    def mm(a, b):
        return _heads_mm(a.astype(BF16), b.astype(BF16))

    def prepare(items, fillers):
        fillers = list(fillers)

        def fill():
            if fillers:
                fillers.pop(0)()

        n = range(len(items))
        sq = [a for a, _ in items]
        rs = [pl.ds(j * c, c) for _, j in items]
        q = [qkv_ref[sq[j], rs[j], 0:WIDTH] for j in n]
        k = [qkv_ref[sq[j], rs[j], WIDTH:2 * WIDTH] for j in n]
        gc = [gc_s[sq[j], rs[j], :] for j in n]
        g_last = [gc_s[a, pl.ds(j * c + c - 1, 1), :] for a, j in items]
        kb = [k[j] * beta_s[sq[j], rs[j], :] for j in n]
        kkqk = [_heads_mm_nt(jnp.concatenate([kb[j], q[j]], axis=0).astype(BF16), k[j].astype(BF16))
                for j in n]
        fill()
        gc_row = [jnp.sum(jnp.where(eye, gc[j], 0.0), axis=0, keepdims=True) for j in n]
        decay = [jnp.exp(jnp.where(causal, gc[j] - gc_row[j], -jnp.inf)) for j in n]
        nm = [jnp.where(strict, kkqk[j][0:c] * decay[j], 0.0) for j in n]
        for j in n:
            qkd_s[sq[j], rs[j], :] = (kkqk[j][c:2 * c] * decay[j]).astype(BF16)
        x = [eye.astype(F32) - jnp.where((row >> 1) == (col >> 1), nm[j], 0.0) for j in n]
        for lb in range(1, 6):
            off = ((row >> (lb + 1)) == (col >> (lb + 1))) & ((row >> lb) != (col >> lb))
            t1 = [mm(jnp.where(off, nm[j], 0.0), x[j]) for j in n]
            fill()
            t2 = [mm(x[j], t1[j]) for j in n]
            if lb % 2 == 0:
                fill()
            x = [x[j] - t2[j] for j in n]
        xb = [x[j].astype(BF16) for j in n]
        egc = [jnp.exp(gc[j]) for j in n]
        for j in n:
            u_s[sq[j], rs[j], :] = _heads_mm(
                xb[j], (qkv_ref[sq[j], rs[j], 2 * WIDTH:3 * WIDTH] * beta_s[sq[j], rs[j], :]).astype(BF16))
        for j in n:
            w_s[sq[j], rs[j], :] = _heads_mm(xb[j], (kb[j] * egc[j]).astype(BF16)).astype(BF16)
        for j in n:
            qd_s[sq[j], rs[j], :] = (q[j] * egc[j]).astype(BF16)
            kd_s[sq[j], rs[j], :] = (k[j] * jnp.exp(g_last[j] - gc[j])).astype(BF16)
        while fillers:
            fill()

    chunks = [(a, j) for a in range(nb) for j in range(t // c)]
    groups = [chunks[i:i + group] for i in range(0, len(chunks), group)]
    for a, j in groups[0]:
        gate_rows(a, j * c, (j + 1) * c)
    for gi, items in enumerate(groups):
        nxt = groups[gi + 1] if gi + 1 < len(groups) else []
        prepare(items, [functools.partial(gate_rows, a, j * c, (j + 1) * c) for a, j in nxt])

    def chunk_stages(i):
        n = range(nb)
        r = pl.ds(pl.multiple_of(i * c, c), c)
        pairs = range(WIDTH // PAIR)
        g_last = [gc_s[a, pl.ds(i * c + c - 1, 1), :] for a in n]
        s = [[state_ref[a, p] for p in pairs] for a in n]
        lhs = [_pairs(jnp.concatenate([w_s[a, r, :], qd_s[a, r, :]], axis=0)) for a in n]
        ws = [jnp.concatenate([_dot(lhs[a][p], s[a][p].astype(BF16)) for p in pairs], axis=1)
              for a in n]
        yield
        v_new = [(u_s[a, r, :] - ws[a][0:c]).astype(BF16) for a in n]
        v_diag = [[_pair_diag(vn) for vn in _pairs(v_new[a])] for a in n]
        upd = [[_dot_tn(_pair_diag(kd), v_diag[a][p]) for p, kd in enumerate(_pairs(kd_s[a, r, :]))]
               for a in n]
        yield
        for a in n:
            decay_end = _pairs(jnp.exp(g_last[a]))
            for p in pairs:
                state_ref[a, p] = s[a][p] * decay_end[p] + upd[a][p]
        yield
        o = [ws[a][c:2 * c] + jnp.concatenate(
            [_dot(qp, v_diag[a][p]) for p, qp in enumerate(_pairs(qkd_s[a, r, :]))], axis=1) for a in n]
        yield
        ms = [_heads_sum((o[a] * o[a]).astype(BF16), pair_ones) * (1.0 / HEAD_DIM) for a in n]
        yield
        for a in n:
            z = z_ref[a, r, :]
            o_ref[a, r, :] = o[a] * lax.rsqrt(ms[a] + EPS) * nw * (z * _sigmoid(z))

    state_stages = 3
    per_trip = 4 if (t // c) % 4 == 0 else 2

    def chunk_group(ii, carry):
        _emit_pipelined([chunk_stages(per_trip * ii + j) for j in range(per_trip)], state_stages)
        return carry

    assert (t // c) % per_trip == 0
    lax.fori_loop(0, t // c // per_trip, chunk_group, 0)


def _gdn(gqkv, gz, gates, par, norm_w, layer, bsz, seq, t, nb):
    row = lambda n: pl.BlockSpec((nb, t, n), lambda b, j: (b, j, 0))
    lay = lambda a, b2: pl.BlockSpec((None, a, b2), lambda b, j: (layer, 0, 0))
    const = lambda shape: pl.BlockSpec(shape, lambda b, j: (0,) * len(shape))
    seq3 = lambda a: a.reshape(bsz, seq, a.shape[-1])
    f32_buf = pltpu.VMEM((nb, t, WIDTH), F32)
    bf16_buf = pltpu.VMEM((nb, t, WIDTH), BF16)
    out = pl.pallas_call(
        functools.partial(_gdn_kernel, t=t, group=8, nb=nb),
        grid=(bsz // nb, seq // t),
        in_specs=[row(3 * WIDTH), row(WIDTH), row(GATE_LANES),
                  lay(2, GATE_LANES), lay(1, WIDTH),
                  const((2, GATE_LANES, WIDTH)), const((PAIR, PAIR))],
        out_specs=row(WIDTH),
        out_shape=jax.ShapeDtypeStruct((bsz, seq, WIDTH), F32),
        scratch_shapes=[
            pltpu.VMEM((nb, WIDTH // PAIR, PAIR, PAIR), F32),
            f32_buf, f32_buf,
            f32_buf,
            bf16_buf, bf16_buf, bf16_buf, bf16_buf,
        ],
        compiler_params=pltpu.CompilerParams(
            dimension_semantics=("arbitrary", "arbitrary"), vmem_limit_bytes=VMEM_LIMIT),
    )(seq3(gqkv), seq3(gz), seq3(gates), par, norm_w,
      jnp.stack([_expander(0), _expander(HEADS)]), _pair_mask().astype(BF16))
    return out.reshape(bsz * seq, WIDTH)


def _mlstm_kernel(m_ref, gt_ref, par_ref, nw_ref, exp_ref, ones_ref, o_ref,
                  cn_ref, mx_ref, ig_s, b_s, cm_s, *, t, nb):
    c = CHUNK

    @pl.when(pl.program_id(1) == 0)
    def _():
        cn_ref[...] = jnp.zeros_like(cn_ref)
        mx_ref[...] = jnp.zeros_like(mx_ref)

    pair_ones = ones_ref[...]

    def front(a, carry):
        pre = gt_ref[a] + par_ref[...]
        capped = GATE_SOFTCAP * jnp.tanh(pre * (1.0 / GATE_SOFTCAP))
        log_f = -_softplus(-capped)
        b_all = _chunk_cumsum(log_f)
        ig = _dot_sel(capped, exp_ref[0])
        b = _dot_sel(b_all, exp_ref[1])
        pos = _iota2((t, WIDTH), 0) & (c - 1)
        cm = ig - b
        for sh in (1, 2, 4, 8, 16, 32):
            cm = jnp.where(pos >= sh, jnp.maximum(cm, pltpu.roll(cm, sh, axis=0)), cm)
        ig_s[a] = ig
        b_s[a] = b
        cm_s[a] = cm
        return carry

    lax.fori_loop(0, nb, front, 0)

    row = _iota2((c, WIDTH), 0)
    col = _iota2((c, WIDTH), 1) & (c - 1)
    causal = col <= row
    eye = col == row
    nw = nw_ref[...]

    def chunk_stages(i):
        n = range(nb)
        pairs = range(WIDTH // PAIR)
        r = pl.ds(pl.multiple_of(i * c, c), c)
        last = pl.ds(i * c + c - 1, 1)
        qb = [m_ref[a, r, 0:WIDTH].astype(BF16) for a in n]
        k = [m_ref[a, r, WIDTH:2 * WIDTH] for a in n]
        vb = [m_ref[a, r, 2 * WIDTH:3 * WIDTH].astype(BF16) for a in n]
        ig = [ig_s[a, r, :] for a in n]
        b = [b_s[a, r, :] for a in n]
        m_intra = [b[a] + cm_s[a, r, :] for a in n]
        b_last = [b_s[a, last, :] for a in n]
        m_chunk = [b_last[a] + cm_s[a, last, :] for a in n]
        qk = [_heads_mm_nt(qb[a], k[a].astype(BF16)) for a in n]
        yield
        gate_row = [jnp.sum(jnp.where(eye, ig[a] - b[a], 0.0), axis=0, keepdims=True) for a in n]
        qk = [qk[a] * jnp.exp(jnp.where(causal, b[a] + gate_row[a], -jnp.inf) - m_intra[a]) for a in n]
        yield
        v_ones = [[jnp.concatenate([_pair_diag(vp), pair_ones], axis=1) for vp in _pairs(vb[a])]
                  for a in n]
        qk_parts = [_split2(qk[a]) for a in n]
        intra = [[_dot(qp, v_ones[a][p]) for p, qp in enumerate(_pairs(qk_parts[a][0]))] for a in n]
        yield
        num_intra = [jnp.concatenate([intra[a][p][:, 0:PAIR] for p in pairs], axis=1) for a in n]
        den_intra = [jnp.concatenate([intra[a][p][:, PAIR:2 * PAIR] for p in pairs], axis=1)
                     + _heads_sum(qk_parts[a][1], pair_ones) for a in n]
        ke = [(k[a] * jnp.exp(b_last[a] - b[a] + ig[a] - m_chunk[a])).astype(BF16) for a in n]
        own = [[_dot_tn(_pair_diag(kp), v_ones[a][p]) for p, kp in enumerate(_pairs(ke[a]))]
               for a in n]
        yield
        m_prev = [mx_ref[a] for a in n]
        cn = [[cn_ref[a, p] for p in pairs] for a in n]
        inter = [[_dot(qp, cn[a][p].astype(BF16)) for p, qp in enumerate(_pairs(qb[a]))] for a in n]
        yield
        for a in n:
            m_new = jnp.maximum(b_last[a] + m_prev[a], m_chunk[a])
            s_old = _pairs(jnp.exp(b_last[a] + m_prev[a] - m_new))
            s_new = _pairs(jnp.exp(m_chunk[a] - m_new))
            for p in pairs:
                cn_ref[a, p] = (jnp.concatenate([s_old[p], s_old[p]], axis=1) * cn[a][p]
                                + jnp.concatenate([s_new[p], s_new[p]], axis=1) * own[a][p])
            mx_ref[a] = m_new
        yield
        h = []
        for a in n:
            pre_m = b[a] + m_prev[a]
            m_t = jnp.maximum(pre_m, m_intra[a])
            s_inter = jnp.exp(pre_m - m_t)
            s_intra = jnp.exp(m_intra[a] - m_t)
            num_inter = jnp.concatenate([inter[a][p][:, 0:PAIR] for p in pairs], axis=1)
            den_inter = jnp.concatenate([inter[a][p][:, PAIR:2 * PAIR] for p in pairs], axis=1)
            num = s_inter * num_inter + s_intra * num_intra[a]
            den = s_inter * den_inter + s_intra * den_intra[a]
            h.append(num / jnp.maximum(jnp.abs(den), jnp.exp(-m_t)))
        yield
        ms = [_heads_sum((h[a] * h[a]).astype(BF16), pair_ones) * (1.0 / HEAD_DIM) for a in n]
        yield
        for a in n:
            o_ref[a, r, :] = (h[a] * lax.rsqrt(ms[a] + EPS) * nw
                              * _sigmoid(m_ref[a, r, 3 * WIDTH:4 * WIDTH]))

    state_free = 4
    per_trip = 4 if (t // c) % 4 == 0 else 2

    def chunk_group(ii, carry):
        _emit_pipelined([chunk_stages(per_trip * ii + j) for j in range(per_trip)], state_free)
        return carry

    assert (t // c) % per_trip == 0
    lax.fori_loop(0, t // c // per_trip, chunk_group, 0)


def _mlstm(mqkvo, gates, par, norm_w, layer, bsz, seq, t, nb):
    row = lambda n: pl.BlockSpec((nb, t, n), lambda b, j: (b, j, 0))
    lay = lambda a, b2: pl.BlockSpec((None, a, b2), lambda b, j: (layer, 0, 0))
    const = lambda shape: pl.BlockSpec(shape, lambda b, j: (0,) * len(shape))
    seq3 = lambda a: a.reshape(bsz, seq, a.shape[-1])
    f32_buf = pltpu.VMEM((nb, t, WIDTH), F32)
    out = pl.pallas_call(
        functools.partial(_mlstm_kernel, t=t, nb=nb),
        grid=(bsz // nb, seq // t),
        in_specs=[row(4 * WIDTH), row(GATE_LANES), lay(1, GATE_LANES), lay(1, WIDTH),
                  const((2, GATE_LANES, WIDTH)), const((PAIR, PAIR))],
        out_specs=row(WIDTH),
        out_shape=jax.ShapeDtypeStruct((bsz, seq, WIDTH), F32),
        scratch_shapes=[
            pltpu.VMEM((nb, WIDTH // PAIR, PAIR, 2 * PAIR), F32),
            pltpu.VMEM((nb, 1, WIDTH), F32),
            f32_buf, f32_buf, f32_buf,
        ],
        compiler_params=pltpu.CompilerParams(
            dimension_semantics=("arbitrary", "arbitrary"), vmem_limit_bytes=VMEM_LIMIT),
    )(seq3(mqkvo), seq3(gates), par, norm_w,
      jnp.stack([_expander(2 * HEADS), _expander(3 * HEADS)]),
      _pair_mask().astype(BF16))
    return out.reshape(bsz * seq, WIDTH)


def _swa_stages(q_ref, kvc_ref, kvp_ref, sink_ref, o_ref, first, tq):
    tb = SWA_BLOCK
    kvw = SWA_KV_WIDTH
    nq = tq // tb
    chains = [(i, g) for i in range(nq) for g in range(2)]
    st = {}

    def setup():
        kcat = jnp.concatenate([kvp_ref[:, 0:kvw], kvc_ref[:, 0:kvw]], axis=0)
        vcat = jnp.concatenate([kvp_ref[:, kvw:2 * kvw], kvc_ref[:, kvw:2 * kvw]], axis=0)
        st["v_t"] = vcat.T.astype(BF16)
        kr = pltpu.roll(kcat, HEAD_DIM, axis=1)
        lo = _iota2((tb + tq, kvw), 1) < HEAD_DIM

        def place(x_lo, x_hi):
            return jnp.where(lo, x_lo, 0.0).astype(BF16), jnp.where(lo, 0.0, x_hi).astype(BF16)

        st["k"] = (place(kcat, kr), place(kr, kcat))
        st["ki"] = _iota2((2 * tb, tb), 0)
        qi = _iota2((2 * tb, tb), 1)
        st["in_window"] = (st["ki"] > qi) & (st["ki"] <= qi + tb)

    def scores(ci):
        i, g = chains[ci]
        keys = jnp.concatenate([st["k"][g][0][i * tb:(i + 2) * tb],
                                st["k"][g][1][i * tb:(i + 2) * tb]], axis=0)
        qs = jnp.concatenate([q_ref[i * tb:(i + 1) * tb, 256 * g:256 * g + 128],
                              q_ref[i * tb:(i + 1) * tb, 256 * g + 128:256 * g + 256]], axis=0)
        st["s", ci] = _dot_nt(keys, qs.astype(BF16))

    def probs(ci):
        i, g = chains[ci]
        sc = st.pop(("s", ci))
        mask = st["in_window"]
        if i == 0:
            mask = mask & (st["ki"] >= jnp.where(first, tb, 0))
        ps, invs = [], []
        for part in range(2):
            for e in range(2):
                sink = sink_ref[2 * g + part:2 * g + part + 1, e * 2 * tb:e * 2 * tb + 1]
                se = jnp.where(mask, sc[e * 2 * tb:(e + 1) * 2 * tb, part * tb:(part + 1) * tb],
                               -jnp.inf)
                mx = jnp.maximum(jnp.max(se, axis=0, keepdims=True), sink)
                p = jnp.exp(se - mx)
                ps.append(p.astype(BF16))
                invs.append(1.0 / (jnp.sum(p, axis=0, keepdims=True) + jnp.exp(sink - mx)))
        st["p", ci] = jnp.concatenate(ps, axis=-1)
        st["inv", ci] = jnp.concatenate(invs, axis=-1)

    def attend(ci):
        i, g = chains[ci]
        o = _dot(st["v_t"][g * HEAD_DIM:(g + 1) * HEAD_DIM, i * tb:(i + 2) * tb],
                 st.pop(("p", ci))) * st.pop(("inv", ci))
        for part in range(2):
            j = 2 * g + part
            pair_t = jnp.concatenate([o[:, 2 * part * tb:(2 * part + 1) * tb],
                                      o[:, (2 * part + 1) * tb:(2 * part + 2) * tb]], axis=0)
            o_ref[i * tb:(i + 1) * tb, 128 * j:128 * (j + 1)] = pair_t.T

    n = len(chains)
    stages = [setup]
    for ci in range(n + 2):
        stages += [functools.partial(f, cj) for f, cj in ((scores, ci), (probs, ci - 1), (attend, ci - 2))
                   if 0 <= cj < n]
    return stages


def _post_kernel(x_ref, ya_ref, yb_ref, p_ref, wo_ref, nmlp_ref, wup_ref, wdn_ref,
                 nple_ref, wg_ref, wp_ref, nfin_ref,
                 sq0_ref, skv0_ref, sqn_ref, skvn_ref, kvpn_ref, sink_ref,
                 o_ref, yc_ref, *, final, tf, tiles_per_seq):
    t = x_ref.shape[0]
    step = pl.program_id(0)

    @pl.when(step == 0)
    def _():
        for stage in _swa_stages(sq0_ref, skv0_ref, kvpn_ref, sink_ref, yc_ref, True, t):
            stage()

    x = x_ref[...]
    x = x + _dot(ya_ref[...].astype(BF16), wo_ref[0:WIDTH, :])
    x = x + _dot(yb_ref[...].astype(BF16), wo_ref[WIDTH:2 * WIDTH, :])
    x = x + _dot(yc_ref[...].astype(BF16), wo_ref[2 * WIDTH:, :])
    stages = _swa_stages(sqn_ref, skvn_ref, kvpn_ref, sink_ref, yc_ref,
                         (step + 1) % tiles_per_seq == 0, t)
    slots = 2 * (D_FF // tf)
    per_slot = -(-len(stages) // slots)

    def fill():
        for _ in range(per_slot):
            if stages:
                stages.pop(0)()

    r = _rms_scale(x)
    h = (x * nmlp_ref[...]).astype(BF16)
    acc = jnp.zeros_like(x)
    for f in range(0, D_FF, tf):
        u = jnp.maximum(_dot(h, wup_ref[:, f:f + tf]), 0.0)
        fill()
        acc = acc + _dot((u * u).astype(BF16), wdn_ref[f:f + tf, :])
        fill()
    x = x + (r * r) * acc
    gate = _sigmoid(_rms_scale(x) * _dot((x * nple_ref[...]).astype(BF16), wg_ref[...]))
    x = x + gate * _dot(p_ref[...].astype(BF16), wp_ref[...])
    if final:
        x = _rms(x, nfin_ref[...])
    o_ref[...] = x


def _post(x, ya, yb, sq, skv, sinks_e, p, w_out, norm_mlp, w_up, w_down, norm_ple, w_gate, w_proj,
          norm_final, layer, tm, seq, final):
    m = x.shape[0]
    nt = m // tm
    blocks_per_tile = tm // SWA_BLOCK
    row = lambda n: pl.BlockSpec((tm, n), lambda i: (i, 0))
    first = lambda n: pl.BlockSpec((tm, n), lambda i: (0, 0))
    nxt = lambda n: pl.BlockSpec((tm, n), lambda i: (jnp.minimum(i + 1, nt - 1), 0))
    lay = lambda a, b: pl.BlockSpec((None, a, b), lambda i: (layer, 0, 0),
                                    pipeline_mode=pl.Buffered(1))
    return pl.pallas_call(
        functools.partial(_post_kernel, final=final, tf=512, tiles_per_seq=seq // tm),
        grid=(nt,),
        in_specs=[
            row(D_MODEL), row(WIDTH), row(WIDTH),
            pl.BlockSpec((None, tm, PLE_DIM), lambda i: (layer, i, 0)),
            lay(D_MODEL, D_MODEL), lay(1, D_MODEL), lay(D_MODEL, D_FF), lay(D_FF, D_MODEL),
            lay(1, D_MODEL), lay(D_MODEL, D_MODEL), lay(PLE_DIM, D_MODEL),
            pl.BlockSpec((1, D_MODEL), lambda i: (0, 0)),
            first(SWA_WIDTH), first(2 * SWA_KV_WIDTH), nxt(SWA_WIDTH), nxt(2 * SWA_KV_WIDTH),
            pl.BlockSpec((SWA_BLOCK, 2 * SWA_KV_WIDTH),
                         lambda i: ((i + 1) * blocks_per_tile - 1, 0)),
            pl.BlockSpec((None, SWA_WIDTH // 128, 4 * SWA_BLOCK), lambda i: (layer, 0, 0)),
        ],
        out_specs=row(D_MODEL),
        out_shape=jax.ShapeDtypeStruct((m, D_MODEL), F32),
        scratch_shapes=[pltpu.VMEM((tm, SWA_WIDTH), F32)],
        compiler_params=pltpu.CompilerParams(
            dimension_semantics=("arbitrary",), vmem_limit_bytes=VMEM_LIMIT),
    )(x, ya, yb, p, w_out, norm_mlp, w_up, w_down, norm_ple, w_gate, w_proj, norm_final,
      sq, skv, sq, skv, skv, sinks_e)


def _gate_row(depth, pieces):
    out = jnp.zeros((depth, 1, GATE_LANES), F32)
    for first_col, vals in pieces:
        out = out.at[:, 0, first_col:first_col + HEADS].set(vals.astype(F32))
    return out


def _rope_tables(positions):
    half = ROPE_DIM // 2
    inv_freq = ROPE_THETA ** (-jnp.arange(0, ROPE_DIM, 2, dtype=F32) / ROPE_DIM)
    dim = jnp.arange(128) % HEAD_DIM
    ang = positions.astype(F32).reshape(-1)[:, None] * inv_freq[dim % half][None, :]
    cos_t = jnp.where(dim < ROPE_DIM, jnp.cos(ang), 1.0)
    sin_t = jnp.where(dim < half, -jnp.sin(ang), jnp.where(dim < ROPE_DIM, jnp.sin(ang), 0.0))
    return cos_t, sin_t


def kernel(x, p, positions, w_in, conv_w, gdn_a_log, gdn_dt_bias, gdn_norm, mlstm_i_bias,
           mlstm_f_bias, mlstm_norm, attn_sinks, w_out, norm_mix, norm_mlp, w_up, w_down,
           norm_ple, w_ple_gate, w_ple_proj, norm_final):
    bsz, seq, d = x.shape
    depth = w_in.shape[0]
    m = bsz * seq
    tm = min(512, seq)
    t_mix = min(512, seq)
    nb_mix = 4 if bsz % 4 == 0 else 1
    nb_gdn = 8 if bsz % 8 == 0 else nb_mix
    t_gdn = min(2048 // nb_gdn, seq)

    cos_t, sin_t = _rope_tables(positions)
    gdn_par = jnp.concatenate([_gate_row(depth, [(HEADS, gdn_a_log)]),
                               _gate_row(depth, [(HEADS, gdn_dt_bias)])], axis=1)
    mlstm_par = _gate_row(depth, [(2 * HEADS, mlstm_i_bias), (3 * HEADS, mlstm_f_bias)])
    gdn_nw = jnp.tile(gdn_norm.astype(F32), (1, HEADS))[:, None, :]
    mlstm_nw = mlstm_norm.astype(F32)[:, None, :]
    sinks_e = jnp.repeat(attn_sinks.astype(F32), 2 * SWA_BLOCK, axis=-1).reshape(
        depth, SWA_WIDTH // 128, 4 * SWA_BLOCK)
    row3 = lambda a: a.astype(F32)[:, None, :]
    wo_b, wup_b, wdn_b = w_out.astype(BF16), w_up.astype(BF16), w_down.astype(BF16)
    wg_b, wp_b = w_ple_gate.astype(BF16), w_ple_proj.astype(BF16)
    nmix, nmlp, nple = row3(norm_mix), row3(norm_mlp), row3(norm_ple)
    nfin = norm_final.astype(F32)[None, :]
    p2 = p.reshape(depth, m, PLE_DIM)

    xf = x.reshape(m, d)
    for i in range(depth):
        gqkv, gz, mqkvo, sq, skv, gates = _proj(xf, nmix, w_in.astype(F32), cos_t, sin_t,
                                                conv_w.astype(F32), i, tm, seq)
        ya = _gdn(gqkv, gz, gates, gdn_par, gdn_nw, i, bsz, seq, t_gdn, nb_gdn)
        yb = _mlstm(mqkvo, gates, mlstm_par, mlstm_nw, i, bsz, seq, t_mix, nb_mix)
        xf = _post(xf, ya, yb, sq, skv, sinks_e, p2, wo_b, nmlp, wup_b, wdn_b, nple, wg_b, wp_b, nfin,
                   i, tm, seq, final=(i == depth - 1))
    return xf.reshape(bsz, seq, d)
```

```python
import functools

import jax
import jax.numpy as jnp
from jax import lax
from jax.experimental import pallas as pl
from jax.experimental.pallas import tpu as pltpu

F32 = jnp.float32
BF16 = jnp.bfloat16

D_MODEL = 1024
DEPTH = 4
HEAD_DIM = 64
PLE_DIM = 256
D_FF = 4 * D_MODEL
EPS = 1e-6
HEADS = 4
WIDTH = HEADS * HEAD_DIM
PAIR = 2 * HEAD_DIM
CHUNK = 64
CONV_WIDTH = 4
GATE_SOFTCAP = 15.0
SWA_WIDTH = 512
SWA_KV_WIDTH = 128
SWA_BLOCK = 128
ROPE_DIM = 16
ROPE_THETA = 500000.0
GATE_LANES = 128
IN_COLS_PACKED = 2944
OFF_GQKV, OFF_GZ, OFF_M, OFF_SQ, OFF_SKV, OFF_GATES = 0, 768, 1024, 2048, 2560, 2816
IN_A = 4 * WIDTH
IN_M0 = IN_A + 2 * HEADS
IN_G2 = IN_M0 + 4 * WIDTH
IN_S0 = IN_G2 + 2 * HEADS
IN_S1 = IN_S0 + SWA_WIDTH + 2 * SWA_KV_WIDTH

VMEM_LIMIT = 56 * 1024 * 1024


def _dot(a, b):
    return jnp.dot(a, b, preferred_element_type=F32)


def _dot_nt(a, b):
    return lax.dot_general(a, b, (((1,), (1,)), ((), ())), preferred_element_type=F32)


def _dot_tn(a, b):
    return lax.dot_general(a, b, (((0,), (0,)), ((), ())), preferred_element_type=F32)


def _split2(x):
    hi = x.astype(BF16)
    lo = (x - hi.astype(F32)).astype(BF16)
    return hi, lo


def _dot_sel(x, sel):
    hi, lo = _split2(x)
    return _dot(hi, sel) + _dot(lo, sel)


def _chunk_cumsum(x):
    pos = _iota2(x.shape, 0) & (CHUNK - 1)
    sh = 1
    while sh < CHUNK:
        x = x + jnp.where(pos >= sh, pltpu.roll(x, sh, axis=0), 0.0)
        sh *= 2
    return x


def _sigmoid(x):
    return 1.0 / (1.0 + jnp.exp(-x))


def _softplus(x):
    return jnp.maximum(x, 0.0) + jnp.log1p(jnp.exp(-jnp.abs(x)))


def _rms(x, g):
    return x * lax.rsqrt(jnp.mean(x * x, axis=-1, keepdims=True) + EPS) * g


def _rms_scale(x):
    return lax.rsqrt(jnp.mean(x * x, axis=-1, keepdims=True) + EPS)


def _iota2(shape, dim):
    return lax.broadcasted_iota(jnp.int32, shape, dim)


def _pairs(x):
    return [x[:, p * PAIR:(p + 1) * PAIR] for p in range(WIDTH // PAIR)]


def _pair_mask():
    return (_iota2((PAIR, PAIR), 0) >> 6) == (_iota2((PAIR, PAIR), 1) >> 6)


def _pair_diag(x):
    low = _iota2(x.shape, 1) < HEAD_DIM
    zero = jnp.zeros((), x.dtype)
    return jnp.concatenate([jnp.where(low, x, zero), jnp.where(low, zero, x)], axis=0)


def _heads_mm(a, b):
    return jnp.concatenate([_dot(ap, _pair_diag(bp)) for ap, bp in zip(_pairs(a), _pairs(b))], axis=1)


def _heads_mm_nt(a, b):
    return jnp.concatenate([_dot_nt(ap, _pair_diag(bp)) for ap, bp in zip(_pairs(a), _pairs(b))],
                           axis=1)


def _heads_sum(x, pair_ones):
    return jnp.concatenate([_dot(xp, pair_ones) for xp in _pairs(x)], axis=1)


def _emit_pipelined(stage_gens, lead):
    pending, live, step = list(stage_gens), [], 0
    while pending or live:
        if pending and step % lead == 0:
            live.append(pending.pop(0))
        for g in list(live):
            if next(g, StopIteration) is StopIteration:
                live.remove(g)
        step += 1


def _expander(first_col):
    r = _iota2((GATE_LANES, WIDTH), 0)
    c = _iota2((GATE_LANES, WIDTH), 1)
    return (r == first_col + (c >> 6)).astype(BF16)


def _rope(x, cos_t, sin_t, reps):
    n = x.shape[-1]
    if reps > 1:
        cos_t = jnp.concatenate([cos_t] * reps, axis=-1)
        sin_t = jnp.concatenate([sin_t] * reps, axis=-1)
    half = ROPE_DIM // 2
    x_up = pltpu.roll(x, n - half, axis=1)
    x_dn = pltpu.roll(x, half, axis=1)
    first_half = (_iota2(x.shape, 1) & (HEAD_DIM - 1)) < half
    return x * cos_t + jnp.where(first_half, x_up, x_dn) * sin_t


def _repack_w_in(w_in_ref, w_ref):
    rows = 128
    scale = HEAD_DIM ** -0.5
    lane_m = _iota2((1, 4 * WIDTH), 1)
    scale_m = jnp.where((lane_m >= WIDTH) & (lane_m < 2 * WIDTH), scale, 1.0)
    scale_s = jnp.where(_iota2((1, SWA_WIDTH + 2 * SWA_KV_WIDTH), 1) < SWA_WIDTH, scale, 1.0)

    def body(i, carry):
        r = pl.ds(pl.multiple_of(i * rows, rows), rows)
        w_ref[r, OFF_GQKV:OFF_M] = w_in_ref[r, 0:IN_A].astype(BF16)
        w_ref[r, OFF_M:OFF_SQ] = (w_in_ref[r, IN_M0:IN_G2] * scale_m).astype(BF16)
        w_ref[r, OFF_SQ:OFF_GATES] = (w_in_ref[r, IN_S0:IN_S1] * scale_s).astype(BF16)
        lane = _iota2((rows, GATE_LANES), 1)
        g2_tile = IN_G2 - 2 * HEADS
        gates = jnp.where(lane < 2 * HEADS, w_in_ref[r, IN_A:IN_A + GATE_LANES],
                          jnp.where(lane < 4 * HEADS, w_in_ref[r, g2_tile:g2_tile + GATE_LANES], 0.0))
        w_ref[r, OFF_GATES:IN_COLS_PACKED] = gates.astype(BF16)
        return carry

    lax.fori_loop(0, D_MODEL // rows, body, 0)


def _proj_kernel(x_ref, g_ref, w_in_ref, cos_ref, sin_ref, cw_ref, ones_ref,
                 gqkv_ref, gz_ref, m_ref, sq_ref, skv_ref, gates_ref, tail_ref, buf_ref, w_ref,
                 *, tiles_per_seq):
    t = x_ref.shape[0]

    @pl.when(pl.program_id(0) == 0)
    def _():
        _repack_w_in(w_in_ref, w_ref)

    @pl.when(pl.program_id(0) % tiles_per_seq == 0)
    def _():
        tail_ref[...] = jnp.zeros_like(tail_ref)

    x = x_ref[...]
    hb = (x * g_ref[...]).astype(BF16)
    scale = _rms_scale(x)

    def proj(lo, hi):
        return scale * _dot(hb, w_ref[:, lo:hi])

    a = proj(OFF_GQKV, OFF_M)
    buf_ref[0:8, :] = tail_ref[...]
    buf_ref[8:8 + t, :] = a[:, 0:OFF_GZ]
    tail_ref[...] = buf_ref[t:t + 8, :]
    gz_ref[...] = a[:, OFF_GZ:OFF_M]
    cw = cw_ref[...]
    pair_ones = ones_ref[...]

    def conv_silu(lo, hi):
        acc = buf_ref[5:5 + t, lo:hi] * cw[0:1, lo:hi]
        for j in range(1, CONV_WIDTH):
            acc = acc + buf_ref[5 + j:5 + j + t, lo:hi] * cw[j:j + 1, lo:hi]
        return acc * _sigmoid(acc)

    def l2n(v):
        return v * lax.rsqrt(_heads_sum((v * v).astype(BF16), pair_ones) + EPS)

    cos_t, sin_t = cos_ref[...], sin_ref[...]
    s = proj(OFF_SQ, IN_COLS_PACKED)
    q = conv_silu(0, WIDTH)
    gqkv_ref[:, 0:WIDTH] = l2n(q) * (HEAD_DIM ** -0.5)
    m_ref[:, 0:2 * WIDTH] = proj(OFF_M, OFF_M + 2 * WIDTH)
    sq_ref[...] = _rope(s[:, 0:SWA_WIDTH], cos_t, sin_t, SWA_WIDTH // 128)
    skv_ref[:, 0:SWA_KV_WIDTH] = _rope(s[:, SWA_WIDTH:SWA_WIDTH + SWA_KV_WIDTH], cos_t, sin_t, 1)
    skv_ref[:, SWA_KV_WIDTH:] = s[:, SWA_WIDTH + SWA_KV_WIDTH:SWA_WIDTH + 2 * SWA_KV_WIDTH]
    gates_ref[...] = s[:, SWA_WIDTH + 2 * SWA_KV_WIDTH:]
    k = conv_silu(WIDTH, 2 * WIDTH)
    gqkv_ref[:, WIDTH:2 * WIDTH] = l2n(k)
    gqkv_ref[:, 2 * WIDTH:3 * WIDTH] = conv_silu(2 * WIDTH, 3 * WIDTH)
    m_ref[:, 2 * WIDTH:4 * WIDTH] = proj(OFF_M + 2 * WIDTH, OFF_SQ)


def _proj(x, norm_w, w_in, cos_t, sin_t, conv_w, layer, tm, seq):
    m = x.shape[0]
    row = lambda n: pl.BlockSpec((tm, n), lambda i: (i, 0))
    widths = (768, 256, 1024, SWA_WIDTH, 2 * SWA_KV_WIDTH, GATE_LANES)
    return pl.pallas_call(
        functools.partial(_proj_kernel, tiles_per_seq=seq // tm),
        grid=(m // tm,),
        in_specs=[
            row(D_MODEL),
            pl.BlockSpec((None, 1, D_MODEL), lambda i: (layer, 0, 0)),
            pl.BlockSpec((None, D_MODEL, w_in.shape[-1]), lambda i: (layer, 0, 0),
                         pipeline_mode=pl.Buffered(1)),
            row(128), row(128),
            pl.BlockSpec((None, CONV_WIDTH, 3 * WIDTH), lambda i: (layer, 0, 0)),
            pl.BlockSpec((PAIR, PAIR), lambda i: (0, 0)),
        ],
        out_specs=[row(n) for n in widths],
        out_shape=[jax.ShapeDtypeStruct((m, n), F32) for n in widths],
        scratch_shapes=[
            pltpu.VMEM((8, 3 * WIDTH), F32),
            pltpu.VMEM((tm + 8, 3 * WIDTH), F32),
            pltpu.VMEM((D_MODEL, IN_COLS_PACKED), BF16),
        ],
        compiler_params=pltpu.CompilerParams(
            dimension_semantics=("arbitrary",), vmem_limit_bytes=VMEM_LIMIT),
    )(x, norm_w, w_in, cos_t, sin_t, conv_w, _pair_mask().astype(BF16))


def _gdn_kernel(qkv_ref, z_ref, gt_ref, par_ref, nw_ref, exp_ref, ones_ref, o_ref,
                state_ref, beta_s, gc_s, u_s, w_s, qd_s, kd_s, qkd_s, *, t, group, nb):
    c = CHUNK

    @pl.when(pl.program_id(1) == 0)
    def _():
        state_ref[...] = jnp.zeros_like(state_ref)

    pair_ones = ones_ref[...]

    def gate_rows(a, lo, hi):
        gates = gt_ref[a, lo:hi, :]
        par = par_ref[...]
        beta = _sigmoid(gates)
        g = -jnp.exp(par[0:1, :]) * _softplus(gates + par[1:2, :])
        beta_s[a, lo:hi, :] = _dot_sel(beta, exp_ref[0])
        gc_s[a, lo:hi, :] = _dot_sel(_chunk_cumsum(g), exp_ref[1])

    row = _iota2((c, WIDTH), 0)
    col = _iota2((c, WIDTH), 1) & (c - 1)
    causal = col <= row
    strict = col < row
    eye = col == row
    nw = nw_ref[...]

    def mm(a, b):
        return _heads_mm(a.astype(BF16), b.astype(BF16))

    def prepare(items, fillers):
        fillers = list(fillers)

        def fill():
            if fillers:
                fillers.pop(0)()

        n = range(len(items))
        sq = [a for a, _ in items]
        rs = [pl.ds(j * c, c) for _, j in items]
        q = [qkv_ref[sq[j], rs[j], 0:WIDTH] for j in n]
        k = [qkv_ref[sq[j], rs[j], WIDTH:2 * WIDTH] for j in n]
        gc = [gc_s[sq[j], rs[j], :] for j in n]
        g_last = [gc_s[a, pl.ds(j * c + c - 1, 1), :] for a, j in items]
        kb = [k[j] * beta_s[sq[j], rs[j], :] for j in n]
        kkqk = [_heads_mm_nt(jnp.concatenate([kb[j], q[j]], axis=0).astype(BF16), k[j].astype(BF16))
                for j in n]
        fill()
        gc_row = [jnp.sum(jnp.where(eye, gc[j], 0.0), axis=0, keepdims=True) for j in n]
        decay = [jnp.exp(jnp.where(causal, gc[j] - gc_row[j], -jnp.inf)) for j in n]
        nm = [jnp.where(strict, kkqk[j][0:c] * decay[j], 0.0) for j in n]
        for j in n:
            qkd_s[sq[j], rs[j], :] = (kkqk[j][c:2 * c] * decay[j]).astype(BF16)
        x = [eye.astype(F32) - jnp.where((row >> 1) == (col >> 1), nm[j], 0.0) for j in n]
        for lb in range(1, 6):
            off = ((row >> (lb + 1)) == (col >> (lb + 1))) & ((row >> lb) != (col >> lb))
            t1 = [mm(jnp.where(off, nm[j], 0.0), x[j]) for j in n]
            fill()
            t2 = [mm(x[j], t1[j]) for j in n]
            if lb % 2 == 0:
                fill()
            x = [x[j] - t2[j] for j in n]
        xb = [x[j].astype(BF16) for j in n]
        egc = [jnp.exp(gc[j]) for j in n]
        vbeta = [(qkv_ref[sq[j], rs[j], 2 * WIDTH:3 * WIDTH] * beta_s[sq[j], rs[j], :]).astype(BF16)
                 for j in n]
        kbd = [(kb[j] * egc[j]).astype(BF16) for j in n]
        uw = [[_dot(xp, jnp.concatenate([_pair_diag(vp), _pair_diag(kp)], axis=1))
               for xp, vp, kp in zip(_pairs(xb[j]), _pairs(vbeta[j]), _pairs(kbd[j]))] for j in n]
        for j in n:
            u_s[sq[j], rs[j], :] = jnp.concatenate([p[:, 0:PAIR] for p in uw[j]], axis=1)
            w_s[sq[j], rs[j], :] = jnp.concatenate([p[:, PAIR:2 * PAIR] for p in uw[j]], axis=1).astype(BF16)
        for j in n:
            qd_s[sq[j], rs[j], :] = (q[j] * egc[j]).astype(BF16)
            kd_s[sq[j], rs[j], :] = (k[j] * jnp.exp(g_last[j] - gc[j])).astype(BF16)
        while fillers:
            fill()

    chunks = [(a, j) for a in range(nb) for j in range(t // c)]
    groups = [chunks[i:i + group] for i in range(0, len(chunks), group)]
    for a, j in groups[0]:
        gate_rows(a, j * c, (j + 1) * c)
    for gi, items in enumerate(groups):
        nxt = groups[gi + 1] if gi + 1 < len(groups) else []
        prepare(items, [functools.partial(gate_rows, a, j * c, (j + 1) * c) for a, j in nxt])

    def chunk_stages(i):
        n = range(nb)
        r = pl.ds(pl.multiple_of(i * c, c), c)
        pairs = range(WIDTH // PAIR)
        g_last = [gc_s[a, pl.ds(i * c + c - 1, 1), :] for a in n]
        s = [[state_ref[a, p] for p in pairs] for a in n]
        lhs = [_pairs(jnp.concatenate([w_s[a, r, :], qd_s[a, r, :]], axis=0)) for a in n]
        ws = [jnp.concatenate([_dot(lhs[a][p], s[a][p].astype(BF16)) for p in pairs], axis=1)
              for a in n]
        yield
        v_new = [(u_s[a, r, :] - ws[a][0:c]).astype(BF16) for a in n]
        v_diag = [[_pair_diag(vn) for vn in _pairs(v_new[a])] for a in n]
        upd = [[_dot_tn(_pair_diag(kd), v_diag[a][p]) for p, kd in enumerate(_pairs(kd_s[a, r, :]))]
               for a in n]
        yield
        for a in n:
            decay_end = _pairs(jnp.exp(g_last[a]))
            for p in pairs:
                state_ref[a, p] = s[a][p] * decay_end[p] + upd[a][p]
        yield
        o = [ws[a][c:2 * c] + jnp.concatenate(
            [_dot(qp, v_diag[a][p]) for p, qp in enumerate(_pairs(qkd_s[a, r, :]))], axis=1) for a in n]
        yield
        ms = [_heads_sum((o[a] * o[a]).astype(BF16), pair_ones) * (1.0 / HEAD_DIM) for a in n]
        yield
        for a in n:
            z = z_ref[a, r, :]
            o_ref[a, r, :] = o[a] * lax.rsqrt(ms[a] + EPS) * nw * (z * _sigmoid(z))

    state_stages = 3
    per_trip = 4 if (t // c) % 4 == 0 else 2

    def chunk_group(ii, carry):
        _emit_pipelined([chunk_stages(per_trip * ii + j) for j in range(per_trip)], state_stages)
        return carry

    assert (t // c) % per_trip == 0
    lax.fori_loop(0, t // c // per_trip, chunk_group, 0)


def _gdn(gqkv, gz, gates, par, norm_w, layer, bsz, seq, t, nb):
    row = lambda n: pl.BlockSpec((nb, t, n), lambda b, j: (b, j, 0))
    lay = lambda a, b2: pl.BlockSpec((None, a, b2), lambda b, j: (layer, 0, 0))
    const = lambda shape: pl.BlockSpec(shape, lambda b, j: (0,) * len(shape))
    seq3 = lambda a: a.reshape(bsz, seq, a.shape[-1])
    f32_buf = pltpu.VMEM((nb, t, WIDTH), F32)
    bf16_buf = pltpu.VMEM((nb, t, WIDTH), BF16)
    out = pl.pallas_call(
        functools.partial(_gdn_kernel, t=t, group=8, nb=nb),
        grid=(bsz // nb, seq // t),
        in_specs=[row(3 * WIDTH), row(WIDTH), row(GATE_LANES),
                  lay(2, GATE_LANES), lay(1, WIDTH),
                  const((2, GATE_LANES, WIDTH)), const((PAIR, PAIR))],
        out_specs=row(WIDTH),
        out_shape=jax.ShapeDtypeStruct((bsz, seq, WIDTH), F32),
        scratch_shapes=[
            pltpu.VMEM((nb, WIDTH // PAIR, PAIR, PAIR), F32),
            f32_buf, f32_buf,
            f32_buf,
            bf16_buf, bf16_buf, bf16_buf, bf16_buf,
        ],
        compiler_params=pltpu.CompilerParams(
            dimension_semantics=("arbitrary", "arbitrary"), vmem_limit_bytes=VMEM_LIMIT),
    )(seq3(gqkv), seq3(gz), seq3(gates), par, norm_w,
      jnp.stack([_expander(0), _expander(HEADS)]), _pair_mask().astype(BF16))
    return out.reshape(bsz * seq, WIDTH)


def _mlstm_kernel(m_ref, gt_ref, par_ref, nw_ref, exp_ref, ones_ref, o_ref,
                  cn_ref, mx_ref, ig_s, b_s, cm_s, *, t, nb):
    c = CHUNK

    @pl.when(pl.program_id(1) == 0)
    def _():
        cn_ref[...] = jnp.zeros_like(cn_ref)
        mx_ref[...] = jnp.zeros_like(mx_ref)

    pair_ones = ones_ref[...]

    def front(a, carry):
        pre = gt_ref[a] + par_ref[...]
        capped = GATE_SOFTCAP * jnp.tanh(pre * (1.0 / GATE_SOFTCAP))
        log_f = -_softplus(-capped)
        b_all = _chunk_cumsum(log_f)
        ig = _dot_sel(capped, exp_ref[0])
        b = _dot_sel(b_all, exp_ref[1])
        pos = _iota2((t, WIDTH), 0) & (c - 1)
        cm = ig - b
        for sh in (1, 2, 4, 8, 16, 32):
            cm = jnp.where(pos >= sh, jnp.maximum(cm, pltpu.roll(cm, sh, axis=0)), cm)
        ig_s[a] = ig
        b_s[a] = b
        cm_s[a] = cm
        return carry

    lax.fori_loop(0, nb, front, 0)

    row = _iota2((c, WIDTH), 0)
    col = _iota2((c, WIDTH), 1) & (c - 1)
    causal = col <= row
    eye = col == row
    nw = nw_ref[...]

    def chunk_stages(i):
        n = range(nb)
        pairs = range(WIDTH // PAIR)
        r = pl.ds(pl.multiple_of(i * c, c), c)
        last = pl.ds(i * c + c - 1, 1)
        qb = [m_ref[a, r, 0:WIDTH].astype(BF16) for a in n]
        k = [m_ref[a, r, WIDTH:2 * WIDTH] for a in n]
        vb = [m_ref[a, r, 2 * WIDTH:3 * WIDTH].astype(BF16) for a in n]
        ig = [ig_s[a, r, :] for a in n]
        b = [b_s[a, r, :] for a in n]
        m_intra = [b[a] + cm_s[a, r, :] for a in n]
        b_last = [b_s[a, last, :] for a in n]
        m_chunk = [b_last[a] + cm_s[a, last, :] for a in n]
        qk = [_heads_mm_nt(qb[a], k[a].astype(BF16)) for a in n]
        yield
        gate_row = [jnp.sum(jnp.where(eye, ig[a] - b[a], 0.0), axis=0, keepdims=True) for a in n]
        qk = [qk[a] * jnp.exp(jnp.where(causal, b[a] + gate_row[a], -jnp.inf) - m_intra[a]) for a in n]
        yield
        v_ones = [[jnp.concatenate([_pair_diag(vp), pair_ones], axis=1) for vp in _pairs(vb[a])]
                  for a in n]
        qk_parts = [_split2(qk[a]) for a in n]
        intra = [[_dot(qp, v_ones[a][p]) for p, qp in enumerate(_pairs(qk_parts[a][0]))] for a in n]
        yield
        num_intra = [jnp.concatenate([intra[a][p][:, 0:PAIR] for p in pairs], axis=1) for a in n]
        den_intra = [jnp.concatenate([intra[a][p][:, PAIR:2 * PAIR] for p in pairs], axis=1)
                     + _heads_sum(qk_parts[a][1], pair_ones) for a in n]
        ke = [(k[a] * jnp.exp(b_last[a] - b[a] + ig[a] - m_chunk[a])).astype(BF16) for a in n]
        own = [[_dot_tn(_pair_diag(kp), v_ones[a][p]) for p, kp in enumerate(_pairs(ke[a]))]
               for a in n]
        yield
        m_prev = [mx_ref[a] for a in n]
        cn = [[cn_ref[a, p] for p in pairs] for a in n]
        inter = [[_dot(qp, cn[a][p].astype(BF16)) for p, qp in enumerate(_pairs(qb[a]))] for a in n]
        yield
        for a in n:
            m_new = jnp.maximum(b_last[a] + m_prev[a], m_chunk[a])
            s_old = _pairs(jnp.exp(b_last[a] + m_prev[a] - m_new))
            s_new = _pairs(jnp.exp(m_chunk[a] - m_new))
            for p in pairs:
                cn_ref[a, p] = (jnp.concatenate([s_old[p], s_old[p]], axis=1) * cn[a][p]
                                + jnp.concatenate([s_new[p], s_new[p]], axis=1) * own[a][p])
            mx_ref[a] = m_new
        yield
        h = []
        for a in n:
            pre_m = b[a] + m_prev[a]
            m_t = jnp.maximum(pre_m, m_intra[a])
            s_inter = jnp.exp(pre_m - m_t)
            s_intra = jnp.exp(m_intra[a] - m_t)
            num_inter = jnp.concatenate([inter[a][p][:, 0:PAIR] for p in pairs], axis=1)
            den_inter = jnp.concatenate([inter[a][p][:, PAIR:2 * PAIR] for p in pairs], axis=1)
            num = s_inter * num_inter + s_intra * num_intra[a]
            den = s_inter * den_inter + s_intra * den_intra[a]
            h.append(num / jnp.maximum(jnp.abs(den), jnp.exp(-m_t)))
        yield
        ms = [_heads_sum((h[a] * h[a]).astype(BF16), pair_ones) * (1.0 / HEAD_DIM) for a in n]
        yield
        for a in n:
            o_ref[a, r, :] = (h[a] * lax.rsqrt(ms[a] + EPS) * nw
                              * _sigmoid(m_ref[a, r, 3 * WIDTH:4 * WIDTH]))

    state_free = 4
    per_trip = 4 if (t // c) % 4 == 0 else 2

    def chunk_group(ii, carry):
        _emit_pipelined([chunk_stages(per_trip * ii + j) for j in range(per_trip)], state_free)
        return carry

    assert (t // c) % per_trip == 0
    lax.fori_loop(0, t // c // per_trip, chunk_group, 0)


def _mlstm(mqkvo, gates, par, norm_w, layer, bsz, seq, t, nb):
    row = lambda n: pl.BlockSpec((nb, t, n), lambda b, j: (b, j, 0))
    lay = lambda a, b2: pl.BlockSpec((None, a, b2), lambda b, j: (layer, 0, 0))
    const = lambda shape: pl.BlockSpec(shape, lambda b, j: (0,) * len(shape))
    seq3 = lambda a: a.reshape(bsz, seq, a.shape[-1])
    f32_buf = pltpu.VMEM((nb, t, WIDTH), F32)
    out = pl.pallas_call(
        functools.partial(_mlstm_kernel, t=t, nb=nb),
        grid=(bsz // nb, seq // t),
        in_specs=[row(4 * WIDTH), row(GATE_LANES), lay(1, GATE_LANES), lay(1, WIDTH),
                  const((2, GATE_LANES, WIDTH)), const((PAIR, PAIR))],
        out_specs=row(WIDTH),
        out_shape=jax.ShapeDtypeStruct((bsz, seq, WIDTH), F32),
        scratch_shapes=[
            pltpu.VMEM((nb, WIDTH // PAIR, PAIR, 2 * PAIR), F32),
            pltpu.VMEM((nb, 1, WIDTH), F32),
            f32_buf, f32_buf, f32_buf,
        ],
        compiler_params=pltpu.CompilerParams(
            dimension_semantics=("arbitrary", "arbitrary"), vmem_limit_bytes=VMEM_LIMIT),
    )(seq3(mqkvo), seq3(gates), par, norm_w,
      jnp.stack([_expander(2 * HEADS), _expander(3 * HEADS)]),
      _pair_mask().astype(BF16))
    return out.reshape(bsz * seq, WIDTH)


def _swa_stages(q_ref, kvc_ref, kvp_ref, sink_ref, o_ref, first, tq):
    tb = SWA_BLOCK
    kvw = SWA_KV_WIDTH
    nq = tq // tb
    chains = [(i, g) for i in range(nq) for g in range(2)]
    st = {}

    def setup():
        kcat = jnp.concatenate([kvp_ref[:, 0:kvw], kvc_ref[:, 0:kvw]], axis=0)
        vcat = jnp.concatenate([kvp_ref[:, kvw:2 * kvw], kvc_ref[:, kvw:2 * kvw]], axis=0)
        st["v_t"] = vcat.T.astype(BF16)
        kr = pltpu.roll(kcat, HEAD_DIM, axis=1)
        lo = _iota2((tb + tq, kvw), 1) < HEAD_DIM

        def place(x_lo, x_hi):
            return jnp.where(lo, x_lo, 0.0).astype(BF16), jnp.where(lo, 0.0, x_hi).astype(BF16)

        st["k"] = (place(kcat, kr), place(kr, kcat))
        st["ki"] = _iota2((2 * tb, tb), 0)
        qi = _iota2((2 * tb, tb), 1)
        st["in_window"] = (st["ki"] > qi) & (st["ki"] <= qi + tb)

    def scores(ci):
        i, g = chains[ci]
        keys = jnp.concatenate([st["k"][g][0][i * tb:(i + 2) * tb],
                                st["k"][g][1][i * tb:(i + 2) * tb]], axis=0)
        qs = jnp.concatenate([q_ref[i * tb:(i + 1) * tb, 256 * g:256 * g + 128],
                              q_ref[i * tb:(i + 1) * tb, 256 * g + 128:256 * g + 256]], axis=0)
        st["s", ci] = _dot_nt(keys, qs.astype(BF16))

    def probs(ci):
        i, g = chains[ci]
        sc = st.pop(("s", ci))
        mask = st["in_window"]
        if i == 0:
            mask = mask & (st["ki"] >= jnp.where(first, tb, 0))
        ps, invs = [], []
        for part in range(2):
            for e in range(2):
                sink = sink_ref[2 * g + part:2 * g + part + 1, e * 2 * tb:e * 2 * tb + 1]
                se = jnp.where(mask, sc[e * 2 * tb:(e + 1) * 2 * tb, part * tb:(part + 1) * tb],
                               -jnp.inf)
                mx = jnp.maximum(jnp.max(se, axis=0, keepdims=True), sink)
                p = jnp.exp(se - mx)
                ps.append(p.astype(BF16))
                invs.append(1.0 / (jnp.sum(p, axis=0, keepdims=True) + jnp.exp(sink - mx)))
        st["p", ci] = jnp.concatenate(ps, axis=-1)
        st["inv", ci] = jnp.concatenate(invs, axis=-1)

    def attend(ci):
        i, g = chains[ci]
        o = _dot(st["v_t"][g * HEAD_DIM:(g + 1) * HEAD_DIM, i * tb:(i + 2) * tb],
                 st.pop(("p", ci))) * st.pop(("inv", ci))
        for part in range(2):
            j = 2 * g + part
            pair_t = jnp.concatenate([o[:, 2 * part * tb:(2 * part + 1) * tb],
                                      o[:, (2 * part + 1) * tb:(2 * part + 2) * tb]], axis=0)
            o_ref[i * tb:(i + 1) * tb, 128 * j:128 * (j + 1)] = pair_t.T

    n = len(chains)
    stages = [setup]
    for ci in range(n + 2):
        stages += [functools.partial(f, cj) for f, cj in ((scores, ci), (probs, ci - 1), (attend, ci - 2))
                   if 0 <= cj < n]
    return stages


def _post_kernel(x_ref, ya_ref, yb_ref, p_ref, wo_ref, nmlp_ref, wup_ref, wdn_ref,
                 nple_ref, wg_ref, wp_ref, nfin_ref,
                 sq0_ref, skv0_ref, sqn_ref, skvn_ref, kvpn_ref, sink_ref,
                 o_ref, yc_ref, *, final, tf, tiles_per_seq):
    t = x_ref.shape[0]
    step = pl.program_id(0)

    @pl.when(step == 0)
    def _():
        for stage in _swa_stages(sq0_ref, skv0_ref, kvpn_ref, sink_ref, yc_ref, True, t):
            stage()

    x = x_ref[...]
    x = x + _dot(ya_ref[...].astype(BF16), wo_ref[0:WIDTH, :])
    x = x + _dot(yb_ref[...].astype(BF16), wo_ref[WIDTH:2 * WIDTH, :])
    x = x + _dot(yc_ref[...].astype(BF16), wo_ref[2 * WIDTH:, :])
    stages = _swa_stages(sqn_ref, skvn_ref, kvpn_ref, sink_ref, yc_ref,
                         (step + 1) % tiles_per_seq == 0, t)
    slots = 2 * (D_FF // tf)
    per_slot = -(-len(stages) // slots)

    def fill():
        for _ in range(per_slot):
            if stages:
                stages.pop(0)()

    r = _rms_scale(x)
    h = (x * nmlp_ref[...]).astype(BF16)
    acc = jnp.zeros_like(x)
    for f in range(0, D_FF, tf):
        u = jnp.maximum(_dot(h, wup_ref[:, f:f + tf]), 0.0)
        fill()
        acc = acc + _dot((u * u).astype(BF16), wdn_ref[f:f + tf, :])
        fill()
    x = x + (r * r) * acc
    gate = _sigmoid(_rms_scale(x) * _dot((x * nple_ref[...]).astype(BF16), wg_ref[...]))
    x = x + gate * _dot(p_ref[...].astype(BF16), wp_ref[...])
    if final:
        x = _rms(x, nfin_ref[...])
    o_ref[...] = x


def _post(x, ya, yb, sq, skv, sinks_e, p, w_out, norm_mlp, w_up, w_down, norm_ple, w_gate, w_proj,
          norm_final, layer, tm, seq, final):
    m = x.shape[0]
    nt = m // tm
    blocks_per_tile = tm // SWA_BLOCK
    row = lambda n: pl.BlockSpec((tm, n), lambda i: (i, 0))
    first = lambda n: pl.BlockSpec((tm, n), lambda i: (0, 0))
    nxt = lambda n: pl.BlockSpec((tm, n), lambda i: (jnp.minimum(i + 1, nt - 1), 0))
    lay = lambda a, b: pl.BlockSpec((None, a, b), lambda i: (layer, 0, 0),
                                    pipeline_mode=pl.Buffered(1))
    return pl.pallas_call(
        functools.partial(_post_kernel, final=final, tf=512, tiles_per_seq=seq // tm),
        grid=(nt,),
        in_specs=[
            row(D_MODEL), row(WIDTH), row(WIDTH),
            pl.BlockSpec((None, tm, PLE_DIM), lambda i: (layer, i, 0)),
            lay(D_MODEL, D_MODEL), lay(1, D_MODEL), lay(D_MODEL, D_FF), lay(D_FF, D_MODEL),
            lay(1, D_MODEL), lay(D_MODEL, D_MODEL), lay(PLE_DIM, D_MODEL),
            pl.BlockSpec((1, D_MODEL), lambda i: (0, 0)),
            first(SWA_WIDTH), first(2 * SWA_KV_WIDTH), nxt(SWA_WIDTH), nxt(2 * SWA_KV_WIDTH),
            pl.BlockSpec((SWA_BLOCK, 2 * SWA_KV_WIDTH),
                         lambda i: ((i + 1) * blocks_per_tile - 1, 0)),
            pl.BlockSpec((None, SWA_WIDTH // 128, 4 * SWA_BLOCK), lambda i: (layer, 0, 0)),
        ],
        out_specs=row(D_MODEL),
        out_shape=jax.ShapeDtypeStruct((m, D_MODEL), F32),
        scratch_shapes=[pltpu.VMEM((tm, SWA_WIDTH), F32)],
        compiler_params=pltpu.CompilerParams(
            dimension_semantics=("arbitrary",), vmem_limit_bytes=VMEM_LIMIT),
    )(x, ya, yb, p, w_out, norm_mlp, w_up, w_down, norm_ple, w_gate, w_proj, norm_final,
      sq, skv, sq, skv, skv, sinks_e)


def _gate_row(depth, pieces):
    out = jnp.zeros((depth, 1, GATE_LANES), F32)
    for first_col, vals in pieces:
        out = out.at[:, 0, first_col:first_col + HEADS].set(vals.astype(F32))
    return out


def _rope_tables(positions):
    half = ROPE_DIM // 2
    inv_freq = ROPE_THETA ** (-jnp.arange(0, ROPE_DIM, 2, dtype=F32) / ROPE_DIM)
    dim = jnp.arange(128) % HEAD_DIM
    ang = positions.astype(F32).reshape(-1)[:, None] * inv_freq[dim % half][None, :]
    cos_t = jnp.where(dim < ROPE_DIM, jnp.cos(ang), 1.0)
    sin_t = jnp.where(dim < half, -jnp.sin(ang), jnp.where(dim < ROPE_DIM, jnp.sin(ang), 0.0))
    return cos_t, sin_t


def kernel(x, p, positions, w_in, conv_w, gdn_a_log, gdn_dt_bias, gdn_norm, mlstm_i_bias,
           mlstm_f_bias, mlstm_norm, attn_sinks, w_out, norm_mix, norm_mlp, w_up, w_down,
           norm_ple, w_ple_gate, w_ple_proj, norm_final):
    bsz, seq, d = x.shape
    depth = w_in.shape[0]
    m = bsz * seq
    tm = min(512, seq)
    t_mix = min(512, seq)
    nb_mix = 4 if bsz % 4 == 0 else 1
    nb_gdn = 8 if bsz % 8 == 0 else nb_mix
    t_gdn = min(2048 // nb_gdn, seq)

    cos_t, sin_t = _rope_tables(positions)
    gdn_par = jnp.concatenate([_gate_row(depth, [(HEADS, gdn_a_log)]),
                               _gate_row(depth, [(HEADS, gdn_dt_bias)])], axis=1)
    mlstm_par = _gate_row(depth, [(2 * HEADS, mlstm_i_bias), (3 * HEADS, mlstm_f_bias)])
    gdn_nw = jnp.tile(gdn_norm.astype(F32), (1, HEADS))[:, None, :]
    mlstm_nw = mlstm_norm.astype(F32)[:, None, :]
    sinks_e = jnp.repeat(attn_sinks.astype(F32), 2 * SWA_BLOCK, axis=-1).reshape(
        depth, SWA_WIDTH // 128, 4 * SWA_BLOCK)
    row3 = lambda a: a.astype(F32)[:, None, :]
    wo_b, wup_b, wdn_b = w_out.astype(BF16), w_up.astype(BF16), w_down.astype(BF16)
    wg_b, wp_b = w_ple_gate.astype(BF16), w_ple_proj.astype(BF16)
    nmix, nmlp, nple = row3(norm_mix), row3(norm_mlp), row3(norm_ple)
    nfin = norm_final.astype(F32)[None, :]
    p2 = p.reshape(depth, m, PLE_DIM)

    xf = x.reshape(m, d)
    for i in range(depth):
        gqkv, gz, mqkvo, sq, skv, gates = _proj(xf, nmix, w_in.astype(F32), cos_t, sin_t,
                                                conv_w.astype(F32), i, tm, seq)
        ya = _gdn(gqkv, gz, gates, gdn_par, gdn_nw, i, bsz, seq, t_gdn, nb_gdn)
        yb = _mlstm(mqkvo, gates, mlstm_par, mlstm_nw, i, bsz, seq, t_mix, nb_mix)
        xf = _post(xf, ya, yb, sq, skv, sinks_e, p2, wo_b, nmlp, wup_b, wdn_b, nple, wg_b, wp_b, nfin,
                   i, tm, seq, final=(i == depth - 1))
    return xf.reshape(bsz, seq, d)
```

```python
import functools

import jax
import jax.numpy as jnp
from jax import lax
from jax.experimental import pallas as pl
from jax.experimental.pallas import tpu as pltpu

F32 = jnp.float32
BF16 = jnp.bfloat16

D_MODEL = 1024
DEPTH = 4
HEAD_DIM = 64
PLE_DIM = 256
D_FF = 4 * D_MODEL
EPS = 1e-6
HEADS = 4
WIDTH = HEADS * HEAD_DIM
PAIR = 2 * HEAD_DIM
CHUNK = 64
CONV_WIDTH = 4
GATE_SOFTCAP = 15.0
SWA_WIDTH = 512
SWA_KV_WIDTH = 128
SWA_BLOCK = 128
ROPE_DIM = 16
ROPE_THETA = 500000.0
GATE_LANES = 128
IN_COLS_PACKED = 2944
OFF_GQKV, OFF_GZ, OFF_M, OFF_SQ, OFF_SKV, OFF_GATES = 0, 768, 1024, 2048, 2560, 2816
IN_A = 4 * WIDTH
IN_M0 = IN_A + 2 * HEADS
IN_G2 = IN_M0 + 4 * WIDTH
IN_S0 = IN_G2 + 2 * HEADS
IN_S1 = IN_S0 + SWA_WIDTH + 2 * SWA_KV_WIDTH

VMEM_LIMIT = 56 * 1024 * 1024


def _dot(a, b):
    return jnp.dot(a, b, preferred_element_type=F32)


def _dot_nt(a, b):
    return lax.dot_general(a, b, (((1,), (1,)), ((), ())), preferred_element_type=F32)


def _dot_tn(a, b):
    return lax.dot_general(a, b, (((0,), (0,)), ((), ())), preferred_element_type=F32)


def _split2(x):
    hi = x.astype(BF16)
    lo = (x - hi.astype(F32)).astype(BF16)
    return hi, lo


def _dot_sel(x, sel):
    hi, lo = _split2(x)
    return _dot(hi, sel) + _dot(lo, sel)


def _chunk_cumsum(x):
    pos = _iota2(x.shape, 0) & (CHUNK - 1)
    sh = 1
    while sh < CHUNK:
        x = x + jnp.where(pos >= sh, pltpu.roll(x, sh, axis=0), 0.0)
        sh *= 2
    return x


def _sigmoid(x):
    return 1.0 / (1.0 + jnp.exp(-x))


def _softplus(x):
    return jnp.maximum(x, 0.0) + jnp.log1p(jnp.exp(-jnp.abs(x)))


def _rms(x, g):
    return x * lax.rsqrt(jnp.mean(x * x, axis=-1, keepdims=True) + EPS) * g


def _rms_scale(x):
    return lax.rsqrt(jnp.mean(x * x, axis=-1, keepdims=True) + EPS)


def _iota2(shape, dim):
    return lax.broadcasted_iota(jnp.int32, shape, dim)


def _pairs(x):
    return [x[:, p * PAIR:(p + 1) * PAIR] for p in range(WIDTH // PAIR)]


def _pair_mask():
    return (_iota2((PAIR, PAIR), 0) >> 6) == (_iota2((PAIR, PAIR), 1) >> 6)


def _pair_diag(x):
    low = _iota2(x.shape, 1) < HEAD_DIM
    zero = jnp.zeros((), x.dtype)
    return jnp.concatenate([jnp.where(low, x, zero), jnp.where(low, zero, x)], axis=0)


def _heads_mm(a, b):
    return jnp.concatenate([_dot(ap, _pair_diag(bp)) for ap, bp in zip(_pairs(a), _pairs(b))], axis=1)


def _heads_mm_nt(a, b):
    return jnp.concatenate([_dot_nt(ap, _pair_diag(bp)) for ap, bp in zip(_pairs(a), _pairs(b))],
                           axis=1)


def _heads_sum(x, pair_ones):
    return jnp.concatenate([_dot(xp, pair_ones) for xp in _pairs(x)], axis=1)


def _emit_pipelined(stage_gens, lead):
    pending, live, step = list(stage_gens), [], 0
    while pending or live:
        if pending and step % lead == 0:
            live.append(pending.pop(0))
        for g in list(live):
            if next(g, StopIteration) is StopIteration:
                live.remove(g)
        step += 1


def _expander(first_col):
    r = _iota2((GATE_LANES, WIDTH), 0)
    c = _iota2((GATE_LANES, WIDTH), 1)
    return (r == first_col + (c >> 6)).astype(BF16)


def _rope(x, cos_t, sin_t, reps):
    n = x.shape[-1]
    if reps > 1:
        cos_t = jnp.concatenate([cos_t] * reps, axis=-1)
        sin_t = jnp.concatenate([sin_t] * reps, axis=-1)
    half = ROPE_DIM // 2
    x_up = pltpu.roll(x, n - half, axis=1)
    x_dn = pltpu.roll(x, half, axis=1)
    first_half = (_iota2(x.shape, 1) & (HEAD_DIM - 1)) < half
    return x * cos_t + jnp.where(first_half, x_up, x_dn) * sin_t


def _repack_w_in(w_in_ref, w_ref):
    rows = 128
    scale = HEAD_DIM ** -0.5
    lane_m = _iota2((1, 4 * WIDTH), 1)
    scale_m = jnp.where((lane_m >= WIDTH) & (lane_m < 2 * WIDTH), scale, 1.0)
    scale_s = jnp.where(_iota2((1, SWA_WIDTH + 2 * SWA_KV_WIDTH), 1) < SWA_WIDTH, scale, 1.0)

    def body(i, carry):
        r = pl.ds(pl.multiple_of(i * rows, rows), rows)
        w_ref[r, OFF_GQKV:OFF_M] = w_in_ref[r, 0:IN_A].astype(BF16)
        w_ref[r, OFF_M:OFF_SQ] = (w_in_ref[r, IN_M0:IN_G2] * scale_m).astype(BF16)
        w_ref[r, OFF_SQ:OFF_GATES] = (w_in_ref[r, IN_S0:IN_S1] * scale_s).astype(BF16)
        lane = _iota2((rows, GATE_LANES), 1)
        g2_tile = IN_G2 - 2 * HEADS
        gates = jnp.where(lane < 2 * HEADS, w_in_ref[r, IN_A:IN_A + GATE_LANES],
                          jnp.where(lane < 4 * HEADS, w_in_ref[r, g2_tile:g2_tile + GATE_LANES], 0.0))
        w_ref[r, OFF_GATES:IN_COLS_PACKED] = gates.astype(BF16)
        return carry

    lax.fori_loop(0, D_MODEL // rows, body, 0)


def _proj_kernel(x_ref, g_ref, w_in_ref, cos_ref, sin_ref, cw_ref, ones_ref,
                 gqkv_ref, gz_ref, m_ref, sq_ref, skv_ref, gates_ref, tail_ref, buf_ref, w_ref,
                 *, tiles_per_seq):
    t = x_ref.shape[0]

    @pl.when(pl.program_id(0) == 0)
    def _():
        _repack_w_in(w_in_ref, w_ref)

    @pl.when(pl.program_id(0) % tiles_per_seq == 0)
    def _():
        tail_ref[...] = jnp.zeros_like(tail_ref)

    x = x_ref[...]
    hb = (x * g_ref[...]).astype(BF16)
    scale = _rms_scale(x)

    def proj(lo, hi):
        return scale * _dot(hb, w_ref[:, lo:hi])

    a = proj(OFF_GQKV, OFF_M)
    buf_ref[0:8, :] = tail_ref[...]
    buf_ref[8:8 + t, :] = a[:, 0:OFF_GZ]
    tail_ref[...] = buf_ref[t:t + 8, :]
    gz_ref[...] = a[:, OFF_GZ:OFF_M]
    cw = cw_ref[...]
    pair_ones = ones_ref[...]

    def conv_silu(lo, hi):
        acc = buf_ref[5:5 + t, lo:hi] * cw[0:1, lo:hi]
        for j in range(1, CONV_WIDTH):
            acc = acc + buf_ref[5 + j:5 + j + t, lo:hi] * cw[j:j + 1, lo:hi]
        return acc * _sigmoid(acc)

    def l2n(v):
        return v * lax.rsqrt(_heads_sum((v * v).astype(BF16), pair_ones) + EPS)

    cos_t, sin_t = cos_ref[...], sin_ref[...]
    s = proj(OFF_SQ, IN_COLS_PACKED)
    q = conv_silu(0, WIDTH)
    gqkv_ref[:, 0:WIDTH] = l2n(q) * (HEAD_DIM ** -0.5)
    m_ref[:, 0:2 * WIDTH] = proj(OFF_M, OFF_M + 2 * WIDTH)
    sq_ref[...] = _rope(s[:, 0:SWA_WIDTH], cos_t, sin_t, SWA_WIDTH // 128)
    skv_ref[:, 0:SWA_KV_WIDTH] = _rope(s[:, SWA_WIDTH:SWA_WIDTH + SWA_KV_WIDTH], cos_t, sin_t, 1)
    skv_ref[:, SWA_KV_WIDTH:] = s[:, SWA_WIDTH + SWA_KV_WIDTH:SWA_WIDTH + 2 * SWA_KV_WIDTH]
    gates_ref[...] = s[:, SWA_WIDTH + 2 * SWA_KV_WIDTH:]
    k = conv_silu(WIDTH, 2 * WIDTH)
    gqkv_ref[:, WIDTH:2 * WIDTH] = l2n(k)
    gqkv_ref[:, 2 * WIDTH:3 * WIDTH] = conv_silu(2 * WIDTH, 3 * WIDTH)
    m_ref[:, 2 * WIDTH:4 * WIDTH] = proj(OFF_M + 2 * WIDTH, OFF_SQ)


def _proj(x, norm_w, w_in, cos_t, sin_t, conv_w, layer, tm, seq):
    m = x.shape[0]
    row = lambda n: pl.BlockSpec((tm, n), lambda i: (i, 0))
    widths = (768, 256, 1024, SWA_WIDTH, 2 * SWA_KV_WIDTH, GATE_LANES)
    return pl.pallas_call(
        functools.partial(_proj_kernel, tiles_per_seq=seq // tm),
        grid=(m // tm,),
        in_specs=[
            row(D_MODEL),
            pl.BlockSpec((None, 1, D_MODEL), lambda i: (layer, 0, 0)),
            pl.BlockSpec((None, D_MODEL, w_in.shape[-1]), lambda i: (layer, 0, 0),
                         pipeline_mode=pl.Buffered(1)),
            row(128), row(128),
            pl.BlockSpec((None, CONV_WIDTH, 3 * WIDTH), lambda i: (layer, 0, 0)),
            pl.BlockSpec((PAIR, PAIR), lambda i: (0, 0)),
        ],
        out_specs=[row(n) for n in widths],
        out_shape=[jax.ShapeDtypeStruct((m, n), F32) for n in widths],
        scratch_shapes=[
            pltpu.VMEM((8, 3 * WIDTH), F32),
            pltpu.VMEM((tm + 8, 3 * WIDTH), F32),
            pltpu.VMEM((D_MODEL, IN_COLS_PACKED), BF16),
        ],
        compiler_params=pltpu.CompilerParams(
            dimension_semantics=("arbitrary",), vmem_limit_bytes=VMEM_LIMIT),
    )(x, norm_w, w_in, cos_t, sin_t, conv_w, _pair_mask().astype(BF16))


def _gdn_kernel(qkv_ref, z_ref, gt_ref, par_ref, nw_ref, exp_ref, ones_ref, o_ref,
                state_ref, beta_s, gc_s, u_s, w_s, qd_s, kd_s, qkd_s, *, t, group, nb):
    c = CHUNK

    @pl.when(pl.program_id(1) == 0)
    def _():
        state_ref[...] = jnp.zeros_like(state_ref)

    pair_ones = ones_ref[...]

    def gate_rows(a, lo, hi):
        gates = gt_ref[a, lo:hi, :]
        par = par_ref[...]
        beta = _sigmoid(gates)
        g = -jnp.exp(par[0:1, :]) * _softplus(gates + par[1:2, :])
        beta_s[a, lo:hi, :] = _dot_sel(beta, exp_ref[0])
        gc_s[a, lo:hi, :] = _dot_sel(_chunk_cumsum(g), exp_ref[1])

    row = _iota2((c, WIDTH), 0)
    col = _iota2((c, WIDTH), 1) & (c - 1)
    causal = col <= row
    strict = col < row
    eye = col == row
    nw = nw_ref[...]

    def mm(a, b):
        return _heads_mm(a.astype(BF16), b.astype(BF16))

    def prepare(items, fillers):
        fillers = list(fillers)

        def fill():
            if fillers:
                fillers.pop(0)()

        n = range(len(items))
        sq = [a for a, _ in items]
        rs = [pl.ds(j * c, c) for _, j in items]
        q = [qkv_ref[sq[j], rs[j], 0:WIDTH] for j in n]
        k = [qkv_ref[sq[j], rs[j], WIDTH:2 * WIDTH] for j in n]
        gc = [gc_s[sq[j], rs[j], :] for j in n]
        g_last = [gc_s[a, pl.ds(j * c + c - 1, 1), :] for a, j in items]
        kb = [k[j] * beta_s[sq[j], rs[j], :] for j in n]
        kkqk = [_heads_mm_nt(jnp.concatenate([kb[j], q[j]], axis=0).astype(BF16), k[j].astype(BF16))
                for j in n]
        fill()
        gc_row = [jnp.sum(jnp.where(eye, gc[j], 0.0), axis=0, keepdims=True) for j in n]
        decay = [jnp.exp(jnp.where(causal, gc[j] - gc_row[j], -jnp.inf)) for j in n]
        nm = [jnp.where(strict, kkqk[j][0:c] * decay[j], 0.0) for j in n]
        for j in n:
            qkd_s[sq[j], rs[j], :] = (kkqk[j][c:2 * c] * decay[j]).astype(BF16)
        x = [eye.astype(F32) - jnp.where((row >> 1) == (col >> 1), nm[j], 0.0) for j in n]
        for lb in range(1, 6):
            off = ((row >> (lb + 1)) == (col >> (lb + 1))) & ((row >> lb) != (col >> lb))
            t1 = [mm(jnp.where(off, nm[j], 0.0), x[j]) for j in n]
            fill()
            t2 = [mm(x[j], t1[j]) for j in n]
            if lb % 2 == 0:
                fill()
            x = [x[j] - t2[j] for j in n]
        xb = [x[j].astype(BF16) for j in n]
        egc = [jnp.exp(gc[j]) for j in n]
        vbeta = [(qkv_ref[sq[j], rs[j], 2 * WIDTH:3 * WIDTH] * beta_s[sq[j], rs[j], :]).astype(BF16)
                 for j in n]
        kbd = [(kb[j] * egc[j]).astype(BF16) for j in n]
        uw = [[_dot(xp, jnp.concatenate([_pair_diag(vp), _pair_diag(kp)], axis=1))
               for xp, vp, kp in zip(_pairs(xb[j]), _pairs(vbeta[j]), _pairs(kbd[j]))] for j in n]
        for j in n:
            u_s[sq[j], rs[j], :] = jnp.concatenate([p[:, 0:PAIR] for p in uw[j]], axis=1)
            w_s[sq[j], rs[j], :] = jnp.concatenate([p[:, PAIR:2 * PAIR] for p in uw[j]], axis=1).astype(BF16)
        for j in n:
            qd_s[sq[j], rs[j], :] = (q[j] * egc[j]).astype(BF16)
            kd_s[sq[j], rs[j], :] = (k[j] * jnp.exp(g_last[j] - gc[j])).astype(BF16)
        while fillers:
            fill()

    chunks = [(a, j) for a in range(nb) for j in range(t // c)]
    groups = [chunks[i:i + group] for i in range(0, len(chunks), group)]
    for a, j in groups[0]:
        gate_rows(a, j * c, (j + 1) * c)
    for gi, items in enumerate(groups):
        nxt = groups[gi + 1] if gi + 1 < len(groups) else []
        prepare(items, [functools.partial(gate_rows, a, j * c, (j + 1) * c) for a, j in nxt])

    def chunk_stages(i):
        n = range(nb)
        r = pl.ds(pl.multiple_of(i * c, c), c)
        pairs = range(WIDTH // PAIR)
        g_last = [gc_s[a, pl.ds(i * c + c - 1, 1), :] for a in n]
        s = [[state_ref[a, p] for p in pairs] for a in n]
        lhs = [_pairs(jnp.concatenate([w_s[a, r, :], qd_s[a, r, :]], axis=0)) for a in n]
        ws = [jnp.concatenate([_dot(lhs[a][p], s[a][p].astype(BF16)) for p in pairs], axis=1)
              for a in n]
        yield
        v_new = [(u_s[a, r, :] - ws[a][0:c]).astype(BF16) for a in n]
        v_diag = [[_pair_diag(vn) for vn in _pairs(v_new[a])] for a in n]
        upd = [[_dot_tn(_pair_diag(kd), v_diag[a][p]) for p, kd in enumerate(_pairs(kd_s[a, r, :]))]
               for a in n]
        yield
        for a in n:
            decay_end = _pairs(jnp.exp(g_last[a]))
            for p in pairs:
                state_ref[a, p] = s[a][p] * decay_end[p] + upd[a][p]
        yield
        o = [ws[a][c:2 * c] + jnp.concatenate(
            [_dot(qp, v_diag[a][p]) for p, qp in enumerate(_pairs(qkd_s[a, r, :]))], axis=1) for a in n]
        yield
        ms = [_heads_sum((o[a] * o[a]).astype(BF16), pair_ones) * (1.0 / HEAD_DIM) for a in n]
        yield
        for a in n:
            z = z_ref[a, r, :]
            o_ref[a, r, :] = o[a] * lax.rsqrt(ms[a] + EPS) * nw * (z * _sigmoid(z))

    state_stages = 3
    per_trip = 4 if (t // c) % 4 == 0 else 2

    def chunk_group(ii, carry):
        _emit_pipelined([chunk_stages(per_trip * ii + j) for j in range(per_trip)], state_stages)
        return carry

    assert (t // c) % per_trip == 0
    lax.fori_loop(0, t // c // per_trip, chunk_group, 0)


def _gdn(gqkv, gz, gates, par, norm_w, layer, bsz, seq, t, nb):
    row = lambda n: pl.BlockSpec((nb, t, n), lambda b, j: (b, j, 0))
    lay = lambda a, b2: pl.BlockSpec((None, a, b2), lambda b, j: (layer, 0, 0))
    const = lambda shape: pl.BlockSpec(shape, lambda b, j: (0,) * len(shape))
    seq3 = lambda a: a.reshape(bsz, seq, a.shape[-1])
    f32_buf = pltpu.VMEM((nb, t, WIDTH), F32)
    bf16_buf = pltpu.VMEM((nb, t, WIDTH), BF16)
    out = pl.pallas_call(
        functools.partial(_gdn_kernel, t=t, group=8, nb=nb),
        grid=(bsz // nb, seq // t),
        in_specs=[row(3 * WIDTH), row(WIDTH), row(GATE_LANES),
                  lay(2, GATE_LANES), lay(1, WIDTH),
                  const((2, GATE_LANES, WIDTH)), const((PAIR, PAIR))],
        out_specs=row(WIDTH),
        out_shape=jax.ShapeDtypeStruct((bsz, seq, WIDTH), F32),
        scratch_shapes=[
            pltpu.VMEM((nb, WIDTH // PAIR, PAIR, PAIR), F32),
            f32_buf, f32_buf,
            f32_buf,
            bf16_buf, bf16_buf, bf16_buf, bf16_buf,
        ],
        compiler_params=pltpu.CompilerParams(
            dimension_semantics=("arbitrary", "arbitrary"), vmem_limit_bytes=VMEM_LIMIT),
    )(seq3(gqkv), seq3(gz), seq3(gates), par, norm_w,
      jnp.stack([_expander(0), _expander(HEADS)]), _pair_mask().astype(BF16))
    return out.reshape(bsz * seq, WIDTH)


def _mlstm_kernel(m_ref, gt_ref, par_ref, nw_ref, exp_ref, ones_ref, o_ref,
                  cn_ref, mx_ref, ig_s, b_s, cm_s, *, t, nb):
    c = CHUNK

    @pl.when(pl.program_id(1) == 0)
    def _():
        cn_ref[...] = jnp.zeros_like(cn_ref)
        mx_ref[...] = jnp.zeros_like(mx_ref)

    pair_ones = ones_ref[...]

    def front(a, carry):
        pre = gt_ref[a] + par_ref[...]
        capped = GATE_SOFTCAP * jnp.tanh(pre * (1.0 / GATE_SOFTCAP))
        log_f = -_softplus(-capped)
        b_all = _chunk_cumsum(log_f)
        ig = _dot_sel(capped, exp_ref[0])
        b = _dot_sel(b_all, exp_ref[1])
        pos = _iota2((t, WIDTH), 0) & (c - 1)
        cm = ig - b
        for sh in (1, 2, 4, 8, 16, 32):
            cm = jnp.where(pos >= sh, jnp.maximum(cm, pltpu.roll(cm, sh, axis=0)), cm)
        ig_s[a] = ig
        b_s[a] = b
        cm_s[a] = cm
        return carry

    lax.fori_loop(0, nb, front, 0)

    row = _iota2((c, WIDTH), 0)
    col = _iota2((c, WIDTH), 1) & (c - 1)
    causal = col <= row
    eye = col == row
    nw = nw_ref[...]

    def chunk_stages(i):
        n = range(nb)
        pairs = range(WIDTH // PAIR)
        r = pl.ds(pl.multiple_of(i * c, c), c)
        last = pl.ds(i * c + c - 1, 1)
        qb = [m_ref[a, r, 0:WIDTH].astype(BF16) for a in n]
        k = [m_ref[a, r, WIDTH:2 * WIDTH] for a in n]
        vb = [m_ref[a, r, 2 * WIDTH:3 * WIDTH].astype(BF16) for a in n]
        ig = [ig_s[a, r, :] for a in n]
        b = [b_s[a, r, :] for a in n]
        m_intra = [b[a] + cm_s[a, r, :] for a in n]
        b_last = [b_s[a, last, :] for a in n]
        m_chunk = [b_last[a] + cm_s[a, last, :] for a in n]
        qk = [_heads_mm_nt(qb[a], k[a].astype(BF16)) for a in n]
        yield
        gate_row = [jnp.sum(jnp.where(eye, ig[a] - b[a], 0.0), axis=0, keepdims=True) for a in n]
        qk = [qk[a] * jnp.exp(jnp.where(causal, b[a] + gate_row[a], -jnp.inf) - m_intra[a]) for a in n]
        yield
        v_ones = [[jnp.concatenate([_pair_diag(vp), pair_ones], axis=1) for vp in _pairs(vb[a])]
                  for a in n]
        qk_parts = [_split2(qk[a]) for a in n]
        intra = [[_dot(qp, v_ones[a][p]) for p, qp in enumerate(_pairs(qk_parts[a][0]))] for a in n]
        yield
        num_intra = [jnp.concatenate([intra[a][p][:, 0:PAIR] for p in pairs], axis=1) for a in n]
        den_intra = [jnp.concatenate([intra[a][p][:, PAIR:2 * PAIR] for p in pairs], axis=1)
                     + _heads_sum(qk_parts[a][1], pair_ones) for a in n]
        ke = [(k[a] * jnp.exp(b_last[a] - b[a] + ig[a] - m_chunk[a])).astype(BF16) for a in n]
        own = [[_dot_tn(_pair_diag(kp), v_ones[a][p]) for p, kp in enumerate(_pairs(ke[a]))]
               for a in n]
        yield
        m_prev = [mx_ref[a] for a in n]
        cn = [[cn_ref[a, p] for p in pairs] for a in n]
        inter = [[_dot(qp, cn[a][p].astype(BF16)) for p, qp in enumerate(_pairs(qb[a]))] for a in n]
        yield
        for a in n:
            m_new = jnp.maximum(b_last[a] + m_prev[a], m_chunk[a])
            s_old = _pairs(jnp.exp(b_last[a] + m_prev[a] - m_new))
            s_new = _pairs(jnp.exp(m_chunk[a] - m_new))
            for p in pairs:
                cn_ref[a, p] = (jnp.concatenate([s_old[p], s_old[p]], axis=1) * cn[a][p]
                                + jnp.concatenate([s_new[p], s_new[p]], axis=1) * own[a][p])
            mx_ref[a] = m_new
        yield
        h = []
        for a in n:
            pre_m = b[a] + m_prev[a]
            m_t = jnp.maximum(pre_m, m_intra[a])
            s_inter = jnp.exp(pre_m - m_t)
            s_intra = jnp.exp(m_intra[a] - m_t)
            num_inter = jnp.concatenate([inter[a][p][:, 0:PAIR] for p in pairs], axis=1)
            den_inter = jnp.concatenate([inter[a][p][:, PAIR:2 * PAIR] for p in pairs], axis=1)
            num = s_inter * num_inter + s_intra * num_intra[a]
            den = s_inter * den_inter + s_intra * den_intra[a]
            h.append(num / jnp.maximum(jnp.abs(den), jnp.exp(-m_t)))
        yield
        ms = [_heads_sum((h[a] * h[a]).astype(BF16), pair_ones) * (1.0 / HEAD_DIM) for a in n]
        yield
        for a in n:
            o_ref[a, r, :] = (h[a] * lax.rsqrt(ms[a] + EPS) * nw
                              * _sigmoid(m_ref[a, r, 3 * WIDTH:4 * WIDTH]))

    state_free = 4
    per_trip = 4 if (t // c) % 4 == 0 else 2

    def chunk_group(ii, carry):
        _emit_pipelined([chunk_stages(per_trip * ii + j) for j in range(per_trip)], state_free)
        return carry

    assert (t // c) % per_trip == 0
    lax.fori_loop(0, t // c // per_trip, chunk_group, 0)


def _mlstm(mqkvo, gates, par, norm_w, layer, bsz, seq, t, nb):
    row = lambda n: pl.BlockSpec((nb, t, n), lambda b, j: (b, j, 0))
    lay = lambda a, b2: pl.BlockSpec((None, a, b2), lambda b, j: (layer, 0, 0))
    const = lambda shape: pl.BlockSpec(shape, lambda b, j: (0,) * len(shape))
    seq3 = lambda a: a.reshape(bsz, seq, a.shape[-1])
    f32_buf = pltpu.VMEM((nb, t, WIDTH), F32)
    out = pl.pallas_call(
        functools.partial(_mlstm_kernel, t=t, nb=nb),
        grid=(bsz // nb, seq // t),
        in_specs=[row(4 * WIDTH), row(GATE_LANES), lay(1, GATE_LANES), lay(1, WIDTH),
                  const((2, GATE_LANES, WIDTH)), const((PAIR, PAIR))],
        out_specs=row(WIDTH),
        out_shape=jax.ShapeDtypeStruct((bsz, seq, WIDTH), F32),
        scratch_shapes=[
            pltpu.VMEM((nb, WIDTH // PAIR, PAIR, 2 * PAIR), F32),
            pltpu.VMEM((nb, 1, WIDTH), F32),
            f32_buf, f32_buf, f32_buf,
        ],
        compiler_params=pltpu.CompilerParams(
            dimension_semantics=("arbitrary", "arbitrary"), vmem_limit_bytes=VMEM_LIMIT),
    )(seq3(mqkvo), seq3(gates), par, norm_w,
      jnp.stack([_expander(2 * HEADS), _expander(3 * HEADS)]),
      _pair_mask().astype(BF16))
    return out.reshape(bsz * seq, WIDTH)


def _swa_stages(q_ref, kvc_ref, kvp_ref, sink_ref, o_ref, first, tq):
    tb = SWA_BLOCK
    kvw = SWA_KV_WIDTH
    nq = tq // tb
    chains = [(i, g) for i in range(nq) for g in range(2)]
    st = {}

    def setup():
        kcat = jnp.concatenate([kvp_ref[:, 0:kvw], kvc_ref[:, 0:kvw]], axis=0)
        vcat = jnp.concatenate([kvp_ref[:, kvw:2 * kvw], kvc_ref[:, kvw:2 * kvw]], axis=0)
        st["v_t"] = vcat.T.astype(BF16)
        kr = pltpu.roll(kcat, HEAD_DIM, axis=1)
        lo = _iota2((tb + tq, kvw), 1) < HEAD_DIM

        def place(x_lo, x_hi):
            return jnp.where(lo, x_lo, 0.0).astype(BF16), jnp.where(lo, 0.0, x_hi).astype(BF16)

        st["k"] = (place(kcat, kr), place(kr, kcat))
        st["ki"] = _iota2((2 * tb, tb), 0)
        qi = _iota2((2 * tb, tb), 1)
        st["in_window"] = (st["ki"] > qi) & (st["ki"] <= qi + tb)

    def scores(ci):
        i, g = chains[ci]
        keys = jnp.concatenate([st["k"][g][0][i * tb:(i + 2) * tb],
                                st["k"][g][1][i * tb:(i + 2) * tb]], axis=0)
        qs = jnp.concatenate([q_ref[i * tb:(i + 1) * tb, 256 * g:256 * g + 128],
                              q_ref[i * tb:(i + 1) * tb, 256 * g + 128:256 * g + 256]], axis=0)
        st["s", ci] = _dot_nt(keys, qs.astype(BF16))

    def probs(ci):
        i, g = chains[ci]
        sc = st.pop(("s", ci))
        mask = st["in_window"]
        if i == 0:
            mask = mask & (st["ki"] >= jnp.where(first, tb, 0))
        ps, invs = [], []
        for part in range(2):
            for e in range(2):
                sink = sink_ref[2 * g + part:2 * g + part + 1, e * 2 * tb:e * 2 * tb + 1]
                se = jnp.where(mask, sc[e * 2 * tb:(e + 1) * 2 * tb, part * tb:(part + 1) * tb],
                               -jnp.inf)
                mx = jnp.maximum(jnp.max(se, axis=0, keepdims=True), sink)
                p = jnp.exp(se - mx)
                ps.append(p.astype(BF16))
                invs.append(1.0 / (jnp.sum(p, axis=0, keepdims=True) + jnp.exp(sink - mx)))
        st["p", ci] = jnp.concatenate(ps, axis=-1)
        st["inv", ci] = jnp.concatenate(invs, axis=-1)

    def attend(ci):
        i, g = chains[ci]
        o = _dot(st["v_t"][g * HEAD_DIM:(g + 1) * HEAD_DIM, i * tb:(i + 2) * tb],
                 st.pop(("p", ci))) * st.pop(("inv", ci))
        for part in range(2):
            j = 2 * g + part
            pair_t = jnp.concatenate([o[:, 2 * part * tb:(2 * part + 1) * tb],
                                      o[:, (2 * part + 1) * tb:(2 * part + 2) * tb]], axis=0)
            o_ref[i * tb:(i + 1) * tb, 128 * j:128 * (j + 1)] = pair_t.T

    n = len(chains)
    stages = [setup]
    for ci in range(n + 2):
        stages += [functools.partial(f, cj) for f, cj in ((scores, ci), (probs, ci - 1), (attend, ci - 2))
                   if 0 <= cj < n]
    return stages


def _post_kernel(x_ref, ya_ref, yb_ref, p_ref, wo_ref, nmlp_ref, wup_ref, wdn_ref,
                 nple_ref, wg_ref, wp_ref, nfin_ref,
                 sq0_ref, skv0_ref, sqn_ref, skvn_ref, kvpn_ref, sink_ref,
                 o_ref, yc_ref, *, final, tf, tiles_per_seq):
    t = x_ref.shape[0]
    step = pl.program_id(0)

    @pl.when(step == 0)
    def _():
        for stage in _swa_stages(sq0_ref, skv0_ref, kvpn_ref, sink_ref, yc_ref, True, t):
            stage()

    y = jnp.concatenate([ya_ref[...].astype(BF16), yb_ref[...].astype(BF16), yc_ref[...].astype(BF16)],
                        axis=1)
    x = x_ref[...] + _dot(y, wo_ref[...])
    stages = _swa_stages(sqn_ref, skvn_ref, kvpn_ref, sink_ref, yc_ref,
                         (step + 1) % tiles_per_seq == 0, t)
    slots = 2 * (D_FF // tf)
    per_slot = -(-len(stages) // slots)

    def fill():
        for _ in range(per_slot):
            if stages:
                stages.pop(0)()

    r = _rms_scale(x)
    h = (x * nmlp_ref[...]).astype(BF16)
    acc = jnp.zeros_like(x)
    for f in range(0, D_FF, tf):
        u = jnp.maximum(_dot(h, wup_ref[:, f:f + tf]), 0.0)
        fill()
        acc = acc + _dot((u * u).astype(BF16), wdn_ref[f:f + tf, :])
        fill()
    x = x + (r * r) * acc
    gate = _sigmoid(_rms_scale(x) * _dot((x * nple_ref[...]).astype(BF16), wg_ref[...]))
    x = x + gate * _dot(p_ref[...].astype(BF16), wp_ref[...])
    if final:
        x = _rms(x, nfin_ref[...])
    o_ref[...] = x


def _post(x, ya, yb, sq, skv, sinks_e, p, w_out, norm_mlp, w_up, w_down, norm_ple, w_gate, w_proj,
          norm_final, layer, tm, seq, final):
    m = x.shape[0]
    nt = m // tm
    blocks_per_tile = tm // SWA_BLOCK
    row = lambda n: pl.BlockSpec((tm, n), lambda i: (i, 0))
    first = lambda n: pl.BlockSpec((tm, n), lambda i: (0, 0))
    nxt = lambda n: pl.BlockSpec((tm, n), lambda i: (jnp.minimum(i + 1, nt - 1), 0))
    lay = lambda a, b: pl.BlockSpec((None, a, b), lambda i: (layer, 0, 0),
                                    pipeline_mode=pl.Buffered(1))
    return pl.pallas_call(
        functools.partial(_post_kernel, final=final, tf=512, tiles_per_seq=seq // tm),
        grid=(nt,),
        in_specs=[
            row(D_MODEL), row(WIDTH), row(WIDTH),
            pl.BlockSpec((None, tm, PLE_DIM), lambda i: (layer, i, 0)),
            lay(D_MODEL, D_MODEL), lay(1, D_MODEL), lay(D_MODEL, D_FF), lay(D_FF, D_MODEL),
            lay(1, D_MODEL), lay(D_MODEL, D_MODEL), lay(PLE_DIM, D_MODEL),
            pl.BlockSpec((1, D_MODEL), lambda i: (0, 0)),
            first(SWA_WIDTH), first(2 * SWA_KV_WIDTH), nxt(SWA_WIDTH), nxt(2 * SWA_KV_WIDTH),
            pl.BlockSpec((SWA_BLOCK, 2 * SWA_KV_WIDTH),
                         lambda i: ((i + 1) * blocks_per_tile - 1, 0)),
            pl.BlockSpec((None, SWA_WIDTH // 128, 4 * SWA_BLOCK), lambda i: (layer, 0, 0)),
        ],
        out_specs=row(D_MODEL),
        out_shape=jax.ShapeDtypeStruct((m, D_MODEL), F32),
        scratch_shapes=[pltpu.VMEM((tm, SWA_WIDTH), F32)],
        compiler_params=pltpu.CompilerParams(
            dimension_semantics=("arbitrary",), vmem_limit_bytes=VMEM_LIMIT),
    )(x, ya, yb, p, w_out, norm_mlp, w_up, w_down, norm_ple, w_gate, w_proj, norm_final,
      sq, skv, sq, skv, skv, sinks_e)


def _gate_row(depth, pieces):
    out = jnp.zeros((depth, 1, GATE_LANES), F32)
    for first_col, vals in pieces:
        out = out.at[:, 0, first_col:first_col + HEADS].set(vals.astype(F32))
    return out


def _rope_tables(positions):
    half = ROPE_DIM // 2
    inv_freq = ROPE_THETA ** (-jnp.arange(0, ROPE_DIM, 2, dtype=F32) / ROPE_DIM)
    dim = jnp.arange(128) % HEAD_DIM
    ang = positions.astype(F32).reshape(-1)[:, None] * inv_freq[dim % half][None, :]
    cos_t = jnp.where(dim < ROPE_DIM, jnp.cos(ang), 1.0)
    sin_t = jnp.where(dim < half, -jnp.sin(ang), jnp.where(dim < ROPE_DIM, jnp.sin(ang), 0.0))
    return cos_t, sin_t


def kernel(x, p, positions, w_in, conv_w, gdn_a_log, gdn_dt_bias, gdn_norm, mlstm_i_bias,
           mlstm_f_bias, mlstm_norm, attn_sinks, w_out, norm_mix, norm_mlp, w_up, w_down,
           norm_ple, w_ple_gate, w_ple_proj, norm_final):
    bsz, seq, d = x.shape
    depth = w_in.shape[0]
    m = bsz * seq
    tm = min(512, seq)
    t_mix = min(512, seq)
    nb_mix = 4 if bsz % 4 == 0 else 1
    nb_gdn = 8 if bsz % 8 == 0 else nb_mix
    t_gdn = min(2048 // nb_gdn, seq)

    cos_t, sin_t = _rope_tables(positions)
    gdn_par = jnp.concatenate([_gate_row(depth, [(HEADS, gdn_a_log)]),
                               _gate_row(depth, [(HEADS, gdn_dt_bias)])], axis=1)
    mlstm_par = _gate_row(depth, [(2 * HEADS, mlstm_i_bias), (3 * HEADS, mlstm_f_bias)])
    gdn_nw = jnp.tile(gdn_norm.astype(F32), (1, HEADS))[:, None, :]
    mlstm_nw = mlstm_norm.astype(F32)[:, None, :]
    sinks_e = jnp.repeat(attn_sinks.astype(F32), 2 * SWA_BLOCK, axis=-1).reshape(
        depth, SWA_WIDTH // 128, 4 * SWA_BLOCK)
    row3 = lambda a: a.astype(F32)[:, None, :]
    wo_b, wup_b, wdn_b = w_out.astype(BF16), w_up.astype(BF16), w_down.astype(BF16)
    wg_b, wp_b = w_ple_gate.astype(BF16), w_ple_proj.astype(BF16)
    nmix, nmlp, nple = row3(norm_mix), row3(norm_mlp), row3(norm_ple)
    nfin = norm_final.astype(F32)[None, :]
    p2 = p.reshape(depth, m, PLE_DIM)

    xf = x.reshape(m, d)
    for i in range(depth):
        gqkv, gz, mqkvo, sq, skv, gates = _proj(xf, nmix, w_in.astype(F32), cos_t, sin_t,
                                                conv_w.astype(F32), i, tm, seq)
        ya = _gdn(gqkv, gz, gates, gdn_par, gdn_nw, i, bsz, seq, t_gdn, nb_gdn)
        yb = _mlstm(mqkvo, gates, mlstm_par, mlstm_nw, i, bsz, seq, t_mix, nb_mix)
        xf = _post(xf, ya, yb, sq, skv, sinks_e, p2, wo_b, nmlp, wup_b, wdn_b, nple, wg_b, wp_b, nfin,
                   i, tm, seq, final=(i == depth - 1))
    return xf.reshape(bsz, seq, d)
```

```python
import functools

import jax
import jax.numpy as jnp
from jax import lax
from jax.experimental import pallas as pl
from jax.experimental.pallas import tpu as pltpu

F32 = jnp.float32
BF16 = jnp.bfloat16

D_MODEL = 1024
DEPTH = 4
HEAD_DIM = 64
PLE_DIM = 256
D_FF = 4 * D_MODEL
EPS = 1e-6
HEADS = 4
WIDTH = HEADS * HEAD_DIM
PAIR = 2 * HEAD_DIM
CHUNK = 64
CONV_WIDTH = 4
GATE_SOFTCAP = 15.0
SWA_WIDTH = 512
SWA_KV_WIDTH = 128
SWA_BLOCK = 128
ROPE_DIM = 16
ROPE_THETA = 500000.0
GATE_LANES = 128
IN_COLS_PACKED = 2944
OFF_GQKV, OFF_GZ, OFF_M, OFF_SQ, OFF_SKV, OFF_GATES = 0, 768, 1024, 2048, 2560, 2816
IN_A = 4 * WIDTH
IN_M0 = IN_A + 2 * HEADS
IN_G2 = IN_M0 + 4 * WIDTH
IN_S0 = IN_G2 + 2 * HEADS
IN_S1 = IN_S0 + SWA_WIDTH + 2 * SWA_KV_WIDTH

VMEM_LIMIT = 56 * 1024 * 1024


def _dot(a, b):
    return jnp.dot(a, b, preferred_element_type=F32)


def _dot_nt(a, b):
    return lax.dot_general(a, b, (((1,), (1,)), ((), ())), preferred_element_type=F32)


def _dot_tn(a, b):
    return lax.dot_general(a, b, (((0,), (0,)), ((), ())), preferred_element_type=F32)


def _split2(x):
    hi = x.astype(BF16)
    lo = (x - hi.astype(F32)).astype(BF16)
    return hi, lo


def _dot_sel(x, sel):
    hi, lo = _split2(x)
    return _dot(hi, sel) + _dot(lo, sel)


def _chunk_cumsum(x):
    pos = _iota2(x.shape, 0) & (CHUNK - 1)
    sh = 1
    while sh < CHUNK:
        x = x + jnp.where(pos >= sh, pltpu.roll(x, sh, axis=0), 0.0)
        sh *= 2
    return x


def _sigmoid(x):
    return 1.0 / (1.0 + jnp.exp(-x))


def _softplus(x):
    return jnp.maximum(x, 0.0) + jnp.log1p(jnp.exp(-jnp.abs(x)))


def _rms(x, g):
    return x * lax.rsqrt(jnp.mean(x * x, axis=-1, keepdims=True) + EPS) * g


def _rms_scale(x):
    return lax.rsqrt(jnp.mean(x * x, axis=-1, keepdims=True) + EPS)


def _iota2(shape, dim):
    return lax.broadcasted_iota(jnp.int32, shape, dim)


def _pairs(x):
    return [x[:, p * PAIR:(p + 1) * PAIR] for p in range(WIDTH // PAIR)]


def _pair_mask():
    return (_iota2((PAIR, PAIR), 0) >> 6) == (_iota2((PAIR, PAIR), 1) >> 6)


def _pair_diag(x):
    low = _iota2(x.shape, 1) < HEAD_DIM
    zero = jnp.zeros((), x.dtype)
    return jnp.concatenate([jnp.where(low, x, zero), jnp.where(low, zero, x)], axis=0)


def _heads_mm(a, b):
    return jnp.concatenate([_dot(ap, _pair_diag(bp)) for ap, bp in zip(_pairs(a), _pairs(b))], axis=1)


def _heads_mm_nt(a, b):
    return jnp.concatenate([_dot_nt(ap, _pair_diag(bp)) for ap, bp in zip(_pairs(a), _pairs(b))],
                           axis=1)


def _heads_sum(x, pair_ones):
    return jnp.concatenate([_dot(xp, pair_ones) for xp in _pairs(x)], axis=1)


def _emit_pipelined(stage_gens, lead):
    pending, live, step = list(stage_gens), [], 0
    while pending or live:
        if pending and step % lead == 0:
            live.append(pending.pop(0))
        for g in list(live):
            if next(g, StopIteration) is StopIteration:
                live.remove(g)
        step += 1


def _expander(first_col):
    r = _iota2((GATE_LANES, WIDTH), 0)
    c = _iota2((GATE_LANES, WIDTH), 1)
    return (r == first_col + (c >> 6)).astype(BF16)


def _rope(x, cos_t, sin_t, reps):
    n = x.shape[-1]
    if reps > 1:
        cos_t = jnp.concatenate([cos_t] * reps, axis=-1)
        sin_t = jnp.concatenate([sin_t] * reps, axis=-1)
    half = ROPE_DIM // 2
    x_up = pltpu.roll(x, n - half, axis=1)
    x_dn = pltpu.roll(x, half, axis=1)
    first_half = (_iota2(x.shape, 1) & (HEAD_DIM - 1)) < half
    return x * cos_t + jnp.where(first_half, x_up, x_dn) * sin_t


def _repack_w_in(w_in_ref, w_ref):
    rows = 128
    scale = HEAD_DIM ** -0.5
    lane_m = _iota2((1, 4 * WIDTH), 1)
    scale_m = jnp.where((lane_m >= WIDTH) & (lane_m < 2 * WIDTH), scale, 1.0)
    scale_s = jnp.where(_iota2((1, SWA_WIDTH + 2 * SWA_KV_WIDTH), 1) < SWA_WIDTH, scale, 1.0)

    def body(i, carry):
        r = pl.ds(pl.multiple_of(i * rows, rows), rows)
        w_ref[r, OFF_GQKV:OFF_M] = w_in_ref[r, 0:IN_A].astype(BF16)
        w_ref[r, OFF_M:OFF_SQ] = (w_in_ref[r, IN_M0:IN_G2] * scale_m).astype(BF16)
        w_ref[r, OFF_SQ:OFF_GATES] = (w_in_ref[r, IN_S0:IN_S1] * scale_s).astype(BF16)
        lane = _iota2((rows, GATE_LANES), 1)
        g2_tile = IN_G2 - 2 * HEADS
        gates = jnp.where(lane < 2 * HEADS, w_in_ref[r, IN_A:IN_A + GATE_LANES],
                          jnp.where(lane < 4 * HEADS, w_in_ref[r, g2_tile:g2_tile + GATE_LANES], 0.0))
        w_ref[r, OFF_GATES:IN_COLS_PACKED] = gates.astype(BF16)
        return carry

    lax.fori_loop(0, D_MODEL // rows, body, 0)


def _proj_kernel(x_ref, g_ref, w_in_ref, cos_ref, sin_ref, cw_ref, ones_ref,
                 gqkv_ref, gz_ref, m_ref, sq_ref, skv_ref, gates_ref, tail_ref, buf_ref, w_ref,
                 *, tiles_per_seq):
    t = x_ref.shape[0]

    @pl.when(pl.program_id(0) == 0)
    def _():
        _repack_w_in(w_in_ref, w_ref)

    @pl.when(pl.program_id(0) % tiles_per_seq == 0)
    def _():
        tail_ref[...] = jnp.zeros_like(tail_ref)

    x = x_ref[...]
    hb = (x * g_ref[...]).astype(BF16)
    scale = _rms_scale(x)

    def proj(lo, hi):
        return scale * _dot(hb, w_ref[:, lo:hi])

    a = proj(OFF_GQKV, OFF_M)
    buf_ref[0:8, :] = tail_ref[...]
    buf_ref[8:8 + t, :] = a[:, 0:OFF_GZ]
    tail_ref[...] = buf_ref[t:t + 8, :]
    gz_ref[...] = a[:, OFF_GZ:OFF_M]
    cw = cw_ref[...]
    pair_ones = ones_ref[...]

    def conv_silu(lo, hi):
        acc = buf_ref[5:5 + t, lo:hi] * cw[0:1, lo:hi]
        for j in range(1, CONV_WIDTH):
            acc = acc + buf_ref[5 + j:5 + j + t, lo:hi] * cw[j:j + 1, lo:hi]
        return acc * _sigmoid(acc)

    def l2n(v):
        return v * lax.rsqrt(_heads_sum((v * v).astype(BF16), pair_ones) + EPS)

    cos_t, sin_t = cos_ref[...], sin_ref[...]
    s = proj(OFF_SQ, IN_COLS_PACKED)
    q = conv_silu(0, WIDTH)
    gqkv_ref[:, 0:WIDTH] = l2n(q) * (HEAD_DIM ** -0.5)
    m_ref[:, 0:2 * WIDTH] = proj(OFF_M, OFF_M + 2 * WIDTH)
    sq_ref[...] = _rope(s[:, 0:SWA_WIDTH], cos_t, sin_t, SWA_WIDTH // 128)
    skv_ref[:, 0:SWA_KV_WIDTH] = _rope(s[:, SWA_WIDTH:SWA_WIDTH + SWA_KV_WIDTH], cos_t, sin_t, 1)
    skv_ref[:, SWA_KV_WIDTH:] = s[:, SWA_WIDTH + SWA_KV_WIDTH:SWA_WIDTH + 2 * SWA_KV_WIDTH]
    gates_ref[...] = s[:, SWA_WIDTH + 2 * SWA_KV_WIDTH:]
    k = conv_silu(WIDTH, 2 * WIDTH)
    gqkv_ref[:, WIDTH:2 * WIDTH] = l2n(k)
    gqkv_ref[:, 2 * WIDTH:3 * WIDTH] = conv_silu(2 * WIDTH, 3 * WIDTH)
    m_ref[:, 2 * WIDTH:4 * WIDTH] = proj(OFF_M + 2 * WIDTH, OFF_SQ)


def _proj(x, norm_w, w_in, cos_t, sin_t, conv_w, layer, tm, seq):
    m = x.shape[0]
    row = lambda n: pl.BlockSpec((tm, n), lambda i: (i, 0))
    widths = (768, 256, 1024, SWA_WIDTH, 2 * SWA_KV_WIDTH, GATE_LANES)
    return pl.pallas_call(
        functools.partial(_proj_kernel, tiles_per_seq=seq // tm),
        grid=(m // tm,),
        in_specs=[
            row(D_MODEL),
            pl.BlockSpec((None, 1, D_MODEL), lambda i: (layer, 0, 0)),
            pl.BlockSpec((None, D_MODEL, w_in.shape[-1]), lambda i: (layer, 0, 0),
                         pipeline_mode=pl.Buffered(1)),
            row(128), row(128),
            pl.BlockSpec((None, CONV_WIDTH, 3 * WIDTH), lambda i: (layer, 0, 0)),
            pl.BlockSpec((PAIR, PAIR), lambda i: (0, 0)),
        ],
        out_specs=[row(n) for n in widths],
        out_shape=[jax.ShapeDtypeStruct((m, n), F32) for n in widths],
        scratch_shapes=[
            pltpu.VMEM((8, 3 * WIDTH), F32),
            pltpu.VMEM((tm + 8, 3 * WIDTH), F32),
            pltpu.VMEM((D_MODEL, IN_COLS_PACKED), BF16),
        ],
        compiler_params=pltpu.CompilerParams(
            dimension_semantics=("arbitrary",), vmem_limit_bytes=VMEM_LIMIT),
    )(x, norm_w, w_in, cos_t, sin_t, conv_w, _pair_mask().astype(BF16))


def _gdn_kernel(qkv_ref, z_ref, gt_ref, par_ref, nw_ref, exp_ref, ones_ref, o_ref,
                state_ref, beta_s, gc_s, u_s, w_s, qd_s, kd_s, qkd_s, *, t, group, nb):
    c = CHUNK

    @pl.when(pl.program_id(1) == 0)
    def _():
        state_ref[...] = jnp.zeros_like(state_ref)

    pair_ones = ones_ref[...]

    def gate_rows(a, lo, hi):
        gates = gt_ref[a, lo:hi, :]
        par = par_ref[...]
        beta = _sigmoid(gates)
        g = -jnp.exp(par[0:1, :]) * _softplus(gates + par[1:2, :])
        beta_s[a, lo:hi, :] = _dot_sel(beta, exp_ref[0])
        gc_s[a, lo:hi, :] = _dot_sel(_chunk_cumsum(g), exp_ref[1])

    row = _iota2((c, WIDTH), 0)
    col = _iota2((c, WIDTH), 1) & (c - 1)
    causal = col <= row
    strict = col < row
    eye = col == row
    nw = nw_ref[...]

    def mm(a, b):
        return _heads_mm(a.astype(BF16), b.astype(BF16))

    def prepare(items, fillers):
        fillers = list(fillers)

        def fill():
            if fillers:
                fillers.pop(0)()

        n = range(len(items))
        sq = [a for a, _ in items]
        rs = [pl.ds(j * c, c) for _, j in items]
        q = [qkv_ref[sq[j], rs[j], 0:WIDTH] for j in n]
        k = [qkv_ref[sq[j], rs[j], WIDTH:2 * WIDTH] for j in n]
        gc = [gc_s[sq[j], rs[j], :] for j in n]
        g_last = [gc_s[a, pl.ds(j * c + c - 1, 1), :] for a, j in items]
        kb = [k[j] * beta_s[sq[j], rs[j], :] for j in n]
        kkqk = [_heads_mm_nt(jnp.concatenate([kb[j], q[j]], axis=0).astype(BF16), k[j].astype(BF16))
                for j in n]
        fill()
        gc_row = [jnp.sum(jnp.where(eye, gc[j], 0.0), axis=0, keepdims=True) for j in n]
        decay = [jnp.exp(jnp.where(causal, gc[j] - gc_row[j], -jnp.inf)) for j in n]
        nm = [jnp.where(strict, kkqk[j][0:c] * decay[j], 0.0) for j in n]
        for j in n:
            qkd_s[sq[j], rs[j], :] = (kkqk[j][c:2 * c] * decay[j]).astype(BF16)
        x = [eye.astype(F32) - jnp.where((row >> 1) == (col >> 1), nm[j], 0.0) for j in n]
        for lb in range(1, 6):
            off = ((row >> (lb + 1)) == (col >> (lb + 1))) & ((row >> lb) != (col >> lb))
            t1 = [mm(jnp.where(off, nm[j], 0.0), x[j]) for j in n]
            fill()
            t2 = [mm(x[j], t1[j]) for j in n]
            if lb % 2 == 0:
                fill()
            x = [x[j] - t2[j] for j in n]
        xb = [x[j].astype(BF16) for j in n]
        egc = [jnp.exp(gc[j]) for j in n]
        vbeta = [(qkv_ref[sq[j], rs[j], 2 * WIDTH:3 * WIDTH] * beta_s[sq[j], rs[j], :]).astype(BF16)
                 for j in n]
        kbd = [(kb[j] * egc[j]).astype(BF16) for j in n]
        uw = [[_dot(xp, jnp.concatenate([_pair_diag(vp), _pair_diag(kp)], axis=1))
               for xp, vp, kp in zip(_pairs(xb[j]), _pairs(vbeta[j]), _pairs(kbd[j]))] for j in n]
        for j in n:
            u_s[sq[j], rs[j], :] = jnp.concatenate([p[:, 0:PAIR] for p in uw[j]], axis=1)
            w_s[sq[j], rs[j], :] = jnp.concatenate([p[:, PAIR:2 * PAIR] for p in uw[j]], axis=1).astype(BF16)
        for j in n:
            qd_s[sq[j], rs[j], :] = (q[j] * egc[j]).astype(BF16)
            kd_s[sq[j], rs[j], :] = (k[j] * jnp.exp(g_last[j] - gc[j])).astype(BF16)
        while fillers:
            fill()

    chunks = [(a, j) for a in range(nb) for j in range(t // c)]
    groups = [chunks[i:i + group] for i in range(0, len(chunks), group)]
    for a, j in groups[0]:
        gate_rows(a, j * c, (j + 1) * c)
    for gi, items in enumerate(groups):
        nxt = groups[gi + 1] if gi + 1 < len(groups) else []
        prepare(items, [functools.partial(gate_rows, a, j * c, (j + 1) * c) for a, j in nxt])

    def chunk_stages(i):
        n = range(nb)
        r = pl.ds(pl.multiple_of(i * c, c), c)
        pairs = range(WIDTH // PAIR)
        g_last = [gc_s[a, pl.ds(i * c + c - 1, 1), :] for a in n]
        s = [[state_ref[a, p] for p in pairs] for a in n]
        lhs = [_pairs(jnp.concatenate([w_s[a, r, :], qd_s[a, r, :]], axis=0)) for a in n]
        ws = [jnp.concatenate([_dot(lhs[a][p], s[a][p].astype(BF16)) for p in pairs], axis=1)
              for a in n]
        yield
        v_new = [(u_s[a, r, :] - ws[a][0:c]).astype(BF16) for a in n]
        v_diag = [[_pair_diag(vn) for vn in _pairs(v_new[a])] for a in n]
        upd = [[_dot_tn(_pair_diag(kd), v_diag[a][p]) for p, kd in enumerate(_pairs(kd_s[a, r, :]))]
               for a in n]
        yield
        for a in n:
            decay_end = _pairs(jnp.exp(g_last[a]))
            for p in pairs:
                state_ref[a, p] = s[a][p] * decay_end[p] + upd[a][p]
        yield
        o = [ws[a][c:2 * c] + jnp.concatenate(
            [_dot(qp, v_diag[a][p]) for p, qp in enumerate(_pairs(qkd_s[a, r, :]))], axis=1) for a in n]
        yield
        ms = [_heads_sum((o[a] * o[a]).astype(BF16), pair_ones) * (1.0 / HEAD_DIM) for a in n]
        yield
        for a in n:
            z = z_ref[a, r, :]
            o_ref[a, r, :] = o[a] * lax.rsqrt(ms[a] + EPS) * nw * (z * _sigmoid(z))

    state_stages = 3
    per_trip = 4 if (t // c) % 4 == 0 else 2

    def chunk_group(ii, carry):
        _emit_pipelined([chunk_stages(per_trip * ii + j) for j in range(per_trip)], state_stages)
        return carry

    assert (t // c) % per_trip == 0
    lax.fori_loop(0, t // c // per_trip, chunk_group, 0)


def _gdn(gqkv, gz, gates, par, norm_w, layer, bsz, seq, t, nb):
    row = lambda n: pl.BlockSpec((nb, t, n), lambda b, j: (b, j, 0))
    lay = lambda a, b2: pl.BlockSpec((None, a, b2), lambda b, j: (layer, 0, 0))
    const = lambda shape: pl.BlockSpec(shape, lambda b, j: (0,) * len(shape))
    seq3 = lambda a: a.reshape(bsz, seq, a.shape[-1])
    f32_buf = pltpu.VMEM((nb, t, WIDTH), F32)
    bf16_buf = pltpu.VMEM((nb, t, WIDTH), BF16)
    out = pl.pallas_call(
        functools.partial(_gdn_kernel, t=t, group=8, nb=nb),
        grid=(bsz // nb, seq // t),
        in_specs=[row(3 * WIDTH), row(WIDTH), row(GATE_LANES),
                  lay(2, GATE_LANES), lay(1, WIDTH),
                  const((2, GATE_LANES, WIDTH)), const((PAIR, PAIR))],
        out_specs=row(WIDTH),
        out_shape=jax.ShapeDtypeStruct((bsz, seq, WIDTH), F32),
        scratch_shapes=[
            pltpu.VMEM((nb, WIDTH // PAIR, PAIR, PAIR), F32),
            f32_buf, f32_buf,
            f32_buf,
            bf16_buf, bf16_buf, bf16_buf, bf16_buf,
        ],
        compiler_params=pltpu.CompilerParams(
            dimension_semantics=("arbitrary", "arbitrary"), vmem_limit_bytes=VMEM_LIMIT),
    )(seq3(gqkv), seq3(gz), seq3(gates), par, norm_w,
      jnp.stack([_expander(0), _expander(HEADS)]), _pair_mask().astype(BF16))
    return out.reshape(bsz * seq, WIDTH)


def _mlstm_kernel(m_ref, gt_ref, par_ref, nw_ref, exp_ref, ones_ref, o_ref,
                  cn_ref, mx_ref, ig_s, b_s, cm_s, *, t, nb):
    c = CHUNK

    @pl.when(pl.program_id(1) == 0)
    def _():
        cn_ref[...] = jnp.zeros_like(cn_ref)
        mx_ref[...] = jnp.zeros_like(mx_ref)

    pair_ones = ones_ref[...]

    def front(a, carry):
        pre = gt_ref[a] + par_ref[...]
        capped = GATE_SOFTCAP * jnp.tanh(pre * (1.0 / GATE_SOFTCAP))
        log_f = -_softplus(-capped)
        b_all = _chunk_cumsum(log_f)
        ig = _dot_sel(capped, exp_ref[0])
        b = _dot_sel(b_all, exp_ref[1])
        pos = _iota2((t, WIDTH), 0) & (c - 1)
        cm = ig - b
        for sh in (1, 2, 4, 8, 16, 32):
            cm = jnp.where(pos >= sh, jnp.maximum(cm, pltpu.roll(cm, sh, axis=0)), cm)
        ig_s[a] = ig
        b_s[a] = b
        cm_s[a] = cm
        return carry

    lax.fori_loop(0, nb, front, 0)

    row = _iota2((c, WIDTH), 0)
    col = _iota2((c, WIDTH), 1) & (c - 1)
    causal = col <= row
    eye = col == row
    nw = nw_ref[...]

    def chunk_stages(i):
        n = range(nb)
        pairs = range(WIDTH // PAIR)
        r = pl.ds(pl.multiple_of(i * c, c), c)
        last = pl.ds(i * c + c - 1, 1)
        qb = [m_ref[a, r, 0:WIDTH].astype(BF16) for a in n]
        k = [m_ref[a, r, WIDTH:2 * WIDTH] for a in n]
        vb = [m_ref[a, r, 2 * WIDTH:3 * WIDTH].astype(BF16) for a in n]
        ig = [ig_s[a, r, :] for a in n]
        b = [b_s[a, r, :] for a in n]
        m_intra = [b[a] + cm_s[a, r, :] for a in n]
        b_last = [b_s[a, last, :] for a in n]
        m_chunk = [b_last[a] + cm_s[a, last, :] for a in n]
        qk = [_heads_mm_nt(qb[a], k[a].astype(BF16)) for a in n]
        yield
        gate_row = [jnp.sum(jnp.where(eye, ig[a] - b[a], 0.0), axis=0, keepdims=True) for a in n]
        qk = [qk[a] * jnp.exp(jnp.where(causal, b[a] + gate_row[a], -jnp.inf) - m_intra[a]) for a in n]
        yield
        v_ones = [[jnp.concatenate([_pair_diag(vp), pair_ones], axis=1) for vp in _pairs(vb[a])]
                  for a in n]
        qk_parts = [_split2(qk[a]) for a in n]
        intra = [[_dot(qp, v_ones[a][p]) for p, qp in enumerate(_pairs(qk_parts[a][0]))] for a in n]
        yield
        num_intra = [jnp.concatenate([intra[a][p][:, 0:PAIR] for p in pairs], axis=1) for a in n]
        den_intra = [jnp.concatenate([intra[a][p][:, PAIR:2 * PAIR] for p in pairs], axis=1)
                     + _heads_sum(qk_parts[a][1], pair_ones) for a in n]
        ke = [(k[a] * jnp.exp(b_last[a] - b[a] + ig[a] - m_chunk[a])).astype(BF16) for a in n]
        own = [[_dot_tn(_pair_diag(kp), v_ones[a][p]) for p, kp in enumerate(_pairs(ke[a]))]
               for a in n]
        yield
        m_prev = [mx_ref[a] for a in n]
        cn = [[cn_ref[a, p] for p in pairs] for a in n]
        inter = [[_dot(qp, cn[a][p].astype(BF16)) for p, qp in enumerate(_pairs(qb[a]))] for a in n]
        yield
        for a in n:
            m_new = jnp.maximum(b_last[a] + m_prev[a], m_chunk[a])
            s_old = _pairs(jnp.exp(b_last[a] + m_prev[a] - m_new))
            s_new = _pairs(jnp.exp(m_chunk[a] - m_new))
            for p in pairs:
                cn_ref[a, p] = (jnp.concatenate([s_old[p], s_old[p]], axis=1) * cn[a][p]
                                + jnp.concatenate([s_new[p], s_new[p]], axis=1) * own[a][p])
            mx_ref[a] = m_new
        yield
        h = []
        for a in n:
            pre_m = b[a] + m_prev[a]
            m_t = jnp.maximum(pre_m, m_intra[a])
            s_inter = jnp.exp(pre_m - m_t)
            s_intra = jnp.exp(m_intra[a] - m_t)
            num_inter = jnp.concatenate([inter[a][p][:, 0:PAIR] for p in pairs], axis=1)
            den_inter = jnp.concatenate([inter[a][p][:, PAIR:2 * PAIR] for p in pairs], axis=1)
            num = s_inter * num_inter + s_intra * num_intra[a]
            den = s_inter * den_inter + s_intra * den_intra[a]
            h.append(num / jnp.maximum(jnp.abs(den), jnp.exp(-m_t)))
        yield
        ms = [_heads_sum((h[a] * h[a]).astype(BF16), pair_ones) * (1.0 / HEAD_DIM) for a in n]
        yield
        for a in n:
            o_ref[a, r, :] = (h[a] * lax.rsqrt(ms[a] + EPS) * nw
                              * _sigmoid(m_ref[a, r, 3 * WIDTH:4 * WIDTH]))

    state_free = 4
    per_trip = 4 if (t // c) % 4 == 0 else 2

    def chunk_group(ii, carry):
        _emit_pipelined([chunk_stages(per_trip * ii + j) for j in range(per_trip)], state_free)
        return carry

    assert (t // c) % per_trip == 0
    lax.fori_loop(0, t // c // per_trip, chunk_group, 0)


def _mlstm(mqkvo, gates, par, norm_w, layer, bsz, seq, t, nb):
    row = lambda n: pl.BlockSpec((nb, t, n), lambda b, j: (b, j, 0))
    lay = lambda a, b2: pl.BlockSpec((None, a, b2), lambda b, j: (layer, 0, 0))
    const = lambda shape: pl.BlockSpec(shape, lambda b, j: (0,) * len(shape))
    seq3 = lambda a: a.reshape(bsz, seq, a.shape[-1])
    f32_buf = pltpu.VMEM((nb, t, WIDTH), F32)
    out = pl.pallas_call(
        functools.partial(_mlstm_kernel, t=t, nb=nb),
        grid=(bsz // nb, seq // t),
        in_specs=[row(4 * WIDTH), row(GATE_LANES), lay(1, GATE_LANES), lay(1, WIDTH),
                  const((2, GATE_LANES, WIDTH)), const((PAIR, PAIR))],
        out_specs=row(WIDTH),
        out_shape=jax.ShapeDtypeStruct((bsz, seq, WIDTH), F32),
        scratch_shapes=[
            pltpu.VMEM((nb, WIDTH // PAIR, PAIR, 2 * PAIR), F32),
            pltpu.VMEM((nb, 1, WIDTH), F32),
            f32_buf, f32_buf, f32_buf,
        ],
        compiler_params=pltpu.CompilerParams(
            dimension_semantics=("arbitrary", "arbitrary"), vmem_limit_bytes=VMEM_LIMIT),
    )(seq3(mqkvo), seq3(gates), par, norm_w,
      jnp.stack([_expander(2 * HEADS), _expander(3 * HEADS)]),
      _pair_mask().astype(BF16))
    return out.reshape(bsz * seq, WIDTH)


def _swa_stages(q_ref, kvc_ref, kvp_ref, sink_ref, o_ref, first, tq):
    tb = SWA_BLOCK
    kvw = SWA_KV_WIDTH
    nq = tq // tb
    chains = [(i, g) for i in range(nq) for g in range(2)]
    st = {}

    def setup():
        kcat = jnp.concatenate([kvp_ref[:, 0:kvw], kvc_ref[:, 0:kvw]], axis=0)
        vcat = jnp.concatenate([kvp_ref[:, kvw:2 * kvw], kvc_ref[:, kvw:2 * kvw]], axis=0)
        st["v_t"] = vcat.T.astype(BF16)
        kr = pltpu.roll(kcat, HEAD_DIM, axis=1)
        lo = _iota2((tb + tq, kvw), 1) < HEAD_DIM

        def place(x_lo, x_hi):
            return jnp.where(lo, x_lo, 0.0).astype(BF16), jnp.where(lo, 0.0, x_hi).astype(BF16)

        st["k"] = (place(kcat, kr), place(kr, kcat))
        st["ki"] = _iota2((2 * tb, tb), 0)
        qi = _iota2((2 * tb, tb), 1)
        st["in_window"] = (st["ki"] > qi) & (st["ki"] <= qi + tb)

    def scores(ci):
        i, g = chains[ci]
        keys = jnp.concatenate([st["k"][g][0][i * tb:(i + 2) * tb],
                                st["k"][g][1][i * tb:(i + 2) * tb]], axis=0)
        qs = jnp.concatenate([q_ref[i * tb:(i + 1) * tb, 256 * g:256 * g + 128],
                              q_ref[i * tb:(i + 1) * tb, 256 * g + 128:256 * g + 256]], axis=0)
        st["s", ci] = _dot_nt(keys, qs.astype(BF16))

    def probs(ci):
        i, g = chains[ci]
        sc = st.pop(("s", ci))
        mask = st["in_window"]
        if i == 0:
            mask = mask & (st["ki"] >= jnp.where(first, tb, 0))
        ps, invs = [], []
        for part in range(2):
            for e in range(2):
                sink = sink_ref[2 * g + part:2 * g + part + 1, e * 2 * tb:e * 2 * tb + 1]
                se = jnp.where(mask, sc[e * 2 * tb:(e + 1) * 2 * tb, part * tb:(part + 1) * tb],
                               -jnp.inf)
                mx = jnp.maximum(jnp.max(se, axis=0, keepdims=True), sink)
                p = jnp.exp(se - mx)
                ps.append(p.astype(BF16))
                invs.append(1.0 / (jnp.sum(p, axis=0, keepdims=True) + jnp.exp(sink - mx)))
        st["p", ci] = jnp.concatenate(ps, axis=-1)
        st["inv", ci] = jnp.concatenate(invs, axis=-1)

    def attend(ci):
        i, g = chains[ci]
        o = _dot(st["v_t"][g * HEAD_DIM:(g + 1) * HEAD_DIM, i * tb:(i + 2) * tb],
                 st.pop(("p", ci))) * st.pop(("inv", ci))
        for part in range(2):
            j = 2 * g + part
            pair_t = jnp.concatenate([o[:, 2 * part * tb:(2 * part + 1) * tb],
                                      o[:, (2 * part + 1) * tb:(2 * part + 2) * tb]], axis=0)
            o_ref[i * tb:(i + 1) * tb, 128 * j:128 * (j + 1)] = pair_t.T

    n = len(chains)
    stages = [setup]
    for ci in range(n + 2):
        stages += [functools.partial(f, cj) for f, cj in ((scores, ci), (probs, ci - 1), (attend, ci - 2))
                   if 0 <= cj < n]
    return stages


def _post_kernel(x_ref, ya_ref, yb_ref, p_ref, wo_ref, nmlp_ref, wup_ref, wdn_ref,
                 nple_ref, wg_ref, wp_ref, nfin_ref,
                 sq0_ref, skv0_ref, sqn_ref, skvn_ref, kvpn_ref, sink_ref,
                 o_ref, yc_ref, *, final, tf, tiles_per_seq):
    t = x_ref.shape[0]
    step = pl.program_id(0)

    @pl.when(step == 0)
    def _():
        for stage in _swa_stages(sq0_ref, skv0_ref, kvpn_ref, sink_ref, yc_ref, True, t):
            stage()

    y = jnp.concatenate([ya_ref[...].astype(BF16), yb_ref[...].astype(BF16), yc_ref[...].astype(BF16)],
                        axis=1)
    x = x_ref[...] + _dot(y, wo_ref[...])
    stages = _swa_stages(sqn_ref, skvn_ref, kvpn_ref, sink_ref, yc_ref,
                         (step + 1) % tiles_per_seq == 0, t)
    slots = 2 * (D_FF // tf)
    per_slot = -(-len(stages) // slots)

    def fill():
        for _ in range(per_slot):
            if stages:
                stages.pop(0)()

    r = _rms_scale(x)
    h = (x * nmlp_ref[...]).astype(BF16)
    acts = []
    for f in range(0, D_FF, tf):
        u = jnp.maximum(_dot(h, wup_ref[:, f:f + tf]), 0.0)
        acts.append((u * u).astype(BF16))
        fill()
        fill()
    x = x + (r * r) * _dot(jnp.concatenate(acts, axis=1), wdn_ref[...])
    gate = _sigmoid(_rms_scale(x) * _dot((x * nple_ref[...]).astype(BF16), wg_ref[...]))
    x = x + gate * _dot(p_ref[...].astype(BF16), wp_ref[...])
    if final:
        x = _rms(x, nfin_ref[...])
    o_ref[...] = x


def _post(x, ya, yb, sq, skv, sinks_e, p, w_out, norm_mlp, w_up, w_down, norm_ple, w_gate, w_proj,
          norm_final, layer, tm, seq, final):
    m = x.shape[0]
    nt = m // tm
    blocks_per_tile = tm // SWA_BLOCK
    row = lambda n: pl.BlockSpec((tm, n), lambda i: (i, 0))
    first = lambda n: pl.BlockSpec((tm, n), lambda i: (0, 0))
    nxt = lambda n: pl.BlockSpec((tm, n), lambda i: (jnp.minimum(i + 1, nt - 1), 0))
    lay = lambda a, b: pl.BlockSpec((None, a, b), lambda i: (layer, 0, 0),
                                    pipeline_mode=pl.Buffered(1))
    return pl.pallas_call(
        functools.partial(_post_kernel, final=final, tf=512, tiles_per_seq=seq // tm),
        grid=(nt,),
        in_specs=[
            row(D_MODEL), row(WIDTH), row(WIDTH),
            pl.BlockSpec((None, tm, PLE_DIM), lambda i: (layer, i, 0)),
            lay(D_MODEL, D_MODEL), lay(1, D_MODEL), lay(D_MODEL, D_FF), lay(D_FF, D_MODEL),
            lay(1, D_MODEL), lay(D_MODEL, D_MODEL), lay(PLE_DIM, D_MODEL),
            pl.BlockSpec((1, D_MODEL), lambda i: (0, 0)),
            first(SWA_WIDTH), first(2 * SWA_KV_WIDTH), nxt(SWA_WIDTH), nxt(2 * SWA_KV_WIDTH),
            pl.BlockSpec((SWA_BLOCK, 2 * SWA_KV_WIDTH),
                         lambda i: ((i + 1) * blocks_per_tile - 1, 0)),
            pl.BlockSpec((None, SWA_WIDTH // 128, 4 * SWA_BLOCK), lambda i: (layer, 0, 0)),
        ],
        out_specs=row(D_MODEL),
        out_shape=jax.ShapeDtypeStruct((m, D_MODEL), F32),
        scratch_shapes=[pltpu.VMEM((tm, SWA_WIDTH), F32)],
        compiler_params=pltpu.CompilerParams(
            dimension_semantics=("arbitrary",), vmem_limit_bytes=VMEM_LIMIT),
    )(x, ya, yb, p, w_out, norm_mlp, w_up, w_down, norm_ple, w_gate, w_proj, norm_final,
      sq, skv, sq, skv, skv, sinks_e)


def _gate_row(depth, pieces):
    out = jnp.zeros((depth, 1, GATE_LANES), F32)
    for first_col, vals in pieces:
        out = out.at[:, 0, first_col:first_col + HEADS].set(vals.astype(F32))
    return out


def _rope_tables(positions):
    half = ROPE_DIM // 2
    inv_freq = ROPE_THETA ** (-jnp.arange(0, ROPE_DIM, 2, dtype=F32) / ROPE_DIM)
    dim = jnp.arange(128) % HEAD_DIM
    ang = positions.astype(F32).reshape(-1)[:, None] * inv_freq[dim % half][None, :]
    cos_t = jnp.where(dim < ROPE_DIM, jnp.cos(ang), 1.0)
    sin_t = jnp.where(dim < half, -jnp.sin(ang), jnp.where(dim < ROPE_DIM, jnp.sin(ang), 0.0))
    return cos_t, sin_t


def kernel(x, p, positions, w_in, conv_w, gdn_a_log, gdn_dt_bias, gdn_norm, mlstm_i_bias,
           mlstm_f_bias, mlstm_norm, attn_sinks, w_out, norm_mix, norm_mlp, w_up, w_down,
           norm_ple, w_ple_gate, w_ple_proj, norm_final):
    bsz, seq, d = x.shape
    depth = w_in.shape[0]
    m = bsz * seq
    tm = min(512, seq)
    t_mix = min(512, seq)
    nb_mix = 4 if bsz % 4 == 0 else 1
    nb_gdn = 8 if bsz % 8 == 0 else nb_mix
    t_gdn = min(2048 // nb_gdn, seq)

    cos_t, sin_t = _rope_tables(positions)
    gdn_par = jnp.concatenate([_gate_row(depth, [(HEADS, gdn_a_log)]),
                               _gate_row(depth, [(HEADS, gdn_dt_bias)])], axis=1)
    mlstm_par = _gate_row(depth, [(2 * HEADS, mlstm_i_bias), (3 * HEADS, mlstm_f_bias)])
    gdn_nw = jnp.tile(gdn_norm.astype(F32), (1, HEADS))[:, None, :]
    mlstm_nw = mlstm_norm.astype(F32)[:, None, :]
    sinks_e = jnp.repeat(attn_sinks.astype(F32), 2 * SWA_BLOCK, axis=-1).reshape(
        depth, SWA_WIDTH // 128, 4 * SWA_BLOCK)
    row3 = lambda a: a.astype(F32)[:, None, :]
    wo_b, wup_b, wdn_b = w_out.astype(BF16), w_up.astype(BF16), w_down.astype(BF16)
    wg_b, wp_b = w_ple_gate.astype(BF16), w_ple_proj.astype(BF16)
    nmix, nmlp, nple = row3(norm_mix), row3(norm_mlp), row3(norm_ple)
    nfin = norm_final.astype(F32)[None, :]
    p2 = p.reshape(depth, m, PLE_DIM)

    xf = x.reshape(m, d)
    for i in range(depth):
        gqkv, gz, mqkvo, sq, skv, gates = _proj(xf, nmix, w_in.astype(F32), cos_t, sin_t,
                                                conv_w.astype(F32), i, tm, seq)
        ya = _gdn(gqkv, gz, gates, gdn_par, gdn_nw, i, bsz, seq, t_gdn, nb_gdn)
        yb = _mlstm(mqkvo, gates, mlstm_par, mlstm_nw, i, bsz, seq, t_mix, nb_mix)
        xf = _post(xf, ya, yb, sq, skv, sinks_e, p2, wo_b, nmlp, wup_b, wdn_b, nple, wg_b, wp_b, nfin,
                   i, tm, seq, final=(i == depth - 1))
    return xf.reshape(bsz, seq, d)
```
